```python
import math
import jax, jax.numpy as jnp
from jax import lax
import numpy as np

D_MODEL = 2048
BATCH = 8
SEQ = 2048
DEPTH = 2

N_MIXERS = 2
NSA_HEADS = 16
NSA_KV_GROUPS = 4
NSA_HEAD_DIM = D_MODEL // NSA_HEADS
CMP_BLOCK = 32
CMP_STRIDE = 16
SEL_BLOCK = 64
SEL_TOP_N = 8
WINDOW = 512
Q_BLOCK = 128
SEL_Q_BLOCK = 32
FORCE_SCORE = 1.0e4
HGRN_HEADS = 16
HGRN_KDIM = D_MODEL // HGRN_HEADS
HGRN_VDIM = D_MODEL // HGRN_HEADS
HGRN_CHUNK = 64
D_FF = 4 * D_MODEL
REL_BUCKETS = 32
REL_MAX_DIST = 128
RMS_EPS = 1e-6
NEG_INF = -1.0e30

kernel_name = "nsa_hgrn2_interleaved_trunk"


def rmsnorm(x, g):
    x32 = x.astype(jnp.float32)
    y = x32 * lax.rsqrt(jnp.mean(x32 * x32, axis=-1, keepdims=True) + RMS_EPS)
    return (y * g.astype(jnp.float32)).astype(x.dtype)


def rel_bucket(dist):
    n = jnp.maximum(dist, 0)
    max_exact = REL_BUCKETS // 2
    nf = jnp.maximum(n, 1).astype(jnp.float32)
    large = max_exact + (jnp.log(nf / max_exact) / math.log(REL_MAX_DIST / max_exact)
                         * (REL_BUCKETS - max_exact)).astype(jnp.int32)
    large = jnp.minimum(large, REL_BUCKETS - 1)
    return jnp.where(n < max_exact, n, large)


def compress_blocks(kraw, pe, w1, w2):
    B, S, G, Dh = kraw.shape
    ratio = CMP_BLOCK // CMP_STRIDE
    nh = S // CMP_STRIDE
    nc = nh - ratio + 1
    half = kraw.reshape(B, nh, CMP_STRIDE, G, Dh)
    blocks = jnp.concatenate([half[:, j:j + nc] for j in range(ratio)], axis=2)
    blocks = blocks + pe[:, None, :]
    flat = blocks.transpose(0, 1, 3, 2, 4).reshape(B, nc, G, CMP_BLOCK * Dh)
    return jax.nn.gelu(flat @ w1) @ w2


def nsa_mixer(h, w_in, cmp_pe, cmp_w1, cmp_w2, w_out, rel_table):
    B, S, _ = h.shape
    H, G, Dh = NSA_HEADS, NSA_KV_GROUPS, NSA_HEAD_DIM
    R = H // G
    dt = h.dtype
    proj = h @ w_in
    sizes = [H * Dh] + [G * Dh] * 6 + [3 * H]
    q, kc, vc, ks, vs, kw, vw, gl = jnp.split(proj, [int(v) for v in np.cumsum(sizes)[:-1]], axis=-1)
    q = q.reshape(B, S, G, R, Dh) * jnp.asarray(Dh ** -0.5, dt)
    kc, vc, ks, vs, kw, vw = [a.reshape(B, S, G, Dh) for a in (kc, vc, ks, vs, kw, vw)]
    gates = jax.nn.sigmoid(gl.astype(jnp.float32)).astype(dt).reshape(B, S, 3, G, R)
    pos = jnp.arange(S, dtype=jnp.int32)

    k_cmp = compress_blocks(kc, cmp_pe[0], cmp_w1[0], cmp_w2[0])
    v_cmp = compress_blocks(vc, cmp_pe[1], cmp_w1[1], cmp_w2[1])
    nc = k_cmp.shape[1]
    c_start = jnp.arange(nc, dtype=jnp.int32) * CMP_STRIDE
    c_end = c_start + CMP_BLOCK - 1
    dist_c = pos[:, None] - c_end[None, :]
    valid_c = dist_c >= 0
    bias_c = rel_table[rel_bucket(dist_c)].reshape(S, nc, G, R).transpose(2, 3, 0, 1)
    logit_c = jnp.einsum('bsgrd,bcgd->bgrsc', q, k_cmp).astype(jnp.float32) + bias_c
    logit_c = jnp.where(valid_c, logit_c, NEG_INF)
    any_c = jnp.any(valid_c, axis=-1).astype(jnp.float32)[:, None]
    p_c = jax.nn.softmax(logit_c, axis=-1) * any_c
    o_c = jnp.einsum('bgrsc,bcgd->bsgrd', p_c.astype(dt), v_cmp)

    nb = S // SEL_BLOCK
    jb = jnp.arange(nb, dtype=jnp.int32)
    b_start = jb * SEL_BLOCK
    overlap = ((c_start[:, None] <= b_start[None, :] + SEL_BLOCK - 1)
               & (c_end[:, None] >= b_start[None, :])).astype(jnp.float32)
    imp = jnp.einsum('bgrsc,cn->bgsn', p_c, overlap)
    q_blk = pos // SEL_BLOCK
    forced = (jb[None, :] == 0) | (jb[None, :] == q_blk[:, None]) | (jb[None, :] == q_blk[:, None] - 1)
    future = jb[None, :] > q_blk[:, None]
    imp = jnp.where(forced, FORCE_SCORE, jnp.where(future, -1.0, imp))
    n_sel = min(SEL_TOP_N, nb)
    _, sel_idx = lax.top_k(imp, n_sel)

    k_blk = ks.reshape(B, nb, SEL_BLOCK, G, Dh).transpose(0, 3, 1, 2, 4)
    v_blk = vs.reshape(B, nb, SEL_BLOCK, G, Dh).transpose(0, 3, 1, 2, 4)
    nqc = S // SEL_Q_BLOCK
    q_ch = q.reshape(B, nqc, SEL_Q_BLOCK, G, R, Dh).transpose(1, 0, 2, 3, 4, 5)
    idx_ch = sel_idx.reshape(B, G, nqc, SEL_Q_BLOCK, n_sel).transpose(2, 0, 1, 3, 4)
    t_ch = pos.reshape(nqc, SEL_Q_BLOCK)
    table_g = rel_table.reshape(REL_BUCKETS, G, R).transpose(1, 0, 2)
    bi = jnp.arange(B)[:, None, None, None]
    gi = jnp.arange(G)[None, :, None, None]
    gi5 = jnp.arange(G)[None, :, None, None, None]
    offs = jnp.arange(SEL_BLOCK, dtype=jnp.int32)

    def sel_step(args):
        qc, ic, tc = args
        kb = k_blk[bi, gi, ic]
        vb = v_blk[bi, gi, ic]
        kpos = ic[..., None] * SEL_BLOCK + offs
        dist = tc[None, None, :, None, None] - kpos
        bias = table_g[gi5, rel_bucket(dist)].transpose(0, 1, 5, 2, 3, 4)
        logit = jnp.einsum('btgrd,bgtnld->bgrtnl', qc, kb).astype(jnp.float32) + bias
        logit = jnp.where((dist >= 0)[:, :, None], logit, NEG_INF)
        sh = logit.shape
        p = jax.nn.softmax(logit.reshape(sh[:4] + (-1,)), axis=-1).reshape(sh)
        return jnp.einsum('bgrtnl,bgtnld->btgrd', p.astype(dt), vb)

    o_s = lax.map(sel_step, (q_ch, idx_ch, t_ch))
    o_s = o_s.transpose(1, 0, 2, 3, 4, 5).reshape(B, S, G, R, Dh)

    nqb = S // Q_BLOCK
    span = WINDOW + Q_BLOCK
    kw_p = jnp.pad(kw, ((0, 0), (WINDOW, 0), (0, 0), (0, 0)))
    vw_p = jnp.pad(vw, ((0, 0), (WINDOW, 0), (0, 0), (0, 0)))
    q_wb = q.reshape(B, nqb, Q_BLOCK, G, R, Dh).transpose(1, 0, 2, 3, 4, 5)

    def win_step(args):
        qb, c = args
        start = c * Q_BLOCK
        kb = lax.dynamic_slice_in_dim(kw_p, start, span, axis=1)
        vb = lax.dynamic_slice_in_dim(vw_p, start, span, axis=1)
        tq = start + jnp.arange(Q_BLOCK, dtype=jnp.int32)
        kp = start - WINDOW + jnp.arange(span, dtype=jnp.int32)
        dist = tq[:, None] - kp[None, :]
        valid = (dist >= 0) & (dist < WINDOW) & (kp[None, :] >= 0)
        bias = rel_table[rel_bucket(dist)].reshape(Q_BLOCK, span, G, R).transpose(2, 3, 0, 1)
        logit = jnp.einsum('btgrd,bsgd->bgrts', qb, kb).astype(jnp.float32) + bias
        logit = jnp.where(valid, logit, NEG_INF)
        p = jax.nn.softmax(logit, axis=-1)
        return jnp.einsum('bgrts,bsgd->btgrd', p.astype(dt), vb)

    o_w = lax.map(win_step, (q_wb, jnp.arange(nqb, dtype=jnp.int32)))
    o_w = o_w.transpose(1, 0, 2, 3, 4, 5).reshape(B, S, G, R, Dh)

    o = (gates[:, :, 0, :, :, None] * o_c + gates[:, :, 1, :, :, None] * o_s
         + gates[:, :, 2, :, :, None] * o_w)
    return o.reshape(B, S, H * Dh) @ w_out


def hgrn2_mixer(h, w_in, lb, onorm_g, w_out):
    B, S, D = h.shape
    H, dk, dv, C = HGRN_HEADS, HGRN_KDIM, HGRN_VDIM, HGRN_CHUNK
    dt = h.dtype
    proj = h @ w_in
    q, f, i, g = jnp.split(proj, 4, axis=-1)
    q = jax.nn.silu(q.astype(jnp.float32))
    f32 = f.astype(jnp.float32)
    lb = lb.astype(jnp.float32)
    log_f = jnp.logaddexp(jnp.log(lb), jnp.log1p(-lb) + jax.nn.log_sigmoid(f32))
    k = (1.0 - lb) * jax.nn.sigmoid(-f32)
    v = i.astype(jnp.float32)
    nch = S // C

    def to_chunks(a, d):
        return a.reshape(B, nch, C, H, d).transpose(1, 0, 3, 2, 4)

    qc_all, kc_all, vc_all = to_chunks(q, dk), to_chunks(k, dk), to_chunks(v, dv)
    b_all = jnp.cumsum(to_chunks(log_f, dk), axis=3)
    causal = jnp.tril(jnp.ones((C, C), dtype=bool))

    def step(state, args):
        qc, kc, vc, bc = args
        bl = bc[:, :, -1:, :]
        inter = jnp.einsum('bhtk,bhkv->bhtv', qc * jnp.exp(bc), state)
        decay = jnp.exp(jnp.where(causal[:, :, None],
                                  bc[:, :, :, None, :] - bc[:, :, None, :, :], -jnp.inf))
        a = jnp.einsum('bhtk,bhtsk,bhsk->bhts', qc, decay, kc)
        o = inter + jnp.einsum('bhts,bhsv->bhtv', a, vc)
        new_state = (jnp.exp(bl[:, :, 0, :, None]) * state
                     + jnp.einsum('bhsk,bhsv->bhkv', kc * jnp.exp(bl - bc), vc))
        return new_state, o

    s0 = jnp.zeros((B, H, dk, dv), jnp.float32)
    _, o = lax.scan(step, s0, (qc_all, kc_all, vc_all, b_all))
    o = o.transpose(1, 0, 3, 2, 4).reshape(B, S, H, dv)
    o = o * lax.rsqrt(jnp.mean(o * o, axis=-1, keepdims=True) + RMS_EPS)
    o = o.reshape(B, S, D) * onorm_g.astype(jnp.float32)
    o = o * jax.nn.silu(g.astype(jnp.float32))
    return o.astype(dt) @ w_out


def setup_inputs(seed: int = 0) -> dict:
    key = jax.random.key(seed)
    ks = jax.random.split(key, 16)
    n_a = (DEPTH + 1) // 2
    n_b = DEPTH // 2
    H, G, Dh = NSA_HEADS, NSA_KV_GROUPS, NSA_HEAD_DIM
    nsa_in = H * Dh + 6 * G * Dh + 3 * H
    nrm = lambda k, shape, fan: jax.random.normal(k, shape, jnp.float32) * (fan ** -0.5)
    return {
        "x": jax.random.normal(ks[0], (BATCH, SEQ, D_MODEL), jnp.float32),
        "norm_g": 1.0 + 0.02 * jax.random.normal(ks[1], (DEPTH, 4, D_MODEL), jnp.float32),
        "rel_table": 0.5 * jax.random.normal(ks[2], (REL_BUCKETS, NSA_HEADS), jnp.float32),
        "nsa_w_in": nrm(ks[3], (n_a, D_MODEL, nsa_in), D_MODEL),
        "nsa_cmp_pe": 0.1 * jax.random.normal(ks[4], (n_a, 2, CMP_BLOCK, Dh), jnp.float32),
        "nsa_cmp_w1": nrm(ks[5], (n_a, 2, CMP_BLOCK * Dh, Dh), CMP_BLOCK * Dh),
        "nsa_cmp_w2": nrm(ks[6], (n_a, 2, Dh, Dh), Dh),
        "nsa_w_out": nrm(ks[7], (n_a, H * Dh, D_MODEL), H * Dh),
        "hgrn_w_in": nrm(ks[8], (n_b, D_MODEL, 4 * D_MODEL), D_MODEL),
        "hgrn_lb": 0.1 * jax.random.normal(ks[9], (DEPTH, D_MODEL), jnp.float32),
        "hgrn_onorm": 1.0 + 0.02 * jax.random.normal(ks[10], (n_b, D_MODEL), jnp.float32),
        "hgrn_w_out": nrm(ks[11], (n_b, D_MODEL, D_MODEL), D_MODEL),
        "mlp_w_up": nrm(ks[12], (DEPTH, D_MODEL, D_FF), D_MODEL),
        "mlp_w_down": nrm(ks[13], (DEPTH, D_FF, D_MODEL), D_FF),
    }


def reference(x, norm_g, rel_table, nsa_w_in, nsa_cmp_pe, nsa_cmp_w1, nsa_cmp_w2, nsa_w_out,
              hgrn_w_in, hgrn_lb, hgrn_onorm, hgrn_w_out, mlp_w_up, mlp_w_down):
    lbs = jnp.cumsum(jax.nn.softmax(hgrn_lb.astype(jnp.float32), axis=0), axis=0)
    lbs = lbs - lbs[0:1]
    for layer in range(DEPTH):
        j = layer // N_MIXERS
        hn = rmsnorm(x, norm_g[layer, 0])
        if layer % N_MIXERS == 0:
            y = nsa_mixer(hn, nsa_w_in[j], nsa_cmp_pe[j], nsa_cmp_w1[j], nsa_cmp_w2[j],
                          nsa_w_out[j], rel_table)
        else:
            y = hgrn2_mixer(hn, hgrn_w_in[j], lbs[layer], hgrn_onorm[j], hgrn_w_out[j])
        x = x + rmsnorm(y, norm_g[layer, 1])
        hn = rmsnorm(x, norm_g[layer, 2])
        y = jnp.square(jax.nn.relu(hn @ mlp_w_up[layer])) @ mlp_w_down[layer]
        x = x + rmsnorm(y, norm_g[layer, 3])
    return x
```

```python
import functools
import math

import numpy as np
import jax
import jax.numpy as jnp
from jax import lax
from jax.experimental import pallas as pl
from jax.experimental.pallas import tpu as pltpu

F32 = jnp.float32
BF16 = jnp.bfloat16

N_HEADS = 16
N_GROUPS = 4
HEADS_PER_GROUP = N_HEADS // N_GROUPS
HEAD_DIM = 128
CMP_BLOCK = 32
CMP_STRIDE = 16
SEL_BLOCK = 64
SEL_TOP_N = 8
WINDOW = 512
FORCE_SCORE = 1.0e4
REL_BUCKETS = 32
REL_MAX_DIST = 128
RMS_EPS = 1e-6
NEG_INF = -1.0e30

LANES = 128
VMEM_BYTES_V7X = 64 * 1024 * 1024
VMEM_LIMIT_CAP = VMEM_BYTES_V7X - 8 * 1024 * 1024

ATT_TILE = 256
HGRN_CHUNK = 128
HGRN_SUB = 16
NORM_ROWS = 256


def _vmem_limit(nbytes):
    return int(min(VMEM_LIMIT_CAP, max(32 * 1024 * 1024, nbytes)))


def _mm_body(*refs, norm, epi, nk, tm):
    it = iter(refs)
    x_ref = next(it)
    g_ref = next(it) if norm else None
    w_ref = next(it)
    cs_ref = next(it) if epi == "colscale" else None
    res_ref = next(it) if epi == "resnorm" else None
    go_ref = next(it) if epi == "resnorm" else None
    o_ref = next(it)
    hn_ref = next(it) if norm else None
    acc_ref = next(it) if nk > 1 else None

    j = pl.program_id(1)
    k = pl.program_id(2)

    if norm:
        @pl.when(j == 0)
        def _():
            g = g_ref[...]

            def step(r, c):
                rows = pl.ds(pl.multiple_of(r * NORM_ROWS, NORM_ROWS), NORM_ROWS)
                xs = x_ref[rows, :]
                ms = jnp.mean(xs * xs, axis=-1, keepdims=True)
                hn_ref[rows, :] = (xs * lax.rsqrt(ms + RMS_EPS) * g).astype(BF16)
                return c

            lax.fori_loop(0, tm // NORM_ROWS, step, 0)

        lhs = hn_ref[...]
    else:
        lhs = x_ref[...]

    part = jnp.dot(lhs, w_ref[...], preferred_element_type=F32)

    def finish(acc):
        if epi == "colscale":
            acc = acc * cs_ref[...]
        elif epi == "relu2":
            acc = jnp.square(jnp.maximum(acc, 0.0))
        elif epi == "resnorm":
            ms = jnp.mean(acc * acc, axis=-1, keepdims=True)
            acc = res_ref[...] + acc * lax.rsqrt(ms + RMS_EPS) * go_ref[...]
        o_ref[...] = acc.astype(o_ref.dtype)

    if nk == 1:
        finish(part)
    else:
        @pl.when(k == 0)
        def _():
            acc_ref[...] = part

        @pl.when(k > 0)
        def _():
            acc_ref[...] += part

        @pl.when(k == nk - 1)
        def _():
            finish(acc_ref[...])


def _matmul(x, w, *, tm, tn, tk=None, norm_g=None, epi="none", colscale=None,
            res=None, res_g=None, out_dtype=BF16, name="mm"):
    M, K = x.shape
    N = w.shape[1]
    norm = norm_g is not None
    tk = K if tk is None else tk
    nk = K // tk
    assert M % tm == 0 and N % tn == 0 and K % tk == 0
    assert not (norm and nk != 1)
    assert not (epi == "resnorm" and tn != N)

    in_specs = [pl.BlockSpec((tm, tk), lambda i, j, k: (i, k))]
    args = [x]
    if norm:
        in_specs.append(pl.BlockSpec((1, K), lambda i, j, k: (0, 0)))
        args.append(norm_g.reshape(1, K).astype(F32))
    in_specs.append(pl.BlockSpec((tk, tn), lambda i, j, k: (k, j)))
    args.append(w)
    if epi == "colscale":
        in_specs.append(pl.BlockSpec((1, tn), lambda i, j, k: (0, j)))
        args.append(colscale)
    if epi == "resnorm":
        in_specs.append(pl.BlockSpec((tm, tn), lambda i, j, k: (i, j)))
        args.append(res)
        in_specs.append(pl.BlockSpec((1, tn), lambda i, j, k: (0, j)))
        args.append(res_g.reshape(1, N).astype(F32))

    scratch = []
    if norm:
        scratch.append(pltpu.VMEM((tm, K), BF16))
    if nk > 1:
        scratch.append(pltpu.VMEM((tm, tn), F32))

    xb = x.dtype.itemsize
    ob = jnp.dtype(out_dtype).itemsize
    est = (2 * tm * tk * xb + 2 * tk * tn * 2 + 2 * tm * tn * ob
           + (tm * K * 2 if norm else 0) + (tm * tn * 4 if nk > 1 else 0)
           + (2 * tm * tn * 4 if epi == "resnorm" else 0)
           + 3 * tm * tn * 4)

    return pl.pallas_call(
        functools.partial(_mm_body, norm=norm, epi=epi, nk=nk, tm=tm),
        out_shape=jax.ShapeDtypeStruct((M, N), out_dtype),
        grid=(M // tm, N // tn, nk),
        in_specs=in_specs,
        out_specs=pl.BlockSpec((tm, tn), lambda i, j, k: (i, j)),
        scratch_shapes=scratch,
        compiler_params=pltpu.CompilerParams(
            dimension_semantics=("arbitrary", "arbitrary", "arbitrary"),
            vmem_limit_bytes=_vmem_limit(est)),
        name=name,
    )(*args)


def _rel_bucket_np(dist):
    n = np.maximum(dist, 0)
    max_exact = REL_BUCKETS // 2
    nf = np.maximum(n, 1).astype(np.float32)
    ratio = np.log(nf / np.float32(max_exact)) / np.float32(math.log(REL_MAX_DIST / max_exact))
    large = max_exact + (ratio * np.float32(REL_BUCKETS - max_exact)).astype(np.int32)
    large = np.minimum(large, REL_BUCKETS - 1)
    return np.where(n < max_exact, n, large).astype(np.int32)


@functools.lru_cache(maxsize=None)
def _static_maps(seq):
    n_cmp = LANES
    pos = np.arange(seq, dtype=np.int32)[:, None]
    c_end = np.arange(n_cmp, dtype=np.int32)[None, :] * CMP_STRIDE + CMP_BLOCK - 1
    bucket_c = _rel_bucket_np(pos - c_end)
    t = np.arange(ATT_TILE, dtype=np.int32)[:, None]
    k = np.arange(ATT_TILE, dtype=np.int32)[None, :]
    bucket_t = np.stack([_rel_bucket_np(t - k), _rel_bucket_np(ATT_TILE + t - k)])
    assert _rel_bucket_np(np.array([ATT_TILE + 1]))[0] == REL_BUCKETS - 1
    nc = seq // CMP_STRIDE - CMP_BLOCK // CMP_STRIDE + 1
    nb = seq // SEL_BLOCK
    c_start = np.arange(nc)[:, None] * CMP_STRIDE
    b_start = np.arange(nb)[None, :] * SEL_BLOCK
    ov = ((c_start <= b_start + SEL_BLOCK - 1) & (c_start + CMP_BLOCK - 1 >= b_start))
    overlap = np.zeros((LANES, LANES), np.float32)
    overlap[:nc, :nb] = ov
    return bucket_c, bucket_t, overlap


def _bias_body(tab_ref, bc_ref, bt_ref, oc_ref, ot_ref, *, seq):
    h = pl.program_id(0)

    def lookup(bmap):
        acc = jnp.zeros(bmap.shape, F32)
        for b in range(REL_BUCKETS):
            acc = jnp.where(bmap == b, tab_ref[b, h], acc)
        return acc

    def step(r, c):
        rows = pl.ds(pl.multiple_of(r * ATT_TILE, ATT_TILE), ATT_TILE)
        oc_ref[0, rows, :] = lookup(bc_ref[rows, :])
        return c

    lax.fori_loop(0, seq // ATT_TILE, step, 0)

    tt = lax.broadcasted_iota(jnp.int32, (ATT_TILE, ATT_TILE), 0)
    kk = lax.broadcasted_iota(jnp.int32, (ATT_TILE, ATT_TILE), 1)
    far = jnp.full((ATT_TILE, ATT_TILE), tab_ref[REL_BUCKETS - 1, h], F32)
    ot_ref[0, 0] = far
    ot_ref[1, 0] = lookup(bt_ref[1])
    ot_ref[2, 0] = jnp.where(kk <= tt, lookup(bt_ref[0]), NEG_INF)
    ot_ref[3, 0] = jnp.where(kk > tt, far, NEG_INF)


def _bias_tables(rel_table, seq):
    bucket_c, bucket_t, _ = _static_maps(seq)
    return pl.pallas_call(
        functools.partial(_bias_body, seq=seq),
        out_shape=(jax.ShapeDtypeStruct((N_HEADS, seq, LANES), F32),
                   jax.ShapeDtypeStruct((4, N_HEADS, ATT_TILE, ATT_TILE), F32)),
        grid=(N_HEADS,),
        in_specs=[pl.BlockSpec(memory_space=pltpu.SMEM),
                  pl.BlockSpec((seq, LANES), lambda h: (0, 0)),
                  pl.BlockSpec((2, ATT_TILE, ATT_TILE), lambda h: (0, 0, 0))],
        out_specs=(pl.BlockSpec((1, seq, LANES), lambda h: (h, 0, 0)),
                   pl.BlockSpec((4, 1, ATT_TILE, ATT_TILE), lambda h: (0, h, 0, 0))),
        compiler_params=pltpu.CompilerParams(dimension_semantics=("arbitrary",)),
        name="rel_bias",
    )(rel_table.astype(F32), jnp.asarray(bucket_c), jnp.asarray(bucket_t))


def _compress_body(xk_ref, xv_ref, w1_ref, pe_ref, w2_ref, ok_ref, ov_ref):
    def one(x_ref, idx, o_ref):
        x = x_ref[0, 0].astype(F32)
        a0 = jnp.dot((x + pe_ref[idx, 0]).astype(BF16), w1_ref[idx, 0],
                     preferred_element_type=F32)
        a1 = jnp.dot((x + pe_ref[idx, 1]).astype(BF16), w1_ref[idx, 1],
                     preferred_element_type=F32)
        pre = a0 + pltpu.roll(a1, LANES - 1, 0)
        hid = jax.nn.gelu(pre).astype(BF16)
        o_ref[0, 0] = jnp.dot(hid, w2_ref[idx], preferred_element_type=F32).astype(BF16)

    one(xk_ref, 0, ok_ref)
    one(xv_ref, 1, ov_ref)


def _compress(xk, xv, w1, pe, w2):
    B, G = xk.shape[:2]
    half = CMP_STRIDE * HEAD_DIM
    spec_x = pl.BlockSpec((1, 1, LANES, half), lambda b, g: (b, g, 0, 0))
    spec_o = pl.BlockSpec((1, 1, LANES, HEAD_DIM), lambda b, g: (b, g, 0, 0))
    out = jax.ShapeDtypeStruct((B, G, LANES, HEAD_DIM), BF16)
    return pl.pallas_call(
        _compress_body,
        out_shape=(out, out),
        grid=(B, G),
        in_specs=[spec_x, spec_x,
                  pl.BlockSpec((2, 2, half, HEAD_DIM), lambda b, g: (0, 0, 0, 0)),
                  pl.BlockSpec((2, 2, 1, half), lambda b, g: (0, 0, 0, 0)),
                  pl.BlockSpec((2, HEAD_DIM, HEAD_DIM), lambda b, g: (0, 0, 0))],
        out_specs=(spec_o, spec_o),
        compiler_params=pltpu.CompilerParams(dimension_semantics=("arbitrary", "arbitrary")),
        name="nsa_compress",
    )(xk, xv, w1, pe, w2)


def _nsa_body(q_ref, ks_ref, vs_ref, kw_ref, vw_ref, kc_ref, vc_ref, bc_ref, bt_ref,
              ovl_ref, gl_ref, o_ref, m_sc, l_sc, acc_sc):
    R = HEADS_PER_GROUP
    tq = ATT_TILE
    qi = pl.program_id(2)
    nt = (((1,), (1,)), ((), ()))

    q = q_ref[0]
    q4 = jnp.concatenate([q[:, r * HEAD_DIM:(r + 1) * HEAD_DIM] for r in range(R)], axis=0)

    pos3 = qi * tq + lax.broadcasted_iota(jnp.int32, (1, tq, 1), 1)
    pos2 = qi * tq + lax.broadcasted_iota(jnp.int32, (tq, 1), 0)

    sc = lax.dot_general(q4, kc_ref[0, 0], nt, preferred_element_type=F32)
    sc = sc.reshape(R, tq, LANES) + bc_ref[...]
    cidx = lax.broadcasted_iota(jnp.int32, (1, 1, LANES), 2)
    valid = (cidx * CMP_STRIDE + (CMP_BLOCK - 1) <= pos3) & (cidx < LANES - 1)
    sc = jnp.where(valid, sc, NEG_INF)
    mc = jnp.max(sc, axis=-1, keepdims=True)
    ec = jnp.exp(sc - mc)
    pc = ec / jnp.sum(ec, axis=-1, keepdims=True)
    pc = jnp.where(pos3 >= CMP_BLOCK - 1, pc, 0.0)
    o_cmp = jnp.dot(pc.reshape(R * tq, LANES).astype(BF16), vc_ref[0, 0],
                    preferred_element_type=F32)

    psum = pc[0]
    for r in range(1, R):
        psum = psum + pc[r]
    p_hi = psum.astype(BF16)
    p_lo = (psum - p_hi.astype(F32)).astype(BF16)
    ovl = ovl_ref[...]
    imp = (jnp.dot(p_hi, ovl, preferred_element_type=F32)
           + jnp.dot(p_lo, ovl, preferred_element_type=F32))
    jb = lax.broadcasted_iota(jnp.int32, (1, LANES), 1)
    q_blk = lax.shift_right_logical(pos2, int(math.log2(SEL_BLOCK)))
    forced = (jb == 0) | (jb == q_blk) | (jb == q_blk - 1)
    future = jb > q_blk
    nb = ks_ref.shape[1] // SEL_BLOCK
    imp = jnp.where(forced, FORCE_SCORE, jnp.where(future, -1.0, imp))
    imp = jnp.where(jb < nb, imp, -3.0)
    cnt = jnp.zeros((tq, LANES), F32)
    for i in range(nb):
        col = imp[:, i:i + 1]
        beats = (col > imp) | ((col == imp) & (jb > i))
        cnt = cnt + jnp.where(beats, 1.0, 0.0)
    sel = jnp.where((cnt < float(min(SEL_TOP_N, nb))) & (jb < nb), 1.0, 0.0).astype(BF16)

    blk_i = lax.broadcasted_iota(jnp.int32, (LANES, tq), 0)
    key_blk = lax.shift_right_logical(lax.broadcasted_iota(jnp.int32, (LANES, tq), 1),
                                      int(math.log2(SEL_BLOCK)))
    blocks_per_tile = tq // SEL_BLOCK

    def flash(k_ref, v_ref, j_lo, j_hi, kind_of, use_sel):
        m_sc[...] = jnp.full(m_sc.shape, NEG_INF, F32)
        l_sc[...] = jnp.zeros(l_sc.shape, F32)
        acc_sc[...] = jnp.zeros(acc_sc.shape, F32)

        def body(j, c):
            rows = pl.ds(pl.multiple_of(j * tq, tq), tq)
            kt = k_ref[0, rows, :]
            vt = v_ref[0, rows, :]
            s = lax.dot_general(q4, kt, nt, preferred_element_type=F32)
            s = s.reshape(R, tq, tq) + bt_ref[kind_of(j)]
            if use_sel:
                expand = jnp.where(blk_i == key_blk + j * blocks_per_tile, 1.0, 0.0).astype(BF16)
                msk = jnp.dot(sel, expand, preferred_element_type=F32) > 0.5
                s = jnp.where(msk[None], s, NEG_INF)
            m_old = m_sc[...]
            m_new = jnp.maximum(m_old, jnp.max(s, axis=-1, keepdims=True))
            alpha = jnp.exp(m_old - m_new)
            p = jnp.exp(s - m_new)
            l_sc[...] = alpha * l_sc[...] + jnp.sum(p, axis=-1, keepdims=True)
            pv = jnp.dot(p.reshape(R * tq, tq).astype(BF16), vt, preferred_element_type=F32)
            acc_sc[...] = acc_sc[...] * alpha + pv.reshape(R, tq, HEAD_DIM)
            m_sc[...] = m_new
            return c

        lax.fori_loop(j_lo, j_hi, body, 0)
        return acc_sc[...] / l_sc[...]

    o_sel = flash(ks_ref, vs_ref, 0, qi + 1,
                  lambda j: jnp.clip(j - qi + 2, 0, 2), True)
    n_win = WINDOW // tq
    o_win = flash(kw_ref, vw_ref, jnp.maximum(qi - n_win, 0), qi + 1,
                  lambda j: jnp.where(j == qi - n_win, 3, j - qi + 2), False)

    gates = jax.nn.sigmoid(gl_ref[0, 0])
    outs = []
    for r in range(R):
        o_r = (gates[:, r:r + 1] * o_cmp[r * tq:(r + 1) * tq]
               + gates[:, R + r:R + r + 1] * o_sel[r]
               + gates[:, 2 * R + r:2 * R + r + 1] * o_win[r])
        outs.append(o_r)
    o_ref[0] = jnp.concatenate(outs, axis=1).astype(o_ref.dtype)


def _nsa_attention(proj, kcmp, vcmp, bias_c, bias_t, gates_t, B, S):
    assert WINDOW % ATT_TILE == 0 and S % ATT_TILE == 0
    R, G, tq = HEADS_PER_GROUP, N_GROUPS, ATT_TILE
    _, _, overlap = _static_maps(S)
    q_cols = N_HEADS

    def kv_spec(slot):
        return pl.BlockSpec((1, S, HEAD_DIM),
                            lambda b, g, i, slot=slot: (b, 0, q_cols + slot * G + g))

    cmp_spec = pl.BlockSpec((1, 1, LANES, HEAD_DIM), lambda b, g, i: (b, g, 0, 0))
    in_specs = [
        pl.BlockSpec((1, tq, R * HEAD_DIM), lambda b, g, i: (b, i, g)),
        kv_spec(2), kv_spec(3), kv_spec(4), kv_spec(5),
        cmp_spec, cmp_spec,
        pl.BlockSpec((R, tq, LANES), lambda b, g, i: (g, i, 0)),
        pl.BlockSpec((4, R, tq, tq), lambda b, g, i: (0, g, 0, 0)),
        pl.BlockSpec((LANES, LANES), lambda b, g, i: (0, 0)),
        pl.BlockSpec((1, 1, tq, 3 * R), lambda b, g, i: (b, g, i, 0)),
    ]
    est = (2 * 4 * S * HEAD_DIM * 2 + 2 * 4 * R * tq * tq * 4 + 2 * R * tq * LANES * 4
           + 3 * R * tq * LANES * 4 + 8 * R * tq * tq * 4 + 4 * tq * R * HEAD_DIM * 2)
    return pl.pallas_call(
        _nsa_body,
        out_shape=jax.ShapeDtypeStruct((B, S, N_HEADS * HEAD_DIM), BF16),
        grid=(B, G, S // tq),
        in_specs=in_specs,
        out_specs=pl.BlockSpec((1, tq, R * HEAD_DIM), lambda b, g, i: (b, i, g)),
        scratch_shapes=[pltpu.VMEM((R, tq, 1), F32), pltpu.VMEM((R, tq, 1), F32),
                        pltpu.VMEM((R, tq, HEAD_DIM), F32)],
        compiler_params=pltpu.CompilerParams(
            dimension_semantics=("arbitrary", "arbitrary", "arbitrary"),
            vmem_limit_bytes=_vmem_limit(est)),
        name="nsa_attention",
    )(proj, proj, proj, proj, proj, kcmp, vcmp, bias_c, bias_t,
      jnp.asarray(overlap, BF16), gates_t)


def _hgrn_body(q_ref, f_ref, i_ref, g_ref, lb_ref, gn_ref, o_ref, st_ref, *, layer):
    C = HGRN_CHUNK
    n_chunks = q_ref.shape[1] // C
    nt = (((1,), (1,)), ((), ()))

    lbp = lb_ref[...]
    e = jnp.exp(lbp - jnp.max(lbp, axis=0, keepdims=True))
    sm = e / jnp.sum(e, axis=0, keepdims=True)
    cum = sm[0:1]
    first = cum
    for d in range(1, layer + 1):
        cum = cum + sm[d:d + 1]
    lb = cum - first
    log_lb = jnp.log(lb)
    log_1m = jnp.log1p(-lb)
    gn = gn_ref[...]

    rows = lax.broadcasted_iota(jnp.int32, (C, 1), 0)
    cols = lax.broadcasted_iota(jnp.int32, (1, C), 1)
    rows_s = lax.broadcasted_iota(jnp.int32, (HGRN_SUB, 1), 0)

    st_ref[...] = jnp.zeros(st_ref.shape, F32)

    def chunk(c, carry):
        sl = pl.ds(pl.multiple_of(c * C, C), C)
        qr = q_ref[0, sl, :]
        fr = f_ref[0, sl, :]
        v = i_ref[0, sl, :]
        gr = g_ref[0, sl, :]
        q = jax.nn.silu(qr)
        log_f = jnp.logaddexp(log_lb, log_1m + jax.nn.log_sigmoid(fr))
        k = (1.0 - lb) * jax.nn.sigmoid(-fr)

        b = log_f
        sh = 1
        while sh < C:
            b = b + jnp.where(rows >= sh, pltpu.roll(b, sh, 0), 0.0)
            sh *= 2

        vb = v.astype(BF16)
        st = st_ref[...]
        o = lax.dot_general((q * jnp.exp(b)).astype(BF16), st.astype(BF16), nt,
                            preferred_element_type=F32)

        a = jnp.zeros((C, C), F32)
        half = C // 2
        while half >= HGRN_SUB:
            grp = 2 * half
            anc = jnp.concatenate(
                [jnp.broadcast_to(b[g0 + half - 1:g0 + half], (grp, b.shape[1]))
                 for g0 in range(0, C, grp)], axis=0)
            ql = (q * jnp.exp(jnp.minimum(b - anc, 0.0))).astype(BF16)
            kl = (k * jnp.exp(jnp.minimum(anc - b, 0.0))).astype(BF16)
            al = lax.dot_general(ql, kl, nt, preferred_element_type=F32)
            msk = (((rows & (grp - 1)) >= half) & ((cols & (grp - 1)) < half)
                   & ((rows & ~(grp - 1)) == (cols & ~(grp - 1))))
            a = a + jnp.where(msk, al, 0.0)
            half //= 2

        pieces = []
        for blk in range(C // HGRN_SUB):
            r0 = blk * HGRN_SUB
            bt = b[r0:r0 + HGRN_SUB]
            qt = q[r0:r0 + HGRN_SUB]
            arow = jnp.zeros((HGRN_SUB, C), F32)
            for s in range(HGRN_SUB):
                bs = b[r0 + s:r0 + s + 1]
                ks = k[r0 + s:r0 + s + 1]
                w = jnp.exp(jnp.minimum(bt - bs, 0.0)) * qt * ks
                col = jnp.sum(w, axis=-1, keepdims=True)
                arow = jnp.where((cols == r0 + s) & (rows_s >= s), col, arow)
            pieces.append(arow)
        a = a + jnp.concatenate(pieces, axis=0)

        o = o + jnp.dot(a.astype(BF16), vb, preferred_element_type=F32)

        b_last = b[C - 1:C]
        kh = (k * jnp.exp(b_last - b)).astype(BF16)
        st_ref[...] = st * jnp.exp(b_last) + jnp.dot(v.T.astype(BF16), kh,
                                                     preferred_element_type=F32)

        ms = jnp.mean(o * o, axis=-1, keepdims=True)
        o = o * lax.rsqrt(ms + RMS_EPS) * gn * jax.nn.silu(gr)
        o_ref[0, sl, :] = o.astype(o_ref.dtype)
        return carry

    lax.fori_loop(0, n_chunks, chunk, 0)


def _hgrn(proj, hgrn_lb, onorm, layer, B, S):
    H = N_HEADS
    depth = hgrn_lb.shape[0]

    def spec(part):
        return pl.BlockSpec((1, S, HEAD_DIM), lambda b, h, part=part: (b, 0, part * H + h))

    est = 2 * 4 * S * HEAD_DIM * 4 + 2 * S * HEAD_DIM * 2 + 64 * HGRN_CHUNK * HGRN_CHUNK * 4
    return pl.pallas_call(
        functools.partial(_hgrn_body, layer=layer),
        out_shape=jax.ShapeDtypeStruct((B, S, H * HEAD_DIM), BF16),
        grid=(B, H),
        in_specs=[spec(0), spec(1), spec(2), spec(3),
                  pl.BlockSpec((depth, HEAD_DIM), lambda b, h: (0, h)),
                  pl.BlockSpec((1, HEAD_DIM), lambda b, h: (0, h))],
        out_specs=pl.BlockSpec((1, S, HEAD_DIM), lambda b, h: (b, 0, h)),
        scratch_shapes=[pltpu.VMEM((HEAD_DIM, HEAD_DIM), F32)],
        compiler_params=pltpu.CompilerParams(
            dimension_semantics=("arbitrary", "arbitrary"),
            vmem_limit_bytes=_vmem_limit(est)),
        name="hgrn2_recurrence",
    )(proj, proj, proj, proj, hgrn_lb.astype(F32), onorm.reshape(1, -1).astype(F32))


def _mlp(xf, g_in, g_out, w_up, w_down):
    hid = _matmul(xf, w_up.astype(BF16), tm=1024, tn=1024, norm_g=g_in, epi="relu2",
                  out_dtype=BF16, name="mlp_up")
    return _matmul(hid, w_down.astype(BF16), tm=512, tn=w_down.shape[1], tk=1024,
                   epi="resnorm", res=xf, res_g=g_out, out_dtype=F32, name="mlp_down")


def _nsa_layer(xf, B, S, g_in, g_out, rel_table, w_in, cmp_pe, cmp_w1, cmp_w2, w_out):
    D = xf.shape[1]
    G, R, Dh = N_GROUPS, HEADS_PER_GROUP, HEAD_DIM
    n_main = N_HEADS * Dh + 6 * G * Dh
    n_gate = 3 * N_HEADS
    w_main = w_in[:, :n_main].astype(BF16)
    w_gate = jnp.pad(w_in[:, n_main:], ((0, 0), (0, LANES - n_gate))).astype(BF16)
    colscale = jnp.concatenate([jnp.full((N_HEADS * Dh,), Dh ** -0.5, F32),
                                jnp.ones((6 * G * Dh,), F32)])[None]

    proj = _matmul(xf, w_main, tm=1024, tn=1024, norm_g=g_in, epi="colscale",
                   colscale=colscale, out_dtype=BF16, name="nsa_proj")
    glog = _matmul(xf, w_gate, tm=1024, tn=LANES, norm_g=g_in, out_dtype=F32,
                   name="nsa_gate_proj")
    gates_t = (glog[:, :n_gate].reshape(B, S, 3, G, R).transpose(0, 3, 1, 2, 4)
               .reshape(B, G, S, 3 * R))

    proj3 = proj.reshape(B, S, n_main)
    nh = S // CMP_STRIDE

    def grouped(slot):
        c0 = N_HEADS * Dh + slot * G * Dh
        a = proj3[:, :, c0:c0 + G * Dh].reshape(B, nh, CMP_STRIDE, G, Dh)
        return a.transpose(0, 3, 1, 2, 4).reshape(B, G, nh, CMP_STRIDE * Dh)

    half = CMP_STRIDE * Dh
    kcmp, vcmp = _compress(grouped(0), grouped(1),
                           cmp_w1.reshape(2, 2, half, Dh).astype(BF16),
                           cmp_pe.reshape(2, 2, 1, half).astype(F32),
                           cmp_w2.astype(BF16))

    bias_c, bias_t = _bias_tables(rel_table, S)
    attn = _nsa_attention(proj3, kcmp, vcmp, bias_c, bias_t, gates_t, B, S)
    return _matmul(attn.reshape(B * S, D), w_out.astype(BF16), tm=512, tn=D,
                   epi="resnorm", res=xf, res_g=g_out, out_dtype=F32, name="nsa_out")


def _hgrn_layer(xf, B, S, layer, g_in, g_out, w_in, hgrn_lb, onorm, w_out):
    D = xf.shape[1]
    proj = _matmul(xf, w_in.astype(BF16), tm=1024, tn=1024, norm_g=g_in,
                   out_dtype=F32, name="hgrn_proj")
    mixed = _hgrn(proj.reshape(B, S, 4 * D), hgrn_lb, onorm, layer, B, S)
    return _matmul(mixed.reshape(B * S, D), w_out.astype(BF16), tm=512, tn=D,
                   epi="resnorm", res=xf, res_g=g_out, out_dtype=F32, name="hgrn_out")


def kernel(x, norm_g, rel_table, nsa_w_in, nsa_cmp_pe, nsa_cmp_w1, nsa_cmp_w2, nsa_w_out,
           hgrn_w_in, hgrn_lb, hgrn_onorm, hgrn_w_out, mlp_w_up, mlp_w_down):
    B, S, D = x.shape
    depth = norm_g.shape[0]
    assert D == N_HEADS * HEAD_DIM and S % ATT_TILE == 0 and S % HGRN_CHUNK == 0
    xf = x.reshape(B * S, D).astype(F32)
    for layer in range(depth):
        j = layer // 2
        if layer % 2 == 0:
            xf = _nsa_layer(xf, B, S, norm_g[layer, 0], norm_g[layer, 1], rel_table,
                            nsa_w_in[j], nsa_cmp_pe[j], nsa_cmp_w1[j], nsa_cmp_w2[j],
                            nsa_w_out[j])
        else:
            xf = _hgrn_layer(xf, B, S, layer, norm_g[layer, 0], norm_g[layer, 1],
                             hgrn_w_in[j], hgrn_lb, hgrn_onorm[j], hgrn_w_out[j])
        xf = _mlp(xf, norm_g[layer, 2], norm_g[layer, 3], mlp_w_up[layer], mlp_w_down[layer])
    return xf.reshape(B, S, D).astype(x.dtype)
```

```python
import functools
import math

import numpy as np
import jax
import jax.numpy as jnp
from jax import lax
from jax.experimental import pallas as pl
from jax.experimental.pallas import tpu as pltpu

F32 = jnp.float32
BF16 = jnp.bfloat16

N_HEADS = 16
N_GROUPS = 4
HEADS_PER_GROUP = N_HEADS // N_GROUPS
HEAD_DIM = 128
CMP_BLOCK = 32
CMP_STRIDE = 16
SEL_BLOCK = 64
SEL_TOP_N = 8
WINDOW = 512
FORCE_SCORE = 1.0e4
REL_BUCKETS = 32
REL_MAX_DIST = 128
RMS_EPS = 1e-6
NEG_INF = -1.0e30

LANES = 128
VMEM_BYTES_V7X = 64 * 1024 * 1024
VMEM_LIMIT_CAP = VMEM_BYTES_V7X - 8 * 1024 * 1024

ATT_TILE = 256
ATT_ROWS = 128
HGRN_CHUNK = 128
HGRN_SUB = 16
NORM_ROWS = 256


def _vmem_limit(nbytes):
    return int(min(VMEM_LIMIT_CAP, max(32 * 1024 * 1024, nbytes)))


def _mm_body(*refs, norm, epi, nk, tm):
    it = iter(refs)
    x_ref = next(it)
    g_ref = next(it) if norm else None
    w_ref = next(it)
    cs_ref = next(it) if epi == "colscale" else None
    res_ref = next(it) if epi == "resnorm" else None
    go_ref = next(it) if epi == "resnorm" else None
    o_ref = next(it)
    hn_ref = next(it) if norm else None
    acc_ref = next(it) if nk > 1 else None

    j = pl.program_id(1)
    k = pl.program_id(2)

    if norm:
        @pl.when(j == 0)
        def _():
            g = g_ref[...]

            def step(r, c):
                rows = pl.ds(pl.multiple_of(r * NORM_ROWS, NORM_ROWS), NORM_ROWS)
                xs = x_ref[rows, :]
                ms = jnp.mean(xs * xs, axis=-1, keepdims=True)
                hn_ref[rows, :] = (xs * lax.rsqrt(ms + RMS_EPS) * g).astype(BF16)
                return c

            lax.fori_loop(0, tm // NORM_ROWS, step, 0)

        lhs = hn_ref[...]
    else:
        lhs = x_ref[...]

    part = jnp.dot(lhs, w_ref[...], preferred_element_type=F32)

    def finish(acc):
        if epi == "colscale":
            acc = acc * cs_ref[...]
        elif epi == "relu2":
            acc = jnp.square(jnp.maximum(acc, 0.0))
        elif epi == "resnorm":
            ms = jnp.mean(acc * acc, axis=-1, keepdims=True)
            acc = res_ref[...] + acc * lax.rsqrt(ms + RMS_EPS) * go_ref[...]
        o_ref[...] = acc.astype(o_ref.dtype)

    if nk == 1:
        finish(part)
    else:
        @pl.when(k == 0)
        def _():
            acc_ref[...] = part

        @pl.when(k > 0)
        def _():
            acc_ref[...] += part

        @pl.when(k == nk - 1)
        def _():
            finish(acc_ref[...])


def _matmul(x, w, *, tm, tn, tk=None, norm_g=None, epi="none", colscale=None,
            res=None, res_g=None, out_dtype=BF16, name="mm"):
    M, K = x.shape
    N = w.shape[1]
    norm = norm_g is not None
    tk = K if tk is None else tk
    nk = K // tk
    assert M % tm == 0 and N % tn == 0 and K % tk == 0
    assert not (norm and nk != 1)
    assert not (epi == "resnorm" and tn != N)

    in_specs = [pl.BlockSpec((tm, tk), lambda i, j, k: (i, k))]
    args = [x]
    if norm:
        in_specs.append(pl.BlockSpec((1, K), lambda i, j, k: (0, 0)))
        args.append(norm_g.reshape(1, K).astype(F32))
    in_specs.append(pl.BlockSpec((tk, tn), lambda i, j, k: (k, j)))
    args.append(w)
    if epi == "colscale":
        in_specs.append(pl.BlockSpec((1, tn), lambda i, j, k: (0, j)))
        args.append(colscale)
    if epi == "resnorm":
        in_specs.append(pl.BlockSpec((tm, tn), lambda i, j, k: (i, j)))
        args.append(res)
        in_specs.append(pl.BlockSpec((1, tn), lambda i, j, k: (0, j)))
        args.append(res_g.reshape(1, N).astype(F32))

    scratch = []
    if norm:
        scratch.append(pltpu.VMEM((tm, K), BF16))
    if nk > 1:
        scratch.append(pltpu.VMEM((tm, tn), F32))

    xb = x.dtype.itemsize
    ob = jnp.dtype(out_dtype).itemsize
    est = (2 * tm * tk * xb + 2 * tk * tn * 2 + 2 * tm * tn * ob
           + (tm * K * 2 if norm else 0) + (tm * tn * 4 if nk > 1 else 0)
           + (2 * tm * tn * 4 if epi == "resnorm" else 0)
           + 3 * tm * tn * 4)

    return pl.pallas_call(
        functools.partial(_mm_body, norm=norm, epi=epi, nk=nk, tm=tm),
        out_shape=jax.ShapeDtypeStruct((M, N), out_dtype),
        grid=(M // tm, N // tn, nk),
        in_specs=in_specs,
        out_specs=pl.BlockSpec((tm, tn), lambda i, j, k: (i, j)),
        scratch_shapes=scratch,
        compiler_params=pltpu.CompilerParams(
            dimension_semantics=("arbitrary", "arbitrary", "arbitrary"),
            vmem_limit_bytes=_vmem_limit(est)),
        name=name,
    )(*args)


def _rel_bucket_np(dist):
    n = np.maximum(dist, 0)
    max_exact = REL_BUCKETS // 2
    nf = np.maximum(n, 1).astype(np.float32)
    ratio = np.log(nf / np.float32(max_exact)) / np.float32(math.log(REL_MAX_DIST / max_exact))
    large = max_exact + (ratio * np.float32(REL_BUCKETS - max_exact)).astype(np.int32)
    large = np.minimum(large, REL_BUCKETS - 1)
    return np.where(n < max_exact, n, large).astype(np.int32)


@functools.lru_cache(maxsize=None)
def _static_maps(seq):
    n_cmp = LANES
    pos = np.arange(seq, dtype=np.int32)[:, None]
    c_end = np.arange(n_cmp, dtype=np.int32)[None, :] * CMP_STRIDE + CMP_BLOCK - 1
    bucket_c = _rel_bucket_np(pos - c_end)
    t = np.arange(ATT_TILE, dtype=np.int32)[:, None]
    k = np.arange(ATT_TILE, dtype=np.int32)[None, :]
    bucket_t = np.stack([_rel_bucket_np(t - k), _rel_bucket_np(ATT_TILE + t - k)])
    assert _rel_bucket_np(np.array([ATT_TILE + 1]))[0] == REL_BUCKETS - 1
    nc = seq // CMP_STRIDE - CMP_BLOCK // CMP_STRIDE + 1
    nb = seq // SEL_BLOCK
    c_start = np.arange(nc)[:, None] * CMP_STRIDE
    b_start = np.arange(nb)[None, :] * SEL_BLOCK
    ov = ((c_start <= b_start + SEL_BLOCK - 1) & (c_start + CMP_BLOCK - 1 >= b_start))
    overlap = np.zeros((LANES, LANES), np.float32)
    overlap[:nc, :nb] = ov
    return bucket_c, bucket_t, overlap


def _bias_body(tab_ref, bc_ref, bt_ref, oc_ref, ot_ref, *, seq):
    h = pl.program_id(0)

    def lookup(bmap):
        acc = jnp.zeros(bmap.shape, F32)
        for b in range(REL_BUCKETS):
            acc = jnp.where(bmap == b, tab_ref[b, h], acc)
        return acc

    def step(r, c):
        rows = pl.ds(pl.multiple_of(r * ATT_TILE, ATT_TILE), ATT_TILE)
        oc_ref[0, rows, :] = lookup(bc_ref[rows, :])
        return c

    lax.fori_loop(0, seq // ATT_TILE, step, 0)

    tt = lax.broadcasted_iota(jnp.int32, (ATT_TILE, ATT_TILE), 0)
    kk = lax.broadcasted_iota(jnp.int32, (ATT_TILE, ATT_TILE), 1)
    far = jnp.full((ATT_TILE, ATT_TILE), tab_ref[REL_BUCKETS - 1, h], F32)
    ot_ref[0, 0] = far
    ot_ref[1, 0] = lookup(bt_ref[1])
    ot_ref[2, 0] = jnp.where(kk <= tt, lookup(bt_ref[0]), NEG_INF)
    ot_ref[3, 0] = jnp.where(kk > tt, far, NEG_INF)


def _bias_tables(rel_table, seq):
    bucket_c, bucket_t, _ = _static_maps(seq)
    return pl.pallas_call(
        functools.partial(_bias_body, seq=seq),
        out_shape=(jax.ShapeDtypeStruct((N_HEADS, seq, LANES), F32),
                   jax.ShapeDtypeStruct((4, N_HEADS, ATT_TILE, ATT_TILE), F32)),
        grid=(N_HEADS,),
        in_specs=[pl.BlockSpec(memory_space=pltpu.SMEM),
                  pl.BlockSpec((seq, LANES), lambda h: (0, 0)),
                  pl.BlockSpec((2, ATT_TILE, ATT_TILE), lambda h: (0, 0, 0))],
        out_specs=(pl.BlockSpec((1, seq, LANES), lambda h: (h, 0, 0)),
                   pl.BlockSpec((4, 1, ATT_TILE, ATT_TILE), lambda h: (0, h, 0, 0))),
        compiler_params=pltpu.CompilerParams(dimension_semantics=("arbitrary",)),
        name="rel_bias",
    )(rel_table.astype(F32), jnp.asarray(bucket_c), jnp.asarray(bucket_t))


def _compress_body(xk_ref, xv_ref, w1_ref, pe_ref, w2_ref, ok_ref, ov_ref):
    def one(x_ref, idx, o_ref):
        x = x_ref[0, 0].astype(F32)
        a0 = jnp.dot((x + pe_ref[idx, 0]).astype(BF16), w1_ref[idx, 0],
                     preferred_element_type=F32)
        a1 = jnp.dot((x + pe_ref[idx, 1]).astype(BF16), w1_ref[idx, 1],
                     preferred_element_type=F32)
        pre = a0 + pltpu.roll(a1, LANES - 1, 0)
        hid = jax.nn.gelu(pre).astype(BF16)
        o_ref[0, 0] = jnp.dot(hid, w2_ref[idx], preferred_element_type=F32).astype(BF16)

    one(xk_ref, 0, ok_ref)
    one(xv_ref, 1, ov_ref)


def _compress(xk, xv, w1, pe, w2):
    B, G = xk.shape[:2]
    half = CMP_STRIDE * HEAD_DIM
    spec_x = pl.BlockSpec((1, 1, LANES, half), lambda b, g: (b, g, 0, 0))
    spec_o = pl.BlockSpec((1, 1, LANES, HEAD_DIM), lambda b, g: (b, g, 0, 0))
    out = jax.ShapeDtypeStruct((B, G, LANES, HEAD_DIM), BF16)
    return pl.pallas_call(
        _compress_body,
        out_shape=(out, out),
        grid=(B, G),
        in_specs=[spec_x, spec_x,
                  pl.BlockSpec((2, 2, half, HEAD_DIM), lambda b, g: (0, 0, 0, 0)),
                  pl.BlockSpec((2, 2, 1, half), lambda b, g: (0, 0, 0, 0)),
                  pl.BlockSpec((2, HEAD_DIM, HEAD_DIM), lambda b, g: (0, 0, 0))],
        out_specs=(spec_o, spec_o),
        compiler_params=pltpu.CompilerParams(dimension_semantics=("arbitrary", "arbitrary")),
        name="nsa_compress",
    )(xk, xv, w1, pe, w2)


def _nsa_body(q_ref, ks_ref, vs_ref, kw_ref, vw_ref, kc_ref, vc_ref, bc_ref, bt_ref,
              ovl_ref, exp_ref, gl_ref, o_ref, kts_sc, ktw_sc, q4_sc, m_sc, acc_sc):
    R = HEADS_PER_GROUP
    tq = ATT_TILE
    qi = pl.program_id(2)
    n_tiles = ks_ref.shape[1] // tq
    nb = ovl_ref.shape[0]
    nt = (((1,), (1,)), ((), ()))

    @pl.when(qi == 0)
    def _():
        def tr(j, c):
            rows = pl.ds(pl.multiple_of(j * tq, tq), tq)
            kts_sc[j] = ks_ref[0, rows, :].astype(F32).T.astype(BF16)
            ktw_sc[j] = kw_ref[0, rows, :].astype(F32).T.astype(BF16)
            return c

        lax.fori_loop(0, n_tiles, tr, 0)

    q = q_ref[0]
    q4 = jnp.concatenate([q[:, r * HEAD_DIM:(r + 1) * HEAD_DIM] for r in range(R)], axis=0)
    q4_sc[...] = q4

    pos3 = qi * tq + lax.broadcasted_iota(jnp.int32, (1, tq, 1), 1)

    sc = lax.dot_general(q4, kc_ref[0, 0], nt, preferred_element_type=F32)
    sc = sc.reshape(R, tq, LANES) + bc_ref[...]
    cidx = lax.broadcasted_iota(jnp.int32, (1, 1, LANES), 2)
    valid = (cidx * CMP_STRIDE + (CMP_BLOCK - 1) <= pos3) & (cidx < LANES - 1)
    sc = jnp.where(valid, sc, NEG_INF)
    mc = jnp.max(sc, axis=-1, keepdims=True)
    ec = jnp.exp(sc - mc)
    pc = ec / jnp.sum(ec, axis=-1, keepdims=True)
    pc = jnp.where(pos3 >= CMP_BLOCK - 1, pc, 0.0)
    o_cmp = jnp.dot(pc.reshape(R * tq, LANES).astype(BF16), vc_ref[0, 0],
                    preferred_element_type=F32)

    psum = pc[0]
    for r in range(1, R):
        psum = psum + pc[r]
    p_hi = psum.astype(BF16)
    p_lo = (psum - p_hi.astype(F32)).astype(BF16)
    ovt = ovl_ref[...]
    imp = (lax.dot_general(ovt, p_hi, nt, preferred_element_type=F32)
           + lax.dot_general(ovt, p_lo, nt, preferred_element_type=F32))
    jb = lax.broadcasted_iota(jnp.int32, (nb, 1), 0)
    pos_t = qi * tq + lax.broadcasted_iota(jnp.int32, (1, tq), 1)
    q_blk = lax.shift_right_logical(pos_t, int(math.log2(SEL_BLOCK)))
    forced = (jb == 0) | (jb == q_blk) | (jb == q_blk - 1)
    future = jb > q_blk
    imp = jnp.where(forced, FORCE_SCORE, jnp.where(future, -1.0, imp))
    cnt = jnp.zeros((nb, tq), F32)
    for i in range(nb):
        row = imp[i:i + 1, :]
        beats = (row > imp) | ((row == imp) & (jb > i))
        cnt = cnt + jnp.where(beats, 1.0, 0.0)
    sel_t = jnp.where(cnt < float(min(SEL_TOP_N, nb)), 1.0, 0.0)
    sel = jnp.concatenate([sel_t, jnp.zeros((LANES - nb, tq), F32)], axis=0).T.astype(BF16)

    ones = jnp.ones((tq, HEAD_DIM), BF16)
    n_chunks = R * tq // ATT_ROWS
    chunks_per_head = tq // ATT_ROWS

    def flash(kt_sc, v_ref, j_lo, j_hi, kind_of, use_sel):
        m_sc[...] = jnp.full(m_sc.shape, NEG_INF, F32)
        acc_sc[...] = jnp.zeros(acc_sc.shape, F32)

        def body(j, c):
            rows = pl.ds(pl.multiple_of(j * tq, tq), tq)
            kind = kind_of(j)
            s = jnp.dot(q4_sc[...], kt_sc[j], preferred_element_type=F32)
            if use_sel:
                msk = jnp.dot(sel, exp_ref[j], preferred_element_type=F32) > 0.5
            p_parts, alphas = [], []
            for ci in range(n_chunks):
                r, hh = divmod(ci, chunks_per_head)
                rs = slice(ci * ATT_ROWS, (ci + 1) * ATT_ROWS)
                qs = slice(hh * ATT_ROWS, (hh + 1) * ATT_ROWS)
                sc_ = s[rs] + bt_ref[kind, r, qs, :]
                if use_sel:
                    sc_ = jnp.where(msk[qs], sc_, NEG_INF)
                m_old = m_sc[rs]
                m_new = jnp.maximum(m_old, jnp.max(sc_, axis=-1, keepdims=True))
                alphas.append(jnp.exp(m_old - m_new))
                m_sc[rs] = m_new
                p_parts.append(jnp.concatenate(
                    [jnp.exp(sc_[:, k0:k0 + LANES] - m_new) for k0 in range(0, tq, LANES)],
                    axis=1).astype(BF16))
            v_ext = jnp.concatenate([v_ref[0, rows, :], ones], axis=1)
            pv = jnp.dot(jnp.concatenate(p_parts, axis=0), v_ext, preferred_element_type=F32)
            for ci in range(n_chunks):
                rs = slice(ci * ATT_ROWS, (ci + 1) * ATT_ROWS)
                a2 = jnp.concatenate([alphas[ci], alphas[ci]], axis=1)
                acc_sc[rs] = acc_sc[rs] * a2 + pv[rs]
            return c

        lax.fori_loop(j_lo, j_hi, body, 0)
        acc = acc_sc[...]
        return acc[:, :HEAD_DIM] / acc[:, HEAD_DIM:]

    o_sel = flash(kts_sc, vs_ref, 0, qi + 1,
                  lambda j: jnp.clip(j - qi + 2, 0, 2), True)
    n_win = WINDOW // tq
    o_win = flash(ktw_sc, vw_ref, jnp.maximum(qi - n_win, 0), qi + 1,
                  lambda j: jnp.where(j == qi - n_win, 3, j - qi + 2), False)

    gates = jax.nn.sigmoid(gl_ref[0, 0])
    outs = []
    for r in range(R):
        hs = slice(r * tq, (r + 1) * tq)
        o_r = (gates[:, r:r + 1] * o_cmp[hs]
               + gates[:, R + r:R + r + 1] * o_sel[hs]
               + gates[:, 2 * R + r:2 * R + r + 1] * o_win[hs])
        outs.append(o_r)
    o_ref[0] = jnp.concatenate(outs, axis=1).astype(o_ref.dtype)


def _nsa_attention(proj, kcmp, vcmp, bias_c, bias_t, gates_t, B, S):
    assert WINDOW % ATT_TILE == 0 and S % ATT_TILE == 0
    assert S // CMP_STRIDE == LANES and S // SEL_BLOCK <= LANES
    R, G, tq = HEADS_PER_GROUP, N_GROUPS, ATT_TILE
    n_tiles = S // tq
    nb = S // SEL_BLOCK
    _, _, overlap = _static_maps(S)
    overlap_t = np.ascontiguousarray(overlap.T[:nb])
    blk_of_key = (np.arange(S) // SEL_BLOCK).reshape(n_tiles, 1, tq)
    expand = (np.arange(LANES).reshape(1, LANES, 1) == blk_of_key).astype(np.float32)
    q_cols = N_HEADS

    def kv_spec(slot):
        return pl.BlockSpec((1, S, HEAD_DIM),
                            lambda b, g, i, slot=slot: (b, 0, q_cols + slot * G + g))

    cmp_spec = pl.BlockSpec((1, 1, LANES, HEAD_DIM), lambda b, g, i: (b, g, 0, 0))
    in_specs = [
        pl.BlockSpec((1, tq, R * HEAD_DIM), lambda b, g, i: (b, i, g)),
        kv_spec(2), kv_spec(3), kv_spec(4), kv_spec(5),
        cmp_spec, cmp_spec,
        pl.BlockSpec((R, tq, LANES), lambda b, g, i: (g, i, 0)),
        pl.BlockSpec((4, R, tq, tq), lambda b, g, i: (0, g, 0, 0)),
        pl.BlockSpec((nb, LANES), lambda b, g, i: (0, 0)),
        pl.BlockSpec((n_tiles, LANES, tq), lambda b, g, i: (0, 0, 0)),
        pl.BlockSpec((1, 1, tq, 3 * R), lambda b, g, i: (b, g, i, 0)),
    ]
    est = (2 * 4 * S * HEAD_DIM * 2 + 2 * 4 * R * tq * tq * 4 + 2 * R * tq * LANES * 4
           + 2 * S * LANES * 2 + 2 * S * HEAD_DIM * 2 + R * tq * (LANES + 2 * HEAD_DIM) * 4
           + 3 * R * tq * LANES * 4 + 8 * R * tq * tq * 4 + 4 * tq * R * HEAD_DIM * 2)
    return pl.pallas_call(
        _nsa_body,
        out_shape=jax.ShapeDtypeStruct((B, S, N_HEADS * HEAD_DIM), BF16),
        grid=(B, G, n_tiles),
        in_specs=in_specs,
        out_specs=pl.BlockSpec((1, tq, R * HEAD_DIM), lambda b, g, i: (b, i, g)),
        scratch_shapes=[pltpu.VMEM((n_tiles, HEAD_DIM, tq), BF16),
                        pltpu.VMEM((n_tiles, HEAD_DIM, tq), BF16),
                        pltpu.VMEM((R * tq, HEAD_DIM), BF16),
                        pltpu.VMEM((R * tq, LANES), F32),
                        pltpu.VMEM((R * tq, 2 * HEAD_DIM), F32)],
        compiler_params=pltpu.CompilerParams(
            dimension_semantics=("arbitrary", "arbitrary", "arbitrary"),
            vmem_limit_bytes=_vmem_limit(est)),
        name="nsa_attention",
    )(proj, proj, proj, proj, proj, kcmp, vcmp, bias_c, bias_t,
      jnp.asarray(overlap_t, BF16), jnp.asarray(expand, BF16), gates_t)


def _hgrn_body(q_ref, f_ref, i_ref, g_ref, lb_ref, gn_ref, o_ref, st_ref, *, layer):
    C = HGRN_CHUNK
    n_chunks = q_ref.shape[1] // C
    nt = (((1,), (1,)), ((), ()))

    lbp = lb_ref[...]
    e = jnp.exp(lbp - jnp.max(lbp, axis=0, keepdims=True))
    sm = e / jnp.sum(e, axis=0, keepdims=True)
    cum = sm[0:1]
    first = cum
    for d in range(1, layer + 1):
        cum = cum + sm[d:d + 1]
    lb = cum - first
    log_lb = jnp.log(lb)
    log_1m = jnp.log1p(-lb)
    gn = gn_ref[...]

    rows = lax.broadcasted_iota(jnp.int32, (C, 1), 0)
    cols = lax.broadcasted_iota(jnp.int32, (1, C), 1)
    rows_s = lax.broadcasted_iota(jnp.int32, (HGRN_SUB, 1), 0)

    st_ref[...] = jnp.zeros(st_ref.shape, F32)

    def chunk(c, carry):
        sl = pl.ds(pl.multiple_of(c * C, C), C)
        qr = q_ref[0, sl, :]
        fr = f_ref[0, sl, :]
        v = i_ref[0, sl, :]
        gr = g_ref[0, sl, :]
        q = jax.nn.silu(qr)
        log_f = jnp.logaddexp(log_lb, log_1m + jax.nn.log_sigmoid(fr))
        k = (1.0 - lb) * jax.nn.sigmoid(-fr)

        b = log_f
        sh = 1
        while sh < C:
            b = b + jnp.where(rows >= sh, pltpu.roll(b, sh, 0), 0.0)
            sh *= 2

        vb = v.astype(BF16)
        st = st_ref[...]
        o = lax.dot_general((q * jnp.exp(b)).astype(BF16), st.astype(BF16), nt,
                            preferred_element_type=F32)

        a = jnp.zeros((C, C), F32)
        half = C // 2
        while half >= HGRN_SUB:
            grp = 2 * half
            anc = jnp.concatenate(
                [jnp.broadcast_to(b[g0 + half - 1:g0 + half], (grp, b.shape[1]))
                 for g0 in range(0, C, grp)], axis=0)
            ql = (q * jnp.exp(jnp.minimum(b - anc, 0.0))).astype(BF16)
            kl = (k * jnp.exp(jnp.minimum(anc - b, 0.0))).astype(BF16)
            al = lax.dot_general(ql, kl, nt, preferred_element_type=F32)
            msk = (((rows & (grp - 1)) >= half) & ((cols & (grp - 1)) < half)
                   & ((rows & ~(grp - 1)) == (cols & ~(grp - 1))))
            a = a + jnp.where(msk, al, 0.0)
            half //= 2

        pieces = []
        for blk in range(C // HGRN_SUB):
            r0 = blk * HGRN_SUB
            bt = b[r0:r0 + HGRN_SUB]
            qt = q[r0:r0 + HGRN_SUB]
            arow = jnp.zeros((HGRN_SUB, C), F32)
            for s in range(HGRN_SUB):
                bs = b[r0 + s:r0 + s + 1]
                ks = k[r0 + s:r0 + s + 1]
                w = jnp.exp(jnp.minimum(bt - bs, 0.0)) * qt * ks
                col = jnp.sum(w, axis=-1, keepdims=True)
                arow = jnp.where((cols == r0 + s) & (rows_s >= s), col, arow)
            pieces.append(arow)
        a = a + jnp.concatenate(pieces, axis=0)

        o = o + jnp.dot(a.astype(BF16), vb, preferred_element_type=F32)

        b_last = b[C - 1:C]
        kh = (k * jnp.exp(b_last - b)).astype(BF16)
        st_ref[...] = st * jnp.exp(b_last) + jnp.dot(v.T.astype(BF16), kh,
                                                     preferred_element_type=F32)

        ms = jnp.mean(o * o, axis=-1, keepdims=True)
        o = o * lax.rsqrt(ms + RMS_EPS) * gn * jax.nn.silu(gr)
        o_ref[0, sl, :] = o.astype(o_ref.dtype)
        return carry

    lax.fori_loop(0, n_chunks, chunk, 0)


def _hgrn(proj, hgrn_lb, onorm, layer, B, S):
    H = N_HEADS
    depth = hgrn_lb.shape[0]

    def spec(part):
        return pl.BlockSpec((1, S, HEAD_DIM), lambda b, h, part=part: (b, 0, part * H + h))

    est = 2 * 4 * S * HEAD_DIM * 4 + 2 * S * HEAD_DIM * 2 + 64 * HGRN_CHUNK * HGRN_CHUNK * 4
    return pl.pallas_call(
        functools.partial(_hgrn_body, layer=layer),
        out_shape=jax.ShapeDtypeStruct((B, S, H * HEAD_DIM), BF16),
        grid=(B, H),
        in_specs=[spec(0), spec(1), spec(2), spec(3),
                  pl.BlockSpec((depth, HEAD_DIM), lambda b, h: (0, h)),
                  pl.BlockSpec((1, HEAD_DIM), lambda b, h: (0, h))],
        out_specs=pl.BlockSpec((1, S, HEAD_DIM), lambda b, h: (b, 0, h)),
        scratch_shapes=[pltpu.VMEM((HEAD_DIM, HEAD_DIM), F32)],
        compiler_params=pltpu.CompilerParams(
            dimension_semantics=("arbitrary", "arbitrary"),
            vmem_limit_bytes=_vmem_limit(est)),
        name="hgrn2_recurrence",
    )(proj, proj, proj, proj, hgrn_lb.astype(F32), onorm.reshape(1, -1).astype(F32))


def _mlp(xf, g_in, g_out, w_up, w_down):
    hid = _matmul(xf, w_up.astype(BF16), tm=1024, tn=1024, norm_g=g_in, epi="relu2",
                  out_dtype=BF16, name="mlp_up")
    return _matmul(hid, w_down.astype(BF16), tm=512, tn=w_down.shape[1], tk=1024,
                   epi="resnorm", res=xf, res_g=g_out, out_dtype=F32, name="mlp_down")


def _nsa_layer(xf, B, S, g_in, g_out, rel_table, w_in, cmp_pe, cmp_w1, cmp_w2, w_out):
    D = xf.shape[1]
    G, R, Dh = N_GROUPS, HEADS_PER_GROUP, HEAD_DIM
    n_main = N_HEADS * Dh + 6 * G * Dh
    n_gate = 3 * N_HEADS
    w_main = w_in[:, :n_main].astype(BF16)
    w_gate = jnp.pad(w_in[:, n_main:], ((0, 0), (0, LANES - n_gate))).astype(BF16)
    colscale = jnp.concatenate([jnp.full((N_HEADS * Dh,), Dh ** -0.5, F32),
                                jnp.ones((6 * G * Dh,), F32)])[None]

    proj = _matmul(xf, w_main, tm=1024, tn=1024, norm_g=g_in, epi="colscale",
                   colscale=colscale, out_dtype=BF16, name="nsa_proj")
    glog = _matmul(xf, w_gate, tm=1024, tn=LANES, norm_g=g_in, out_dtype=F32,
                   name="nsa_gate_proj")
    gates_t = (glog[:, :n_gate].reshape(B, S, 3, G, R).transpose(0, 3, 1, 2, 4)
               .reshape(B, G, S, 3 * R))

    proj3 = proj.reshape(B, S, n_main)
    nh = S // CMP_STRIDE

    def grouped(slot):
        c0 = N_HEADS * Dh + slot * G * Dh
        a = proj3[:, :, c0:c0 + G * Dh].reshape(B, nh, CMP_STRIDE, G, Dh)
        return a.transpose(0, 3, 1, 2, 4).reshape(B, G, nh, CMP_STRIDE * Dh)

    half = CMP_STRIDE * Dh
    kcmp, vcmp = _compress(grouped(0), grouped(1),
                           cmp_w1.reshape(2, 2, half, Dh).astype(BF16),
                           cmp_pe.reshape(2, 2, 1, half).astype(F32),
                           cmp_w2.astype(BF16))

    bias_c, bias_t = _bias_tables(rel_table, S)
    attn = _nsa_attention(proj3, kcmp, vcmp, bias_c, bias_t, gates_t, B, S)
    return _matmul(attn.reshape(B * S, D), w_out.astype(BF16), tm=512, tn=D,
                   epi="resnorm", res=xf, res_g=g_out, out_dtype=F32, name="nsa_out")


def _hgrn_layer(xf, B, S, layer, g_in, g_out, w_in, hgrn_lb, onorm, w_out):
    D = xf.shape[1]
    proj = _matmul(xf, w_in.astype(BF16), tm=1024, tn=1024, norm_g=g_in,
                   out_dtype=F32, name="hgrn_proj")
    mixed = _hgrn(proj.reshape(B, S, 4 * D), hgrn_lb, onorm, layer, B, S)
    return _matmul(mixed.reshape(B * S, D), w_out.astype(BF16), tm=512, tn=D,
                   epi="resnorm", res=xf, res_g=g_out, out_dtype=F32, name="hgrn_out")


def kernel(x, norm_g, rel_table, nsa_w_in, nsa_cmp_pe, nsa_cmp_w1, nsa_cmp_w2, nsa_w_out,
           hgrn_w_in, hgrn_lb, hgrn_onorm, hgrn_w_out, mlp_w_up, mlp_w_down):
    B, S, D = x.shape
    depth = norm_g.shape[0]
    assert D == N_HEADS * HEAD_DIM and S % ATT_TILE == 0 and S % HGRN_CHUNK == 0
    xf = x.reshape(B * S, D).astype(F32)
    for layer in range(depth):
        j = layer // 2
        if layer % 2 == 0:
            xf = _nsa_layer(xf, B, S, norm_g[layer, 0], norm_g[layer, 1], rel_table,
                            nsa_w_in[j], nsa_cmp_pe[j], nsa_cmp_w1[j], nsa_cmp_w2[j],
                            nsa_w_out[j])
        else:
            xf = _hgrn_layer(xf, B, S, layer, norm_g[layer, 0], norm_g[layer, 1],
                             hgrn_w_in[j], hgrn_lb, hgrn_onorm[j], hgrn_w_out[j])
        xf = _mlp(xf, norm_g[layer, 2], norm_g[layer, 3], mlp_w_up[layer], mlp_w_down[layer])
    return xf.reshape(B, S, D).astype(x.dtype)
```

```python
import functools
import math

import numpy as np
import jax
import jax.numpy as jnp
from jax import lax
from jax.experimental import pallas as pl
from jax.experimental.pallas import tpu as pltpu

F32 = jnp.float32
BF16 = jnp.bfloat16

N_HEADS = 16
N_GROUPS = 4
HEADS_PER_GROUP = N_HEADS // N_GROUPS
HEAD_DIM = 128
CMP_BLOCK = 32
CMP_STRIDE = 16
SEL_BLOCK = 64
SEL_TOP_N = 8
WINDOW = 512
FORCE_SCORE = 1.0e4
REL_BUCKETS = 32
REL_MAX_DIST = 128
RMS_EPS = 1e-6
NEG_INF = -1.0e30
LOG2_E = math.log2(math.e)

LANES = 128
VMEM_BYTES_V7X = 64 * 1024 * 1024
VMEM_LIMIT_CAP = VMEM_BYTES_V7X - 8 * 1024 * 1024

ATT_TILE = 256
ATT_ROWS = 128
HGRN_CHUNK = 128
HGRN_SUB = 8
HGRN_HEADS_PER_STEP = 4
NORM_ROWS = 256


def _vmem_limit(nbytes):
    return int(min(VMEM_LIMIT_CAP, max(32 * 1024 * 1024, nbytes)))


def _mm_body(*refs, norm, epi, nk, tm):
    it = iter(refs)
    x_ref = next(it)
    g_ref = next(it) if norm else None
    w_ref = next(it)
    cs_ref = next(it) if epi == "colscale" else None
    res_ref = next(it) if epi == "resnorm" else None
    go_ref = next(it) if epi == "resnorm" else None
    o_ref = next(it)
    hn_ref = next(it) if norm else None
    acc_ref = next(it) if nk > 1 else None

    j = pl.program_id(1)
    k = pl.program_id(2)

    if norm:
        @pl.when(j == 0)
        def _():
            g = g_ref[...]

            def step(r, c):
                rows = pl.ds(pl.multiple_of(r * NORM_ROWS, NORM_ROWS), NORM_ROWS)
                xs = x_ref[rows, :]
                ms = jnp.mean(xs * xs, axis=-1, keepdims=True)
                hn_ref[rows, :] = (xs * lax.rsqrt(ms + RMS_EPS) * g).astype(BF16)
                return c

            lax.fori_loop(0, tm // NORM_ROWS, step, 0)

        lhs = hn_ref[...]
    else:
        lhs = x_ref[...]

    part = jnp.dot(lhs, w_ref[...], preferred_element_type=F32)

    def finish(acc):
        if epi == "colscale":
            acc = acc * cs_ref[...]
        elif epi == "relu2":
            acc = jnp.square(jnp.maximum(acc, 0.0))
        elif epi == "resnorm":
            ms = jnp.mean(acc * acc, axis=-1, keepdims=True)
            acc = res_ref[...] + acc * lax.rsqrt(ms + RMS_EPS) * go_ref[...]
        o_ref[...] = acc.astype(o_ref.dtype)

    if nk == 1:
        finish(part)
    else:
        @pl.when(k == 0)
        def _():
            acc_ref[...] = part

        @pl.when(k > 0)
        def _():
            acc_ref[...] += part

        @pl.when(k == nk - 1)
        def _():
            finish(acc_ref[...])


def _matmul(x, w, *, tm, tn, tk=None, norm_g=None, epi="none", colscale=None,
            res=None, res_g=None, out_dtype=BF16, name="mm"):
    M, K = x.shape
    N = w.shape[1]
    norm = norm_g is not None
    tk = K if tk is None else tk
    nk = K // tk
    assert M % tm == 0 and N % tn == 0 and K % tk == 0
    assert not (norm and nk != 1)
    assert not (epi == "resnorm" and tn != N)

    in_specs = [pl.BlockSpec((tm, tk), lambda i, j, k: (i, k))]
    args = [x]
    if norm:
        in_specs.append(pl.BlockSpec((1, K), lambda i, j, k: (0, 0)))
        args.append(norm_g.reshape(1, K).astype(F32))
    in_specs.append(pl.BlockSpec((tk, tn), lambda i, j, k: (k, j)))
    args.append(w)
    if epi == "colscale":
        in_specs.append(pl.BlockSpec((1, tn), lambda i, j, k: (0, j)))
        args.append(colscale)
    if epi == "resnorm":
        in_specs.append(pl.BlockSpec((tm, tn), lambda i, j, k: (i, j)))
        args.append(res)
        in_specs.append(pl.BlockSpec((1, tn), lambda i, j, k: (0, j)))
        args.append(res_g.reshape(1, N).astype(F32))

    scratch = []
    if norm:
        scratch.append(pltpu.VMEM((tm, K), BF16))
    if nk > 1:
        scratch.append(pltpu.VMEM((tm, tn), F32))

    xb = x.dtype.itemsize
    ob = jnp.dtype(out_dtype).itemsize
    est = (2 * tm * tk * xb + 2 * tk * tn * 2 + 2 * tm * tn * ob
           + (tm * K * 2 if norm else 0) + (tm * tn * 4 if nk > 1 else 0)
           + (2 * tm * tn * 4 if epi == "resnorm" else 0)
           + 3 * tm * tn * 4)

    return pl.pallas_call(
        functools.partial(_mm_body, norm=norm, epi=epi, nk=nk, tm=tm),
        out_shape=jax.ShapeDtypeStruct((M, N), out_dtype),
        grid=(M // tm, N // tn, nk),
        in_specs=in_specs,
        out_specs=pl.BlockSpec((tm, tn), lambda i, j, k: (i, j)),
        scratch_shapes=scratch,
        compiler_params=pltpu.CompilerParams(
            dimension_semantics=("arbitrary", "arbitrary", "arbitrary"),
            vmem_limit_bytes=_vmem_limit(est)),
        name=name,
    )(*args)


def _rel_bucket_np(dist):
    n = np.maximum(dist, 0)
    max_exact = REL_BUCKETS // 2
    nf = np.maximum(n, 1).astype(np.float32)
    ratio = np.log(nf / np.float32(max_exact)) / np.float32(math.log(REL_MAX_DIST / max_exact))
    large = max_exact + (ratio * np.float32(REL_BUCKETS - max_exact)).astype(np.int32)
    large = np.minimum(large, REL_BUCKETS - 1)
    return np.where(n < max_exact, n, large).astype(np.int32)


@functools.lru_cache(maxsize=None)
def _static_maps(seq):
    n_cmp = LANES
    pos = np.arange(seq, dtype=np.int32)[:, None]
    c_end = np.arange(n_cmp, dtype=np.int32)[None, :] * CMP_STRIDE + CMP_BLOCK - 1
    bucket_c = _rel_bucket_np(pos - c_end)
    t = np.arange(ATT_TILE, dtype=np.int32)[:, None]
    k = np.arange(ATT_TILE, dtype=np.int32)[None, :]
    bucket_t = np.stack([_rel_bucket_np(t - k), _rel_bucket_np(ATT_TILE + t - k)])
    assert _rel_bucket_np(np.array([ATT_TILE + 1]))[0] == REL_BUCKETS - 1
    nc = seq // CMP_STRIDE - CMP_BLOCK // CMP_STRIDE + 1
    nb = seq // SEL_BLOCK
    c_start = np.arange(nc)[:, None] * CMP_STRIDE
    b_start = np.arange(nb)[None, :] * SEL_BLOCK
    ov = ((c_start <= b_start + SEL_BLOCK - 1) & (c_start + CMP_BLOCK - 1 >= b_start))
    overlap = np.zeros((LANES, LANES), np.float32)
    overlap[:nc, :nb] = ov
    return bucket_c, bucket_t, overlap


def _bias_body(tab_ref, bc_ref, bt_ref, oc_ref, ot_ref, *, seq):
    h = pl.program_id(0)

    def lookup(bmap):
        acc = jnp.zeros(bmap.shape, F32)
        for b in range(REL_BUCKETS):
            acc = jnp.where(bmap == b, tab_ref[b, h], acc)
        return acc

    def step(r, c):
        rows = pl.ds(pl.multiple_of(r * ATT_TILE, ATT_TILE), ATT_TILE)
        oc_ref[0, rows, :] = lookup(bc_ref[rows, :])
        return c

    lax.fori_loop(0, seq // ATT_TILE, step, 0)

    tt = lax.broadcasted_iota(jnp.int32, (ATT_TILE, ATT_TILE), 0)
    kk = lax.broadcasted_iota(jnp.int32, (ATT_TILE, ATT_TILE), 1)
    far = jnp.full((ATT_TILE, ATT_TILE), tab_ref[REL_BUCKETS - 1, h], F32)
    ot_ref[0, 0] = far
    ot_ref[1, 0] = lookup(bt_ref[1])
    ot_ref[2, 0] = jnp.where(kk <= tt, lookup(bt_ref[0]), NEG_INF)
    ot_ref[3, 0] = jnp.where(kk > tt, far, NEG_INF)


def _bias_tables(rel_table, seq):
    bucket_c, bucket_t, _ = _static_maps(seq)
    return pl.pallas_call(
        functools.partial(_bias_body, seq=seq),
        out_shape=(jax.ShapeDtypeStruct((N_HEADS, seq, LANES), F32),
                   jax.ShapeDtypeStruct((4, N_HEADS, ATT_TILE, ATT_TILE), F32)),
        grid=(N_HEADS,),
        in_specs=[pl.BlockSpec(memory_space=pltpu.SMEM),
                  pl.BlockSpec((seq, LANES), lambda h: (0, 0)),
                  pl.BlockSpec((2, ATT_TILE, ATT_TILE), lambda h: (0, 0, 0))],
        out_specs=(pl.BlockSpec((1, seq, LANES), lambda h: (h, 0, 0)),
                   pl.BlockSpec((4, 1, ATT_TILE, ATT_TILE), lambda h: (0, h, 0, 0))),
        compiler_params=pltpu.CompilerParams(dimension_semantics=("arbitrary",)),
        name="rel_bias",
    )(rel_table.astype(F32), jnp.asarray(bucket_c), jnp.asarray(bucket_t))


def _compress_body(xk_ref, xv_ref, w1_ref, pe_ref, w2_ref, ok_ref, ov_ref):
    def one(x_ref, idx, o_ref):
        x = x_ref[0, 0].astype(F32)
        a0 = jnp.dot((x + pe_ref[idx, 0]).astype(BF16), w1_ref[idx, 0],
                     preferred_element_type=F32)
        a1 = jnp.dot((x + pe_ref[idx, 1]).astype(BF16), w1_ref[idx, 1],
                     preferred_element_type=F32)
        pre = a0 + pltpu.roll(a1, LANES - 1, 0)
        hid = jax.nn.gelu(pre).astype(BF16)
        o_ref[0, 0] = jnp.dot(hid, w2_ref[idx], preferred_element_type=F32).astype(BF16)

    one(xk_ref, 0, ok_ref)
    one(xv_ref, 1, ov_ref)


def _compress(xk, xv, w1, pe, w2):
    B, G = xk.shape[:2]
    half = CMP_STRIDE * HEAD_DIM
    spec_x = pl.BlockSpec((1, 1, LANES, half), lambda b, g: (b, g, 0, 0))
    spec_o = pl.BlockSpec((1, 1, LANES, HEAD_DIM), lambda b, g: (b, g, 0, 0))
    out = jax.ShapeDtypeStruct((B, G, LANES, HEAD_DIM), BF16)
    return pl.pallas_call(
        _compress_body,
        out_shape=(out, out),
        grid=(B, G),
        in_specs=[spec_x, spec_x,
                  pl.BlockSpec((2, 2, half, HEAD_DIM), lambda b, g: (0, 0, 0, 0)),
                  pl.BlockSpec((2, 2, 1, half), lambda b, g: (0, 0, 0, 0)),
                  pl.BlockSpec((2, HEAD_DIM, HEAD_DIM), lambda b, g: (0, 0, 0))],
        out_specs=(spec_o, spec_o),
        compiler_params=pltpu.CompilerParams(dimension_semantics=("arbitrary", "arbitrary")),
        name="nsa_compress",
    )(xk, xv, w1, pe, w2)


def _nsa_body(q_ref, ks_ref, vs_ref, kw_ref, vw_ref, kc_ref, vc_ref, bc_ref, bt_ref,
              ovl_ref, exp_ref, gl_ref, o_ref, kts_sc, ktw_sc, q4_sc, m_sc, acc_sc):
    R = HEADS_PER_GROUP
    tq = ATT_TILE
    qi = pl.program_id(2)
    n_tiles = ks_ref.shape[1] // tq
    nb = ovl_ref.shape[0]
    nt = (((1,), (1,)), ((), ()))

    @pl.when(qi == 0)
    def _():
        def tr(j, c):
            rows = pl.ds(pl.multiple_of(j * tq, tq), tq)
            kts_sc[j] = ks_ref[0, rows, :].astype(F32).T.astype(BF16)
            ktw_sc[j] = kw_ref[0, rows, :].astype(F32).T.astype(BF16)
            return c

        lax.fori_loop(0, n_tiles, tr, 0)

    q = q_ref[0]
    q4 = jnp.concatenate([q[:, r * HEAD_DIM:(r + 1) * HEAD_DIM] for r in range(R)], axis=0)
    q4_sc[...] = q4

    pos3 = qi * tq + lax.broadcasted_iota(jnp.int32, (1, tq, 1), 1)

    sc = lax.dot_general(q4, kc_ref[0, 0], nt, preferred_element_type=F32)
    sc = sc.reshape(R, tq, LANES) + bc_ref[...]
    cidx = lax.broadcasted_iota(jnp.int32, (1, 1, LANES), 2)
    valid = (cidx * CMP_STRIDE + (CMP_BLOCK - 1) <= pos3) & (cidx < LANES - 1)
    sc = jnp.where(valid, sc, NEG_INF)
    mc = jnp.max(sc, axis=-1, keepdims=True)
    ec = jnp.exp(sc - mc)
    pc = ec / jnp.sum(ec, axis=-1, keepdims=True)
    pc = jnp.where(pos3 >= CMP_BLOCK - 1, pc, 0.0)
    o_cmp = jnp.dot(pc.reshape(R * tq, LANES).astype(BF16), vc_ref[0, 0],
                    preferred_element_type=F32)

    psum = pc[0]
    for r in range(1, R):
        psum = psum + pc[r]
    p_hi = psum.astype(BF16)
    p_lo = (psum - p_hi.astype(F32)).astype(BF16)
    ovt = ovl_ref[...]
    imp = (lax.dot_general(ovt, p_hi, nt, preferred_element_type=F32)
           + lax.dot_general(ovt, p_lo, nt, preferred_element_type=F32))
    jb = lax.broadcasted_iota(jnp.int32, (nb, 1), 0)
    pos_t = qi * tq + lax.broadcasted_iota(jnp.int32, (1, tq), 1)
    q_blk = lax.shift_right_logical(pos_t, int(math.log2(SEL_BLOCK)))
    forced = (jb == 0) | (jb == q_blk) | (jb == q_blk - 1)
    future = jb > q_blk
    imp = jnp.where(forced, FORCE_SCORE, jnp.where(future, -1.0, imp))
    cnt = jnp.zeros((nb, tq), F32)
    for i in range(nb):
        row = imp[i:i + 1, :]
        beats = (row > imp) | ((row == imp) & (jb > i))
        cnt = cnt + jnp.where(beats, 1.0, 0.0)
    sel_t = jnp.where(cnt < float(min(SEL_TOP_N, nb)), 1.0, 0.0)
    sel = jnp.concatenate([sel_t, jnp.zeros((LANES - nb, tq), F32)], axis=0).T.astype(BF16)

    ones = jnp.ones((tq, HEAD_DIM), BF16)
    n_chunks = R * tq // ATT_ROWS
    chunks_per_head = tq // ATT_ROWS

    def flash(kt_sc, v_ref, j_lo, j_hi, kind_of, use_sel):
        m_sc[...] = jnp.full(m_sc.shape, NEG_INF, F32)
        acc_sc[...] = jnp.zeros(acc_sc.shape, F32)

        def body(j, c):
            rows = pl.ds(pl.multiple_of(j * tq, tq), tq)
            kind = kind_of(j)
            s = jnp.dot(q4_sc[...], kt_sc[j], preferred_element_type=F32)
            if use_sel:
                msk = jnp.dot(sel, exp_ref[j], preferred_element_type=F32) > 0.5
            p_parts, alphas = [], []
            for ci in range(n_chunks):
                r, hh = divmod(ci, chunks_per_head)
                rs = slice(ci * ATT_ROWS, (ci + 1) * ATT_ROWS)
                qs = slice(hh * ATT_ROWS, (hh + 1) * ATT_ROWS)
                sc_ = s[rs] + bt_ref[kind, r, qs, :]
                if use_sel:
                    sc_ = jnp.where(msk[qs], sc_, NEG_INF)
                m_old = m_sc[rs]
                m_new = jnp.maximum(m_old, jnp.max(sc_, axis=-1, keepdims=True))
                alphas.append(jnp.exp(m_old - m_new))
                m_sc[rs] = m_new
                p_parts.append(jnp.concatenate(
                    [jnp.exp(sc_[:, k0:k0 + LANES] - m_new) for k0 in range(0, tq, LANES)],
                    axis=1).astype(BF16))
            v_ext = jnp.concatenate([v_ref[0, rows, :], ones], axis=1)
            pv = jnp.dot(jnp.concatenate(p_parts, axis=0), v_ext, preferred_element_type=F32)
            for ci in range(n_chunks):
                rs = slice(ci * ATT_ROWS, (ci + 1) * ATT_ROWS)
                a2 = jnp.concatenate([alphas[ci], alphas[ci]], axis=1)
                acc_sc[rs] = acc_sc[rs] * a2 + pv[rs]
            return c

        lax.fori_loop(j_lo, j_hi, body, 0)
        acc = acc_sc[...]
        return acc[:, :HEAD_DIM] / acc[:, HEAD_DIM:]

    o_sel = flash(kts_sc, vs_ref, 0, qi + 1,
                  lambda j: jnp.clip(j - qi + 2, 0, 2), True)
    n_win = WINDOW // tq
    o_win = flash(ktw_sc, vw_ref, jnp.maximum(qi - n_win, 0), qi + 1,
                  lambda j: jnp.where(j == qi - n_win, 3, j - qi + 2), False)

    gates = jax.nn.sigmoid(gl_ref[0, 0])
    outs = []
    for r in range(R):
        hs = slice(r * tq, (r + 1) * tq)
        o_r = (gates[:, r:r + 1] * o_cmp[hs]
               + gates[:, R + r:R + r + 1] * o_sel[hs]
               + gates[:, 2 * R + r:2 * R + r + 1] * o_win[hs])
        outs.append(o_r)
    o_ref[0] = jnp.concatenate(outs, axis=1).astype(o_ref.dtype)


def _nsa_attention(proj, kcmp, vcmp, bias_c, bias_t, gates_t, B, S):
    assert WINDOW % ATT_TILE == 0 and S % ATT_TILE == 0
    assert S // CMP_STRIDE == LANES and S // SEL_BLOCK <= LANES
    R, G, tq = HEADS_PER_GROUP, N_GROUPS, ATT_TILE
    n_tiles = S // tq
    nb = S // SEL_BLOCK
    _, _, overlap = _static_maps(S)
    overlap_t = np.ascontiguousarray(overlap.T[:nb])
    blk_of_key = (np.arange(S) // SEL_BLOCK).reshape(n_tiles, 1, tq)
    expand = (np.arange(LANES).reshape(1, LANES, 1) == blk_of_key).astype(np.float32)
    q_cols = N_HEADS

    def kv_spec(slot):
        return pl.BlockSpec((1, S, HEAD_DIM),
                            lambda b, g, i, slot=slot: (b, 0, q_cols + slot * G + g))

    cmp_spec = pl.BlockSpec((1, 1, LANES, HEAD_DIM), lambda b, g, i: (b, g, 0, 0))
    in_specs = [
        pl.BlockSpec((1, tq, R * HEAD_DIM), lambda b, g, i: (b, i, g)),
        kv_spec(2), kv_spec(3), kv_spec(4), kv_spec(5),
        cmp_spec, cmp_spec,
        pl.BlockSpec((R, tq, LANES), lambda b, g, i: (g, i, 0)),
        pl.BlockSpec((4, R, tq, tq), lambda b, g, i: (0, g, 0, 0)),
        pl.BlockSpec((nb, LANES), lambda b, g, i: (0, 0)),
        pl.BlockSpec((n_tiles, LANES, tq), lambda b, g, i: (0, 0, 0)),
        pl.BlockSpec((1, 1, tq, 3 * R), lambda b, g, i: (b, g, i, 0)),
    ]
    est = (2 * 4 * S * HEAD_DIM * 2 + 2 * 4 * R * tq * tq * 4 + 2 * R * tq * LANES * 4
           + 2 * S * LANES * 2 + 2 * S * HEAD_DIM * 2 + R * tq * (LANES + 2 * HEAD_DIM) * 4
           + 3 * R * tq * LANES * 4 + 8 * R * tq * tq * 4 + 4 * tq * R * HEAD_DIM * 2)
    return pl.pallas_call(
        _nsa_body,
        out_shape=jax.ShapeDtypeStruct((B, S, N_HEADS * HEAD_DIM), BF16),
        grid=(B, G, n_tiles),
        in_specs=in_specs,
        out_specs=pl.BlockSpec((1, tq, R * HEAD_DIM), lambda b, g, i: (b, i, g)),
        scratch_shapes=[pltpu.VMEM((n_tiles, HEAD_DIM, tq), BF16),
                        pltpu.VMEM((n_tiles, HEAD_DIM, tq), BF16),
                        pltpu.VMEM((R * tq, HEAD_DIM), BF16),
                        pltpu.VMEM((R * tq, LANES), F32),
                        pltpu.VMEM((R * tq, 2 * HEAD_DIM), F32)],
        compiler_params=pltpu.CompilerParams(
            dimension_semantics=("arbitrary", "arbitrary", "arbitrary"),
            vmem_limit_bytes=_vmem_limit(est)),
        name="nsa_attention",
    )(proj, proj, proj, proj, proj, kcmp, vcmp, bias_c, bias_t,
      jnp.asarray(overlap_t, BF16), jnp.asarray(expand, BF16), gates_t)


@functools.lru_cache(maxsize=None)
def _hgrn_masks():
    C = HGRN_CHUNK
    t = np.arange(C)[:, None]
    s = np.arange(C)[None, :]
    masks = [(t // HGRN_SUB == s // HGRN_SUB) & (s <= t)]
    half = C // 2
    while half >= HGRN_SUB:
        grp = 2 * half
        masks.append((t // grp == s // grp) & (t % grp >= half) & (s % grp < half))
        half //= 2
    assert np.array_equal(np.sum(masks, axis=0), (s <= t).astype(int))
    return np.stack(masks).astype(np.float32), (s <= t).astype(np.float32)


def _hgrn_body(q_ref, f_ref, i_ref, g_ref, lb_ref, gn_ref, msk_ref, tril_ref, o_ref,
               st_all, b_all, k_all, q_all, *, layer, heads):
    C = HGRN_CHUNK
    n_chunks = q_ref.shape[1] // C
    nt = (((1,), (1,)), ((), ()))

    lbp = lb_ref[...]
    e = jnp.exp(lbp - jnp.max(lbp, axis=0, keepdims=True))
    sm = e / jnp.sum(e, axis=0, keepdims=True)
    cum = sm[0:1]
    first = cum
    for d in range(1, layer + 1):
        cum = cum + sm[d:d + 1]
    lb_all = cum - first
    log_lb_all = jnp.log(lb_all)
    log_1m_all = jnp.log1p(-lb_all)
    gn_all = gn_ref[...]

    cols = lax.broadcasted_iota(jnp.int32, (1, C), 1)
    tril = tril_ref[...]

    st_all[...] = jnp.zeros(st_all.shape, F32)

    def chunk(c, carry):
        for hh in range(heads):
            one_head(c, hh)
        return carry

    def one_head(c, hh):
        sl = pl.ds(pl.multiple_of(c * C, C), C)
        hs = slice(hh * HEAD_DIM, (hh + 1) * HEAD_DIM)
        lb, log_lb, log_1m, gn = lb_all[:, hs], log_lb_all[:, hs], log_1m_all[:, hs], gn_all[:, hs]
        st_ref, b_sc, k_sc, q_sc = st_all.at[hh], b_all.at[hh], k_all.at[hh], q_all.at[hh]
        qr = q_ref[0, sl, hs]
        x = f_ref[0, sl, hs]
        v = i_ref[0, sl, hs]
        gr = g_ref[0, sl, hs]
        q = jax.nn.silu(qr)
        ex = jnp.exp(-jnp.abs(x))
        r1 = 1.0 / (1.0 + ex)
        k = (1.0 - lb) * jnp.where(x >= 0.0, ex * r1, r1)
        c2 = log_1m + (jnp.minimum(x, 0.0) - jnp.log1p(ex))
        log_f = jnp.maximum(log_lb, c2) + jnp.log1p(jnp.exp(-jnp.abs(log_lb - c2)))

        lf_hi = log_f.astype(BF16)
        lf_lo = (log_f - lf_hi.astype(F32)).astype(BF16)
        b = (jnp.dot(tril, lf_hi, preferred_element_type=F32)
             + jnp.dot(tril, lf_lo, preferred_element_type=F32)) * LOG2_E
        b_sc[...] = b
        k_sc[...] = k
        q_sc[...] = q

        vb = v.astype(BF16)
        st = st_ref[...]
        o = lax.dot_general((q * jnp.exp2(b)).astype(BF16), st.astype(BF16), nt,
                            preferred_element_type=F32)

        pieces = []
        for blk in range(C // HGRN_SUB):
            r0 = blk * HGRN_SUB
            bt = b_sc[r0:r0 + HGRN_SUB, :]
            qt = q_sc[r0:r0 + HGRN_SUB, :]
            arow = jnp.zeros((HGRN_SUB, C), F32)
            for s in range(HGRN_SUB):
                bs = b_sc[r0 + s:r0 + s + 1, :]
                ks = k_sc[r0 + s:r0 + s + 1, :]
                col = jnp.sum(jnp.exp2(bt - bs) * qt * ks, axis=-1, keepdims=True)
                arow = jnp.where(cols == r0 + s, col, arow)
            pieces.append(arow)
        a = jnp.where(msk_ref[0] > 0.5, jnp.concatenate(pieces, axis=0), 0.0)

        half = C // 2
        lvl = 1
        while half >= HGRN_SUB:
            grp = 2 * half
            anc = jnp.concatenate(
                [jnp.broadcast_to(b_sc[g0 + half - 1:g0 + half, :], (grp, b.shape[1]))
                 for g0 in range(0, C, grp)], axis=0)
            e = jnp.exp2(-jnp.abs(b - anc))
            al = lax.dot_general((q * e).astype(BF16), (k * e).astype(BF16), nt,
                                 preferred_element_type=F32)
            a = jnp.where(msk_ref[lvl] > 0.5, al, a)
            half //= 2
            lvl += 1

        o = o + jnp.dot(a.astype(BF16), vb, preferred_element_type=F32)

        b_last = b_sc[C - 1:C, :]
        kh = (k * jnp.exp2(b_last - b)).astype(BF16)
        st_ref[...] = st * jnp.exp2(b_last) + jnp.dot(v.T.astype(BF16), kh,
                                                     preferred_element_type=F32)

        ms = jnp.mean(o * o, axis=-1, keepdims=True)
        o = o * lax.rsqrt(ms + RMS_EPS) * gn * jax.nn.silu(gr)
        o_ref[0, sl, hs] = o.astype(o_ref.dtype)

    lax.fori_loop(0, n_chunks, chunk, 0)


def _hgrn(proj, hgrn_lb, onorm, layer, B, S):
    H = N_HEADS
    depth = hgrn_lb.shape[0]

    nh = HGRN_HEADS_PER_STEP
    width = nh * HEAD_DIM
    steps = H // nh

    def spec(part):
        return pl.BlockSpec((1, S, width), lambda b, h, part=part: (b, 0, part * steps + h))

    masks, tril = _hgrn_masks()
    C = HGRN_CHUNK
    est = 2 * 4 * S * width * 4 + 2 * S * width * 2 + nh * 64 * C * C * 4
    return pl.pallas_call(
        functools.partial(_hgrn_body, layer=layer, heads=nh),
        out_shape=jax.ShapeDtypeStruct((B, S, H * HEAD_DIM), BF16),
        grid=(B, steps),
        in_specs=[spec(0), spec(1), spec(2), spec(3),
                  pl.BlockSpec((depth, width), lambda b, h: (0, h)),
                  pl.BlockSpec((1, width), lambda b, h: (0, h)),
                  pl.BlockSpec(masks.shape, lambda b, h: (0, 0, 0)),
                  pl.BlockSpec((C, C), lambda b, h: (0, 0))],
        out_specs=pl.BlockSpec((1, S, width), lambda b, h: (b, 0, h)),
        scratch_shapes=[pltpu.VMEM((nh, HEAD_DIM, HEAD_DIM), F32),
                        pltpu.VMEM((nh, C, HEAD_DIM), F32),
                        pltpu.VMEM((nh, C, HEAD_DIM), F32),
                        pltpu.VMEM((nh, C, HEAD_DIM), F32)],
        compiler_params=pltpu.CompilerParams(
            dimension_semantics=("arbitrary", "arbitrary"),
            vmem_limit_bytes=_vmem_limit(est)),
        name="hgrn2_recurrence",
    )(proj, proj, proj, proj, hgrn_lb.astype(F32), onorm.reshape(1, -1).astype(F32),
      jnp.asarray(masks), jnp.asarray(tril, BF16))


def _mlp(xf, g_in, g_out, w_up, w_down):
    hid = _matmul(xf, w_up.astype(BF16), tm=1024, tn=1024, norm_g=g_in, epi="relu2",
                  out_dtype=BF16, name="mlp_up")
    return _matmul(hid, w_down.astype(BF16), tm=512, tn=w_down.shape[1], tk=1024,
                   epi="resnorm", res=xf, res_g=g_out, out_dtype=F32, name="mlp_down")


def _nsa_layer(xf, B, S, g_in, g_out, rel_table, w_in, cmp_pe, cmp_w1, cmp_w2, w_out):
    D = xf.shape[1]
    G, R, Dh = N_GROUPS, HEADS_PER_GROUP, HEAD_DIM
    n_main = N_HEADS * Dh + 6 * G * Dh
    n_gate = 3 * N_HEADS
    w_main = w_in[:, :n_main].astype(BF16)
    w_gate = jnp.pad(w_in[:, n_main:], ((0, 0), (0, LANES - n_gate))).astype(BF16)
    colscale = jnp.concatenate([jnp.full((N_HEADS * Dh,), Dh ** -0.5, F32),
                                jnp.ones((6 * G * Dh,), F32)])[None]

    proj = _matmul(xf, w_main, tm=1024, tn=1024, norm_g=g_in, epi="colscale",
                   colscale=colscale, out_dtype=BF16, name="nsa_proj")
    glog = _matmul(xf, w_gate, tm=1024, tn=LANES, norm_g=g_in, out_dtype=F32,
                   name="nsa_gate_proj")
    gates_t = (glog[:, :n_gate].reshape(B, S, 3, G, R).transpose(0, 3, 1, 2, 4)
               .reshape(B, G, S, 3 * R))

    proj3 = proj.reshape(B, S, n_main)
    nh = S // CMP_STRIDE

    def grouped(slot):
        c0 = N_HEADS * Dh + slot * G * Dh
        a = proj3[:, :, c0:c0 + G * Dh].reshape(B, nh, CMP_STRIDE, G, Dh)
        return a.transpose(0, 3, 1, 2, 4).reshape(B, G, nh, CMP_STRIDE * Dh)

    half = CMP_STRIDE * Dh
    kcmp, vcmp = _compress(grouped(0), grouped(1),
                           cmp_w1.reshape(2, 2, half, Dh).astype(BF16),
                           cmp_pe.reshape(2, 2, 1, half).astype(F32),
                           cmp_w2.astype(BF16))

    bias_c, bias_t = _bias_tables(rel_table, S)
    attn = _nsa_attention(proj3, kcmp, vcmp, bias_c, bias_t, gates_t, B, S)
    return _matmul(attn.reshape(B * S, D), w_out.astype(BF16), tm=512, tn=D,
                   epi="resnorm", res=xf, res_g=g_out, out_dtype=F32, name="nsa_out")


def _hgrn_layer(xf, B, S, layer, g_in, g_out, w_in, hgrn_lb, onorm, w_out):
    D = xf.shape[1]
    proj = _matmul(xf, w_in.astype(BF16), tm=1024, tn=1024, norm_g=g_in,
                   out_dtype=F32, name="hgrn_proj")
    mixed = _hgrn(proj.reshape(B, S, 4 * D), hgrn_lb, onorm, layer, B, S)
    return _matmul(mixed.reshape(B * S, D), w_out.astype(BF16), tm=512, tn=D,
                   epi="resnorm", res=xf, res_g=g_out, out_dtype=F32, name="hgrn_out")


def kernel(x, norm_g, rel_table, nsa_w_in, nsa_cmp_pe, nsa_cmp_w1, nsa_cmp_w2, nsa_w_out,
           hgrn_w_in, hgrn_lb, hgrn_onorm, hgrn_w_out, mlp_w_up, mlp_w_down):
    B, S, D = x.shape
    depth = norm_g.shape[0]
    assert D == N_HEADS * HEAD_DIM and S % ATT_TILE == 0 and S % HGRN_CHUNK == 0
    xf = x.reshape(B * S, D).astype(F32)
    for layer in range(depth):
        j = layer // 2
        if layer % 2 == 0:
            xf = _nsa_layer(xf, B, S, norm_g[layer, 0], norm_g[layer, 1], rel_table,
                            nsa_w_in[j], nsa_cmp_pe[j], nsa_cmp_w1[j], nsa_cmp_w2[j],
                            nsa_w_out[j])
        else:
            xf = _hgrn_layer(xf, B, S, layer, norm_g[layer, 0], norm_g[layer, 1],
                             hgrn_w_in[j], hgrn_lb, hgrn_onorm[j], hgrn_w_out[j])
        xf = _mlp(xf, norm_g[layer, 2], norm_g[layer, 3], mlp_w_up[layer], mlp_w_down[layer])
    return xf.reshape(B, S, D).astype(x.dtype)
```

```python
import functools
import math

import numpy as np
import jax
import jax.numpy as jnp
from jax import lax
from jax.experimental import pallas as pl
from jax.experimental.pallas import tpu as pltpu

F32 = jnp.float32
BF16 = jnp.bfloat16

N_HEADS = 16
N_GROUPS = 4
HEADS_PER_GROUP = N_HEADS // N_GROUPS
HEAD_DIM = 128
CMP_BLOCK = 32
CMP_STRIDE = 16
SEL_BLOCK = 64
SEL_TOP_N = 8
WINDOW = 512
FORCE_SCORE = 1.0e4
REL_BUCKETS = 32
REL_MAX_DIST = 128
RMS_EPS = 1e-6
NEG_INF = -1.0e30
LOG2_E = math.log2(math.e)

LANES = 128
SUBLANES = 8
VMEM_BYTES_V7X = 64 * 1024 * 1024
VMEM_LIMIT_CAP = VMEM_BYTES_V7X - 8 * 1024 * 1024

ATT_TILE = 256
ATT_ROWS = 128
HGRN_CHUNK = 128
HGRN_SUB = 8
HGRN_HEADS_PER_STEP = 4
NORM_ROWS = 256


def _vmem_limit(nbytes):
    return int(min(VMEM_LIMIT_CAP, max(32 * 1024 * 1024, nbytes)))


def _mm_body(*refs, norm, epi, nk, tm):
    it = iter(refs)
    x_ref = next(it)
    g_ref = next(it) if norm else None
    w_ref = next(it)
    cs_ref = next(it) if epi == "colscale" else None
    res_ref = next(it) if epi == "resnorm" else None
    go_ref = next(it) if epi == "resnorm" else None
    o_ref = next(it)
    hn_ref = next(it) if norm else None
    acc_ref = next(it) if nk > 1 else None

    j = pl.program_id(1)
    k = pl.program_id(2)

    if norm:
        @pl.when(j == 0)
        def _():
            g = g_ref[...]

            def step(r, c):
                rows = pl.ds(pl.multiple_of(r * NORM_ROWS, NORM_ROWS), NORM_ROWS)
                xs = x_ref[rows, :]
                ms = jnp.mean(xs * xs, axis=-1, keepdims=True)
                hn_ref[rows, :] = (xs * lax.rsqrt(ms + RMS_EPS) * g).astype(BF16)
                return c

            lax.fori_loop(0, tm // NORM_ROWS, step, 0)

        lhs = hn_ref[...]
    else:
        lhs = x_ref[...]

    part = jnp.dot(lhs, w_ref[...], preferred_element_type=F32)

    def finish(acc):
        if epi == "colscale":
            acc = acc * cs_ref[...]
        elif epi == "relu2":
            acc = jnp.square(jnp.maximum(acc, 0.0))
        elif epi == "resnorm":
            ms = jnp.mean(acc * acc, axis=-1, keepdims=True)
            acc = res_ref[...] + acc * lax.rsqrt(ms + RMS_EPS) * go_ref[...]
        o_ref[...] = acc.astype(o_ref.dtype)

    if nk == 1:
        finish(part)
    else:
        @pl.when(k == 0)
        def _():
            acc_ref[...] = part

        @pl.when(k > 0)
        def _():
            acc_ref[...] += part

        @pl.when(k == nk - 1)
        def _():
            finish(acc_ref[...])


def _matmul(x, w, *, tm, tn, tk=None, norm_g=None, epi="none", colscale=None,
            res=None, res_g=None, out_dtype=BF16, name="mm"):
    M, K = x.shape
    N = w.shape[1]
    norm = norm_g is not None
    tk = K if tk is None else tk
    nk = K // tk
    assert M % tm == 0 and N % tn == 0 and K % tk == 0
    assert not (norm and nk != 1)
    assert not (epi == "resnorm" and tn != N)

    in_specs = [pl.BlockSpec((tm, tk), lambda i, j, k: (i, k))]
    args = [x]
    if norm:
        in_specs.append(pl.BlockSpec((1, K), lambda i, j, k: (0, 0)))
        args.append(norm_g.reshape(1, K).astype(F32))
    in_specs.append(pl.BlockSpec((tk, tn), lambda i, j, k: (k, j)))
    args.append(w)
    if epi == "colscale":
        in_specs.append(pl.BlockSpec((1, tn), lambda i, j, k: (0, j)))
        args.append(colscale)
    if epi == "resnorm":
        in_specs.append(pl.BlockSpec((tm, tn), lambda i, j, k: (i, j)))
        args.append(res)
        in_specs.append(pl.BlockSpec((1, tn), lambda i, j, k: (0, j)))
        args.append(res_g.reshape(1, N).astype(F32))

    scratch = []
    if norm:
        scratch.append(pltpu.VMEM((tm, K), BF16))
    if nk > 1:
        scratch.append(pltpu.VMEM((tm, tn), F32))

    xb = x.dtype.itemsize
    ob = jnp.dtype(out_dtype).itemsize
    est = (2 * tm * tk * xb + 2 * tk * tn * 2 + 2 * tm * tn * ob
           + (tm * K * 2 if norm else 0) + (tm * tn * 4 if nk > 1 else 0)
           + (2 * tm * tn * 4 if epi == "resnorm" else 0)
           + 3 * tm * tn * 4)

    return pl.pallas_call(
        functools.partial(_mm_body, norm=norm, epi=epi, nk=nk, tm=tm),
        out_shape=jax.ShapeDtypeStruct((M, N), out_dtype),
        grid=(M // tm, N // tn, nk),
        in_specs=in_specs,
        out_specs=pl.BlockSpec((tm, tn), lambda i, j, k: (i, j)),
        scratch_shapes=scratch,
        compiler_params=pltpu.CompilerParams(
            dimension_semantics=("arbitrary", "arbitrary", "arbitrary"),
            vmem_limit_bytes=_vmem_limit(est)),
        name=name,
    )(*args)


def _rel_bucket_np(dist):
    n = np.maximum(dist, 0)
    max_exact = REL_BUCKETS // 2
    nf = np.maximum(n, 1).astype(np.float32)
    ratio = np.log(nf / np.float32(max_exact)) / np.float32(math.log(REL_MAX_DIST / max_exact))
    large = max_exact + (ratio * np.float32(REL_BUCKETS - max_exact)).astype(np.int32)
    large = np.minimum(large, REL_BUCKETS - 1)
    return np.where(n < max_exact, n, large).astype(np.int32)


@functools.lru_cache(maxsize=None)
def _static_maps(seq):
    n_cmp = LANES
    pos = np.arange(seq, dtype=np.int32)[:, None]
    c_end = np.arange(n_cmp, dtype=np.int32)[None, :] * CMP_STRIDE + CMP_BLOCK - 1
    bucket_c = _rel_bucket_np(pos - c_end)
    t = np.arange(ATT_TILE, dtype=np.int32)[:, None]
    k = np.arange(ATT_TILE, dtype=np.int32)[None, :]
    bucket_t = np.stack([_rel_bucket_np(t - k), _rel_bucket_np(ATT_TILE + t - k)])
    assert _rel_bucket_np(np.array([ATT_TILE + 1]))[0] == REL_BUCKETS - 1
    nc = seq // CMP_STRIDE - CMP_BLOCK // CMP_STRIDE + 1
    nb = seq // SEL_BLOCK
    c_start = np.arange(nc)[:, None] * CMP_STRIDE
    b_start = np.arange(nb)[None, :] * SEL_BLOCK
    ov = ((c_start <= b_start + SEL_BLOCK - 1) & (c_start + CMP_BLOCK - 1 >= b_start))
    overlap = np.zeros((LANES, LANES), np.float32)
    overlap[:nc, :nb] = ov
    return bucket_c, bucket_t, overlap


def _bias_body(tab_ref, bc_ref, bt_ref, oc_ref, ot_ref, of_ref, *, seq):
    h = pl.program_id(0)

    def lookup(bmap):
        acc = jnp.zeros(bmap.shape, F32)
        for b in range(REL_BUCKETS):
            acc = jnp.where(bmap == b, tab_ref[b, h], acc)
        return acc

    def step(r, c):
        rows = pl.ds(pl.multiple_of(r * ATT_TILE, ATT_TILE), ATT_TILE)
        oc_ref[0, rows, :] = lookup(bc_ref[rows, :]) * LOG2_E
        return c

    lax.fori_loop(0, seq // ATT_TILE, step, 0)

    tt = lax.broadcasted_iota(jnp.int32, (ATT_TILE, ATT_TILE), 0)
    kk = lax.broadcasted_iota(jnp.int32, (ATT_TILE, ATT_TILE), 1)
    far = tab_ref[REL_BUCKETS - 1, h]
    of_ref[0] = jnp.full(of_ref.shape[1:], far * LOG2_E, F32)
    ot_ref[0, 0] = jnp.zeros((ATT_TILE, ATT_TILE), F32)
    ot_ref[1, 0] = (lookup(bt_ref[1]) - far) * LOG2_E
    ot_ref[2, 0] = jnp.where(kk <= tt, (lookup(bt_ref[0]) - far) * LOG2_E, NEG_INF)
    ot_ref[3, 0] = jnp.where(kk > tt, 0.0, NEG_INF)


def _bias_tables(rel_table, seq):
    bucket_c, bucket_t, _ = _static_maps(seq)
    return pl.pallas_call(
        functools.partial(_bias_body, seq=seq),
        out_shape=(jax.ShapeDtypeStruct((N_HEADS, seq, LANES), F32),
                   jax.ShapeDtypeStruct((4, N_HEADS, ATT_TILE, ATT_TILE), F32),
                   jax.ShapeDtypeStruct((N_HEADS, SUBLANES, LANES), F32)),
        grid=(N_HEADS,),
        in_specs=[pl.BlockSpec(memory_space=pltpu.SMEM),
                  pl.BlockSpec((seq, LANES), lambda h: (0, 0)),
                  pl.BlockSpec((2, ATT_TILE, ATT_TILE), lambda h: (0, 0, 0))],
        out_specs=(pl.BlockSpec((1, seq, LANES), lambda h: (h, 0, 0)),
                   pl.BlockSpec((4, 1, ATT_TILE, ATT_TILE), lambda h: (0, h, 0, 0)),
                   pl.BlockSpec((1, SUBLANES, LANES), lambda h: (h, 0, 0))),
        compiler_params=pltpu.CompilerParams(dimension_semantics=("arbitrary",)),
        name="rel_bias",
    )(rel_table.astype(F32), jnp.asarray(bucket_c), jnp.asarray(bucket_t))


def _compress_body(xk_ref, xv_ref, w1_ref, pe_ref, w2_ref, ok_ref, ov_ref):
    def one(x_ref, idx, o_ref):
        x = x_ref[0, 0].astype(F32)
        a0 = jnp.dot((x + pe_ref[idx, 0]).astype(BF16), w1_ref[idx, 0],
                     preferred_element_type=F32)
        a1 = jnp.dot((x + pe_ref[idx, 1]).astype(BF16), w1_ref[idx, 1],
                     preferred_element_type=F32)
        pre = a0 + pltpu.roll(a1, LANES - 1, 0)
        hid = jax.nn.gelu(pre).astype(BF16)
        o_ref[0, 0] = jnp.dot(hid, w2_ref[idx], preferred_element_type=F32).astype(BF16)

    one(xk_ref, 0, ok_ref)
    one(xv_ref, 1, ov_ref)


def _compress(xk, xv, w1, pe, w2):
    B, G = xk.shape[:2]
    half = CMP_STRIDE * HEAD_DIM
    spec_x = pl.BlockSpec((1, 1, LANES, half), lambda b, g: (b, g, 0, 0))
    spec_o = pl.BlockSpec((1, 1, LANES, HEAD_DIM), lambda b, g: (b, g, 0, 0))
    out = jax.ShapeDtypeStruct((B, G, LANES, HEAD_DIM), BF16)
    return pl.pallas_call(
        _compress_body,
        out_shape=(out, out),
        grid=(B, G),
        in_specs=[spec_x, spec_x,
                  pl.BlockSpec((2, 2, half, HEAD_DIM), lambda b, g: (0, 0, 0, 0)),
                  pl.BlockSpec((2, 2, 1, half), lambda b, g: (0, 0, 0, 0)),
                  pl.BlockSpec((2, HEAD_DIM, HEAD_DIM), lambda b, g: (0, 0, 0))],
        out_specs=(spec_o, spec_o),
        compiler_params=pltpu.CompilerParams(dimension_semantics=("arbitrary", "arbitrary")),
        name="nsa_compress",
    )(xk, xv, w1, pe, w2)


def _nsa_body(q_ref, ks_ref, vs_ref, kw_ref, vw_ref, kc_ref, vc_ref, bc_ref, bt_ref, far_ref,
              ovl_ref, augs_ref, augw_ref, gl_ref, o_ref, kts_sc, ktw_sc, q4_sc, m_sc, acc_sc):
    R = HEADS_PER_GROUP
    tq = ATT_TILE
    qi = pl.program_id(2)
    n_tiles = ks_ref.shape[1] // tq
    nb = ovl_ref.shape[0]
    nt = (((1,), (1,)), ((), ()))

    @pl.when(qi == 0)
    def _():
        def tr(j, c):
            rows = pl.ds(pl.multiple_of(j * tq, tq), tq)
            kts_sc[j, :HEAD_DIM, :] = ks_ref[0, rows, :].astype(F32).T.astype(BF16)
            kts_sc[j, HEAD_DIM:, :] = augs_ref[j]
            ktw_sc[j, :HEAD_DIM, :] = kw_ref[0, rows, :].astype(F32).T.astype(BF16)
            ktw_sc[j, HEAD_DIM:, :] = augw_ref[0]
            return c

        lax.fori_loop(0, n_tiles, tr, 0)
        zeros = jnp.zeros((HEAD_DIM, tq), BF16)
        kts_sc[n_tiles, :HEAD_DIM, :] = zeros
        kts_sc[n_tiles, HEAD_DIM:, :] = augs_ref[n_tiles]
        ktw_sc[n_tiles, :HEAD_DIM, :] = zeros
        ktw_sc[n_tiles, HEAD_DIM:, :] = augw_ref[1]

    q = q_ref[0]
    q4 = jnp.concatenate([q[:, r * HEAD_DIM:(r + 1) * HEAD_DIM] for r in range(R)], axis=0)
    q4_sc[:, :HEAD_DIM] = q4

    pos3 = qi * tq + lax.broadcasted_iota(jnp.int32, (1, tq, 1), 1)

    sc = lax.dot_general(q4, kc_ref[0, 0], nt, preferred_element_type=F32)
    sc = sc.reshape(R, tq, LANES) + bc_ref[...]
    cidx = lax.broadcasted_iota(jnp.int32, (1, 1, LANES), 2)
    valid = (cidx * CMP_STRIDE + (CMP_BLOCK - 1) <= pos3) & (cidx < LANES - 1)
    sc = jnp.where(valid, sc, NEG_INF)
    mc = jnp.max(sc, axis=-1, keepdims=True)
    ec = jnp.exp2(sc - mc)
    pc = ec / jnp.sum(ec, axis=-1, keepdims=True)
    pc = jnp.where(pos3 >= CMP_BLOCK - 1, pc, 0.0)
    o_cmp = jnp.dot(pc.reshape(R * tq, LANES).astype(BF16), vc_ref[0, 0],
                    preferred_element_type=F32)

    psum = pc[0]
    for r in range(1, R):
        psum = psum + pc[r]
    p_hi = psum.astype(BF16)
    p_lo = (psum - p_hi.astype(F32)).astype(BF16)
    ovt = ovl_ref[...]
    imp = (lax.dot_general(ovt, p_hi, nt, preferred_element_type=F32)
           + lax.dot_general(ovt, p_lo, nt, preferred_element_type=F32))
    jb = lax.broadcasted_iota(jnp.int32, (nb, 1), 0)
    pos_t = qi * tq + lax.broadcasted_iota(jnp.int32, (1, tq), 1)
    q_blk = lax.shift_right_logical(pos_t, int(math.log2(SEL_BLOCK)))
    forced = (jb == 0) | (jb == q_blk) | (jb == q_blk - 1)
    future = jb > q_blk
    imp = jnp.where(forced, FORCE_SCORE, jnp.where(future, -1.0, imp))
    cnt = jnp.zeros((nb, tq), F32)
    for i in range(nb):
        row = imp[i:i + 1, :]
        beats = (row > imp) | ((row == imp) & (jb > i))
        cnt = cnt + jnp.where(beats, 1.0, 0.0)
    sel_t = jnp.where(cnt < float(min(SEL_TOP_N, nb)), 1.0, 0.0)
    sel = jnp.concatenate([sel_t, jnp.zeros((LANES - nb, tq), F32)], axis=0).T

    lane = lax.broadcasted_iota(jnp.int32, (1, LANES), 1)
    sel_pad = jnp.where(lane < nb, (sel - 1.0) * (-NEG_INF), 0.0)
    for r in range(R):
        far = jnp.broadcast_to(far_ref[r, 0:1, :], (tq, LANES))
        far_hi = far.astype(BF16).astype(F32)
        pad = jnp.where(lane == nb, far_hi, jnp.where(lane == nb + 1, far - far_hi, sel_pad))
        pad = jnp.where(lane == nb + 2, 1.0, pad)
        q4_sc[r * tq:(r + 1) * tq, HEAD_DIM:] = pad.astype(BF16)

    ones = jnp.ones((tq, HEAD_DIM), BF16)
    n_chunks = R * tq // ATT_ROWS
    chunks_per_head = tq // ATT_ROWS
    dead = n_tiles

    def keys(kt_sc, tiles):
        return jnp.concatenate([kt_sc[t] for t in tiles], axis=1)

    def values(v_ref, tiles):
        parts = []
        for t in tiles:
            rows = pl.ds(pl.multiple_of(t * tq, tq), tq)
            parts.append(jnp.concatenate([v_ref[0, rows, :], ones], axis=1))
        return jnp.concatenate(parts, axis=0)

    def logits(ci, kt, kinds):
        r, hh = divmod(ci, chunks_per_head)
        rs = slice(ci * ATT_ROWS, (ci + 1) * ATT_ROWS)
        qs = slice(hh * ATT_ROWS, (hh + 1) * ATT_ROWS)
        s = jnp.dot(q4_sc[rs, :], kt, preferred_element_type=F32)
        return s + jnp.concatenate([bt_ref[kd, r, qs, :] for kd in kinds], axis=1)

    def probs(s, m):
        return jnp.concatenate([jnp.exp2(s[:, k0:k0 + LANES] - m)
                                for k0 in range(0, s.shape[1], LANES)], axis=1).astype(BF16)

    m_sc[...] = jnp.full(m_sc.shape, NEG_INF, F32)
    acc_sc[...] = jnp.zeros(acc_sc.shape, F32)

    def pair(pi, c):
        ja = 2 * pi
        jb = ja + 1
        kt = keys(kts_sc, (ja, jnp.where(jb <= qi, jb, dead)))
        vv = values(vs_ref, (ja, jnp.minimum(jb, n_tiles - 1)))
        kinds = (jnp.clip(ja - qi + 2, 0, 2), jnp.clip(jb - qi + 2, 0, 2))
        for ci in range(n_chunks):
            rs = slice(ci * ATT_ROWS, (ci + 1) * ATT_ROWS)
            s = logits(ci, kt, kinds)
            m_old = m_sc[rs]
            m_new = jnp.maximum(m_old, jnp.max(s, axis=-1, keepdims=True))
            alpha = jnp.exp2(m_old - m_new)
            m_sc[rs] = m_new
            pv = jnp.dot(probs(s, m_new), vv, preferred_element_type=F32)
            acc_sc[rs] = acc_sc[rs] * jnp.concatenate([alpha, alpha], axis=1) + pv
        return c

    lax.fori_loop(0, (qi + 2) // 2, pair, 0)
    acc = acc_sc[...]
    o_sel = acc[:, :HEAD_DIM] / acc[:, HEAD_DIM:]

    n_win = WINDOW // tq
    w_tiles = [qi - n_win + t for t in range(n_win + 1)]
    kt = keys(ktw_sc, [jnp.where(t >= 0, t, dead) for t in w_tiles])
    vv = values(vw_ref, [jnp.maximum(t, 0) for t in w_tiles])
    w_kinds = (3,) + (0,) * (n_win - 2) + (1, 2)
    o_parts = []
    for ci in range(n_chunks):
        s = logits(ci, kt, w_kinds)
        pv = jnp.dot(probs(s, jnp.max(s, axis=-1, keepdims=True)), vv,
                     preferred_element_type=F32)
        o_parts.append(pv[:, :HEAD_DIM] / pv[:, HEAD_DIM:])
    o_win = jnp.concatenate(o_parts, axis=0)

    gates = jax.nn.sigmoid(gl_ref[0, 0])
    outs = []
    for r in range(R):
        hs = slice(r * tq, (r + 1) * tq)
        o_r = (gates[:, r:r + 1] * o_cmp[hs]
               + gates[:, R + r:R + r + 1] * o_sel[hs]
               + gates[:, 2 * R + r:2 * R + r + 1] * o_win[hs])
        outs.append(o_r)
    o_ref[0] = jnp.concatenate(outs, axis=1).astype(o_ref.dtype)


def _nsa_attention(proj, kcmp, vcmp, bias_c, bias_t, bias_far, gates_t, B, S):
    assert WINDOW % ATT_TILE == 0 and S % ATT_TILE == 0
    assert S // CMP_STRIDE == LANES and S // SEL_BLOCK <= LANES
    R, G, tq = HEADS_PER_GROUP, N_GROUPS, ATT_TILE
    n_tiles = S // tq
    nb = S // SEL_BLOCK
    _, _, overlap = _static_maps(S)
    overlap_t = np.ascontiguousarray(overlap.T[:nb])
    assert nb + 3 <= LANES and n_tiles % 2 == 0
    blk_of_key = (np.arange(S) // SEL_BLOCK).reshape(n_tiles, 1, tq)
    aug_sel = np.zeros((n_tiles + 1, LANES, tq), np.float32)
    aug_sel[:n_tiles] = np.arange(LANES).reshape(1, LANES, 1) == blk_of_key
    aug_sel[:n_tiles, nb:nb + 2, :] = 1.0
    aug_sel[n_tiles, nb + 2, :] = NEG_INF
    aug_win = np.zeros((2, LANES, tq), np.float32)
    aug_win[0, nb:nb + 2, :] = 1.0
    aug_win[1, nb + 2, :] = NEG_INF
    q_cols = N_HEADS

    def kv_spec(slot):
        return pl.BlockSpec((1, S, HEAD_DIM),
                            lambda b, g, i, slot=slot: (b, 0, q_cols + slot * G + g))

    cmp_spec = pl.BlockSpec((1, 1, LANES, HEAD_DIM), lambda b, g, i: (b, g, 0, 0))
    in_specs = [
        pl.BlockSpec((1, tq, R * HEAD_DIM), lambda b, g, i: (b, i, g)),
        kv_spec(2), kv_spec(3), kv_spec(4), kv_spec(5),
        cmp_spec, cmp_spec,
        pl.BlockSpec((R, tq, LANES), lambda b, g, i: (g, i, 0)),
        pl.BlockSpec((4, R, tq, tq), lambda b, g, i: (0, g, 0, 0)),
        pl.BlockSpec((R, SUBLANES, LANES), lambda b, g, i: (g, 0, 0)),
        pl.BlockSpec((nb, LANES), lambda b, g, i: (0, 0)),
        pl.BlockSpec((n_tiles + 1, LANES, tq), lambda b, g, i: (0, 0, 0)),
        pl.BlockSpec((2, LANES, tq), lambda b, g, i: (0, 0, 0)),
        pl.BlockSpec((1, 1, tq, 3 * R), lambda b, g, i: (b, g, i, 0)),
    ]
    kdim = HEAD_DIM + LANES
    est = (2 * 4 * S * HEAD_DIM * 2 + 2 * 4 * R * tq * tq * 4 + 2 * R * tq * LANES * 4
           + 2 * 2 * S * LANES * 2 + 2 * S * kdim * 2 + R * tq * (kdim * 2 + LANES * 4 + 2 * HEAD_DIM * 4)
           + 3 * R * tq * LANES * 4 + 16 * ATT_ROWS * tq * 4 + 4 * tq * R * HEAD_DIM * 2)
    return pl.pallas_call(
        _nsa_body,
        out_shape=jax.ShapeDtypeStruct((B, S, N_HEADS * HEAD_DIM), BF16),
        grid=(B, G, n_tiles),
        in_specs=in_specs,
        out_specs=pl.BlockSpec((1, tq, R * HEAD_DIM), lambda b, g, i: (b, i, g)),
        scratch_shapes=[pltpu.VMEM((n_tiles + 1, kdim, tq), BF16),
                        pltpu.VMEM((n_tiles + 1, kdim, tq), BF16),
                        pltpu.VMEM((R * tq, kdim), BF16),
                        pltpu.VMEM((R * tq, LANES), F32),
                        pltpu.VMEM((R * tq, 2 * HEAD_DIM), F32)],
        compiler_params=pltpu.CompilerParams(
            dimension_semantics=("arbitrary", "arbitrary", "arbitrary"),
            vmem_limit_bytes=_vmem_limit(est)),
        name="nsa_attention",
    )(proj, proj, proj, proj, proj, kcmp, vcmp, bias_c, bias_t, bias_far,
      jnp.asarray(overlap_t, BF16), jnp.asarray(aug_sel, BF16), jnp.asarray(aug_win, BF16),
      gates_t)


@functools.lru_cache(maxsize=None)
def _hgrn_masks():
    C = HGRN_CHUNK
    t = np.arange(C)[:, None]
    s = np.arange(C)[None, :]
    masks = [(t // HGRN_SUB == s // HGRN_SUB) & (s <= t)]
    half = C // 2
    while half >= HGRN_SUB:
        grp = 2 * half
        masks.append((t // grp == s // grp) & (t % grp >= half) & (s % grp < half))
        half //= 2
    assert np.array_equal(np.sum(masks, axis=0), (s <= t).astype(int))
    return np.stack(masks).astype(np.float32), (s <= t).astype(np.float32)


def _hgrn_body(q_ref, f_ref, i_ref, g_ref, lb_ref, gn_ref, msk_ref, tril_ref, o_ref,
               st_all, b_all, k_all, q_all, *, layer, heads):
    C = HGRN_CHUNK
    n_chunks = q_ref.shape[1] // C
    nt = (((1,), (1,)), ((), ()))

    lbp = lb_ref[...]
    e = jnp.exp(lbp - jnp.max(lbp, axis=0, keepdims=True))
    sm = e / jnp.sum(e, axis=0, keepdims=True)
    cum = sm[0:1]
    first = cum
    for d in range(1, layer + 1):
        cum = cum + sm[d:d + 1]
    lb_all = cum - first
    log_lb_all = jnp.log(lb_all)
    log_1m_all = jnp.log1p(-lb_all)
    gn_all = gn_ref[...]

    cols = lax.broadcasted_iota(jnp.int32, (1, C), 1)
    tril = tril_ref[...]

    st_all[...] = jnp.zeros(st_all.shape, F32)

    def chunk(c, carry):
        for hh in range(heads):
            one_head(c, hh)
        return carry

    def one_head(c, hh):
        sl = pl.ds(pl.multiple_of(c * C, C), C)
        hs = slice(hh * HEAD_DIM, (hh + 1) * HEAD_DIM)
        lb, log_lb, log_1m, gn = lb_all[:, hs], log_lb_all[:, hs], log_1m_all[:, hs], gn_all[:, hs]
        st_ref, b_sc, k_sc, q_sc = st_all.at[hh], b_all.at[hh], k_all.at[hh], q_all.at[hh]
        qr = q_ref[0, sl, hs]
        x = f_ref[0, sl, hs]
        v = i_ref[0, sl, hs]
        gr = g_ref[0, sl, hs]
        q = jax.nn.silu(qr)
        ex = jnp.exp(-jnp.abs(x))
        r1 = 1.0 / (1.0 + ex)
        k = (1.0 - lb) * jnp.where(x >= 0.0, ex * r1, r1)
        c2 = log_1m + (jnp.minimum(x, 0.0) - jnp.log1p(ex))
        log_f = jnp.maximum(log_lb, c2) + jnp.log1p(jnp.exp(-jnp.abs(log_lb - c2)))

        lf_hi = log_f.astype(BF16)
        lf_lo = (log_f - lf_hi.astype(F32)).astype(BF16)
        b = (jnp.dot(tril, lf_hi, preferred_element_type=F32)
             + jnp.dot(tril, lf_lo, preferred_element_type=F32)) * LOG2_E
        b_sc[...] = b
        k_sc[...] = k
        q_sc[...] = q

        vb = v.astype(BF16)
        st = st_ref[...]
        o = lax.dot_general((q * jnp.exp2(b)).astype(BF16), st.astype(BF16), nt,
                            preferred_element_type=F32)

        pieces = []
        for blk in range(C // HGRN_SUB):
            r0 = blk * HGRN_SUB
            bt = b_sc[r0:r0 + HGRN_SUB, :]
            qt = q_sc[r0:r0 + HGRN_SUB, :]
            arow = jnp.zeros((HGRN_SUB, C), F32)
            for s in range(HGRN_SUB):
                bs = b_sc[r0 + s:r0 + s + 1, :]
                ks = k_sc[r0 + s:r0 + s + 1, :]
                col = jnp.sum(jnp.exp2(bt - bs) * qt * ks, axis=-1, keepdims=True)
                arow = jnp.where(cols == r0 + s, col, arow)
            pieces.append(arow)
        a = jnp.where(msk_ref[0] > 0.5, jnp.concatenate(pieces, axis=0), 0.0)

        half = C // 2
        lvl = 1
        while half >= HGRN_SUB:
            grp = 2 * half
            anc = jnp.concatenate(
                [jnp.broadcast_to(b_sc[g0 + half - 1:g0 + half, :], (grp, b.shape[1]))
                 for g0 in range(0, C, grp)], axis=0)
            e = jnp.exp2(-jnp.abs(b - anc))
            al = lax.dot_general((q * e).astype(BF16), (k * e).astype(BF16), nt,
                                 preferred_element_type=F32)
            a = jnp.where(msk_ref[lvl] > 0.5, al, a)
            half //= 2
            lvl += 1

        o = o + jnp.dot(a.astype(BF16), vb, preferred_element_type=F32)

        b_last = b_sc[C - 1:C, :]
        kh = (k * jnp.exp2(b_last - b)).astype(BF16)
        st_ref[...] = st * jnp.exp2(b_last) + jnp.dot(v.T.astype(BF16), kh,
                                                     preferred_element_type=F32)

        ms = jnp.mean(o * o, axis=-1, keepdims=True)
        o = o * lax.rsqrt(ms + RMS_EPS) * gn * jax.nn.silu(gr)
        o_ref[0, sl, hs] = o.astype(o_ref.dtype)

    lax.fori_loop(0, n_chunks, chunk, 0)


def _hgrn(proj, hgrn_lb, onorm, layer, B, S):
    H = N_HEADS
    depth = hgrn_lb.shape[0]

    nh = HGRN_HEADS_PER_STEP
    width = nh * HEAD_DIM
    steps = H // nh

    def spec(part):
        return pl.BlockSpec((1, S, width), lambda b, h, part=part: (b, 0, part * steps + h))

    masks, tril = _hgrn_masks()
    C = HGRN_CHUNK
    est = 2 * 4 * S * width * 4 + 2 * S * width * 2 + nh * 64 * C * C * 4
    return pl.pallas_call(
        functools.partial(_hgrn_body, layer=layer, heads=nh),
        out_shape=jax.ShapeDtypeStruct((B, S, H * HEAD_DIM), BF16),
        grid=(B, steps),
        in_specs=[spec(0), spec(1), spec(2), spec(3),
                  pl.BlockSpec((depth, width), lambda b, h: (0, h)),
                  pl.BlockSpec((1, width), lambda b, h: (0, h)),
                  pl.BlockSpec(masks.shape, lambda b, h: (0, 0, 0)),
                  pl.BlockSpec((C, C), lambda b, h: (0, 0))],
        out_specs=pl.BlockSpec((1, S, width), lambda b, h: (b, 0, h)),
        scratch_shapes=[pltpu.VMEM((nh, HEAD_DIM, HEAD_DIM), F32),
                        pltpu.VMEM((nh, C, HEAD_DIM), F32),
                        pltpu.VMEM((nh, C, HEAD_DIM), F32),
                        pltpu.VMEM((nh, C, HEAD_DIM), F32)],
        compiler_params=pltpu.CompilerParams(
            dimension_semantics=("arbitrary", "arbitrary"),
            vmem_limit_bytes=_vmem_limit(est)),
        name="hgrn2_recurrence",
    )(proj, proj, proj, proj, hgrn_lb.astype(F32), onorm.reshape(1, -1).astype(F32),
      jnp.asarray(masks), jnp.asarray(tril, BF16))


def _mlp(xf, g_in, g_out, w_up, w_down):
    hid = _matmul(xf, w_up.astype(BF16), tm=1024, tn=1024, norm_g=g_in, epi="relu2",
                  out_dtype=BF16, name="mlp_up")
    return _matmul(hid, w_down.astype(BF16), tm=512, tn=w_down.shape[1], tk=1024,
                   epi="resnorm", res=xf, res_g=g_out, out_dtype=F32, name="mlp_down")


def _nsa_layer(xf, B, S, g_in, g_out, rel_table, w_in, cmp_pe, cmp_w1, cmp_w2, w_out):
    D = xf.shape[1]
    G, R, Dh = N_GROUPS, HEADS_PER_GROUP, HEAD_DIM
    n_main = N_HEADS * Dh + 6 * G * Dh
    n_gate = 3 * N_HEADS
    w_main = w_in[:, :n_main].astype(BF16)
    w_gate = jnp.pad(w_in[:, n_main:], ((0, 0), (0, LANES - n_gate))).astype(BF16)
    colscale = jnp.concatenate([jnp.full((N_HEADS * Dh,), Dh ** -0.5 * LOG2_E, F32),
                                jnp.ones((6 * G * Dh,), F32)])[None]

    proj = _matmul(xf, w_main, tm=1024, tn=1024, norm_g=g_in, epi="colscale",
                   colscale=colscale, out_dtype=BF16, name="nsa_proj")
    glog = _matmul(xf, w_gate, tm=1024, tn=LANES, norm_g=g_in, out_dtype=F32,
                   name="nsa_gate_proj")
    gates_t = (glog[:, :n_gate].reshape(B, S, 3, G, R).transpose(0, 3, 1, 2, 4)
               .reshape(B, G, S, 3 * R))

    proj3 = proj.reshape(B, S, n_main)
    nh = S // CMP_STRIDE

    def grouped(slot):
        c0 = N_HEADS * Dh + slot * G * Dh
        a = proj3[:, :, c0:c0 + G * Dh].reshape(B, nh, CMP_STRIDE, G, Dh)
        return a.transpose(0, 3, 1, 2, 4).reshape(B, G, nh, CMP_STRIDE * Dh)

    half = CMP_STRIDE * Dh
    kcmp, vcmp = _compress(grouped(0), grouped(1),
                           cmp_w1.reshape(2, 2, half, Dh).astype(BF16),
                           cmp_pe.reshape(2, 2, 1, half).astype(F32),
                           cmp_w2.astype(BF16))

    bias_c, bias_t, bias_far = _bias_tables(rel_table, S)
    attn = _nsa_attention(proj3, kcmp, vcmp, bias_c, bias_t, bias_far, gates_t, B, S)
    return _matmul(attn.reshape(B * S, D), w_out.astype(BF16), tm=512, tn=D,
                   epi="resnorm", res=xf, res_g=g_out, out_dtype=F32, name="nsa_out")


def _hgrn_layer(xf, B, S, layer, g_in, g_out, w_in, hgrn_lb, onorm, w_out):
    D = xf.shape[1]
    proj = _matmul(xf, w_in.astype(BF16), tm=1024, tn=1024, norm_g=g_in,
                   out_dtype=F32, name="hgrn_proj")
    mixed = _hgrn(proj.reshape(B, S, 4 * D), hgrn_lb, onorm, layer, B, S)
    return _matmul(mixed.reshape(B * S, D), w_out.astype(BF16), tm=512, tn=D,
                   epi="resnorm", res=xf, res_g=g_out, out_dtype=F32, name="hgrn_out")


def kernel(x, norm_g, rel_table, nsa_w_in, nsa_cmp_pe, nsa_cmp_w1, nsa_cmp_w2, nsa_w_out,
           hgrn_w_in, hgrn_lb, hgrn_onorm, hgrn_w_out, mlp_w_up, mlp_w_down):
    B, S, D = x.shape
    depth = norm_g.shape[0]
    assert D == N_HEADS * HEAD_DIM and S % ATT_TILE == 0 and S % HGRN_CHUNK == 0
    xf = x.reshape(B * S, D).astype(F32)
    for layer in range(depth):
        j = layer // 2
        if layer % 2 == 0:
            xf = _nsa_layer(xf, B, S, norm_g[layer, 0], norm_g[layer, 1], rel_table,
                            nsa_w_in[j], nsa_cmp_pe[j], nsa_cmp_w1[j], nsa_cmp_w2[j],
                            nsa_w_out[j])
        else:
            xf = _hgrn_layer(xf, B, S, layer, norm_g[layer, 0], norm_g[layer, 1],
                             hgrn_w_in[j], hgrn_lb, hgrn_onorm[j], hgrn_w_out[j])
        xf = _mlp(xf, norm_g[layer, 2], norm_g[layer, 3], mlp_w_up[layer], mlp_w_down[layer])
    return xf.reshape(B, S, D).astype(x.dtype)
```

```python
import functools
import math

import numpy as np
import jax
import jax.numpy as jnp
from jax import lax
from jax.experimental import pallas as pl
from jax.experimental.pallas import tpu as pltpu

F32 = jnp.float32
BF16 = jnp.bfloat16

N_HEADS = 16
N_GROUPS = 4
HEADS_PER_GROUP = N_HEADS // N_GROUPS
HEAD_DIM = 128
CMP_BLOCK = 32
CMP_STRIDE = 16
SEL_BLOCK = 64
SEL_TOP_N = 8
WINDOW = 512
FORCE_SCORE = 1.0e4
REL_BUCKETS = 32
REL_MAX_DIST = 128
RMS_EPS = 1e-6
NEG_INF = -1.0e30
LOG2_E = math.log2(math.e)

LANES = 128
SUBLANES = 8
VMEM_BYTES_V7X = 64 * 1024 * 1024
VMEM_LIMIT_CAP = VMEM_BYTES_V7X - 8 * 1024 * 1024

ATT_TILE = 256
ATT_ROWS = 128
HGRN_CHUNK = 128
HGRN_SUB = 8
HGRN_HEADS_PER_STEP = 4
NORM_ROWS = 256


def _vmem_limit(nbytes):
    return int(min(VMEM_LIMIT_CAP, max(32 * 1024 * 1024, nbytes)))


def _mm_body(*refs, norm, epi, nk, tm):
    it = iter(refs)
    x_ref = next(it)
    g_ref = next(it) if norm else None
    w_ref = next(it)
    cs_ref = next(it) if epi == "colscale" else None
    res_ref = next(it) if epi == "resnorm" else None
    go_ref = next(it) if epi == "resnorm" else None
    o_ref = next(it)
    hn_ref = next(it) if norm else None
    acc_ref = next(it) if nk > 1 else None

    j = pl.program_id(1)
    k = pl.program_id(2)

    if norm:
        @pl.when(j == 0)
        def _():
            g = g_ref[...]

            def step(r, c):
                rows = pl.ds(pl.multiple_of(r * NORM_ROWS, NORM_ROWS), NORM_ROWS)
                xs = x_ref[rows, :]
                ms = jnp.mean(xs * xs, axis=-1, keepdims=True)
                hn_ref[rows, :] = (xs * lax.rsqrt(ms + RMS_EPS) * g).astype(BF16)
                return c

            lax.fori_loop(0, tm // NORM_ROWS, step, 0)

        lhs = hn_ref[...]
    else:
        lhs = x_ref[...]

    part = jnp.dot(lhs, w_ref[...], preferred_element_type=F32)

    def finish(acc):
        if epi == "colscale":
            acc = acc * cs_ref[...]
        elif epi == "relu2":
            acc = jnp.square(jnp.maximum(acc, 0.0))
        elif epi == "resnorm":
            ms = jnp.mean(acc * acc, axis=-1, keepdims=True)
            acc = res_ref[...] + acc * lax.rsqrt(ms + RMS_EPS) * go_ref[...]
        o_ref[...] = acc.astype(o_ref.dtype)

    if nk == 1:
        finish(part)
    else:
        @pl.when(k == 0)
        def _():
            acc_ref[...] = part

        @pl.when(k > 0)
        def _():
            acc_ref[...] += part

        @pl.when(k == nk - 1)
        def _():
            finish(acc_ref[...])


def _matmul(x, w, *, tm, tn, tk=None, norm_g=None, epi="none", colscale=None,
            res=None, res_g=None, out_dtype=BF16, name="mm"):
    M, K = x.shape
    N = w.shape[1]
    norm = norm_g is not None
    tk = K if tk is None else tk
    nk = K // tk
    assert M % tm == 0 and N % tn == 0 and K % tk == 0
    assert not (norm and nk != 1)
    assert not (epi == "resnorm" and tn != N)

    in_specs = [pl.BlockSpec((tm, tk), lambda i, j, k: (i, k))]
    args = [x]
    if norm:
        in_specs.append(pl.BlockSpec((1, K), lambda i, j, k: (0, 0)))
        args.append(norm_g.reshape(1, K).astype(F32))
    in_specs.append(pl.BlockSpec((tk, tn), lambda i, j, k: (k, j)))
    args.append(w)
    if epi == "colscale":
        in_specs.append(pl.BlockSpec((1, tn), lambda i, j, k: (0, j)))
        args.append(colscale)
    if epi == "resnorm":
        in_specs.append(pl.BlockSpec((tm, tn), lambda i, j, k: (i, j)))
        args.append(res)
        in_specs.append(pl.BlockSpec((1, tn), lambda i, j, k: (0, j)))
        args.append(res_g.reshape(1, N).astype(F32))

    scratch = []
    if norm:
        scratch.append(pltpu.VMEM((tm, K), BF16))
    if nk > 1:
        scratch.append(pltpu.VMEM((tm, tn), F32))

    xb = x.dtype.itemsize
    ob = jnp.dtype(out_dtype).itemsize
    est = (2 * tm * tk * xb + 2 * tk * tn * 2 + 2 * tm * tn * ob
           + (tm * K * 2 if norm else 0) + (tm * tn * 4 if nk > 1 else 0)
           + (2 * tm * tn * 4 if epi == "resnorm" else 0)
           + 3 * tm * tn * 4)

    return pl.pallas_call(
        functools.partial(_mm_body, norm=norm, epi=epi, nk=nk, tm=tm),
        out_shape=jax.ShapeDtypeStruct((M, N), out_dtype),
        grid=(M // tm, N // tn, nk),
        in_specs=in_specs,
        out_specs=pl.BlockSpec((tm, tn), lambda i, j, k: (i, j)),
        scratch_shapes=scratch,
        compiler_params=pltpu.CompilerParams(
            dimension_semantics=("arbitrary", "arbitrary", "arbitrary"),
            vmem_limit_bytes=_vmem_limit(est)),
        name=name,
    )(*args)


def _rel_bucket_np(dist):
    n = np.maximum(dist, 0)
    max_exact = REL_BUCKETS // 2
    nf = np.maximum(n, 1).astype(np.float32)
    ratio = np.log(nf / np.float32(max_exact)) / np.float32(math.log(REL_MAX_DIST / max_exact))
    large = max_exact + (ratio * np.float32(REL_BUCKETS - max_exact)).astype(np.int32)
    large = np.minimum(large, REL_BUCKETS - 1)
    return np.where(n < max_exact, n, large).astype(np.int32)


@functools.lru_cache(maxsize=None)
def _static_maps(seq):
    n_cmp = LANES
    pos = np.arange(seq, dtype=np.int32)[:, None]
    c_end = np.arange(n_cmp, dtype=np.int32)[None, :] * CMP_STRIDE + CMP_BLOCK - 1
    bucket_c = _rel_bucket_np(pos - c_end)
    t = np.arange(ATT_TILE, dtype=np.int32)[:, None]
    k = np.arange(ATT_TILE, dtype=np.int32)[None, :]
    bucket_t = np.stack([_rel_bucket_np(t - k), _rel_bucket_np(ATT_TILE + t - k)])
    assert _rel_bucket_np(np.array([ATT_TILE + 1]))[0] == REL_BUCKETS - 1
    nc = seq // CMP_STRIDE - CMP_BLOCK // CMP_STRIDE + 1
    nb = seq // SEL_BLOCK
    c_start = np.arange(nc)[:, None] * CMP_STRIDE
    b_start = np.arange(nb)[None, :] * SEL_BLOCK
    ov = ((c_start <= b_start + SEL_BLOCK - 1) & (c_start + CMP_BLOCK - 1 >= b_start))
    overlap = np.zeros((LANES, LANES), np.float32)
    overlap[:nc, :nb] = ov
    return bucket_c, bucket_t, overlap


def _bias_body(tab_ref, bc_ref, bt_ref, oc_ref, ot_ref, of_ref, *, seq):
    h = pl.program_id(0)

    def lookup(bmap):
        acc = jnp.zeros(bmap.shape, F32)
        for b in range(REL_BUCKETS):
            acc = jnp.where(bmap == b, tab_ref[b, h], acc)
        return acc

    def step(r, c):
        rows = pl.ds(pl.multiple_of(r * ATT_TILE, ATT_TILE), ATT_TILE)
        oc_ref[0, rows, :] = lookup(bc_ref[rows, :]) * LOG2_E
        return c

    lax.fori_loop(0, seq // ATT_TILE, step, 0)

    tt = lax.broadcasted_iota(jnp.int32, (ATT_TILE, ATT_TILE), 0)
    kk = lax.broadcasted_iota(jnp.int32, (ATT_TILE, ATT_TILE), 1)
    far = tab_ref[REL_BUCKETS - 1, h]
    of_ref[0] = jnp.full(of_ref.shape[1:], far * LOG2_E, F32)
    ot_ref[0, 0] = jnp.zeros((ATT_TILE, ATT_TILE), F32)
    ot_ref[1, 0] = (lookup(bt_ref[1]) - far) * LOG2_E
    ot_ref[2, 0] = jnp.where(kk <= tt, (lookup(bt_ref[0]) - far) * LOG2_E, NEG_INF)
    ot_ref[3, 0] = jnp.where(kk > tt, 0.0, NEG_INF)


def _bias_tables(rel_table, seq):
    bucket_c, bucket_t, _ = _static_maps(seq)
    return pl.pallas_call(
        functools.partial(_bias_body, seq=seq),
        out_shape=(jax.ShapeDtypeStruct((N_HEADS, seq, LANES), F32),
                   jax.ShapeDtypeStruct((4, N_HEADS, ATT_TILE, ATT_TILE), F32),
                   jax.ShapeDtypeStruct((N_HEADS, SUBLANES, LANES), F32)),
        grid=(N_HEADS,),
        in_specs=[pl.BlockSpec(memory_space=pltpu.SMEM),
                  pl.BlockSpec((seq, LANES), lambda h: (0, 0)),
                  pl.BlockSpec((2, ATT_TILE, ATT_TILE), lambda h: (0, 0, 0))],
        out_specs=(pl.BlockSpec((1, seq, LANES), lambda h: (h, 0, 0)),
                   pl.BlockSpec((4, 1, ATT_TILE, ATT_TILE), lambda h: (0, h, 0, 0)),
                   pl.BlockSpec((1, SUBLANES, LANES), lambda h: (h, 0, 0))),
        compiler_params=pltpu.CompilerParams(dimension_semantics=("arbitrary",)),
        name="rel_bias",
    )(rel_table.astype(F32), jnp.asarray(bucket_c), jnp.asarray(bucket_t))


def _compress_body(xk_ref, xv_ref, w1_ref, pe_ref, w2_ref, ok_ref, ov_ref):
    def one(x_ref, idx, o_ref):
        x = x_ref[0, 0].astype(F32)
        a0 = jnp.dot((x + pe_ref[idx, 0]).astype(BF16), w1_ref[idx, 0],
                     preferred_element_type=F32)
        a1 = jnp.dot((x + pe_ref[idx, 1]).astype(BF16), w1_ref[idx, 1],
                     preferred_element_type=F32)
        pre = a0 + pltpu.roll(a1, LANES - 1, 0)
        hid = jax.nn.gelu(pre).astype(BF16)
        o_ref[0, 0] = jnp.dot(hid, w2_ref[idx], preferred_element_type=F32).astype(BF16)

    one(xk_ref, 0, ok_ref)
    one(xv_ref, 1, ov_ref)


def _compress(xk, xv, w1, pe, w2):
    B, G = xk.shape[:2]
    half = CMP_STRIDE * HEAD_DIM
    spec_x = pl.BlockSpec((1, 1, LANES, half), lambda b, g: (b, g, 0, 0))
    spec_o = pl.BlockSpec((1, 1, LANES, HEAD_DIM), lambda b, g: (b, g, 0, 0))
    out = jax.ShapeDtypeStruct((B, G, LANES, HEAD_DIM), BF16)
    return pl.pallas_call(
        _compress_body,
        out_shape=(out, out),
        grid=(B, G),
        in_specs=[spec_x, spec_x,
                  pl.BlockSpec((2, 2, half, HEAD_DIM), lambda b, g: (0, 0, 0, 0)),
                  pl.BlockSpec((2, 2, 1, half), lambda b, g: (0, 0, 0, 0)),
                  pl.BlockSpec((2, HEAD_DIM, HEAD_DIM), lambda b, g: (0, 0, 0))],
        out_specs=(spec_o, spec_o),
        compiler_params=pltpu.CompilerParams(dimension_semantics=("arbitrary", "arbitrary")),
        name="nsa_compress",
    )(xk, xv, w1, pe, w2)


def _nsa_body(q_ref, ks_ref, vs_ref, kw_ref, vw_ref, kc_ref, vc_ref, bc_ref, bt_ref, far_ref,
              ovl_ref, augs_ref, augw_ref, gl_ref, o_ref, kts_sc, ktw_sc, q4_sc, m_sc, acc_sc):
    R = HEADS_PER_GROUP
    tq = ATT_TILE
    qi = pl.program_id(2)
    n_tiles = ks_ref.shape[1] // tq
    nb = ovl_ref.shape[0]
    nt = (((1,), (1,)), ((), ()))

    @pl.when(qi == 0)
    def _():
        def tr(j, c):
            rows = pl.ds(pl.multiple_of(j * tq, tq), tq)
            kts_sc[j, :HEAD_DIM, :] = ks_ref[0, rows, :].astype(F32).T.astype(BF16)
            kts_sc[j, HEAD_DIM:, :] = augs_ref[j]
            ktw_sc[j, :HEAD_DIM, :] = kw_ref[0, rows, :].astype(F32).T.astype(BF16)
            ktw_sc[j, HEAD_DIM:, :] = augw_ref[0]
            return c

        lax.fori_loop(0, n_tiles, tr, 0)
        zeros = jnp.zeros((HEAD_DIM, tq), BF16)
        kts_sc[n_tiles, :HEAD_DIM, :] = zeros
        kts_sc[n_tiles, HEAD_DIM:, :] = augs_ref[n_tiles]
        ktw_sc[n_tiles, :HEAD_DIM, :] = zeros
        ktw_sc[n_tiles, HEAD_DIM:, :] = augw_ref[1]

    q = q_ref[0]
    q4 = jnp.concatenate([q[:, r * HEAD_DIM:(r + 1) * HEAD_DIM] for r in range(R)], axis=0)
    q4_sc[:, :HEAD_DIM] = q4

    pos3 = qi * tq + lax.broadcasted_iota(jnp.int32, (1, tq, 1), 1)

    sc = lax.dot_general(q4, kc_ref[0, 0], nt, preferred_element_type=F32)
    sc = sc.reshape(R, tq, LANES) + bc_ref[...]
    cidx = lax.broadcasted_iota(jnp.int32, (1, 1, LANES), 2)
    valid = (cidx * CMP_STRIDE + (CMP_BLOCK - 1) <= pos3) & (cidx < LANES - 1)
    sc = jnp.where(valid, sc, NEG_INF)
    mc = jnp.max(sc, axis=-1, keepdims=True)
    ec = jnp.exp2(sc - mc)
    pc = ec / jnp.sum(ec, axis=-1, keepdims=True)
    pc = jnp.where(pos3 >= CMP_BLOCK - 1, pc, 0.0)
    o_cmp = jnp.dot(pc.reshape(R * tq, LANES).astype(BF16), vc_ref[0, 0],
                    preferred_element_type=F32)

    psum = pc[0]
    for r in range(1, R):
        psum = psum + pc[r]
    p_hi = psum.astype(BF16)
    p_lo = (psum - p_hi.astype(F32)).astype(BF16)
    ovt = ovl_ref[...]
    imp = (lax.dot_general(ovt, p_hi, nt, preferred_element_type=F32)
           + lax.dot_general(ovt, p_lo, nt, preferred_element_type=F32))
    jb = lax.broadcasted_iota(jnp.int32, (nb, 1), 0)
    pos_t = qi * tq + lax.broadcasted_iota(jnp.int32, (1, tq), 1)
    q_blk = lax.shift_right_logical(pos_t, int(math.log2(SEL_BLOCK)))
    forced = (jb == 0) | (jb == q_blk) | (jb == q_blk - 1)
    future = jb > q_blk
    imp = jnp.where(forced, FORCE_SCORE, jnp.where(future, -1.0, imp))
    cnt = jnp.zeros((nb, tq), F32)
    for i in range(nb):
        row = imp[i:i + 1, :]
        beats = (row > imp) | ((row == imp) & (jb > i))
        cnt = cnt + jnp.where(beats, 1.0, 0.0)
    sel_t = jnp.where(cnt < float(min(SEL_TOP_N, nb)), 1.0, 0.0)
    sel = jnp.concatenate([sel_t, jnp.zeros((LANES - nb, tq), F32)], axis=0).T

    lane = lax.broadcasted_iota(jnp.int32, (1, LANES), 1)
    sel_pad = jnp.where(lane < nb, (sel - 1.0) * (-NEG_INF), 0.0)
    for r in range(R):
        far = jnp.broadcast_to(far_ref[r, 0:1, :], (tq, LANES))
        far_hi = far.astype(BF16).astype(F32)
        pad = jnp.where(lane == nb, far_hi, jnp.where(lane == nb + 1, far - far_hi, sel_pad))
        pad = jnp.where(lane == nb + 2, 1.0, pad)
        q4_sc[r * tq:(r + 1) * tq, HEAD_DIM:] = pad.astype(BF16)

    ones = jnp.ones((tq, HEAD_DIM), BF16)
    n_chunks = R * tq // ATT_ROWS
    chunks_per_head = tq // ATT_ROWS
    dead = n_tiles

    def keys(kt_sc, tiles):
        return jnp.concatenate([kt_sc[t] for t in tiles], axis=1)

    def values(v_ref, tiles):
        parts = []
        for t in tiles:
            rows = pl.ds(pl.multiple_of(t * tq, tq), tq)
            parts.append(jnp.concatenate([v_ref[0, rows, :], ones], axis=1))
        return jnp.concatenate(parts, axis=0)

    def logits(ci, kt, kinds):
        r, hh = divmod(ci, chunks_per_head)
        rs = slice(ci * ATT_ROWS, (ci + 1) * ATT_ROWS)
        qs = slice(hh * ATT_ROWS, (hh + 1) * ATT_ROWS)
        s = jnp.dot(q4_sc[rs, :], kt, preferred_element_type=F32)
        return s + jnp.concatenate([bt_ref[kd, r, qs, :] for kd in kinds], axis=1)

    def probs(s, m):
        return jnp.concatenate([jnp.exp2(s[:, k0:k0 + LANES] - m)
                                for k0 in range(0, s.shape[1], LANES)], axis=1).astype(BF16)

    m_sc[...] = jnp.full(m_sc.shape, NEG_INF, F32)
    acc_sc[...] = jnp.zeros(acc_sc.shape, F32)

    def pair(pi, c):
        ja = 2 * pi
        jb = ja + 1
        kt = keys(kts_sc, (ja, jnp.where(jb <= qi, jb, dead)))
        vv = values(vs_ref, (ja, jnp.minimum(jb, n_tiles - 1)))
        kinds = (jnp.clip(ja - qi + 2, 0, 2), jnp.clip(jb - qi + 2, 0, 2))
        for ci in range(n_chunks):
            rs = slice(ci * ATT_ROWS, (ci + 1) * ATT_ROWS)
            s = logits(ci, kt, kinds)
            m_old = m_sc[rs]
            m_new = jnp.maximum(m_old, jnp.max(s, axis=-1, keepdims=True))
            alpha = jnp.exp2(m_old - m_new)
            m_sc[rs] = m_new
            pv = jnp.dot(probs(s, m_new), vv, preferred_element_type=F32)
            acc_sc[rs] = acc_sc[rs] * jnp.concatenate([alpha, alpha], axis=1) + pv
        return c

    lax.fori_loop(0, (qi + 2) // 2, pair, 0)
    acc = acc_sc[...]
    o_sel = acc[:, :HEAD_DIM] / acc[:, HEAD_DIM:]

    n_win = WINDOW // tq
    w_tiles = [qi - n_win + t for t in range(n_win + 1)]
    kt = keys(ktw_sc, [jnp.where(t >= 0, t, dead) for t in w_tiles])
    vv = values(vw_ref, [jnp.maximum(t, 0) for t in w_tiles])
    w_kinds = (3,) + (0,) * (n_win - 2) + (1, 2)
    o_parts = []
    for ci in range(n_chunks):
        s = logits(ci, kt, w_kinds)
        pv = jnp.dot(probs(s, jnp.max(s, axis=-1, keepdims=True)), vv,
                     preferred_element_type=F32)
        o_parts.append(pv[:, :HEAD_DIM] / pv[:, HEAD_DIM:])
    o_win = jnp.concatenate(o_parts, axis=0)

    gates = jax.nn.sigmoid(gl_ref[0, 0])
    outs = []
    for r in range(R):
        hs = slice(r * tq, (r + 1) * tq)
        o_r = (gates[:, r:r + 1] * o_cmp[hs]
               + gates[:, R + r:R + r + 1] * o_sel[hs]
               + gates[:, 2 * R + r:2 * R + r + 1] * o_win[hs])
        outs.append(o_r)
    o_ref[0] = jnp.concatenate(outs, axis=1).astype(o_ref.dtype)


def _nsa_attention(proj, kcmp, vcmp, bias_c, bias_t, bias_far, gates_t, B, S):
    assert WINDOW % ATT_TILE == 0 and S % ATT_TILE == 0
    assert S // CMP_STRIDE == LANES and S // SEL_BLOCK <= LANES
    R, G, tq = HEADS_PER_GROUP, N_GROUPS, ATT_TILE
    n_tiles = S // tq
    nb = S // SEL_BLOCK
    _, _, overlap = _static_maps(S)
    overlap_t = np.ascontiguousarray(overlap.T[:nb])
    assert nb + 3 <= LANES and n_tiles % 2 == 0
    blk_of_key = (np.arange(S) // SEL_BLOCK).reshape(n_tiles, 1, tq)
    aug_sel = np.zeros((n_tiles + 1, LANES, tq), np.float32)
    aug_sel[:n_tiles] = np.arange(LANES).reshape(1, LANES, 1) == blk_of_key
    aug_sel[:n_tiles, nb:nb + 2, :] = 1.0
    aug_sel[n_tiles, nb + 2, :] = NEG_INF
    aug_win = np.zeros((2, LANES, tq), np.float32)
    aug_win[0, nb:nb + 2, :] = 1.0
    aug_win[1, nb + 2, :] = NEG_INF
    q_cols = N_HEADS

    def kv_spec(slot):
        return pl.BlockSpec((1, S, HEAD_DIM),
                            lambda b, g, i, slot=slot: (b, 0, q_cols + slot * G + g))

    cmp_spec = pl.BlockSpec((1, 1, LANES, HEAD_DIM), lambda b, g, i: (b, g, 0, 0))
    in_specs = [
        pl.BlockSpec((1, tq, R * HEAD_DIM), lambda b, g, i: (b, i, g)),
        kv_spec(2), kv_spec(3), kv_spec(4), kv_spec(5),
        cmp_spec, cmp_spec,
        pl.BlockSpec((R, tq, LANES), lambda b, g, i: (g, i, 0)),
        pl.BlockSpec((4, R, tq, tq), lambda b, g, i: (0, g, 0, 0)),
        pl.BlockSpec((R, SUBLANES, LANES), lambda b, g, i: (g, 0, 0)),
        pl.BlockSpec((nb, LANES), lambda b, g, i: (0, 0)),
        pl.BlockSpec((n_tiles + 1, LANES, tq), lambda b, g, i: (0, 0, 0)),
        pl.BlockSpec((2, LANES, tq), lambda b, g, i: (0, 0, 0)),
        pl.BlockSpec((1, 1, tq, 3 * R), lambda b, g, i: (b, g, i, 0)),
    ]
    kdim = HEAD_DIM + LANES
    est = (2 * 4 * S * HEAD_DIM * 2 + 2 * 4 * R * tq * tq * 4 + 2 * R * tq * LANES * 4
           + 2 * 2 * S * LANES * 2 + 2 * S * kdim * 2 + R * tq * (kdim * 2 + LANES * 4 + 2 * HEAD_DIM * 4)
           + 3 * R * tq * LANES * 4 + 16 * ATT_ROWS * tq * 4 + 4 * tq * R * HEAD_DIM * 2)
    return pl.pallas_call(
        _nsa_body,
        out_shape=jax.ShapeDtypeStruct((B, S, N_HEADS * HEAD_DIM), BF16),
        grid=(B, G, n_tiles),
        in_specs=in_specs,
        out_specs=pl.BlockSpec((1, tq, R * HEAD_DIM), lambda b, g, i: (b, i, g)),
        scratch_shapes=[pltpu.VMEM((n_tiles + 1, kdim, tq), BF16),
                        pltpu.VMEM((n_tiles + 1, kdim, tq), BF16),
                        pltpu.VMEM((R * tq, kdim), BF16),
                        pltpu.VMEM((R * tq, LANES), F32),
                        pltpu.VMEM((R * tq, 2 * HEAD_DIM), F32)],
        compiler_params=pltpu.CompilerParams(
            dimension_semantics=("arbitrary", "arbitrary", "arbitrary"),
            vmem_limit_bytes=_vmem_limit(est)),
        name="nsa_attention",
    )(proj, proj, proj, proj, proj, kcmp, vcmp, bias_c, bias_t, bias_far,
      jnp.asarray(overlap_t, BF16), jnp.asarray(aug_sel, BF16), jnp.asarray(aug_win, BF16),
      gates_t)


@functools.lru_cache(maxsize=None)
def _hgrn_masks():
    C = HGRN_CHUNK
    t = np.arange(C)[:, None]
    s = np.arange(C)[None, :]
    masks = [(t // HGRN_SUB == s // HGRN_SUB) & (s <= t)]
    half = C // 2
    while half >= HGRN_SUB:
        grp = 2 * half
        masks.append((t // grp == s // grp) & (t % grp >= half) & (s % grp < half))
        half //= 2
    assert np.array_equal(np.sum(masks, axis=0), (s <= t).astype(int))
    return np.stack(masks).astype(np.float32), (s <= t).astype(np.float32)


def _hgrn_body(q_ref, f_ref, i_ref, g_ref, lb_ref, gn_ref, msk_ref, tril_ref, o_ref,
               st_all, b_all, k_all, q_all, *, layer, heads):
    C = HGRN_CHUNK
    n_chunks = q_ref.shape[1] // C
    nt = (((1,), (1,)), ((), ()))

    lbp = lb_ref[...]
    e = jnp.exp(lbp - jnp.max(lbp, axis=0, keepdims=True))
    sm = e / jnp.sum(e, axis=0, keepdims=True)
    cum = sm[0:1]
    first = cum
    for d in range(1, layer + 1):
        cum = cum + sm[d:d + 1]
    lb_all = cum - first
    log_lb_all = jnp.log(lb_all)
    log_1m_all = jnp.log1p(-lb_all)
    gn_all = gn_ref[...]

    cols = lax.broadcasted_iota(jnp.int32, (1, C), 1)
    tril = tril_ref[...]

    st_all[...] = jnp.zeros(st_all.shape, F32)

    def chunk(c, carry):
        for hh in range(heads):
            one_head(c, hh)
        return carry

    def one_head(c, hh):
        sl = pl.ds(pl.multiple_of(c * C, C), C)
        hs = slice(hh * HEAD_DIM, (hh + 1) * HEAD_DIM)
        lb, log_lb, log_1m, gn = lb_all[:, hs], log_lb_all[:, hs], log_1m_all[:, hs], gn_all[:, hs]
        st_ref, b_sc, k_sc, q_sc = st_all.at[hh], b_all.at[hh], k_all.at[hh], q_all.at[hh]
        qr = q_ref[0, sl, hs]
        x = f_ref[0, sl, hs]
        v = i_ref[0, sl, hs]
        gr = g_ref[0, sl, hs]
        q = jax.nn.silu(qr)
        ex = jnp.exp(-jnp.abs(x))
        r1 = 1.0 / (1.0 + ex)
        k = (1.0 - lb) * jnp.where(x >= 0.0, ex * r1, r1)
        c2 = log_1m + (jnp.minimum(x, 0.0) - jnp.log1p(ex))
        log_f = jnp.maximum(log_lb, c2) + jnp.log1p(jnp.exp(-jnp.abs(log_lb - c2)))

        lf_hi = log_f.astype(BF16)
        lf_lo = (log_f - lf_hi.astype(F32)).astype(BF16)
        b = (jnp.dot(tril, lf_hi, preferred_element_type=F32)
             + jnp.dot(tril, lf_lo, preferred_element_type=F32)) * LOG2_E
        b_sc[...] = b
        k_sc[...] = k
        q_sc[...] = q

        vb = v.astype(BF16)
        st = st_ref[...]
        o = lax.dot_general((q * jnp.exp2(b)).astype(BF16), st.astype(BF16), nt,
                            preferred_element_type=F32)

        pieces = []
        for blk in range(C // HGRN_SUB):
            r0 = blk * HGRN_SUB
            bt = b_sc[r0:r0 + HGRN_SUB, :]
            qt = q_sc[r0:r0 + HGRN_SUB, :]
            arow = jnp.zeros((HGRN_SUB, C), F32)
            for s in range(HGRN_SUB):
                bs = b_sc[r0 + s:r0 + s + 1, :]
                ks = k_sc[r0 + s:r0 + s + 1, :]
                col = jnp.sum(jnp.exp2(bt - bs) * qt * ks, axis=-1, keepdims=True)
                arow = jnp.where(cols == r0 + s, col, arow)
            pieces.append(arow)
        a = jnp.where(msk_ref[0] > 0.5, jnp.concatenate(pieces, axis=0), 0.0)

        half = C // 2
        lvl = 1
        while half >= HGRN_SUB:
            grp = 2 * half
            anc = jnp.concatenate(
                [jnp.broadcast_to(b_sc[g0 + half - 1:g0 + half, :], (grp, b.shape[1]))
                 for g0 in range(0, C, grp)], axis=0)
            e = jnp.exp2(-jnp.abs(b - anc))
            al = lax.dot_general((q * e).astype(BF16), (k * e).astype(BF16), nt,
                                 preferred_element_type=F32)
            a = jnp.where(msk_ref[lvl] > 0.5, al, a)
            half //= 2
            lvl += 1

        o = o + jnp.dot(a.astype(BF16), vb, preferred_element_type=F32)

        b_last = b_sc[C - 1:C, :]
        kh = (k * jnp.exp2(b_last - b)).astype(BF16)
        st_ref[...] = st * jnp.exp2(b_last) + jnp.dot(v.T.astype(BF16), kh,
                                                     preferred_element_type=F32)

        ms = jnp.mean(o * o, axis=-1, keepdims=True)
        o = o * lax.rsqrt(ms + RMS_EPS) * gn * jax.nn.silu(gr)
        o_ref[0, sl, hs] = o.astype(o_ref.dtype)

    lax.fori_loop(0, n_chunks, chunk, 0)


def _hgrn(proj, hgrn_lb, onorm, layer, B, S):
    H = N_HEADS
    depth = hgrn_lb.shape[0]

    nh = HGRN_HEADS_PER_STEP
    width = nh * HEAD_DIM
    steps = H // nh

    def spec(part):
        return pl.BlockSpec((1, S, width), lambda b, h, part=part: (b, 0, part * steps + h))

    masks, tril = _hgrn_masks()
    C = HGRN_CHUNK
    est = 2 * 4 * S * width * 4 + 2 * S * width * 2 + nh * 64 * C * C * 4
    return pl.pallas_call(
        functools.partial(_hgrn_body, layer=layer, heads=nh),
        out_shape=jax.ShapeDtypeStruct((B, S, H * HEAD_DIM), BF16),
        grid=(B, steps),
        in_specs=[spec(0), spec(1), spec(2), spec(3),
                  pl.BlockSpec((depth, width), lambda b, h: (0, h)),
                  pl.BlockSpec((1, width), lambda b, h: (0, h)),
                  pl.BlockSpec(masks.shape, lambda b, h: (0, 0, 0)),
                  pl.BlockSpec((C, C), lambda b, h: (0, 0))],
        out_specs=pl.BlockSpec((1, S, width), lambda b, h: (b, 0, h)),
        scratch_shapes=[pltpu.VMEM((nh, HEAD_DIM, HEAD_DIM), F32),
                        pltpu.VMEM((nh, C, HEAD_DIM), F32),
                        pltpu.VMEM((nh, C, HEAD_DIM), F32),
                        pltpu.VMEM((nh, C, HEAD_DIM), F32)],
        compiler_params=pltpu.CompilerParams(
            dimension_semantics=("arbitrary", "arbitrary"),
            vmem_limit_bytes=_vmem_limit(est)),
        name="hgrn2_recurrence",
    )(proj, proj, proj, proj, hgrn_lb.astype(F32), onorm.reshape(1, -1).astype(F32),
      jnp.asarray(masks), jnp.asarray(tril, BF16))


def _mlp_body(x_ref, gi_ref, wu_ref, wd_ref, go_ref, o_ref, hn_ref, *, nf, tm):
    f = pl.program_id(1)

    def row_chunks(fn):
        def step(r, c):
            fn(pl.ds(pl.multiple_of(r * NORM_ROWS, NORM_ROWS), NORM_ROWS))
            return c

        lax.fori_loop(0, tm // NORM_ROWS, step, 0)

    @pl.when(f == 0)
    def _():
        g = gi_ref[...]

        def norm_in(rows):
            xs = x_ref[rows, :]
            ms = jnp.mean(xs * xs, axis=-1, keepdims=True)
            hn_ref[rows, :] = (xs * lax.rsqrt(ms + RMS_EPS) * g).astype(BF16)

        row_chunks(norm_in)

    hid = jnp.dot(hn_ref[...], wu_ref[...], preferred_element_type=F32)
    hid = jnp.square(jnp.maximum(hid, 0.0)).astype(BF16)
    part = jnp.dot(hid, wd_ref[...], preferred_element_type=F32)

    @pl.when(f == 0)
    def _():
        o_ref[...] = part

    @pl.when(f > 0)
    def _():
        o_ref[...] += part

    @pl.when(f == nf - 1)
    def _():
        g = go_ref[...]

        def norm_out(rows):
            y = o_ref[rows, :]
            ms = jnp.mean(y * y, axis=-1, keepdims=True)
            o_ref[rows, :] = x_ref[rows, :] + y * lax.rsqrt(ms + RMS_EPS) * g

        row_chunks(norm_out)


def _mlp(xf, g_in, g_out, w_up, w_down, *, tm=512, tf=1024):
    M, D = xf.shape
    F = w_up.shape[1]
    assert M % tm == 0 and F % tf == 0 and tm % NORM_ROWS == 0
    nf = F // tf
    est = (2 * tm * D * 4 + 2 * tm * D * 4 + tm * D * 2 + 2 * 2 * D * tf * 2
           + tm * tf * 6 + 2 * tm * D * 4)
    return pl.pallas_call(
        functools.partial(_mlp_body, nf=nf, tm=tm),
        out_shape=jax.ShapeDtypeStruct((M, D), F32),
        grid=(M // tm, nf),
        in_specs=[pl.BlockSpec((tm, D), lambda i, f: (i, 0)),
                  pl.BlockSpec((1, D), lambda i, f: (0, 0)),
                  pl.BlockSpec((D, tf), lambda i, f: (0, f)),
                  pl.BlockSpec((tf, D), lambda i, f: (f, 0)),
                  pl.BlockSpec((1, D), lambda i, f: (0, 0))],
        out_specs=pl.BlockSpec((tm, D), lambda i, f: (i, 0)),
        scratch_shapes=[pltpu.VMEM((tm, D), BF16)],
        compiler_params=pltpu.CompilerParams(
            dimension_semantics=("arbitrary", "arbitrary"),
            vmem_limit_bytes=_vmem_limit(est)),
        name="mlp",
    )(xf, g_in.reshape(1, D).astype(F32), w_up.astype(BF16), w_down.astype(BF16),
      g_out.reshape(1, D).astype(F32))


def _nsa_layer(xf, B, S, g_in, g_out, rel_table, w_in, cmp_pe, cmp_w1, cmp_w2, w_out):
    D = xf.shape[1]
    G, R, Dh = N_GROUPS, HEADS_PER_GROUP, HEAD_DIM
    n_main = N_HEADS * Dh + 6 * G * Dh
    n_gate = 3 * N_HEADS
    w_main = w_in[:, :n_main].astype(BF16)
    w_gate = jnp.pad(w_in[:, n_main:], ((0, 0), (0, LANES - n_gate))).astype(BF16)
    colscale = jnp.concatenate([jnp.full((N_HEADS * Dh,), Dh ** -0.5 * LOG2_E, F32),
                                jnp.ones((6 * G * Dh,), F32)])[None]

    proj = _matmul(xf, w_main, tm=1024, tn=1024, norm_g=g_in, epi="colscale",
                   colscale=colscale, out_dtype=BF16, name="nsa_proj")
    glog = _matmul(xf, w_gate, tm=1024, tn=LANES, norm_g=g_in, out_dtype=F32,
                   name="nsa_gate_proj")
    gates_t = (glog[:, :n_gate].reshape(B, S, 3, G, R).transpose(0, 3, 1, 2, 4)
               .reshape(B, G, S, 3 * R))

    proj3 = proj.reshape(B, S, n_main)
    nh = S // CMP_STRIDE

    def grouped(slot):
        c0 = N_HEADS * Dh + slot * G * Dh
        a = proj3[:, :, c0:c0 + G * Dh].reshape(B, nh, CMP_STRIDE, G, Dh)
        return a.transpose(0, 3, 1, 2, 4).reshape(B, G, nh, CMP_STRIDE * Dh)

    half = CMP_STRIDE * Dh
    kcmp, vcmp = _compress(grouped(0), grouped(1),
                           cmp_w1.reshape(2, 2, half, Dh).astype(BF16),
                           cmp_pe.reshape(2, 2, 1, half).astype(F32),
                           cmp_w2.astype(BF16))

    bias_c, bias_t, bias_far = _bias_tables(rel_table, S)
    attn = _nsa_attention(proj3, kcmp, vcmp, bias_c, bias_t, bias_far, gates_t, B, S)
    return _matmul(attn.reshape(B * S, D), w_out.astype(BF16), tm=512, tn=D,
                   epi="resnorm", res=xf, res_g=g_out, out_dtype=F32, name="nsa_out")


def _hgrn_layer(xf, B, S, layer, g_in, g_out, w_in, hgrn_lb, onorm, w_out):
    D = xf.shape[1]
    proj = _matmul(xf, w_in.astype(BF16), tm=1024, tn=1024, norm_g=g_in,
                   out_dtype=F32, name="hgrn_proj")
    mixed = _hgrn(proj.reshape(B, S, 4 * D), hgrn_lb, onorm, layer, B, S)
    return _matmul(mixed.reshape(B * S, D), w_out.astype(BF16), tm=512, tn=D,
                   epi="resnorm", res=xf, res_g=g_out, out_dtype=F32, name="hgrn_out")


def kernel(x, norm_g, rel_table, nsa_w_in, nsa_cmp_pe, nsa_cmp_w1, nsa_cmp_w2, nsa_w_out,
           hgrn_w_in, hgrn_lb, hgrn_onorm, hgrn_w_out, mlp_w_up, mlp_w_down):
    B, S, D = x.shape
    depth = norm_g.shape[0]
    assert D == N_HEADS * HEAD_DIM and S % ATT_TILE == 0 and S % HGRN_CHUNK == 0
    xf = x.reshape(B * S, D).astype(F32)
    for layer in range(depth):
        j = layer // 2
        if layer % 2 == 0:
            xf = _nsa_layer(xf, B, S, norm_g[layer, 0], norm_g[layer, 1], rel_table,
                            nsa_w_in[j], nsa_cmp_pe[j], nsa_cmp_w1[j], nsa_cmp_w2[j],
                            nsa_w_out[j])
        else:
            xf = _hgrn_layer(xf, B, S, layer, norm_g[layer, 0], norm_g[layer, 1],
                             hgrn_w_in[j], hgrn_lb, hgrn_onorm[j], hgrn_w_out[j])
        xf = _mlp(xf, norm_g[layer, 2], norm_g[layer, 3], mlp_w_up[layer], mlp_w_down[layer])
    return xf.reshape(B, S, D).astype(x.dtype)
```

```python
import functools
import math

import numpy as np
import jax
import jax.numpy as jnp
from jax import lax
from jax.experimental import pallas as pl
from jax.experimental.pallas import tpu as pltpu

F32 = jnp.float32
BF16 = jnp.bfloat16

N_HEADS = 16
N_GROUPS = 4
HEADS_PER_GROUP = N_HEADS // N_GROUPS
HEAD_DIM = 128
CMP_BLOCK = 32
CMP_STRIDE = 16
SEL_BLOCK = 64
SEL_TOP_N = 8
WINDOW = 512
FORCE_SCORE = 1.0e4
REL_BUCKETS = 32
REL_MAX_DIST = 128
RMS_EPS = 1e-6
NEG_INF = -1.0e30
LOG2_E = math.log2(math.e)

LANES = 128
SUBLANES = 8
VMEM_BYTES_V7X = 64 * 1024 * 1024
VMEM_LIMIT_CAP = VMEM_BYTES_V7X - 8 * 1024 * 1024

ATT_TILE = 256
ATT_ROWS = 128
HGRN_CHUNK = 128
HGRN_SUB = 8
HGRN_HEADS_PER_STEP = 4
NORM_ROWS = 256


def _vmem_limit(nbytes):
    return int(min(VMEM_LIMIT_CAP, max(32 * 1024 * 1024, nbytes)))


def _mm_body(*refs, norm, epi, side, tm):
    it = iter(refs)
    x_ref = next(it)
    g_ref = next(it) if norm else None
    w_ref = next(it)
    ws_ref = next(it) if side else None
    cs_ref = next(it) if epi == "colscale" else None
    res_ref = next(it) if epi == "resnorm" else None
    go_ref = next(it) if epi == "resnorm" else None
    o_ref = next(it)
    os_ref = next(it) if side else None
    hn_ref = next(it) if norm else None

    j = pl.program_id(1)

    if norm:
        @pl.when(j == 0)
        def _():
            g = g_ref[...]

            def step(r, c):
                rows = pl.ds(pl.multiple_of(r * NORM_ROWS, NORM_ROWS), NORM_ROWS)
                xs = x_ref[rows, :]
                ms = jnp.mean(xs * xs, axis=-1, keepdims=True)
                hn_ref[rows, :] = (xs * lax.rsqrt(ms + RMS_EPS) * g).astype(BF16)
                return c

            lax.fori_loop(0, tm // NORM_ROWS, step, 0)

        lhs = hn_ref[...]
    else:
        lhs = x_ref[...]

    if side:
        @pl.when(j == 0)
        def _():
            os_ref[...] = jnp.dot(lhs, ws_ref[...], preferred_element_type=F32)

    acc = jnp.dot(lhs, w_ref[...], preferred_element_type=F32)
    if epi == "colscale":
        acc = acc * cs_ref[...]
    elif epi == "resnorm":
        ms = jnp.mean(acc * acc, axis=-1, keepdims=True)
        acc = res_ref[...] + acc * lax.rsqrt(ms + RMS_EPS) * go_ref[...]
    o_ref[...] = acc.astype(o_ref.dtype)


def _matmul(x, w, *, tm, tn, norm_g=None, epi="none", colscale=None, res=None, res_g=None,
            side_w=None, out_dtype=BF16, name="mm"):
    M, K = x.shape
    N = w.shape[1]
    norm = norm_g is not None
    side = side_w is not None
    assert M % tm == 0 and N % tn == 0
    assert not (epi == "resnorm" and tn != N)

    in_specs = [pl.BlockSpec((tm, K), lambda i, j: (i, 0))]
    args = [x]
    if norm:
        in_specs.append(pl.BlockSpec((1, K), lambda i, j: (0, 0)))
        args.append(norm_g.reshape(1, K).astype(F32))
    in_specs.append(pl.BlockSpec((K, tn), lambda i, j: (0, j)))
    args.append(w)
    ns = side_w.shape[1] if side else 0
    if side:
        in_specs.append(pl.BlockSpec((K, ns), lambda i, j: (0, 0)))
        args.append(side_w)
    if epi == "colscale":
        in_specs.append(pl.BlockSpec((1, tn), lambda i, j: (0, j)))
        args.append(colscale)
    if epi == "resnorm":
        in_specs.append(pl.BlockSpec((tm, tn), lambda i, j: (i, j)))
        args.append(res)
        in_specs.append(pl.BlockSpec((1, tn), lambda i, j: (0, j)))
        args.append(res_g.reshape(1, N).astype(F32))

    out_shape = jax.ShapeDtypeStruct((M, N), out_dtype)
    out_specs = pl.BlockSpec((tm, tn), lambda i, j: (i, j))
    if side:
        out_shape = (out_shape, jax.ShapeDtypeStruct((M, ns), F32))
        out_specs = (out_specs, pl.BlockSpec((tm, ns), lambda i, j: (i, 0)))

    xb = x.dtype.itemsize
    ob = jnp.dtype(out_dtype).itemsize
    est = (2 * tm * K * xb + 2 * K * tn * 2 + 2 * tm * tn * ob + (tm * K * 2 if norm else 0)
           + (2 * tm * tn * 4 if epi == "resnorm" else 0) + 3 * tm * tn * 4
           + 2 * K * ns * 2 + 3 * tm * ns * 4)

    return pl.pallas_call(
        functools.partial(_mm_body, norm=norm, epi=epi, side=side, tm=tm),
        out_shape=out_shape,
        grid=(M // tm, N // tn),
        in_specs=in_specs,
        out_specs=out_specs,
        scratch_shapes=[pltpu.VMEM((tm, K), BF16)] if norm else [],
        compiler_params=pltpu.CompilerParams(
            dimension_semantics=("arbitrary", "arbitrary"),
            vmem_limit_bytes=_vmem_limit(est)),
        name=name,
    )(*args)


def _rel_bucket_np(dist):
    n = np.maximum(dist, 0)
    max_exact = REL_BUCKETS // 2
    nf = np.maximum(n, 1).astype(np.float32)
    ratio = np.log(nf / np.float32(max_exact)) / np.float32(math.log(REL_MAX_DIST / max_exact))
    large = max_exact + (ratio * np.float32(REL_BUCKETS - max_exact)).astype(np.int32)
    large = np.minimum(large, REL_BUCKETS - 1)
    return np.where(n < max_exact, n, large).astype(np.int32)


@functools.lru_cache(maxsize=None)
def _static_maps(seq):
    n_cmp = LANES
    pos = np.arange(seq, dtype=np.int32)[:, None]
    c_end = np.arange(n_cmp, dtype=np.int32)[None, :] * CMP_STRIDE + CMP_BLOCK - 1
    bucket_c = _rel_bucket_np(pos - c_end)
    t = np.arange(ATT_TILE, dtype=np.int32)[:, None]
    k = np.arange(ATT_TILE, dtype=np.int32)[None, :]
    bucket_t = np.stack([_rel_bucket_np(t - k), _rel_bucket_np(ATT_TILE + t - k)])
    assert _rel_bucket_np(np.array([ATT_TILE + 1]))[0] == REL_BUCKETS - 1
    nc = seq // CMP_STRIDE - CMP_BLOCK // CMP_STRIDE + 1
    nb = seq // SEL_BLOCK
    c_start = np.arange(nc)[:, None] * CMP_STRIDE
    b_start = np.arange(nb)[None, :] * SEL_BLOCK
    ov = ((c_start <= b_start + SEL_BLOCK - 1) & (c_start + CMP_BLOCK - 1 >= b_start))
    overlap = np.zeros((LANES, LANES), np.float32)
    overlap[:nc, :nb] = ov
    return bucket_c, bucket_t, overlap


def _bias_body(tab_ref, bc_ref, bt_ref, oc_ref, ot_ref, of_ref, *, seq):
    h = pl.program_id(0)

    def lookup(bmap):
        acc = jnp.zeros(bmap.shape, F32)
        for b in range(REL_BUCKETS):
            acc = jnp.where(bmap == b, tab_ref[b, h], acc)
        return acc

    def step(r, c):
        rows = pl.ds(pl.multiple_of(r * ATT_TILE, ATT_TILE), ATT_TILE)
        oc_ref[0, rows, :] = lookup(bc_ref[rows, :]) * LOG2_E
        return c

    lax.fori_loop(0, seq // ATT_TILE, step, 0)

    tt = lax.broadcasted_iota(jnp.int32, (ATT_TILE, ATT_TILE), 0)
    kk = lax.broadcasted_iota(jnp.int32, (ATT_TILE, ATT_TILE), 1)
    far = tab_ref[REL_BUCKETS - 1, h]
    of_ref[0] = jnp.full(of_ref.shape[1:], far * LOG2_E, F32)
    ot_ref[0, 0] = jnp.zeros((ATT_TILE, ATT_TILE), F32)
    ot_ref[1, 0] = (lookup(bt_ref[1]) - far) * LOG2_E
    ot_ref[2, 0] = jnp.where(kk <= tt, (lookup(bt_ref[0]) - far) * LOG2_E, NEG_INF)
    ot_ref[3, 0] = jnp.where(kk > tt, 0.0, NEG_INF)


def _bias_tables(rel_table, seq):
    bucket_c, bucket_t, _ = _static_maps(seq)
    return pl.pallas_call(
        functools.partial(_bias_body, seq=seq),
        out_shape=(jax.ShapeDtypeStruct((N_HEADS, seq, LANES), F32),
                   jax.ShapeDtypeStruct((4, N_HEADS, ATT_TILE, ATT_TILE), F32),
                   jax.ShapeDtypeStruct((N_HEADS, SUBLANES, LANES), F32)),
        grid=(N_HEADS,),
        in_specs=[pl.BlockSpec(memory_space=pltpu.SMEM),
                  pl.BlockSpec((seq, LANES), lambda h: (0, 0)),
                  pl.BlockSpec((2, ATT_TILE, ATT_TILE), lambda h: (0, 0, 0))],
        out_specs=(pl.BlockSpec((1, seq, LANES), lambda h: (h, 0, 0)),
                   pl.BlockSpec((4, 1, ATT_TILE, ATT_TILE), lambda h: (0, h, 0, 0)),
                   pl.BlockSpec((1, SUBLANES, LANES), lambda h: (h, 0, 0))),
        compiler_params=pltpu.CompilerParams(dimension_semantics=("arbitrary",)),
        name="rel_bias",
    )(rel_table.astype(F32), jnp.asarray(bucket_c), jnp.asarray(bucket_t))


def _compress_body(xk_ref, xv_ref, w1_ref, pe_ref, w2_ref, ok_ref, ov_ref, x_sc):
    n_grp = xk_ref.shape[1] // CMP_STRIDE

    def one(x_ref, idx, o_ref):
        x_sc[...] = x_ref[0].astype(F32)
        x = jnp.concatenate([x_sc[pl.ds(t, n_grp, stride=CMP_STRIDE), :]
                             for t in range(CMP_STRIDE)], axis=1)
        a0 = jnp.dot((x + pe_ref[idx, 0]).astype(BF16), w1_ref[idx, 0],
                     preferred_element_type=F32)
        a1 = jnp.dot((x + pe_ref[idx, 1]).astype(BF16), w1_ref[idx, 1],
                     preferred_element_type=F32)
        pre = a0 + pltpu.roll(a1, LANES - 1, 0)
        hid = jax.nn.gelu(pre).astype(BF16)
        o_ref[0, 0] = jnp.dot(hid, w2_ref[idx], preferred_element_type=F32).astype(BF16)

    one(xk_ref, 0, ok_ref)
    one(xv_ref, 1, ov_ref)


def _compress(proj, w1, pe, w2, B, S):
    G = N_GROUPS
    assert S // CMP_STRIDE == LANES
    half = CMP_STRIDE * HEAD_DIM
    q_cols = N_HEADS

    def spec_x(slot):
        return pl.BlockSpec((1, S, HEAD_DIM), lambda b, g, slot=slot: (b, 0, q_cols + slot * G + g))

    spec_o = pl.BlockSpec((1, 1, LANES, HEAD_DIM), lambda b, g: (b, g, 0, 0))
    out = jax.ShapeDtypeStruct((B, G, LANES, HEAD_DIM), BF16)
    return pl.pallas_call(
        _compress_body,
        out_shape=(out, out),
        grid=(B, G),
        in_specs=[spec_x(0), spec_x(1),
                  pl.BlockSpec((2, 2, half, HEAD_DIM), lambda b, g: (0, 0, 0, 0)),
                  pl.BlockSpec((2, 2, 1, half), lambda b, g: (0, 0, 0, 0)),
                  pl.BlockSpec((2, HEAD_DIM, HEAD_DIM), lambda b, g: (0, 0, 0))],
        out_specs=(spec_o, spec_o),
        scratch_shapes=[pltpu.VMEM((S, HEAD_DIM), F32)],
        compiler_params=pltpu.CompilerParams(dimension_semantics=("arbitrary", "arbitrary")),
        name="nsa_compress",
    )(proj, proj, w1, pe, w2)


def _nsa_body(q_ref, ks_ref, vs_ref, kw_ref, vw_ref, kc_ref, vc_ref, bc_ref, bt_ref, far_ref,
              ovl_ref, augs_ref, augw_ref, gl_ref, o_ref, kts_sc, ktw_sc, q4_sc, m_sc, acc_sc):
    R = HEADS_PER_GROUP
    tq = ATT_TILE
    qi = pl.program_id(2)
    n_tiles = ks_ref.shape[1] // tq
    nb = ovl_ref.shape[0]
    nt = (((1,), (1,)), ((), ()))

    @pl.when(qi == 0)
    def _():
        def tr(j, c):
            rows = pl.ds(pl.multiple_of(j * tq, tq), tq)
            kts_sc[j, :HEAD_DIM, :] = ks_ref[0, rows, :].astype(F32).T.astype(BF16)
            kts_sc[j, HEAD_DIM:, :] = augs_ref[j]
            ktw_sc[j, :HEAD_DIM, :] = kw_ref[0, rows, :].astype(F32).T.astype(BF16)
            ktw_sc[j, HEAD_DIM:, :] = augw_ref[0]
            return c

        lax.fori_loop(0, n_tiles, tr, 0)
        zeros = jnp.zeros((HEAD_DIM, tq), BF16)
        kts_sc[n_tiles, :HEAD_DIM, :] = zeros
        kts_sc[n_tiles, HEAD_DIM:, :] = augs_ref[n_tiles]
        ktw_sc[n_tiles, :HEAD_DIM, :] = zeros
        ktw_sc[n_tiles, HEAD_DIM:, :] = augw_ref[1]

    q = q_ref[0]
    q4 = jnp.concatenate([q[:, r * HEAD_DIM:(r + 1) * HEAD_DIM] for r in range(R)], axis=0)
    q4_sc[:, :HEAD_DIM] = q4

    pos3 = qi * tq + lax.broadcasted_iota(jnp.int32, (1, tq, 1), 1)

    sc = lax.dot_general(q4, kc_ref[0, 0], nt, preferred_element_type=F32)
    sc = sc.reshape(R, tq, LANES) + bc_ref[...]
    cidx = lax.broadcasted_iota(jnp.int32, (1, 1, LANES), 2)
    valid = (cidx * CMP_STRIDE + (CMP_BLOCK - 1) <= pos3) & (cidx < LANES - 1)
    sc = jnp.where(valid, sc, NEG_INF)
    mc = jnp.max(sc, axis=-1, keepdims=True)
    ec = jnp.exp2(sc - mc)
    pc = ec / jnp.sum(ec, axis=-1, keepdims=True)
    pc = jnp.where(pos3 >= CMP_BLOCK - 1, pc, 0.0)
    o_cmp = jnp.dot(pc.reshape(R * tq, LANES).astype(BF16), vc_ref[0, 0],
                    preferred_element_type=F32)

    psum = pc[0]
    for r in range(1, R):
        psum = psum + pc[r]
    p_hi = psum.astype(BF16)
    p_lo = (psum - p_hi.astype(F32)).astype(BF16)
    ovt = ovl_ref[...]
    imp = (lax.dot_general(ovt, p_hi, nt, preferred_element_type=F32)
           + lax.dot_general(ovt, p_lo, nt, preferred_element_type=F32))
    jb = lax.broadcasted_iota(jnp.int32, (nb, 1), 0)
    pos_t = qi * tq + lax.broadcasted_iota(jnp.int32, (1, tq), 1)
    q_blk = lax.shift_right_logical(pos_t, int(math.log2(SEL_BLOCK)))
    forced = (jb == 0) | (jb == q_blk) | (jb == q_blk - 1)
    future = jb > q_blk
    imp = jnp.where(forced, FORCE_SCORE, jnp.where(future, -1.0, imp))
    cnt = jnp.zeros((nb, tq), F32)
    for i in range(nb):
        row = imp[i:i + 1, :]
        beats = (row > imp) | ((row == imp) & (jb > i))
        cnt = cnt + jnp.where(beats, 1.0, 0.0)
    sel_t = jnp.where(cnt < float(min(SEL_TOP_N, nb)), 1.0, 0.0)
    sel = jnp.concatenate([sel_t, jnp.zeros((LANES - nb, tq), F32)], axis=0).T

    lane = lax.broadcasted_iota(jnp.int32, (1, LANES), 1)
    sel_pad = jnp.where(lane < nb, (sel - 1.0) * (-NEG_INF), 0.0)
    for r in range(R):
        far = jnp.broadcast_to(far_ref[r, 0:1, :], (tq, LANES))
        far_hi = far.astype(BF16).astype(F32)
        pad = jnp.where(lane == nb, far_hi, jnp.where(lane == nb + 1, far - far_hi, sel_pad))
        pad = jnp.where(lane == nb + 2, 1.0, pad)
        q4_sc[r * tq:(r + 1) * tq, HEAD_DIM:] = pad.astype(BF16)

    ones = jnp.ones((tq, HEAD_DIM), BF16)
    n_chunks = R * tq // ATT_ROWS
    chunks_per_head = tq // ATT_ROWS
    dead = n_tiles

    def keys(kt_sc, tiles):
        return jnp.concatenate([kt_sc[t] for t in tiles], axis=1)

    def values(v_ref, tiles):
        parts = []
        for t in tiles:
            rows = pl.ds(pl.multiple_of(t * tq, tq), tq)
            parts.append(jnp.concatenate([v_ref[0, rows, :], ones], axis=1))
        return jnp.concatenate(parts, axis=0)

    def logits(ci, kt, kinds):
        r, hh = divmod(ci, chunks_per_head)
        rs = slice(ci * ATT_ROWS, (ci + 1) * ATT_ROWS)
        qs = slice(hh * ATT_ROWS, (hh + 1) * ATT_ROWS)
        s = jnp.dot(q4_sc[rs, :], kt, preferred_element_type=F32)
        return s + jnp.concatenate([bt_ref[kd, r, qs, :] for kd in kinds], axis=1)

    def probs(s, m):
        return jnp.concatenate([jnp.exp2(s[:, k0:k0 + LANES] - m)
                                for k0 in range(0, s.shape[1], LANES)], axis=1).astype(BF16)

    m_sc[...] = jnp.full(m_sc.shape, NEG_INF, F32)
    acc_sc[...] = jnp.zeros(acc_sc.shape, F32)

    def pair(pi, c):
        ja = 2 * pi
        jb = ja + 1
        kt = keys(kts_sc, (ja, jnp.where(jb <= qi, jb, dead)))
        vv = values(vs_ref, (ja, jnp.minimum(jb, n_tiles - 1)))
        kinds = (jnp.clip(ja - qi + 2, 0, 2), jnp.clip(jb - qi + 2, 0, 2))
        for ci in range(n_chunks):
            rs = slice(ci * ATT_ROWS, (ci + 1) * ATT_ROWS)
            s = logits(ci, kt, kinds)
            m_old = m_sc[rs]
            m_new = jnp.maximum(m_old, jnp.max(s, axis=-1, keepdims=True))
            alpha = jnp.exp2(m_old - m_new)
            m_sc[rs] = m_new
            pv = jnp.dot(probs(s, m_new), vv, preferred_element_type=F32)
            acc_sc[rs] = acc_sc[rs] * jnp.concatenate([alpha, alpha], axis=1) + pv
        return c

    lax.fori_loop(0, (qi + 2) // 2, pair, 0)
    acc = acc_sc[...]
    o_sel = acc[:, :HEAD_DIM] / acc[:, HEAD_DIM:]

    n_win = WINDOW // tq
    w_tiles = [qi - n_win + t for t in range(n_win + 1)]
    kt = keys(ktw_sc, [jnp.where(t >= 0, t, dead) for t in w_tiles])
    vv = values(vw_ref, [jnp.maximum(t, 0) for t in w_tiles])
    w_kinds = (3,) + (0,) * (n_win - 2) + (1, 2)
    o_parts = []
    for ci in range(n_chunks):
        s = logits(ci, kt, w_kinds)
        pv = jnp.dot(probs(s, jnp.max(s, axis=-1, keepdims=True)), vv,
                     preferred_element_type=F32)
        o_parts.append(pv[:, :HEAD_DIM] / pv[:, HEAD_DIM:])
    o_win = jnp.concatenate(o_parts, axis=0)

    gates = jax.nn.sigmoid(gl_ref[0, 0])
    outs = []
    for r in range(R):
        hs = slice(r * tq, (r + 1) * tq)
        o_r = (gates[:, r:r + 1] * o_cmp[hs]
               + gates[:, R + r:R + r + 1] * o_sel[hs]
               + gates[:, 2 * R + r:2 * R + r + 1] * o_win[hs])
        outs.append(o_r)
    o_ref[0] = jnp.concatenate(outs, axis=1).astype(o_ref.dtype)


def _nsa_attention(proj, kcmp, vcmp, bias_c, bias_t, bias_far, gates_t, B, S):
    assert WINDOW % ATT_TILE == 0 and S % ATT_TILE == 0
    assert S // CMP_STRIDE == LANES and S // SEL_BLOCK <= LANES
    R, G, tq = HEADS_PER_GROUP, N_GROUPS, ATT_TILE
    n_tiles = S // tq
    nb = S // SEL_BLOCK
    _, _, overlap = _static_maps(S)
    overlap_t = np.ascontiguousarray(overlap.T[:nb])
    assert nb + 3 <= LANES and n_tiles % 2 == 0
    blk_of_key = (np.arange(S) // SEL_BLOCK).reshape(n_tiles, 1, tq)
    aug_sel = np.zeros((n_tiles + 1, LANES, tq), np.float32)
    aug_sel[:n_tiles] = np.arange(LANES).reshape(1, LANES, 1) == blk_of_key
    aug_sel[:n_tiles, nb:nb + 2, :] = 1.0
    aug_sel[n_tiles, nb + 2, :] = NEG_INF
    aug_win = np.zeros((2, LANES, tq), np.float32)
    aug_win[0, nb:nb + 2, :] = 1.0
    aug_win[1, nb + 2, :] = NEG_INF
    q_cols = N_HEADS

    def kv_spec(slot):
        return pl.BlockSpec((1, S, HEAD_DIM),
                            lambda b, g, i, slot=slot: (b, 0, q_cols + slot * G + g))

    cmp_spec = pl.BlockSpec((1, 1, LANES, HEAD_DIM), lambda b, g, i: (b, g, 0, 0))
    in_specs = [
        pl.BlockSpec((1, tq, R * HEAD_DIM), lambda b, g, i: (b, i, g)),
        kv_spec(2), kv_spec(3), kv_spec(4), kv_spec(5),
        cmp_spec, cmp_spec,
        pl.BlockSpec((R, tq, LANES), lambda b, g, i: (g, i, 0)),
        pl.BlockSpec((4, R, tq, tq), lambda b, g, i: (0, g, 0, 0)),
        pl.BlockSpec((R, SUBLANES, LANES), lambda b, g, i: (g, 0, 0)),
        pl.BlockSpec((nb, LANES), lambda b, g, i: (0, 0)),
        pl.BlockSpec((n_tiles + 1, LANES, tq), lambda b, g, i: (0, 0, 0)),
        pl.BlockSpec((2, LANES, tq), lambda b, g, i: (0, 0, 0)),
        pl.BlockSpec((1, 1, tq, 3 * R), lambda b, g, i: (b, g, i, 0)),
    ]
    kdim = HEAD_DIM + LANES
    est = (2 * 4 * S * HEAD_DIM * 2 + 2 * 4 * R * tq * tq * 4 + 2 * R * tq * LANES * 4
           + 2 * 2 * S * LANES * 2 + 2 * S * kdim * 2 + R * tq * (kdim * 2 + LANES * 4 + 2 * HEAD_DIM * 4)
           + 3 * R * tq * LANES * 4 + 16 * ATT_ROWS * tq * 4 + 4 * tq * R * HEAD_DIM * 2)
    return pl.pallas_call(
        _nsa_body,
        out_shape=jax.ShapeDtypeStruct((B, S, N_HEADS * HEAD_DIM), BF16),
        grid=(B, G, n_tiles),
        in_specs=in_specs,
        out_specs=pl.BlockSpec((1, tq, R * HEAD_DIM), lambda b, g, i: (b, i, g)),
        scratch_shapes=[pltpu.VMEM((n_tiles + 1, kdim, tq), BF16),
                        pltpu.VMEM((n_tiles + 1, kdim, tq), BF16),
                        pltpu.VMEM((R * tq, kdim), BF16),
                        pltpu.VMEM((R * tq, LANES), F32),
                        pltpu.VMEM((R * tq, 2 * HEAD_DIM), F32)],
        compiler_params=pltpu.CompilerParams(
            dimension_semantics=("arbitrary", "arbitrary", "arbitrary"),
            vmem_limit_bytes=_vmem_limit(est)),
        name="nsa_attention",
    )(proj, proj, proj, proj, proj, kcmp, vcmp, bias_c, bias_t, bias_far,
      jnp.asarray(overlap_t, BF16), jnp.asarray(aug_sel, BF16), jnp.asarray(aug_win, BF16),
      gates_t)


@functools.lru_cache(maxsize=None)
def _hgrn_masks():
    C = HGRN_CHUNK
    t = np.arange(C)[:, None]
    s = np.arange(C)[None, :]
    masks = [(t // HGRN_SUB == s // HGRN_SUB) & (s <= t)]
    half = C // 2
    while half >= HGRN_SUB:
        grp = 2 * half
        masks.append((t // grp == s // grp) & (t % grp >= half) & (s % grp < half))
        half //= 2
    assert np.array_equal(np.sum(masks, axis=0), (s <= t).astype(int))
    return np.stack(masks).astype(np.float32), (s <= t).astype(np.float32)


def _hgrn_body(q_ref, f_ref, i_ref, g_ref, lb_ref, gn_ref, msk_ref, tril_ref, o_ref,
               st_all, b_all, k_all, q_all, *, layer, heads):
    C = HGRN_CHUNK
    n_chunks = q_ref.shape[1] // C
    nt = (((1,), (1,)), ((), ()))

    lbp = lb_ref[...]
    e = jnp.exp(lbp - jnp.max(lbp, axis=0, keepdims=True))
    sm = e / jnp.sum(e, axis=0, keepdims=True)
    cum = sm[0:1]
    first = cum
    for d in range(1, layer + 1):
        cum = cum + sm[d:d + 1]
    lb_all = cum - first
    log_lb_all = jnp.log(lb_all)
    log_1m_all = jnp.log1p(-lb_all)
    gn_all = gn_ref[...]

    cols = lax.broadcasted_iota(jnp.int32, (1, C), 1)
    tril = tril_ref[...]

    st_all[...] = jnp.zeros(st_all.shape, F32)

    def chunk(c, carry):
        for hh in range(heads):
            one_head(c, hh)
        return carry

    def one_head(c, hh):
        sl = pl.ds(pl.multiple_of(c * C, C), C)
        hs = slice(hh * HEAD_DIM, (hh + 1) * HEAD_DIM)
        lb, log_lb, log_1m, gn = lb_all[:, hs], log_lb_all[:, hs], log_1m_all[:, hs], gn_all[:, hs]
        st_ref, b_sc, k_sc, q_sc = st_all.at[hh], b_all.at[hh], k_all.at[hh], q_all.at[hh]
        qr = q_ref[0, sl, hs].astype(F32)
        x = f_ref[0, sl, hs].astype(F32)
        v = i_ref[0, sl, hs].astype(F32)
        gr = g_ref[0, sl, hs].astype(F32)
        q = jax.nn.silu(qr)
        ex = jnp.exp(-jnp.abs(x))
        r1 = 1.0 / (1.0 + ex)
        k = (1.0 - lb) * jnp.where(x >= 0.0, ex * r1, r1)
        c2 = log_1m + (jnp.minimum(x, 0.0) - jnp.log1p(ex))
        log_f = jnp.maximum(log_lb, c2) + jnp.log1p(jnp.exp(-jnp.abs(log_lb - c2)))

        lf_hi = log_f.astype(BF16)
        lf_lo = (log_f - lf_hi.astype(F32)).astype(BF16)
        b = (jnp.dot(tril, lf_hi, preferred_element_type=F32)
             + jnp.dot(tril, lf_lo, preferred_element_type=F32)) * LOG2_E
        b_sc[...] = b
        k_sc[...] = k
        q_sc[...] = q

        vb = v.astype(BF16)
        st = st_ref[...]
        o = lax.dot_general((q * jnp.exp2(b)).astype(BF16), st.astype(BF16), nt,
                            preferred_element_type=F32)

        pieces = []
        for blk in range(C // HGRN_SUB):
            r0 = blk * HGRN_SUB
            bt = b_sc[r0:r0 + HGRN_SUB, :]
            qt = q_sc[r0:r0 + HGRN_SUB, :]
            arow = jnp.zeros((HGRN_SUB, C), F32)
            for s in range(HGRN_SUB):
                bs = b_sc[r0 + s:r0 + s + 1, :]
                ks = k_sc[r0 + s:r0 + s + 1, :]
                col = jnp.sum(jnp.exp2(bt - bs) * qt * ks, axis=-1, keepdims=True)
                arow = jnp.where(cols == r0 + s, col, arow)
            pieces.append(arow)
        a = jnp.where(msk_ref[0] > 0.5, jnp.concatenate(pieces, axis=0), 0.0)

        half = C // 2
        lvl = 1
        while half >= HGRN_SUB:
            grp = 2 * half
            anc = jnp.concatenate(
                [jnp.broadcast_to(b_sc[g0 + half - 1:g0 + half, :], (grp, b.shape[1]))
                 for g0 in range(0, C, grp)], axis=0)
            e = jnp.exp2(-jnp.abs(b - anc))
            al = lax.dot_general((q * e).astype(BF16), (k * e).astype(BF16), nt,
                                 preferred_element_type=F32)
            a = jnp.where(msk_ref[lvl] > 0.5, al, a)
            half //= 2
            lvl += 1

        o = o + jnp.dot(a.astype(BF16), vb, preferred_element_type=F32)

        b_last = b_sc[C - 1:C, :]
        kh = (k * jnp.exp2(b_last - b)).astype(BF16)
        st_ref[...] = st * jnp.exp2(b_last) + jnp.dot(v.T.astype(BF16), kh,
                                                     preferred_element_type=F32)

        ms = jnp.mean(o * o, axis=-1, keepdims=True)
        o = o * lax.rsqrt(ms + RMS_EPS) * gn * jax.nn.silu(gr)
        o_ref[0, sl, hs] = o.astype(o_ref.dtype)

    lax.fori_loop(0, n_chunks, chunk, 0)


def _hgrn(proj, hgrn_lb, onorm, layer, B, S):
    H = N_HEADS
    depth = hgrn_lb.shape[0]

    nh = HGRN_HEADS_PER_STEP
    width = nh * HEAD_DIM
    steps = H // nh

    def spec(part):
        return pl.BlockSpec((1, S, width), lambda b, h, part=part: (b, 0, part * steps + h))

    masks, tril = _hgrn_masks()
    C = HGRN_CHUNK
    est = 2 * 4 * S * width * 4 + 2 * S * width * 2 + nh * 64 * C * C * 4
    return pl.pallas_call(
        functools.partial(_hgrn_body, layer=layer, heads=nh),
        out_shape=jax.ShapeDtypeStruct((B, S, H * HEAD_DIM), BF16),
        grid=(B, steps),
        in_specs=[spec(0), spec(1), spec(2), spec(3),
                  pl.BlockSpec((depth, width), lambda b, h: (0, h)),
                  pl.BlockSpec((1, width), lambda b, h: (0, h)),
                  pl.BlockSpec(masks.shape, lambda b, h: (0, 0, 0)),
                  pl.BlockSpec((C, C), lambda b, h: (0, 0))],
        out_specs=pl.BlockSpec((1, S, width), lambda b, h: (b, 0, h)),
        scratch_shapes=[pltpu.VMEM((nh, HEAD_DIM, HEAD_DIM), F32),
                        pltpu.VMEM((nh, C, HEAD_DIM), F32),
                        pltpu.VMEM((nh, C, HEAD_DIM), F32),
                        pltpu.VMEM((nh, C, HEAD_DIM), F32)],
        compiler_params=pltpu.CompilerParams(
            dimension_semantics=("arbitrary", "arbitrary"),
            vmem_limit_bytes=_vmem_limit(est)),
        name="hgrn2_recurrence",
    )(proj, proj, proj, proj, hgrn_lb.astype(F32), onorm.reshape(1, -1).astype(F32),
      jnp.asarray(masks), jnp.asarray(tril, BF16))


def _mlp_body(x_ref, gi_ref, wu_ref, wd_ref, go_ref, o_ref, hn_ref, *, nf, tm):
    f = pl.program_id(1)

    def row_chunks(fn):
        def step(r, c):
            fn(pl.ds(pl.multiple_of(r * NORM_ROWS, NORM_ROWS), NORM_ROWS))
            return c

        lax.fori_loop(0, tm // NORM_ROWS, step, 0)

    @pl.when(f == 0)
    def _():
        g = gi_ref[...]

        def norm_in(rows):
            xs = x_ref[rows, :]
            ms = jnp.mean(xs * xs, axis=-1, keepdims=True)
            hn_ref[rows, :] = (xs * lax.rsqrt(ms + RMS_EPS) * g).astype(BF16)

        row_chunks(norm_in)

    hid = jnp.dot(hn_ref[...], wu_ref[...], preferred_element_type=F32)
    hid = jnp.square(jnp.maximum(hid, 0.0)).astype(BF16)
    part = jnp.dot(hid, wd_ref[...], preferred_element_type=F32)

    @pl.when(f == 0)
    def _():
        o_ref[...] = part

    @pl.when(f > 0)
    def _():
        o_ref[...] += part

    @pl.when(f == nf - 1)
    def _():
        g = go_ref[...]

        def norm_out(rows):
            y = o_ref[rows, :]
            ms = jnp.mean(y * y, axis=-1, keepdims=True)
            o_ref[rows, :] = x_ref[rows, :] + y * lax.rsqrt(ms + RMS_EPS) * g

        row_chunks(norm_out)


def _mlp(xf, g_in, g_out, w_up, w_down, layer, *, tm=512, tf=1024):
    M, D = xf.shape
    F = w_up.shape[2]
    assert M % tm == 0 and F % tf == 0 and tm % NORM_ROWS == 0
    nf = F // tf
    est = (2 * tm * D * 4 + 2 * tm * D * 4 + tm * D * 2 + 2 * 2 * D * tf * 2
           + tm * tf * 6 + 2 * tm * D * 4)
    return pl.pallas_call(
        functools.partial(_mlp_body, nf=nf, tm=tm),
        out_shape=jax.ShapeDtypeStruct((M, D), F32),
        grid=(M // tm, nf),
        in_specs=[pl.BlockSpec((tm, D), lambda i, f: (i, 0)),
                  pl.BlockSpec((1, D), lambda i, f: (0, 0)),
                  pl.BlockSpec((None, D, tf), lambda i, f: (layer, 0, f)),
                  pl.BlockSpec((None, tf, D), lambda i, f: (layer, f, 0)),
                  pl.BlockSpec((1, D), lambda i, f: (0, 0))],
        out_specs=pl.BlockSpec((tm, D), lambda i, f: (i, 0)),
        scratch_shapes=[pltpu.VMEM((tm, D), BF16)],
        compiler_params=pltpu.CompilerParams(
            dimension_semantics=("arbitrary", "arbitrary"),
            vmem_limit_bytes=_vmem_limit(est)),
        name="mlp",
    )(xf, g_in.reshape(1, D).astype(F32), w_up.astype(BF16), w_down.astype(BF16),
      g_out.reshape(1, D).astype(F32))


def _nsa_layer(xf, B, S, g_in, g_out, rel_table, w_in, cmp_pe, cmp_w1, cmp_w2, w_out):
    D = xf.shape[1]
    G, R, Dh = N_GROUPS, HEADS_PER_GROUP, HEAD_DIM
    n_main = N_HEADS * Dh + 6 * G * Dh
    n_gate = 3 * N_HEADS
    w_main = w_in[:, :n_main].astype(BF16)
    w_gate = jnp.pad(w_in[:, n_main:], ((0, 0), (0, LANES - n_gate))).astype(BF16)
    colscale = jnp.concatenate([jnp.full((N_HEADS * Dh,), Dh ** -0.5 * LOG2_E, F32),
                                jnp.ones((6 * G * Dh,), F32)])[None]

    proj, glog = _matmul(xf, w_main, tm=1024, tn=1024, norm_g=g_in, epi="colscale",
                         colscale=colscale, side_w=w_gate, out_dtype=BF16, name="nsa_proj")
    gates_t = (glog[:, :n_gate].reshape(B, S, 3, G, R).transpose(0, 3, 1, 2, 4)
               .reshape(B, G, S, 3 * R))

    proj3 = proj.reshape(B, S, n_main)
    half = CMP_STRIDE * Dh
    kcmp, vcmp = _compress(proj3, cmp_w1.reshape(2, 2, half, Dh).astype(BF16),
                           cmp_pe.reshape(2, 2, 1, half).astype(F32),
                           cmp_w2.astype(BF16), B, S)

    bias_c, bias_t, bias_far = _bias_tables(rel_table, S)
    attn = _nsa_attention(proj3, kcmp, vcmp, bias_c, bias_t, bias_far, gates_t, B, S)
    return _matmul(attn.reshape(B * S, D), w_out.astype(BF16), tm=512, tn=D,
                   epi="resnorm", res=xf, res_g=g_out, out_dtype=F32, name="nsa_out")


def _hgrn_layer(xf, B, S, layer, g_in, g_out, w_in, hgrn_lb, onorm, w_out):
    D = xf.shape[1]
    proj = _matmul(xf, w_in.astype(BF16), tm=1024, tn=1024, norm_g=g_in,
                   out_dtype=BF16, name="hgrn_proj")
    mixed = _hgrn(proj.reshape(B, S, 4 * D), hgrn_lb, onorm, layer, B, S)
    return _matmul(mixed.reshape(B * S, D), w_out.astype(BF16), tm=512, tn=D,
                   epi="resnorm", res=xf, res_g=g_out, out_dtype=F32, name="hgrn_out")


def kernel(x, norm_g, rel_table, nsa_w_in, nsa_cmp_pe, nsa_cmp_w1, nsa_cmp_w2, nsa_w_out,
           hgrn_w_in, hgrn_lb, hgrn_onorm, hgrn_w_out, mlp_w_up, mlp_w_down):
    B, S, D = x.shape
    depth = norm_g.shape[0]
    assert D == N_HEADS * HEAD_DIM and S % ATT_TILE == 0 and S % HGRN_CHUNK == 0
    xf = x.reshape(B * S, D).astype(F32)
    for layer in range(depth):
        j = layer // 2
        if layer % 2 == 0:
            xf = _nsa_layer(xf, B, S, norm_g[layer, 0], norm_g[layer, 1], rel_table,
                            nsa_w_in[j], nsa_cmp_pe[j], nsa_cmp_w1[j], nsa_cmp_w2[j],
                            nsa_w_out[j])
        else:
            xf = _hgrn_layer(xf, B, S, layer, norm_g[layer, 0], norm_g[layer, 1],
                             hgrn_w_in[j], hgrn_lb, hgrn_onorm[j], hgrn_w_out[j])
        xf = _mlp(xf, norm_g[layer, 2], norm_g[layer, 3], mlp_w_up, mlp_w_down, layer)
    return xf.reshape(B, S, D).astype(x.dtype)
```

```python
import functools
import math

import numpy as np
import jax
import jax.numpy as jnp
from jax import lax
from jax.experimental import pallas as pl
from jax.experimental.pallas import tpu as pltpu

F32 = jnp.float32
BF16 = jnp.bfloat16

N_HEADS = 16
N_GROUPS = 4
HEADS_PER_GROUP = N_HEADS // N_GROUPS
HEAD_DIM = 128
CMP_BLOCK = 32
CMP_STRIDE = 16
SEL_BLOCK = 64
SEL_TOP_N = 8
WINDOW = 512
FORCE_SCORE = 1.0e4
REL_BUCKETS = 32
REL_MAX_DIST = 128
RMS_EPS = 1e-6
NEG_INF = -1.0e30
LOG2_E = math.log2(math.e)

LANES = 128
SUBLANES = 8
VMEM_BYTES_V7X = 64 * 1024 * 1024
VMEM_LIMIT_CAP = VMEM_BYTES_V7X - 8 * 1024 * 1024

ATT_TILE = 256
ATT_ROWS = 128
HGRN_CHUNK = 128
HGRN_SUB = 8
HGRN_HEADS_PER_STEP = 4
NORM_ROWS = 256


def _vmem_limit(nbytes):
    return int(min(VMEM_LIMIT_CAP, max(32 * 1024 * 1024, nbytes)))


def _mm_body(*refs, norm, epi, side, tm):
    it = iter(refs)
    x_ref = next(it)
    g_ref = next(it) if norm else None
    w_ref = next(it)
    ws_ref = next(it) if side else None
    cs_ref = next(it) if epi == "colscale" else None
    res_ref = next(it) if epi == "resnorm" else None
    go_ref = next(it) if epi == "resnorm" else None
    o_ref = next(it)
    os_ref = next(it) if side else None
    hn_ref = next(it) if norm else None

    j = pl.program_id(1)

    if norm:
        @pl.when(j == 0)
        def _():
            g = g_ref[...]

            def step(r, c):
                rows = pl.ds(pl.multiple_of(r * NORM_ROWS, NORM_ROWS), NORM_ROWS)
                xs = x_ref[rows, :]
                ms = jnp.mean(xs * xs, axis=-1, keepdims=True)
                hn_ref[rows, :] = (xs * lax.rsqrt(ms + RMS_EPS) * g).astype(BF16)
                return c

            lax.fori_loop(0, tm // NORM_ROWS, step, 0)

        lhs = hn_ref[...]
    else:
        lhs = x_ref[...]

    if side:
        @pl.when(j == 0)
        def _():
            os_ref[...] = jnp.dot(lhs, ws_ref[...], preferred_element_type=F32)

    acc = jnp.dot(lhs, w_ref[...], preferred_element_type=F32)
    if epi == "colscale":
        acc = acc * cs_ref[...]
    elif epi == "resnorm":
        ms = jnp.mean(acc * acc, axis=-1, keepdims=True)
        acc = res_ref[...] + acc * lax.rsqrt(ms + RMS_EPS) * go_ref[...]
    o_ref[...] = acc.astype(o_ref.dtype)


def _matmul(x, w, *, tm, tn, norm_g=None, epi="none", colscale=None, res=None, res_g=None,
            side_w=None, out_dtype=BF16, name="mm"):
    M, K = x.shape
    N = w.shape[1]
    norm = norm_g is not None
    side = side_w is not None
    assert M % tm == 0 and N % tn == 0
    assert not (epi == "resnorm" and tn != N)

    in_specs = [pl.BlockSpec((tm, K), lambda i, j: (i, 0))]
    args = [x]
    if norm:
        in_specs.append(pl.BlockSpec((1, K), lambda i, j: (0, 0)))
        args.append(norm_g.reshape(1, K).astype(F32))
    in_specs.append(pl.BlockSpec((K, tn), lambda i, j: (0, j)))
    args.append(w)
    ns = side_w.shape[1] if side else 0
    if side:
        in_specs.append(pl.BlockSpec((K, ns), lambda i, j: (0, 0)))
        args.append(side_w)
    if epi == "colscale":
        in_specs.append(pl.BlockSpec((1, tn), lambda i, j: (0, j)))
        args.append(colscale)
    if epi == "resnorm":
        in_specs.append(pl.BlockSpec((tm, tn), lambda i, j: (i, j)))
        args.append(res)
        in_specs.append(pl.BlockSpec((1, tn), lambda i, j: (0, j)))
        args.append(res_g.reshape(1, N).astype(F32))

    out_shape = jax.ShapeDtypeStruct((M, N), out_dtype)
    out_specs = pl.BlockSpec((tm, tn), lambda i, j: (i, j))
    if side:
        out_shape = (out_shape, jax.ShapeDtypeStruct((M, ns), F32))
        out_specs = (out_specs, pl.BlockSpec((tm, ns), lambda i, j: (i, 0)))

    xb = x.dtype.itemsize
    ob = jnp.dtype(out_dtype).itemsize
    est = (2 * tm * K * xb + 2 * K * tn * 2 + 2 * tm * tn * ob + (tm * K * 2 if norm else 0)
           + (2 * tm * tn * 4 if epi == "resnorm" else 0) + 3 * tm * tn * 4
           + 2 * K * ns * 2 + 3 * tm * ns * 4)

    return pl.pallas_call(
        functools.partial(_mm_body, norm=norm, epi=epi, side=side, tm=tm),
        out_shape=out_shape,
        grid=(M // tm, N // tn),
        in_specs=in_specs,
        out_specs=out_specs,
        scratch_shapes=[pltpu.VMEM((tm, K), BF16)] if norm else [],
        compiler_params=pltpu.CompilerParams(
            dimension_semantics=("arbitrary", "arbitrary"),
            vmem_limit_bytes=_vmem_limit(est)),
        name=name,
    )(*args)


def _rel_bucket_np(dist):
    n = np.maximum(dist, 0)
    max_exact = REL_BUCKETS // 2
    nf = np.maximum(n, 1).astype(np.float32)
    ratio = np.log(nf / np.float32(max_exact)) / np.float32(math.log(REL_MAX_DIST / max_exact))
    large = max_exact + (ratio * np.float32(REL_BUCKETS - max_exact)).astype(np.int32)
    large = np.minimum(large, REL_BUCKETS - 1)
    return np.where(n < max_exact, n, large).astype(np.int32)


@functools.lru_cache(maxsize=None)
def _static_maps(seq):
    n_cmp = LANES
    pos = np.arange(seq, dtype=np.int32)[:, None]
    c_end = np.arange(n_cmp, dtype=np.int32)[None, :] * CMP_STRIDE + CMP_BLOCK - 1
    bucket_c = _rel_bucket_np(pos - c_end)
    t = np.arange(ATT_TILE, dtype=np.int32)[:, None]
    k = np.arange(ATT_TILE, dtype=np.int32)[None, :]
    bucket_t = np.stack([_rel_bucket_np(t - k), _rel_bucket_np(ATT_TILE + t - k)])
    assert _rel_bucket_np(np.array([ATT_TILE + 1]))[0] == REL_BUCKETS - 1
    nc = seq // CMP_STRIDE - CMP_BLOCK // CMP_STRIDE + 1
    nb = seq // SEL_BLOCK
    c_start = np.arange(nc)[:, None] * CMP_STRIDE
    b_start = np.arange(nb)[None, :] * SEL_BLOCK
    ov = ((c_start <= b_start + SEL_BLOCK - 1) & (c_start + CMP_BLOCK - 1 >= b_start))
    overlap = np.zeros((LANES, LANES), np.float32)
    overlap[:nc, :nb] = ov
    return bucket_c, bucket_t, overlap


def _bias_body(tab_ref, bc_ref, bt_ref, oc_ref, ot_ref, of_ref, *, seq):
    h = pl.program_id(0)

    def lookup(bmap):
        acc = jnp.zeros(bmap.shape, F32)
        for b in range(REL_BUCKETS):
            acc = jnp.where(bmap == b, tab_ref[b, h], acc)
        return acc

    def step(r, c):
        rows = pl.ds(pl.multiple_of(r * ATT_TILE, ATT_TILE), ATT_TILE)
        oc_ref[0, rows, :] = lookup(bc_ref[rows, :]) * LOG2_E
        return c

    lax.fori_loop(0, seq // ATT_TILE, step, 0)

    tt = lax.broadcasted_iota(jnp.int32, (ATT_TILE, ATT_TILE), 0)
    kk = lax.broadcasted_iota(jnp.int32, (ATT_TILE, ATT_TILE), 1)
    far = tab_ref[REL_BUCKETS - 1, h]
    of_ref[0] = jnp.full(of_ref.shape[1:], far * LOG2_E, F32)
    ot_ref[0, 0] = jnp.zeros((ATT_TILE, ATT_TILE), F32)
    ot_ref[1, 0] = (lookup(bt_ref[1]) - far) * LOG2_E
    ot_ref[2, 0] = jnp.where(kk <= tt, (lookup(bt_ref[0]) - far) * LOG2_E, NEG_INF)
    ot_ref[3, 0] = jnp.where(kk > tt, 0.0, NEG_INF)


def _bias_tables(rel_table, seq):
    bucket_c, bucket_t, _ = _static_maps(seq)
    return pl.pallas_call(
        functools.partial(_bias_body, seq=seq),
        out_shape=(jax.ShapeDtypeStruct((N_HEADS, seq, LANES), F32),
                   jax.ShapeDtypeStruct((4, N_HEADS, ATT_TILE, ATT_TILE), F32),
                   jax.ShapeDtypeStruct((N_HEADS, SUBLANES, LANES), F32)),
        grid=(N_HEADS,),
        in_specs=[pl.BlockSpec(memory_space=pltpu.SMEM),
                  pl.BlockSpec((seq, LANES), lambda h: (0, 0)),
                  pl.BlockSpec((2, ATT_TILE, ATT_TILE), lambda h: (0, 0, 0))],
        out_specs=(pl.BlockSpec((1, seq, LANES), lambda h: (h, 0, 0)),
                   pl.BlockSpec((4, 1, ATT_TILE, ATT_TILE), lambda h: (0, h, 0, 0)),
                   pl.BlockSpec((1, SUBLANES, LANES), lambda h: (h, 0, 0))),
        compiler_params=pltpu.CompilerParams(dimension_semantics=("arbitrary",)),
        name="rel_bias",
    )(rel_table.astype(F32), jnp.asarray(bucket_c), jnp.asarray(bucket_t))


def _compress_body(xk_ref, xv_ref, w1_ref, pe_ref, w2_ref, ok_ref, ov_ref, x_sc):
    n_grp = xk_ref.shape[1] // CMP_STRIDE

    def one(x_ref, idx, o_ref):
        x_sc[...] = x_ref[0].astype(F32)
        x = jnp.concatenate([x_sc[pl.ds(t, n_grp, stride=CMP_STRIDE), :]
                             for t in range(CMP_STRIDE)], axis=1)
        a0 = jnp.dot((x + pe_ref[idx, 0]).astype(BF16), w1_ref[idx, 0],
                     preferred_element_type=F32)
        a1 = jnp.dot((x + pe_ref[idx, 1]).astype(BF16), w1_ref[idx, 1],
                     preferred_element_type=F32)
        pre = a0 + pltpu.roll(a1, LANES - 1, 0)
        hid = jax.nn.gelu(pre).astype(BF16)
        o_ref[0, 0] = jnp.dot(hid, w2_ref[idx], preferred_element_type=F32).astype(BF16)

    one(xk_ref, 0, ok_ref)
    one(xv_ref, 1, ov_ref)


def _compress(proj, w1, pe, w2, B, S):
    G = N_GROUPS
    assert S // CMP_STRIDE == LANES
    half = CMP_STRIDE * HEAD_DIM
    q_cols = N_HEADS

    def spec_x(slot):
        return pl.BlockSpec((1, S, HEAD_DIM), lambda b, g, slot=slot: (b, 0, q_cols + slot * G + g))

    spec_o = pl.BlockSpec((1, 1, LANES, HEAD_DIM), lambda b, g: (b, g, 0, 0))
    out = jax.ShapeDtypeStruct((B, G, LANES, HEAD_DIM), BF16)
    return pl.pallas_call(
        _compress_body,
        out_shape=(out, out),
        grid=(B, G),
        in_specs=[spec_x(0), spec_x(1),
                  pl.BlockSpec((2, 2, half, HEAD_DIM), lambda b, g: (0, 0, 0, 0)),
                  pl.BlockSpec((2, 2, 1, half), lambda b, g: (0, 0, 0, 0)),
                  pl.BlockSpec((2, HEAD_DIM, HEAD_DIM), lambda b, g: (0, 0, 0))],
        out_specs=(spec_o, spec_o),
        scratch_shapes=[pltpu.VMEM((S, HEAD_DIM), F32)],
        compiler_params=pltpu.CompilerParams(dimension_semantics=("arbitrary", "arbitrary")),
        name="nsa_compress",
    )(proj, proj, w1, pe, w2)


def _nsa_body(q_ref, ks_ref, vs_ref, kw_ref, vw_ref, kc_ref, vc_ref, bc_ref, bt_ref, far_ref,
              ovl_ref, augs_ref, augw_ref, gl_ref, o_ref, kts_sc, ktw_sc, q4_sc, m_sc, acc_sc):
    R = HEADS_PER_GROUP
    tq = ATT_TILE
    qi = pl.program_id(2)
    n_tiles = ks_ref.shape[1] // tq
    nb = ovl_ref.shape[0]
    nt = (((1,), (1,)), ((), ()))

    @pl.when(qi == 0)
    def _():
        def tr(j, c):
            rows = pl.ds(pl.multiple_of(j * tq, tq), tq)
            kts_sc[j, :HEAD_DIM, :] = ks_ref[0, rows, :].astype(F32).T.astype(BF16)
            kts_sc[j, HEAD_DIM:, :] = augs_ref[j]
            ktw_sc[j, :HEAD_DIM, :] = kw_ref[0, rows, :].astype(F32).T.astype(BF16)
            ktw_sc[j, HEAD_DIM:, :] = augw_ref[0]
            return c

        lax.fori_loop(0, n_tiles, tr, 0)
        zeros = jnp.zeros((HEAD_DIM, tq), BF16)
        kts_sc[n_tiles, :HEAD_DIM, :] = zeros
        kts_sc[n_tiles, HEAD_DIM:, :] = augs_ref[n_tiles]
        ktw_sc[n_tiles, :HEAD_DIM, :] = zeros
        ktw_sc[n_tiles, HEAD_DIM:, :] = augw_ref[1]

    q = q_ref[0]
    q4 = jnp.concatenate([q[:, r * HEAD_DIM:(r + 1) * HEAD_DIM] for r in range(R)], axis=0)
    q4_sc[:, :HEAD_DIM] = q4

    pos3 = qi * tq + lax.broadcasted_iota(jnp.int32, (1, tq, 1), 1)

    sc = lax.dot_general(q4, kc_ref[0, 0], nt, preferred_element_type=F32)
    sc = sc.reshape(R, tq, LANES) + bc_ref[...]
    cidx = lax.broadcasted_iota(jnp.int32, (1, 1, LANES), 2)
    valid = (cidx * CMP_STRIDE + (CMP_BLOCK - 1) <= pos3) & (cidx < LANES - 1)
    sc = jnp.where(valid, sc, NEG_INF)
    mc = jnp.max(sc, axis=-1, keepdims=True)
    ec = jnp.exp2(sc - mc)
    pc = ec / jnp.sum(ec, axis=-1, keepdims=True)
    pc = jnp.where(pos3 >= CMP_BLOCK - 1, pc, 0.0)
    o_cmp = jnp.dot(pc.reshape(R * tq, LANES).astype(BF16), vc_ref[0, 0],
                    preferred_element_type=F32)

    psum = pc[0]
    for r in range(1, R):
        psum = psum + pc[r]
    p_hi = psum.astype(BF16)
    p_lo = (psum - p_hi.astype(F32)).astype(BF16)
    ovt = ovl_ref[...]
    imp = (lax.dot_general(ovt, p_hi, nt, preferred_element_type=F32)
           + lax.dot_general(ovt, p_lo, nt, preferred_element_type=F32))
    jb = lax.broadcasted_iota(jnp.int32, (nb, 1), 0)
    pos_t = qi * tq + lax.broadcasted_iota(jnp.int32, (1, tq), 1)
    q_blk = lax.shift_right_logical(pos_t, int(math.log2(SEL_BLOCK)))
    forced = (jb == 0) | (jb == q_blk) | (jb == q_blk - 1)
    future = jb > q_blk
    imp = jnp.where(forced, FORCE_SCORE, jnp.where(future, -1.0, imp))
    cnt = jnp.zeros((nb, tq), F32)
    for i in range(nb):
        row = imp[i:i + 1, :]
        beats = (row > imp) | ((row == imp) & (jb > i))
        cnt = cnt + jnp.where(beats, 1.0, 0.0)
    sel_t = jnp.where(cnt < float(min(SEL_TOP_N, nb)), 1.0, 0.0)
    sel = jnp.concatenate([sel_t, jnp.zeros((LANES - nb, tq), F32)], axis=0).T

    lane = lax.broadcasted_iota(jnp.int32, (1, LANES), 1)
    sel_pad = jnp.where(lane < nb, (sel - 1.0) * (-NEG_INF), 0.0)
    for r in range(R):
        far = jnp.broadcast_to(far_ref[r, 0:1, :], (tq, LANES))
        far_hi = far.astype(BF16).astype(F32)
        pad = jnp.where(lane == nb, far_hi, jnp.where(lane == nb + 1, far - far_hi, sel_pad))
        pad = jnp.where(lane == nb + 2, 1.0, pad)
        q4_sc[r * tq:(r + 1) * tq, HEAD_DIM:] = pad.astype(BF16)

    ones = jnp.ones((tq, HEAD_DIM), BF16)
    n_chunks = R * tq // ATT_ROWS
    chunks_per_head = tq // ATT_ROWS
    dead = n_tiles

    def keys(kt_sc, tiles):
        return jnp.concatenate([kt_sc[t] for t in tiles], axis=1)

    def values(v_ref, tiles):
        parts = []
        for t in tiles:
            rows = pl.ds(pl.multiple_of(t * tq, tq), tq)
            parts.append(jnp.concatenate([v_ref[0, rows, :], ones], axis=1))
        return jnp.concatenate(parts, axis=0)

    def logits(ci, kt, kinds):
        r, hh = divmod(ci, chunks_per_head)
        rs = slice(ci * ATT_ROWS, (ci + 1) * ATT_ROWS)
        qs = slice(hh * ATT_ROWS, (hh + 1) * ATT_ROWS)
        s = jnp.dot(q4_sc[rs, :], kt, preferred_element_type=F32)
        return s + jnp.concatenate([bt_ref[kd, r, qs, :] for kd in kinds], axis=1)

    def probs(s, m):
        return jnp.concatenate([jnp.exp2(s[:, k0:k0 + LANES] - m)
                                for k0 in range(0, s.shape[1], LANES)], axis=1).astype(BF16)

    m_sc[...] = jnp.full(m_sc.shape, NEG_INF, F32)
    acc_sc[...] = jnp.zeros(acc_sc.shape, F32)

    def pair(pi, c):
        ja = 2 * pi
        jb = ja + 1
        kt = keys(kts_sc, (ja, jnp.where(jb <= qi, jb, dead)))
        vv = values(vs_ref, (ja, jnp.minimum(jb, n_tiles - 1)))
        kinds = (jnp.clip(ja - qi + 2, 0, 2), jnp.clip(jb - qi + 2, 0, 2))
        for ci in range(n_chunks):
            rs = slice(ci * ATT_ROWS, (ci + 1) * ATT_ROWS)
            s = logits(ci, kt, kinds)
            m_old = m_sc[rs]
            m_new = jnp.maximum(m_old, jnp.max(s, axis=-1, keepdims=True))
            alpha = jnp.exp2(m_old - m_new)
            m_sc[rs] = m_new
            pv = jnp.dot(probs(s, m_new), vv, preferred_element_type=F32)
            acc_sc[rs] = acc_sc[rs] * jnp.concatenate([alpha, alpha], axis=1) + pv
        return c

    lax.fori_loop(0, (qi + 2) // 2, pair, 0)
    acc = acc_sc[...]
    o_sel = acc[:, :HEAD_DIM] / acc[:, HEAD_DIM:]

    n_win = WINDOW // tq
    w_tiles = [qi - n_win + t for t in range(n_win + 1)]
    kt = keys(ktw_sc, [jnp.where(t >= 0, t, dead) for t in w_tiles])
    vv = values(vw_ref, [jnp.maximum(t, 0) for t in w_tiles])
    w_kinds = (3,) + (0,) * (n_win - 2) + (1, 2)
    o_parts = []
    for ci in range(n_chunks):
        s = logits(ci, kt, w_kinds)
        pv = jnp.dot(probs(s, jnp.max(s, axis=-1, keepdims=True)), vv,
                     preferred_element_type=F32)
        o_parts.append(pv[:, :HEAD_DIM] / pv[:, HEAD_DIM:])
    o_win = jnp.concatenate(o_parts, axis=0)

    gates = jax.nn.sigmoid(gl_ref[0, 0])
    outs = []
    for r in range(R):
        hs = slice(r * tq, (r + 1) * tq)
        o_r = (gates[:, r:r + 1] * o_cmp[hs]
               + gates[:, R + r:R + r + 1] * o_sel[hs]
               + gates[:, 2 * R + r:2 * R + r + 1] * o_win[hs])
        outs.append(o_r)
    o_ref[0] = jnp.concatenate(outs, axis=1).astype(o_ref.dtype)


def _nsa_attention(proj, kcmp, vcmp, bias_c, bias_t, bias_far, gates_t, B, S):
    assert WINDOW % ATT_TILE == 0 and S % ATT_TILE == 0
    assert S // CMP_STRIDE == LANES and S // SEL_BLOCK <= LANES
    R, G, tq = HEADS_PER_GROUP, N_GROUPS, ATT_TILE
    n_tiles = S // tq
    nb = S // SEL_BLOCK
    _, _, overlap = _static_maps(S)
    overlap_t = np.ascontiguousarray(overlap.T[:nb])
    assert nb + 3 <= LANES and n_tiles % 2 == 0
    blk_of_key = (np.arange(S) // SEL_BLOCK).reshape(n_tiles, 1, tq)
    aug_sel = np.zeros((n_tiles + 1, LANES, tq), np.float32)
    aug_sel[:n_tiles] = np.arange(LANES).reshape(1, LANES, 1) == blk_of_key
    aug_sel[:n_tiles, nb:nb + 2, :] = 1.0
    aug_sel[n_tiles, nb + 2, :] = NEG_INF
    aug_win = np.zeros((2, LANES, tq), np.float32)
    aug_win[0, nb:nb + 2, :] = 1.0
    aug_win[1, nb + 2, :] = NEG_INF
    q_cols = N_HEADS

    def kv_spec(slot):
        return pl.BlockSpec((1, S, HEAD_DIM),
                            lambda b, g, i, slot=slot: (b, 0, q_cols + slot * G + g))

    cmp_spec = pl.BlockSpec((1, 1, LANES, HEAD_DIM), lambda b, g, i: (b, g, 0, 0))
    in_specs = [
        pl.BlockSpec((1, tq, R * HEAD_DIM), lambda b, g, i: (b, i, g)),
        kv_spec(2), kv_spec(3), kv_spec(4), kv_spec(5),
        cmp_spec, cmp_spec,
        pl.BlockSpec((R, tq, LANES), lambda b, g, i: (g, i, 0)),
        pl.BlockSpec((4, R, tq, tq), lambda b, g, i: (0, g, 0, 0)),
        pl.BlockSpec((R, SUBLANES, LANES), lambda b, g, i: (g, 0, 0)),
        pl.BlockSpec((nb, LANES), lambda b, g, i: (0, 0)),
        pl.BlockSpec((n_tiles + 1, LANES, tq), lambda b, g, i: (0, 0, 0)),
        pl.BlockSpec((2, LANES, tq), lambda b, g, i: (0, 0, 0)),
        pl.BlockSpec((1, 1, tq, 3 * R), lambda b, g, i: (b, g, i, 0)),
    ]
    kdim = HEAD_DIM + LANES
    est = (2 * 4 * S * HEAD_DIM * 2 + 2 * 4 * R * tq * tq * 4 + 2 * R * tq * LANES * 4
           + 2 * 2 * S * LANES * 2 + 2 * S * kdim * 2 + R * tq * (kdim * 2 + LANES * 4 + 2 * HEAD_DIM * 4)
           + 3 * R * tq * LANES * 4 + 16 * ATT_ROWS * tq * 4 + 4 * tq * R * HEAD_DIM * 2)
    return pl.pallas_call(
        _nsa_body,
        out_shape=jax.ShapeDtypeStruct((B, S, N_HEADS * HEAD_DIM), BF16),
        grid=(B, G, n_tiles),
        in_specs=in_specs,
        out_specs=pl.BlockSpec((1, tq, R * HEAD_DIM), lambda b, g, i: (b, i, g)),
        scratch_shapes=[pltpu.VMEM((n_tiles + 1, kdim, tq), BF16),
                        pltpu.VMEM((n_tiles + 1, kdim, tq), BF16),
                        pltpu.VMEM((R * tq, kdim), BF16),
                        pltpu.VMEM((R * tq, LANES), F32),
                        pltpu.VMEM((R * tq, 2 * HEAD_DIM), F32)],
        compiler_params=pltpu.CompilerParams(
            dimension_semantics=("arbitrary", "arbitrary", "arbitrary"),
            vmem_limit_bytes=_vmem_limit(est)),
        name="nsa_attention",
    )(proj, proj, proj, proj, proj, kcmp, vcmp, bias_c, bias_t, bias_far,
      jnp.asarray(overlap_t, BF16), jnp.asarray(aug_sel, BF16), jnp.asarray(aug_win, BF16),
      gates_t)


@functools.lru_cache(maxsize=None)
def _hgrn_masks():
    C = HGRN_CHUNK
    t = np.arange(C)[:, None]
    s = np.arange(C)[None, :]
    masks = [(t // HGRN_SUB == s // HGRN_SUB) & (s <= t)]
    half = C // 2
    while half >= HGRN_SUB:
        grp = 2 * half
        masks.append((t // grp == s // grp) & (t % grp >= half) & (s % grp < half))
        half //= 2
    assert np.array_equal(np.sum(masks, axis=0), (s <= t).astype(int))
    return np.stack(masks).astype(np.float32), (s <= t).astype(np.float32)


def _hgrn_body(q_ref, f_ref, i_ref, g_ref, lb_ref, gn_ref, msk_ref, tril_ref, o_ref,
               st_all, b_all, k_all, q_all, *, layer, heads):
    C = HGRN_CHUNK
    n_chunks = q_ref.shape[1] // C
    nt = (((1,), (1,)), ((), ()))

    lbp = lb_ref[...]
    e = jnp.exp(lbp - jnp.max(lbp, axis=0, keepdims=True))
    sm = e / jnp.sum(e, axis=0, keepdims=True)
    cum = sm[0:1]
    first = cum
    for d in range(1, layer + 1):
        cum = cum + sm[d:d + 1]
    lb_all = cum - first
    log_lb_all = jnp.log(lb_all)
    log_1m_all = jnp.log1p(-lb_all)
    gn_all = gn_ref[...]

    cols = lax.broadcasted_iota(jnp.int32, (1, C), 1)
    tril = tril_ref[...]

    st_all[...] = jnp.zeros(st_all.shape, F32)

    def chunk(c, carry):
        for hh in range(heads):
            one_head(c, hh)
        return carry

    def one_head(c, hh):
        sl = pl.ds(pl.multiple_of(c * C, C), C)
        hs = slice(hh * HEAD_DIM, (hh + 1) * HEAD_DIM)
        lb, log_lb, log_1m, gn = lb_all[:, hs], log_lb_all[:, hs], log_1m_all[:, hs], gn_all[:, hs]
        st_ref, b_sc, k_sc, q_sc = st_all.at[hh], b_all.at[hh], k_all.at[hh], q_all.at[hh]
        qr = q_ref[0, sl, hs].astype(F32)
        x = f_ref[0, sl, hs].astype(F32)
        v = i_ref[0, sl, hs].astype(F32)
        gr = g_ref[0, sl, hs].astype(F32)
        q = qr / (1.0 + jnp.exp2(qr * (-LOG2_E)))
        ex = jnp.exp2(jnp.abs(x) * (-LOG2_E))
        u = 1.0 + ex
        r1 = 1.0 / u
        k = (1.0 - lb) * jnp.where(x >= 0.0, ex * r1, r1)
        c2 = log_1m + (jnp.minimum(x, 0.0) - jnp.log(u))
        e2 = jnp.exp2(jnp.abs(log_lb - c2) * (-LOG2_E))
        log_f = jnp.maximum(log_lb, c2) + jnp.log(1.0 + e2)

        lf_hi = log_f.astype(BF16)
        lf_lo = (log_f - lf_hi.astype(F32)).astype(BF16)
        b = (jnp.dot(tril, lf_hi, preferred_element_type=F32)
             + jnp.dot(tril, lf_lo, preferred_element_type=F32)) * LOG2_E
        b_sc[...] = b
        k_sc[...] = k
        q_sc[...] = q

        vb = v.astype(BF16)
        st = st_ref[...]
        o = lax.dot_general((q * jnp.exp2(b)).astype(BF16), st.astype(BF16), nt,
                            preferred_element_type=F32)

        pieces = []
        for blk in range(C // HGRN_SUB):
            r0 = blk * HGRN_SUB
            bt = b_sc[r0:r0 + HGRN_SUB, :]
            qt = q_sc[r0:r0 + HGRN_SUB, :]
            arow = jnp.zeros((HGRN_SUB, C), F32)
            for s in range(HGRN_SUB):
                bs = b_sc[r0 + s:r0 + s + 1, :]
                ks = k_sc[r0 + s:r0 + s + 1, :]
                col = jnp.sum(jnp.exp2(bt - bs) * qt * ks, axis=-1, keepdims=True)
                arow = jnp.where(cols == r0 + s, col, arow)
            pieces.append(arow)
        a = jnp.where(msk_ref[0] > 0.5, jnp.concatenate(pieces, axis=0), 0.0)

        half = C // 2
        lvl = 1
        while half >= HGRN_SUB:
            grp = 2 * half
            anc = jnp.concatenate(
                [jnp.broadcast_to(b_sc[g0 + half - 1:g0 + half, :], (grp, b.shape[1]))
                 for g0 in range(0, C, grp)], axis=0)
            e = jnp.exp2(-jnp.abs(b - anc))
            al = lax.dot_general((q * e).astype(BF16), (k * e).astype(BF16), nt,
                                 preferred_element_type=F32)
            a = jnp.where(msk_ref[lvl] > 0.5, al, a)
            half //= 2
            lvl += 1

        o = o + jnp.dot(a.astype(BF16), vb, preferred_element_type=F32)

        b_last = b_sc[C - 1:C, :]
        kh = (k * jnp.exp2(b_last - b)).astype(BF16)
        st_ref[...] = st * jnp.exp2(b_last) + jnp.dot(v.T.astype(BF16), kh,
                                                     preferred_element_type=F32)

        ms = jnp.mean(o * o, axis=-1, keepdims=True)
        o = o * lax.rsqrt(ms + RMS_EPS) * gn * (gr / (1.0 + jnp.exp2(gr * (-LOG2_E))))
        o_ref[0, sl, hs] = o.astype(o_ref.dtype)

    lax.fori_loop(0, n_chunks, chunk, 0)


def _hgrn(proj, hgrn_lb, onorm, layer, B, S):
    H = N_HEADS
    depth = hgrn_lb.shape[0]

    nh = HGRN_HEADS_PER_STEP
    width = nh * HEAD_DIM
    steps = H // nh

    def spec(part):
        return pl.BlockSpec((1, S, width), lambda b, h, part=part: (b, 0, part * steps + h))

    masks, tril = _hgrn_masks()
    C = HGRN_CHUNK
    est = 2 * 4 * S * width * 4 + 2 * S * width * 2 + nh * 64 * C * C * 4
    return pl.pallas_call(
        functools.partial(_hgrn_body, layer=layer, heads=nh),
        out_shape=jax.ShapeDtypeStruct((B, S, H * HEAD_DIM), BF16),
        grid=(B, steps),
        in_specs=[spec(0), spec(1), spec(2), spec(3),
                  pl.BlockSpec((depth, width), lambda b, h: (0, h)),
                  pl.BlockSpec((1, width), lambda b, h: (0, h)),
                  pl.BlockSpec(masks.shape, lambda b, h: (0, 0, 0)),
                  pl.BlockSpec((C, C), lambda b, h: (0, 0))],
        out_specs=pl.BlockSpec((1, S, width), lambda b, h: (b, 0, h)),
        scratch_shapes=[pltpu.VMEM((nh, HEAD_DIM, HEAD_DIM), F32),
                        pltpu.VMEM((nh, C, HEAD_DIM), F32),
                        pltpu.VMEM((nh, C, HEAD_DIM), F32),
                        pltpu.VMEM((nh, C, HEAD_DIM), F32)],
        compiler_params=pltpu.CompilerParams(
            dimension_semantics=("arbitrary", "arbitrary"),
            vmem_limit_bytes=_vmem_limit(est)),
        name="hgrn2_recurrence",
    )(proj, proj, proj, proj, hgrn_lb.astype(F32), onorm.reshape(1, -1).astype(F32),
      jnp.asarray(masks), jnp.asarray(tril, BF16))


def _mlp_body(x_ref, gi_ref, wu_ref, wd_ref, go_ref, o_ref, hn_ref, *, nf, tm):
    f = pl.program_id(1)

    def row_chunks(fn):
        def step(r, c):
            fn(pl.ds(pl.multiple_of(r * NORM_ROWS, NORM_ROWS), NORM_ROWS))
            return c

        lax.fori_loop(0, tm // NORM_ROWS, step, 0)

    @pl.when(f == 0)
    def _():
        g = gi_ref[...]

        def norm_in(rows):
            xs = x_ref[rows, :]
            ms = jnp.mean(xs * xs, axis=-1, keepdims=True)
            hn_ref[rows, :] = (xs * lax.rsqrt(ms + RMS_EPS) * g).astype(BF16)

        row_chunks(norm_in)
        o_ref[...] = jnp.zeros(o_ref.shape, F32)

    hid = jnp.dot(hn_ref[...], wu_ref[...], preferred_element_type=F32)
    hid = jnp.square(jnp.maximum(hid, 0.0)).astype(BF16)
    o_ref[...] += jnp.dot(hid, wd_ref[...], preferred_element_type=F32)

    @pl.when(f == nf - 1)
    def _():
        g = go_ref[...]

        def norm_out(rows):
            y = o_ref[rows, :]
            ms = jnp.mean(y * y, axis=-1, keepdims=True)
            o_ref[rows, :] = x_ref[rows, :] + y * lax.rsqrt(ms + RMS_EPS) * g

        row_chunks(norm_out)


def _mlp(xf, g_in, g_out, w_up, w_down, layer, *, tm=512, tf=1024):
    M, D = xf.shape
    F = w_up.shape[2]
    assert M % tm == 0 and F % tf == 0 and tm % NORM_ROWS == 0
    nf = F // tf
    est = (2 * tm * D * 4 + 2 * tm * D * 4 + tm * D * 2 + 2 * 2 * D * tf * 2
           + tm * tf * 6 + 2 * tm * D * 4)
    return pl.pallas_call(
        functools.partial(_mlp_body, nf=nf, tm=tm),
        out_shape=jax.ShapeDtypeStruct((M, D), F32),
        grid=(M // tm, nf),
        in_specs=[pl.BlockSpec((tm, D), lambda i, f: (i, 0)),
                  pl.BlockSpec((1, D), lambda i, f: (0, 0)),
                  pl.BlockSpec((None, D, tf), lambda i, f: (layer, 0, f)),
                  pl.BlockSpec((None, tf, D), lambda i, f: (layer, f, 0)),
                  pl.BlockSpec((1, D), lambda i, f: (0, 0))],
        out_specs=pl.BlockSpec((tm, D), lambda i, f: (i, 0)),
        scratch_shapes=[pltpu.VMEM((tm, D), BF16)],
        compiler_params=pltpu.CompilerParams(
            dimension_semantics=("arbitrary", "arbitrary"),
            vmem_limit_bytes=_vmem_limit(est)),
        name="mlp",
    )(xf, g_in.reshape(1, D).astype(F32), w_up.astype(BF16), w_down.astype(BF16),
      g_out.reshape(1, D).astype(F32))


def _nsa_layer(xf, B, S, g_in, g_out, rel_table, w_in, cmp_pe, cmp_w1, cmp_w2, w_out):
    D = xf.shape[1]
    G, R, Dh = N_GROUPS, HEADS_PER_GROUP, HEAD_DIM
    n_main = N_HEADS * Dh + 6 * G * Dh
    n_gate = 3 * N_HEADS
    w_main = w_in[:, :n_main].astype(BF16)
    w_gate = jnp.pad(w_in[:, n_main:], ((0, 0), (0, LANES - n_gate))).astype(BF16)
    colscale = jnp.concatenate([jnp.full((N_HEADS * Dh,), Dh ** -0.5 * LOG2_E, F32),
                                jnp.ones((6 * G * Dh,), F32)])[None]

    proj, glog = _matmul(xf, w_main, tm=1024, tn=1024, norm_g=g_in, epi="colscale",
                         colscale=colscale, side_w=w_gate, out_dtype=BF16, name="nsa_proj")
    gates_t = (glog[:, :n_gate].reshape(B, S, 3, G, R).transpose(0, 3, 1, 2, 4)
               .reshape(B, G, S, 3 * R))

    proj3 = proj.reshape(B, S, n_main)
    half = CMP_STRIDE * Dh
    kcmp, vcmp = _compress(proj3, cmp_w1.reshape(2, 2, half, Dh).astype(BF16),
                           cmp_pe.reshape(2, 2, 1, half).astype(F32),
                           cmp_w2.astype(BF16), B, S)

    bias_c, bias_t, bias_far = _bias_tables(rel_table, S)
    attn = _nsa_attention(proj3, kcmp, vcmp, bias_c, bias_t, bias_far, gates_t, B, S)
    return _matmul(attn.reshape(B * S, D), w_out.astype(BF16), tm=512, tn=D,
                   epi="resnorm", res=xf, res_g=g_out, out_dtype=F32, name="nsa_out")


def _hgrn_layer(xf, B, S, layer, g_in, g_out, w_in, hgrn_lb, onorm, w_out):
    D = xf.shape[1]
    proj = _matmul(xf, w_in.astype(BF16), tm=1024, tn=1024, norm_g=g_in,
                   out_dtype=BF16, name="hgrn_proj")
    mixed = _hgrn(proj.reshape(B, S, 4 * D), hgrn_lb, onorm, layer, B, S)
    return _matmul(mixed.reshape(B * S, D), w_out.astype(BF16), tm=512, tn=D,
                   epi="resnorm", res=xf, res_g=g_out, out_dtype=F32, name="hgrn_out")


def kernel(x, norm_g, rel_table, nsa_w_in, nsa_cmp_pe, nsa_cmp_w1, nsa_cmp_w2, nsa_w_out,
           hgrn_w_in, hgrn_lb, hgrn_onorm, hgrn_w_out, mlp_w_up, mlp_w_down):
    B, S, D = x.shape
    depth = norm_g.shape[0]
    assert D == N_HEADS * HEAD_DIM and S % ATT_TILE == 0 and S % HGRN_CHUNK == 0
    xf = x.reshape(B * S, D).astype(F32)
    for layer in range(depth):
        j = layer // 2
        if layer % 2 == 0:
            xf = _nsa_layer(xf, B, S, norm_g[layer, 0], norm_g[layer, 1], rel_table,
                            nsa_w_in[j], nsa_cmp_pe[j], nsa_cmp_w1[j], nsa_cmp_w2[j],
                            nsa_w_out[j])
        else:
            xf = _hgrn_layer(xf, B, S, layer, norm_g[layer, 0], norm_g[layer, 1],
                             hgrn_w_in[j], hgrn_lb, hgrn_onorm[j], hgrn_w_out[j])
        xf = _mlp(xf, norm_g[layer, 2], norm_g[layer, 3], mlp_w_up, mlp_w_down, layer)
    return xf.reshape(B, S, D).astype(x.dtype)
```

```python
import functools
import math

import numpy as np
import jax
import jax.numpy as jnp
from jax import lax
from jax.experimental import pallas as pl
from jax.experimental.pallas import tpu as pltpu

F32 = jnp.float32
BF16 = jnp.bfloat16

N_HEADS = 16
N_GROUPS = 4
HEADS_PER_GROUP = N_HEADS // N_GROUPS
HEAD_DIM = 128
CMP_BLOCK = 32
CMP_STRIDE = 16
SEL_BLOCK = 64
SEL_TOP_N = 8
WINDOW = 512
FORCE_SCORE = 1.0e4
REL_BUCKETS = 32
REL_MAX_DIST = 128
RMS_EPS = 1e-6
NEG_INF = -1.0e30
LOG2_E = math.log2(math.e)

LANES = 128
SUBLANES = 8
VMEM_BYTES_V7X = 64 * 1024 * 1024
VMEM_LIMIT_CAP = VMEM_BYTES_V7X - 8 * 1024 * 1024

ATT_TILE = 256
ATT_ROWS = 128
HGRN_CHUNK = 128
HGRN_SUB = 8
HGRN_HEADS_PER_STEP = 8
NORM_ROWS = 256


def _vmem_limit(nbytes):
    return int(min(VMEM_LIMIT_CAP, max(32 * 1024 * 1024, nbytes)))


def _mm_body(*refs, norm, epi, side, tm):
    it = iter(refs)
    x_ref = next(it)
    g_ref = next(it) if norm else None
    w_ref = next(it)
    ws_ref = next(it) if side else None
    cs_ref = next(it) if epi == "colscale" else None
    res_ref = next(it) if epi == "resnorm" else None
    go_ref = next(it) if epi == "resnorm" else None
    o_ref = next(it)
    os_ref = next(it) if side else None
    hn_ref = next(it) if norm else None

    j = pl.program_id(1)

    if norm:
        @pl.when(j == 0)
        def _():
            g = g_ref[...]

            def step(r, c):
                rows = pl.ds(pl.multiple_of(r * NORM_ROWS, NORM_ROWS), NORM_ROWS)
                xs = x_ref[rows, :]
                ms = jnp.mean(xs * xs, axis=-1, keepdims=True)
                hn_ref[rows, :] = (xs * lax.rsqrt(ms + RMS_EPS) * g).astype(BF16)
                return c

            lax.fori_loop(0, tm // NORM_ROWS, step, 0)

        lhs = hn_ref[...]
    else:
        lhs = x_ref[...]

    if side:
        @pl.when(j == 0)
        def _():
            os_ref[...] = jnp.dot(lhs, ws_ref[...], preferred_element_type=F32)

    acc = jnp.dot(lhs, w_ref[...], preferred_element_type=F32)
    if epi == "colscale":
        acc = acc * cs_ref[...]
    elif epi == "resnorm":
        ms = jnp.mean(acc * acc, axis=-1, keepdims=True)
        acc = res_ref[...] + acc * lax.rsqrt(ms + RMS_EPS) * go_ref[...]
    o_ref[...] = acc.astype(o_ref.dtype)


def _matmul(x, w, *, tm, tn, norm_g=None, epi="none", colscale=None, res=None, res_g=None,
            side_w=None, n_out=None, out_dtype=BF16, name="mm"):
    M, K = x.shape
    N = w.shape[1] if n_out is None else n_out
    assert N <= w.shape[1]
    norm = norm_g is not None
    side = side_w is not None
    assert M % tm == 0 and N % tn == 0
    assert not (epi == "resnorm" and tn != N)

    in_specs = [pl.BlockSpec((tm, K), lambda i, j: (i, 0))]
    args = [x]
    if norm:
        in_specs.append(pl.BlockSpec((1, K), lambda i, j: (0, 0)))
        args.append(norm_g.reshape(1, K).astype(F32))
    in_specs.append(pl.BlockSpec((K, tn), lambda i, j: (0, j)))
    args.append(w)
    ns = side_w.shape[1] if side else 0
    if side:
        in_specs.append(pl.BlockSpec((K, ns), lambda i, j: (0, 0)))
        args.append(side_w)
    if epi == "colscale":
        in_specs.append(pl.BlockSpec((1, tn), lambda i, j: (0, j)))
        args.append(colscale)
    if epi == "resnorm":
        in_specs.append(pl.BlockSpec((tm, tn), lambda i, j: (i, j)))
        args.append(res)
        in_specs.append(pl.BlockSpec((1, tn), lambda i, j: (0, j)))
        args.append(res_g.reshape(1, N).astype(F32))

    out_shape = jax.ShapeDtypeStruct((M, N), out_dtype)
    out_specs = pl.BlockSpec((tm, tn), lambda i, j: (i, j))
    if side:
        out_shape = (out_shape, jax.ShapeDtypeStruct((M, ns), F32))
        out_specs = (out_specs, pl.BlockSpec((tm, ns), lambda i, j: (i, 0)))

    xb = x.dtype.itemsize
    ob = jnp.dtype(out_dtype).itemsize
    est = (2 * tm * K * xb + 2 * K * tn * 2 + 2 * tm * tn * ob + (tm * K * 2 if norm else 0)
           + (2 * tm * tn * 4 if epi == "resnorm" else 0) + 3 * tm * tn * 4
           + 2 * K * ns * 2 + 3 * tm * ns * 4)

    return pl.pallas_call(
        functools.partial(_mm_body, norm=norm, epi=epi, side=side, tm=tm),
        out_shape=out_shape,
        grid=(M // tm, N // tn),
        in_specs=in_specs,
        out_specs=out_specs,
        scratch_shapes=[pltpu.VMEM((tm, K), BF16)] if norm else [],
        compiler_params=pltpu.CompilerParams(
            dimension_semantics=("arbitrary", "arbitrary"),
            vmem_limit_bytes=_vmem_limit(est)),
        name=name,
    )(*args)


def _rel_bucket_np(dist):
    n = np.maximum(dist, 0)
    max_exact = REL_BUCKETS // 2
    nf = np.maximum(n, 1).astype(np.float32)
    ratio = np.log(nf / np.float32(max_exact)) / np.float32(math.log(REL_MAX_DIST / max_exact))
    large = max_exact + (ratio * np.float32(REL_BUCKETS - max_exact)).astype(np.int32)
    large = np.minimum(large, REL_BUCKETS - 1)
    return np.where(n < max_exact, n, large).astype(np.int32)


@functools.lru_cache(maxsize=None)
def _static_maps(seq):
    n_cmp = LANES
    pos = np.arange(seq, dtype=np.int32)[:, None]
    c_end = np.arange(n_cmp, dtype=np.int32)[None, :] * CMP_STRIDE + CMP_BLOCK - 1
    bucket_c = _rel_bucket_np(pos - c_end)
    t = np.arange(ATT_TILE, dtype=np.int32)[:, None]
    k = np.arange(ATT_TILE, dtype=np.int32)[None, :]
    bucket_t = np.stack([_rel_bucket_np(t - k), _rel_bucket_np(ATT_TILE + t - k)])
    assert _rel_bucket_np(np.array([ATT_TILE + 1]))[0] == REL_BUCKETS - 1
    nc = seq // CMP_STRIDE - CMP_BLOCK // CMP_STRIDE + 1
    nb = seq // SEL_BLOCK
    c_start = np.arange(nc)[:, None] * CMP_STRIDE
    b_start = np.arange(nb)[None, :] * SEL_BLOCK
    ov = ((c_start <= b_start + SEL_BLOCK - 1) & (c_start + CMP_BLOCK - 1 >= b_start))
    overlap = np.zeros((LANES, LANES), np.float32)
    overlap[:nc, :nb] = ov
    return bucket_c, bucket_t, overlap


def _bias_body(tab_ref, bc_ref, bt_ref, oc_ref, ot_ref, of_ref, *, seq):
    h = pl.program_id(0)

    def lookup(bmap):
        acc = jnp.zeros(bmap.shape, F32)
        for b in range(REL_BUCKETS):
            acc = jnp.where(bmap == b, tab_ref[b, h], acc)
        return acc

    def step(r, c):
        rows = pl.ds(pl.multiple_of(r * ATT_TILE, ATT_TILE), ATT_TILE)
        oc_ref[0, rows, :] = lookup(bc_ref[rows, :]) * LOG2_E
        return c

    lax.fori_loop(0, seq // ATT_TILE, step, 0)

    tt = lax.broadcasted_iota(jnp.int32, (ATT_TILE, ATT_TILE), 0)
    kk = lax.broadcasted_iota(jnp.int32, (ATT_TILE, ATT_TILE), 1)
    far = tab_ref[REL_BUCKETS - 1, h]
    of_ref[0] = jnp.full(of_ref.shape[1:], far * LOG2_E, F32)
    ot_ref[0, 0] = jnp.zeros((ATT_TILE, ATT_TILE), F32)
    ot_ref[1, 0] = (lookup(bt_ref[1]) - far) * LOG2_E
    ot_ref[2, 0] = jnp.where(kk <= tt, (lookup(bt_ref[0]) - far) * LOG2_E, NEG_INF)
    ot_ref[3, 0] = jnp.where(kk > tt, 0.0, NEG_INF)


def _bias_tables(rel_table, seq):
    bucket_c, bucket_t, _ = _static_maps(seq)
    return pl.pallas_call(
        functools.partial(_bias_body, seq=seq),
        out_shape=(jax.ShapeDtypeStruct((N_HEADS, seq, LANES), F32),
                   jax.ShapeDtypeStruct((4, N_HEADS, ATT_TILE, ATT_TILE), F32),
                   jax.ShapeDtypeStruct((N_HEADS, SUBLANES, LANES), F32)),
        grid=(N_HEADS,),
        in_specs=[pl.BlockSpec(memory_space=pltpu.SMEM),
                  pl.BlockSpec((seq, LANES), lambda h: (0, 0)),
                  pl.BlockSpec((2, ATT_TILE, ATT_TILE), lambda h: (0, 0, 0))],
        out_specs=(pl.BlockSpec((1, seq, LANES), lambda h: (h, 0, 0)),
                   pl.BlockSpec((4, 1, ATT_TILE, ATT_TILE), lambda h: (0, h, 0, 0)),
                   pl.BlockSpec((1, SUBLANES, LANES), lambda h: (h, 0, 0))),
        compiler_params=pltpu.CompilerParams(dimension_semantics=("arbitrary",)),
        name="rel_bias",
    )(rel_table.astype(F32), jnp.asarray(bucket_c), jnp.asarray(bucket_t))


def _compress_body(xk_ref, xv_ref, w1_ref, pe_ref, w2_ref, ok_ref, ov_ref, x_sc):
    n_grp = xk_ref.shape[1] // CMP_STRIDE

    def one(x_ref, idx, o_ref):
        x_sc[...] = x_ref[0].astype(F32)
        x = jnp.concatenate([x_sc[pl.ds(t, n_grp, stride=CMP_STRIDE), :]
                             for t in range(CMP_STRIDE)], axis=1)
        a0 = jnp.dot((x + pe_ref[idx, 0]).astype(BF16), w1_ref[idx, 0],
                     preferred_element_type=F32)
        a1 = jnp.dot((x + pe_ref[idx, 1]).astype(BF16), w1_ref[idx, 1],
                     preferred_element_type=F32)
        pre = a0 + pltpu.roll(a1, LANES - 1, 0)
        hid = jax.nn.gelu(pre).astype(BF16)
        o_ref[0, 0] = jnp.dot(hid, w2_ref[idx], preferred_element_type=F32).astype(BF16)

    one(xk_ref, 0, ok_ref)
    one(xv_ref, 1, ov_ref)


def _compress(proj, w1, pe, w2, B, S):
    G = N_GROUPS
    assert S // CMP_STRIDE == LANES
    half = CMP_STRIDE * HEAD_DIM
    q_cols = N_HEADS

    def spec_x(slot):
        return pl.BlockSpec((1, S, HEAD_DIM), lambda b, g, slot=slot: (b, 0, q_cols + slot * G + g))

    spec_o = pl.BlockSpec((1, 1, LANES, HEAD_DIM), lambda b, g: (b, g, 0, 0))
    out = jax.ShapeDtypeStruct((B, G, LANES, HEAD_DIM), BF16)
    return pl.pallas_call(
        _compress_body,
        out_shape=(out, out),
        grid=(B, G),
        in_specs=[spec_x(0), spec_x(1),
                  pl.BlockSpec((2, 2, half, HEAD_DIM), lambda b, g: (0, 0, 0, 0)),
                  pl.BlockSpec((2, 2, 1, half), lambda b, g: (0, 0, 0, 0)),
                  pl.BlockSpec((2, HEAD_DIM, HEAD_DIM), lambda b, g: (0, 0, 0))],
        out_specs=(spec_o, spec_o),
        scratch_shapes=[pltpu.VMEM((S, HEAD_DIM), F32)],
        compiler_params=pltpu.CompilerParams(dimension_semantics=("arbitrary", "arbitrary")),
        name="nsa_compress",
    )(proj, proj, w1, pe, w2)


def _nsa_body(q_ref, ks_ref, vs_ref, kw_ref, vw_ref, kc_ref, vc_ref, bc_ref, bt_ref, far_ref,
              ovl_ref, augs_ref, augw_ref, gl_ref, o_ref, kts_sc, ktw_sc, q4_sc, qw_sc, m_sc,
              acc_sc, og_sc):
    R = HEADS_PER_GROUP
    tq = ATT_TILE
    qi = pl.program_id(2)
    n_tiles = ks_ref.shape[1] // tq
    nb = ovl_ref.shape[0]
    nt = (((1,), (1,)), ((), ()))

    @pl.when(qi == 0)
    def _():
        def tr(j, c):
            rows = pl.ds(pl.multiple_of(j * tq, tq), tq)
            kts_sc[j, :HEAD_DIM, :] = ks_ref[0, rows, :].astype(F32).T.astype(BF16)
            kts_sc[j, HEAD_DIM:, :] = augs_ref[j]
            ktw_sc[j, :HEAD_DIM, :] = kw_ref[0, rows, :].astype(F32).T.astype(BF16)
            ktw_sc[j, HEAD_DIM:, :] = augw_ref[0]
            return c

        lax.fori_loop(0, n_tiles, tr, 0)
        zeros = jnp.zeros((HEAD_DIM, tq), BF16)
        kts_sc[n_tiles, :HEAD_DIM, :] = zeros
        kts_sc[n_tiles, HEAD_DIM:, :] = augs_ref[n_tiles]
        ktw_sc[n_tiles, :HEAD_DIM, :] = zeros
        ktw_sc[n_tiles, HEAD_DIM:, :] = augw_ref[1]

    q = q_ref[0]
    q4 = jnp.concatenate([q[:, r * HEAD_DIM:(r + 1) * HEAD_DIM] for r in range(R)], axis=0)
    q4_sc[:, :HEAD_DIM] = q4

    pos3 = qi * tq + lax.broadcasted_iota(jnp.int32, (1, tq, 1), 1)

    sc = lax.dot_general(q4, kc_ref[0, 0], nt, preferred_element_type=F32)
    sc = sc.reshape(R, tq, LANES) + bc_ref[...]
    cidx = lax.broadcasted_iota(jnp.int32, (1, 1, LANES), 2)
    valid = (cidx * CMP_STRIDE + (CMP_BLOCK - 1) <= pos3) & (cidx < LANES - 1)
    sc = jnp.where(valid, sc, NEG_INF)
    mc = jnp.max(sc, axis=-1, keepdims=True)
    ec = jnp.exp2(sc - mc)
    pc = ec / jnp.sum(ec, axis=-1, keepdims=True)
    pc = jnp.where(pos3 >= CMP_BLOCK - 1, pc, 0.0)
    o_cmp = jnp.dot(pc.reshape(R * tq, LANES).astype(BF16), vc_ref[0, 0],
                    preferred_element_type=F32)

    psum = pc[0]
    for r in range(1, R):
        psum = psum + pc[r]
    p_hi = psum.astype(BF16)
    p_lo = (psum - p_hi.astype(F32)).astype(BF16)
    ovt = ovl_ref[...]
    imp = (lax.dot_general(ovt, p_hi, nt, preferred_element_type=F32)
           + lax.dot_general(ovt, p_lo, nt, preferred_element_type=F32))
    jb = lax.broadcasted_iota(jnp.int32, (nb, 1), 0)
    pos_t = qi * tq + lax.broadcasted_iota(jnp.int32, (1, tq), 1)
    q_blk = lax.shift_right_logical(pos_t, int(math.log2(SEL_BLOCK)))
    forced = (jb == 0) | (jb == q_blk) | (jb == q_blk - 1)
    future = jb > q_blk
    imp = jnp.where(forced, FORCE_SCORE, jnp.where(future, -1.0, imp))
    cnt = jnp.zeros((nb, tq), F32)
    for i in range(nb):
        row = imp[i:i + 1, :]
        beats = (row > imp) | ((row == imp) & (jb > i))
        cnt = cnt + jnp.where(beats, 1.0, 0.0)
    sel_t = jnp.where(cnt < float(min(SEL_TOP_N, nb)), 1.0, 0.0)
    sel = jnp.concatenate([sel_t, jnp.zeros((LANES - nb, tq), F32)], axis=0).T

    lane = lax.broadcasted_iota(jnp.int32, (1, LANES), 1)
    sel_pad = jnp.where(lane < nb, (sel - 1.0) * (-NEG_INF), 0.0)
    qw_sc[:, :HEAD_DIM] = q4
    for r in range(R):
        far = jnp.broadcast_to(far_ref[r, 0:1, :], (tq, LANES))
        far_hi = far.astype(BF16).astype(F32)
        pad = jnp.where(lane == nb, far_hi, jnp.where(lane == nb + 1, far - far_hi, 0.0))
        pad = jnp.where(lane == nb + 2, 1.0, pad)
        qw_sc[r * tq:(r + 1) * tq, HEAD_DIM:] = pad.astype(BF16)
        q4_sc[r * tq:(r + 1) * tq, HEAD_DIM:] = jnp.where(lane < nb, sel_pad, pad).astype(BF16)

    ones = jnp.ones((tq, HEAD_DIM), BF16)
    n_chunks = R * tq // ATT_ROWS
    chunks_per_head = tq // ATT_ROWS
    dead = n_tiles

    def keys(kt_sc, tiles):
        return jnp.concatenate([kt_sc[t] for t in tiles], axis=1)

    def values(v_ref, tiles):
        parts = []
        for t in tiles:
            rows = pl.ds(pl.multiple_of(t * tq, tq), tq)
            parts.append(jnp.concatenate([v_ref[0, rows, :], ones], axis=1))
        return jnp.concatenate(parts, axis=0)

    def logits(q_sc, ci, kt, kinds):
        r, hh = divmod(ci, chunks_per_head)
        rs = slice(ci * ATT_ROWS, (ci + 1) * ATT_ROWS)
        qs = slice(hh * ATT_ROWS, (hh + 1) * ATT_ROWS)
        s = jnp.dot(q_sc[rs, :], kt, preferred_element_type=F32)
        return s + jnp.concatenate([bt_ref[kd, r, qs, :] for kd in kinds], axis=1)

    def probs(s, m):
        return jnp.concatenate([jnp.exp2(s[:, k0:k0 + LANES] - m)
                                for k0 in range(0, s.shape[1], LANES)], axis=1).astype(BF16)

    gates = jax.nn.sigmoid(gl_ref[0, 0])

    n_win = WINDOW // tq
    w_tiles = [qi - n_win + t for t in range(n_win + 1)]
    kt_w = keys(ktw_sc, [jnp.where(t >= 0, t, dead) for t in w_tiles])
    vv_w = values(vw_ref, [jnp.maximum(t, 0) for t in w_tiles])
    w_kinds = (3,) + (0,) * (n_win - 2) + (1, 2)
    for ci in range(n_chunks):
        r, hh = divmod(ci, chunks_per_head)
        rs = slice(ci * ATT_ROWS, (ci + 1) * ATT_ROWS)
        qs = slice(hh * ATT_ROWS, (hh + 1) * ATT_ROWS)
        s = logits(qw_sc, ci, kt_w, w_kinds)
        pv = jnp.dot(probs(s, jnp.max(s, axis=-1, keepdims=True)), vv_w,
                     preferred_element_type=F32)
        og_sc[rs, :] = (gates[qs, r:r + 1] * o_cmp[rs]
                        + gates[qs, 2 * R + r:2 * R + r + 1]
                        * (pv[:, :HEAD_DIM] / pv[:, HEAD_DIM:]))

    m_sc[...] = jnp.full(m_sc.shape, NEG_INF, F32)
    acc_sc[...] = jnp.zeros(acc_sc.shape, F32)

    def pair(pi, c):
        ja = 2 * pi
        jb = ja + 1
        kt = keys(kts_sc, (ja, jnp.where(jb <= qi, jb, dead)))
        vv = values(vs_ref, (ja, jnp.minimum(jb, n_tiles - 1)))
        kinds = (jnp.clip(ja - qi + 2, 0, 2), jnp.clip(jb - qi + 2, 0, 2))
        for ci in range(n_chunks):
            rs = slice(ci * ATT_ROWS, (ci + 1) * ATT_ROWS)
            s = logits(q4_sc, ci, kt, kinds)
            m_old = m_sc[rs]
            m_new = jnp.maximum(m_old, jnp.max(s, axis=-1, keepdims=True))
            alpha = jnp.exp2(m_old - m_new)
            m_sc[rs] = m_new
            pv = jnp.dot(probs(s, m_new), vv, preferred_element_type=F32)
            acc_sc[rs] = acc_sc[rs] * jnp.concatenate([alpha, alpha], axis=1) + pv
        return c

    lax.fori_loop(0, (qi + 2) // 2, pair, 0)

    g_sel = jax.nn.sigmoid(gl_ref[0, 0])
    outs = []
    for r in range(R):
        hs = slice(r * tq, (r + 1) * tq)
        acc = acc_sc[hs, :]
        outs.append(og_sc[hs, :] + g_sel[:, R + r:R + r + 1]
                    * (acc[:, :HEAD_DIM] / acc[:, HEAD_DIM:]))
    o_ref[0] = jnp.concatenate(outs, axis=1).astype(o_ref.dtype)


def _nsa_attention(proj, kcmp, vcmp, bias_c, bias_t, bias_far, gates_t, B, S):
    assert WINDOW % ATT_TILE == 0 and S % ATT_TILE == 0
    assert S // CMP_STRIDE == LANES and S // SEL_BLOCK <= LANES
    R, G, tq = HEADS_PER_GROUP, N_GROUPS, ATT_TILE
    n_tiles = S // tq
    nb = S // SEL_BLOCK
    _, _, overlap = _static_maps(S)
    overlap_t = np.ascontiguousarray(overlap.T[:nb])
    assert nb + 3 <= LANES and n_tiles % 2 == 0
    blk_of_key = (np.arange(S) // SEL_BLOCK).reshape(n_tiles, 1, tq)
    aug_sel = np.zeros((n_tiles + 1, LANES, tq), np.float32)
    aug_sel[:n_tiles] = np.arange(LANES).reshape(1, LANES, 1) == blk_of_key
    aug_sel[:n_tiles, nb:nb + 2, :] = 1.0
    aug_sel[n_tiles, nb + 2, :] = NEG_INF
    aug_win = np.zeros((2, LANES, tq), np.float32)
    aug_win[0, nb:nb + 2, :] = 1.0
    aug_win[1, nb + 2, :] = NEG_INF
    q_cols = N_HEADS

    def kv_spec(slot):
        return pl.BlockSpec((1, S, HEAD_DIM),
                            lambda b, g, i, slot=slot: (b, 0, q_cols + slot * G + g))

    cmp_spec = pl.BlockSpec((1, 1, LANES, HEAD_DIM), lambda b, g, i: (b, g, 0, 0))
    in_specs = [
        pl.BlockSpec((1, tq, R * HEAD_DIM), lambda b, g, i: (b, i, g)),
        kv_spec(2), kv_spec(3), kv_spec(4), kv_spec(5),
        cmp_spec, cmp_spec,
        pl.BlockSpec((R, tq, LANES), lambda b, g, i: (g, i, 0)),
        pl.BlockSpec((4, R, tq, tq), lambda b, g, i: (0, g, 0, 0)),
        pl.BlockSpec((R, SUBLANES, LANES), lambda b, g, i: (g, 0, 0)),
        pl.BlockSpec((nb, LANES), lambda b, g, i: (0, 0)),
        pl.BlockSpec((n_tiles + 1, LANES, tq), lambda b, g, i: (0, 0, 0)),
        pl.BlockSpec((2, LANES, tq), lambda b, g, i: (0, 0, 0)),
        pl.BlockSpec((1, 1, tq, 3 * R), lambda b, g, i: (b, g, i, 0)),
    ]
    kdim = HEAD_DIM + LANES
    est = (2 * 4 * S * HEAD_DIM * 2 + 2 * 4 * R * tq * tq * 4 + 2 * R * tq * LANES * 4
           + 2 * 2 * S * LANES * 2 + 2 * S * kdim * 2 + R * tq * (kdim * 2 + LANES * 4 + 2 * HEAD_DIM * 4)
           + 3 * R * tq * LANES * 4 + 16 * ATT_ROWS * tq * 4 + 4 * tq * R * HEAD_DIM * 2)
    return pl.pallas_call(
        _nsa_body,
        out_shape=jax.ShapeDtypeStruct((B, S, N_HEADS * HEAD_DIM), BF16),
        grid=(B, G, n_tiles),
        in_specs=in_specs,
        out_specs=pl.BlockSpec((1, tq, R * HEAD_DIM), lambda b, g, i: (b, i, g)),
        scratch_shapes=[pltpu.VMEM((n_tiles + 1, kdim, tq), BF16),
                        pltpu.VMEM((n_tiles + 1, kdim, tq), BF16),
                        pltpu.VMEM((R * tq, kdim), BF16),
                        pltpu.VMEM((R * tq, kdim), BF16),
                        pltpu.VMEM((R * tq, LANES), F32),
                        pltpu.VMEM((R * tq, 2 * HEAD_DIM), F32),
                        pltpu.VMEM((R * tq, HEAD_DIM), F32)],
        compiler_params=pltpu.CompilerParams(
            dimension_semantics=("arbitrary", "arbitrary", "arbitrary"),
            vmem_limit_bytes=_vmem_limit(est)),
        name="nsa_attention",
    )(proj, proj, proj, proj, proj, kcmp, vcmp, bias_c, bias_t, bias_far,
      jnp.asarray(overlap_t, BF16), jnp.asarray(aug_sel, BF16), jnp.asarray(aug_win, BF16),
      gates_t)


@functools.lru_cache(maxsize=None)
def _hgrn_masks():
    C = HGRN_CHUNK
    t = np.arange(C)[:, None]
    s = np.arange(C)[None, :]
    masks = [(t // HGRN_SUB == s // HGRN_SUB) & (s <= t)]
    half = C // 2
    while half >= HGRN_SUB:
        grp = 2 * half
        masks.append((t // grp == s // grp) & (t % grp >= half) & (s % grp < half))
        half //= 2
    assert np.array_equal(np.sum(masks, axis=0), (s <= t).astype(int))
    return np.stack(masks).astype(np.float32), (s <= t).astype(np.float32)


def _hgrn_body(q_ref, f_ref, i_ref, g_ref, lb_ref, gn_ref, msk_ref, tril_ref, o_ref,
               st_all, b_all, k_all, q_all, *, layer, heads):
    C = HGRN_CHUNK
    n_chunks = q_ref.shape[1] // C
    nt = (((1,), (1,)), ((), ()))

    lbp = lb_ref[...]
    e = jnp.exp(lbp - jnp.max(lbp, axis=0, keepdims=True))
    sm = e / jnp.sum(e, axis=0, keepdims=True)
    cum = sm[0:1]
    first = cum
    for d in range(1, layer + 1):
        cum = cum + sm[d:d + 1]
    lb_all = cum - first
    log_lb_all = jnp.log(lb_all)
    log_1m_all = jnp.log1p(-lb_all)
    gn_all = gn_ref[...]

    cols = lax.broadcasted_iota(jnp.int32, (1, C), 1)
    tril = tril_ref[...]

    st_all[...] = jnp.zeros(st_all.shape, F32)

    def chunk(c, carry):
        for hh in range(heads):
            one_head(c, hh)
        return carry

    def one_head(c, hh):
        sl = pl.ds(pl.multiple_of(c * C, C), C)
        hs = slice(hh * HEAD_DIM, (hh + 1) * HEAD_DIM)
        lb, log_lb, log_1m, gn = lb_all[:, hs], log_lb_all[:, hs], log_1m_all[:, hs], gn_all[:, hs]
        st_ref, b_sc, k_sc, q_sc = st_all.at[hh], b_all.at[hh], k_all.at[hh], q_all.at[hh]
        qr = q_ref[0, sl, hs].astype(F32)
        x = f_ref[0, sl, hs].astype(F32)
        v = i_ref[0, sl, hs].astype(F32)
        gr = g_ref[0, sl, hs].astype(F32)
        q = qr / (1.0 + jnp.exp2(qr * (-LOG2_E)))
        ex = jnp.exp2(jnp.abs(x) * (-LOG2_E))
        u = 1.0 + ex
        r1 = 1.0 / u
        k = (1.0 - lb) * jnp.where(x >= 0.0, ex * r1, r1)
        c2 = log_1m + (jnp.minimum(x, 0.0) - jnp.log(u))
        e2 = jnp.exp2(jnp.abs(log_lb - c2) * (-LOG2_E))
        log_f = jnp.maximum(log_lb, c2) + jnp.log(1.0 + e2)

        lf_hi = log_f.astype(BF16)
        lf_lo = (log_f - lf_hi.astype(F32)).astype(BF16)
        b = (jnp.dot(tril, lf_hi, preferred_element_type=F32)
             + jnp.dot(tril, lf_lo, preferred_element_type=F32)) * LOG2_E
        b_sc[...] = b
        k_sc[...] = k
        q_sc[...] = q

        vb = v.astype(BF16)
        st = st_ref[...]
        o = lax.dot_general((q * jnp.exp2(b)).astype(BF16), st.astype(BF16), nt,
                            preferred_element_type=F32)

        pieces = []
        for blk in range(C // HGRN_SUB):
            r0 = blk * HGRN_SUB
            bt = b_sc[r0:r0 + HGRN_SUB, :]
            qt = q_sc[r0:r0 + HGRN_SUB, :]
            arow = jnp.zeros((HGRN_SUB, C), F32)
            for s in range(HGRN_SUB):
                bs = b_sc[r0 + s:r0 + s + 1, :]
                ks = k_sc[r0 + s:r0 + s + 1, :]
                col = jnp.sum(jnp.exp2(bt - bs) * qt * ks, axis=-1, keepdims=True)
                arow = jnp.where(cols == r0 + s, col, arow)
            pieces.append(arow)
        a = jnp.where(msk_ref[0] > 0.5, jnp.concatenate(pieces, axis=0), 0.0)

        half = C // 2
        lvl = 1
        while half >= HGRN_SUB:
            grp = 2 * half
            anc = jnp.concatenate(
                [jnp.broadcast_to(b_sc[g0 + half - 1:g0 + half, :], (grp, b.shape[1]))
                 for g0 in range(0, C, grp)], axis=0)
            e = jnp.exp2(-jnp.abs(b - anc))
            al = lax.dot_general((q * e).astype(BF16), (k * e).astype(BF16), nt,
                                 preferred_element_type=F32)
            a = jnp.where(msk_ref[lvl] > 0.5, al, a)
            half //= 2
            lvl += 1

        o = o + jnp.dot(a.astype(BF16), vb, preferred_element_type=F32)

        b_last = b_sc[C - 1:C, :]
        kh = (k * jnp.exp2(b_last - b)).astype(BF16)
        st_ref[...] = st * jnp.exp2(b_last) + jnp.dot(v.T.astype(BF16), kh,
                                                     preferred_element_type=F32)

        ms = jnp.mean(o * o, axis=-1, keepdims=True)
        o = o * lax.rsqrt(ms + RMS_EPS) * gn * (gr / (1.0 + jnp.exp2(gr * (-LOG2_E))))
        o_ref[0, sl, hs] = o.astype(o_ref.dtype)

    lax.fori_loop(0, n_chunks, chunk, 0)


def _hgrn(proj, hgrn_lb, onorm, layer, B, S):
    H = N_HEADS
    depth = hgrn_lb.shape[0]

    nh = HGRN_HEADS_PER_STEP
    width = nh * HEAD_DIM
    steps = H // nh

    def spec(part):
        return pl.BlockSpec((1, S, width), lambda b, h, part=part: (b, 0, part * steps + h))

    masks, tril = _hgrn_masks()
    C = HGRN_CHUNK
    est = 2 * 4 * S * width * proj.dtype.itemsize + 2 * S * width * 2 + nh * 64 * C * C * 4
    return pl.pallas_call(
        functools.partial(_hgrn_body, layer=layer, heads=nh),
        out_shape=jax.ShapeDtypeStruct((B, S, H * HEAD_DIM), BF16),
        grid=(B, steps),
        in_specs=[spec(0), spec(1), spec(2), spec(3),
                  pl.BlockSpec((depth, width), lambda b, h: (0, h)),
                  pl.BlockSpec((1, width), lambda b, h: (0, h)),
                  pl.BlockSpec(masks.shape, lambda b, h: (0, 0, 0)),
                  pl.BlockSpec((C, C), lambda b, h: (0, 0))],
        out_specs=pl.BlockSpec((1, S, width), lambda b, h: (b, 0, h)),
        scratch_shapes=[pltpu.VMEM((nh, HEAD_DIM, HEAD_DIM), F32),
                        pltpu.VMEM((nh, C, HEAD_DIM), F32),
                        pltpu.VMEM((nh, C, HEAD_DIM), F32),
                        pltpu.VMEM((nh, C, HEAD_DIM), F32)],
        compiler_params=pltpu.CompilerParams(
            dimension_semantics=("arbitrary", "arbitrary"),
            vmem_limit_bytes=_vmem_limit(est)),
        name="hgrn2_recurrence",
    )(proj, proj, proj, proj, hgrn_lb.astype(F32), onorm.reshape(1, -1).astype(F32),
      jnp.asarray(masks), jnp.asarray(tril, BF16))


def _mlp_body(x_ref, gi_ref, wu_ref, wd_ref, go_ref, o_ref, hn_ref, *, nf, tm):
    f = pl.program_id(1)

    def row_chunks(fn):
        def step(r, c):
            fn(pl.ds(pl.multiple_of(r * NORM_ROWS, NORM_ROWS), NORM_ROWS))
            return c

        lax.fori_loop(0, tm // NORM_ROWS, step, 0)

    @pl.when(f == 0)
    def _():
        g = gi_ref[...]

        def norm_in(rows):
            xs = x_ref[rows, :]
            ms = jnp.mean(xs * xs, axis=-1, keepdims=True)
            hn_ref[rows, :] = (xs * lax.rsqrt(ms + RMS_EPS) * g).astype(BF16)

        row_chunks(norm_in)
        o_ref[...] = jnp.zeros(o_ref.shape, F32)

    hid = jnp.dot(hn_ref[...], wu_ref[...], preferred_element_type=F32)
    hid = jnp.square(jnp.maximum(hid, 0.0)).astype(BF16)
    o_ref[...] += jnp.dot(hid, wd_ref[...], preferred_element_type=F32)

    @pl.when(f == nf - 1)
    def _():
        g = go_ref[...]

        def norm_out(rows):
            y = o_ref[rows, :]
            ms = jnp.mean(y * y, axis=-1, keepdims=True)
            o_ref[rows, :] = x_ref[rows, :] + y * lax.rsqrt(ms + RMS_EPS) * g

        row_chunks(norm_out)


def _mlp(xf, g_in, g_out, w_up, w_down, layer, *, tm=512, tf=1024):
    M, D = xf.shape
    F = w_up.shape[2]
    assert M % tm == 0 and F % tf == 0 and tm % NORM_ROWS == 0
    nf = F // tf
    est = (2 * tm * D * 4 + 2 * tm * D * 4 + tm * D * 2 + 2 * 2 * D * tf * 2
           + tm * tf * 6 + 2 * tm * D * 4)
    return pl.pallas_call(
        functools.partial(_mlp_body, nf=nf, tm=tm),
        out_shape=jax.ShapeDtypeStruct((M, D), F32),
        grid=(M // tm, nf),
        in_specs=[pl.BlockSpec((tm, D), lambda i, f: (i, 0)),
                  pl.BlockSpec((1, D), lambda i, f: (0, 0)),
                  pl.BlockSpec((None, D, tf), lambda i, f: (layer, 0, f)),
                  pl.BlockSpec((None, tf, D), lambda i, f: (layer, f, 0)),
                  pl.BlockSpec((1, D), lambda i, f: (0, 0))],
        out_specs=pl.BlockSpec((tm, D), lambda i, f: (i, 0)),
        scratch_shapes=[pltpu.VMEM((tm, D), BF16)],
        compiler_params=pltpu.CompilerParams(
            dimension_semantics=("arbitrary", "arbitrary"),
            vmem_limit_bytes=_vmem_limit(est)),
        name="mlp",
    )(xf, g_in.reshape(1, D).astype(F32), w_up.astype(BF16), w_down.astype(BF16),
      g_out.reshape(1, D).astype(F32))


def _nsa_layer(xf, B, S, g_in, g_out, rel_table, w_in, cmp_pe, cmp_w1, cmp_w2, w_out):
    D = xf.shape[1]
    G, R, Dh = N_GROUPS, HEADS_PER_GROUP, HEAD_DIM
    n_main = N_HEADS * Dh + 6 * G * Dh
    n_gate = 3 * N_HEADS
    w_all = w_in.astype(BF16)
    w_gate = jnp.pad(w_in[:, n_main:], ((0, 0), (0, LANES - n_gate))).astype(BF16)
    colscale = jnp.concatenate([jnp.full((N_HEADS * Dh,), Dh ** -0.5 * LOG2_E, F32),
                                jnp.ones((6 * G * Dh,), F32)])[None]

    proj, glog = _matmul(xf, w_all, tm=1024, tn=1024, norm_g=g_in, epi="colscale",
                         colscale=colscale, side_w=w_gate, n_out=n_main, out_dtype=BF16,
                         name="nsa_proj")
    gates_t = (glog[:, :n_gate].reshape(B, S, 3, G, R).transpose(0, 3, 1, 2, 4)
               .reshape(B, G, S, 3 * R))

    proj3 = proj.reshape(B, S, n_main)
    half = CMP_STRIDE * Dh
    kcmp, vcmp = _compress(proj3, cmp_w1.reshape(2, 2, half, Dh).astype(BF16),
                           cmp_pe.reshape(2, 2, 1, half).astype(F32),
                           cmp_w2.astype(BF16), B, S)

    bias_c, bias_t, bias_far = _bias_tables(rel_table, S)
    attn = _nsa_attention(proj3, kcmp, vcmp, bias_c, bias_t, bias_far, gates_t, B, S)
    return _matmul(attn.reshape(B * S, D), w_out.astype(BF16), tm=512, tn=D,
                   epi="resnorm", res=xf, res_g=g_out, out_dtype=F32, name="nsa_out")


def _hgrn_layer(xf, B, S, layer, g_in, g_out, w_in, hgrn_lb, onorm, w_out):
    D = xf.shape[1]
    proj = _matmul(xf, w_in.astype(BF16), tm=1024, tn=1024, norm_g=g_in,
                   out_dtype=BF16, name="hgrn_proj")
    mixed = _hgrn(proj.reshape(B, S, 4 * D), hgrn_lb, onorm, layer, B, S)
    return _matmul(mixed.reshape(B * S, D), w_out.astype(BF16), tm=512, tn=D,
                   epi="resnorm", res=xf, res_g=g_out, out_dtype=F32, name="hgrn_out")


def kernel(x, norm_g, rel_table, nsa_w_in, nsa_cmp_pe, nsa_cmp_w1, nsa_cmp_w2, nsa_w_out,
           hgrn_w_in, hgrn_lb, hgrn_onorm, hgrn_w_out, mlp_w_up, mlp_w_down):
    B, S, D = x.shape
    depth = norm_g.shape[0]
    assert D == N_HEADS * HEAD_DIM and S % ATT_TILE == 0 and S % HGRN_CHUNK == 0
    xf = x.reshape(B * S, D).astype(F32)
    for layer in range(depth):
        j = layer // 2
        if layer % 2 == 0:
            xf = _nsa_layer(xf, B, S, norm_g[layer, 0], norm_g[layer, 1], rel_table,
                            nsa_w_in[j], nsa_cmp_pe[j], nsa_cmp_w1[j], nsa_cmp_w2[j],
                            nsa_w_out[j])
        else:
            xf = _hgrn_layer(xf, B, S, layer, norm_g[layer, 0], norm_g[layer, 1],
                             hgrn_w_in[j], hgrn_lb, hgrn_onorm[j], hgrn_w_out[j])
        xf = _mlp(xf, norm_g[layer, 2], norm_g[layer, 3], mlp_w_up, mlp_w_down, layer)
    return xf.reshape(B, S, D).astype(x.dtype)
```

```python
import functools
import math

import numpy as np
import jax
import jax.numpy as jnp
from jax import lax
from jax.experimental import pallas as pl
from jax.experimental.pallas import tpu as pltpu

F32 = jnp.float32
BF16 = jnp.bfloat16

N_HEADS = 16
N_GROUPS = 4
HEADS_PER_GROUP = N_HEADS // N_GROUPS
HEAD_DIM = 128
CMP_BLOCK = 32
CMP_STRIDE = 16
SEL_BLOCK = 64
SEL_TOP_N = 8
WINDOW = 512
FORCE_SCORE = 1.0e4
REL_BUCKETS = 32
REL_MAX_DIST = 128
RMS_EPS = 1e-6
NEG_INF = -1.0e30
LOG2_E = math.log2(math.e)

LANES = 128
SUBLANES = 8
VMEM_BYTES_V7X = 64 * 1024 * 1024
VMEM_LIMIT_CAP = VMEM_BYTES_V7X - 8 * 1024 * 1024

ATT_TILE = 256
ATT_ROWS = 128
HGRN_CHUNK = 128
HGRN_SUB = 8
HGRN_HEADS_PER_STEP = 8
NORM_ROWS = 256


def _vmem_limit(nbytes):
    return int(min(VMEM_LIMIT_CAP, max(32 * 1024 * 1024, nbytes)))


def _mm_body(*refs, norm, epi, side, tm):
    it = iter(refs)
    x_ref = next(it)
    g_ref = next(it) if norm else None
    w_ref = next(it)
    ws_ref = next(it) if side else None
    cs_ref = next(it) if epi == "colscale" else None
    res_ref = next(it) if epi == "resnorm" else None
    go_ref = next(it) if epi == "resnorm" else None
    o_ref = next(it)
    os_ref = next(it) if side else None
    hn_ref = next(it) if norm else None

    j = pl.program_id(1)

    if norm:
        @pl.when(j == 0)
        def _():
            g = g_ref[...]

            def step(r, c):
                rows = pl.ds(pl.multiple_of(r * NORM_ROWS, NORM_ROWS), NORM_ROWS)
                xs = x_ref[rows, :]
                ms = jnp.mean(xs * xs, axis=-1, keepdims=True)
                hn_ref[rows, :] = (xs * lax.rsqrt(ms + RMS_EPS) * g).astype(BF16)
                return c

            lax.fori_loop(0, tm // NORM_ROWS, step, 0)

        lhs = hn_ref[...]
    else:
        lhs = x_ref[...]

    if side:
        @pl.when(j == 0)
        def _():
            os_ref[...] = jnp.dot(lhs, ws_ref[...], preferred_element_type=F32)

    acc = jnp.dot(lhs, w_ref[...], preferred_element_type=F32)
    if epi == "colscale":
        acc = acc * cs_ref[...]
    elif epi == "resnorm":
        ms = jnp.mean(acc * acc, axis=-1, keepdims=True)
        acc = res_ref[...] + acc * lax.rsqrt(ms + RMS_EPS) * go_ref[...]
    o_ref[...] = acc.astype(o_ref.dtype)


def _matmul(x, w, *, tm, tn, norm_g=None, epi="none", colscale=None, res=None, res_g=None,
            side_w=None, n_out=None, out_dtype=BF16, name="mm"):
    M, K = x.shape
    N = w.shape[1] if n_out is None else n_out
    assert N <= w.shape[1]
    norm = norm_g is not None
    side = side_w is not None
    assert M % tm == 0 and N % tn == 0
    assert not (epi == "resnorm" and tn != N)

    in_specs = [pl.BlockSpec((tm, K), lambda i, j: (i, 0))]
    args = [x]
    if norm:
        in_specs.append(pl.BlockSpec((1, K), lambda i, j: (0, 0)))
        args.append(norm_g.reshape(1, K).astype(F32))
    in_specs.append(pl.BlockSpec((K, tn), lambda i, j: (0, j)))
    args.append(w)
    ns = side_w.shape[1] if side else 0
    if side:
        in_specs.append(pl.BlockSpec((K, ns), lambda i, j: (0, 0)))
        args.append(side_w)
    if epi == "colscale":
        in_specs.append(pl.BlockSpec((1, tn), lambda i, j: (0, j)))
        args.append(colscale)
    if epi == "resnorm":
        in_specs.append(pl.BlockSpec((tm, tn), lambda i, j: (i, j)))
        args.append(res)
        in_specs.append(pl.BlockSpec((1, tn), lambda i, j: (0, j)))
        args.append(res_g.reshape(1, N).astype(F32))

    out_shape = jax.ShapeDtypeStruct((M, N), out_dtype)
    out_specs = pl.BlockSpec((tm, tn), lambda i, j: (i, j))
    if side:
        out_shape = (out_shape, jax.ShapeDtypeStruct((M, ns), F32))
        out_specs = (out_specs, pl.BlockSpec((tm, ns), lambda i, j: (i, 0)))

    xb = x.dtype.itemsize
    ob = jnp.dtype(out_dtype).itemsize
    est = (2 * tm * K * xb + 2 * K * tn * 2 + 2 * tm * tn * ob + (tm * K * 2 if norm else 0)
           + (2 * tm * tn * 4 if epi == "resnorm" else 0) + 3 * tm * tn * 4
           + 2 * K * ns * 2 + 3 * tm * ns * 4)

    return pl.pallas_call(
        functools.partial(_mm_body, norm=norm, epi=epi, side=side, tm=tm),
        out_shape=out_shape,
        grid=(M // tm, N // tn),
        in_specs=in_specs,
        out_specs=out_specs,
        scratch_shapes=[pltpu.VMEM((tm, K), BF16)] if norm else [],
        compiler_params=pltpu.CompilerParams(
            dimension_semantics=("arbitrary", "arbitrary"),
            vmem_limit_bytes=_vmem_limit(est)),
        name=name,
    )(*args)


def _rel_bucket_np(dist):
    n = np.maximum(dist, 0)
    max_exact = REL_BUCKETS // 2
    nf = np.maximum(n, 1).astype(np.float32)
    ratio = np.log(nf / np.float32(max_exact)) / np.float32(math.log(REL_MAX_DIST / max_exact))
    large = max_exact + (ratio * np.float32(REL_BUCKETS - max_exact)).astype(np.int32)
    large = np.minimum(large, REL_BUCKETS - 1)
    return np.where(n < max_exact, n, large).astype(np.int32)


@functools.lru_cache(maxsize=None)
def _static_maps(seq):
    n_cmp = LANES
    pos = np.arange(seq, dtype=np.int32)[:, None]
    c_end = np.arange(n_cmp, dtype=np.int32)[None, :] * CMP_STRIDE + CMP_BLOCK - 1
    bucket_c = _rel_bucket_np(pos - c_end)
    t = np.arange(ATT_TILE, dtype=np.int32)[:, None]
    k = np.arange(ATT_TILE, dtype=np.int32)[None, :]
    bucket_t = np.stack([_rel_bucket_np(t - k), _rel_bucket_np(ATT_TILE + t - k)])
    assert _rel_bucket_np(np.array([ATT_TILE + 1]))[0] == REL_BUCKETS - 1
    nc = seq // CMP_STRIDE - CMP_BLOCK // CMP_STRIDE + 1
    nb = seq // SEL_BLOCK
    c_start = np.arange(nc)[:, None] * CMP_STRIDE
    b_start = np.arange(nb)[None, :] * SEL_BLOCK
    ov = ((c_start <= b_start + SEL_BLOCK - 1) & (c_start + CMP_BLOCK - 1 >= b_start))
    overlap = np.zeros((LANES, LANES), np.float32)
    overlap[:nc, :nb] = ov
    return bucket_c, bucket_t, overlap


def _bias_body(tab_ref, bc_ref, bt_ref, oc_ref, ot_ref, of_ref, *, seq):
    h = pl.program_id(0)

    lane = lax.broadcasted_iota(jnp.int32, (SUBLANES, LANES), 1)
    tab_row = jnp.zeros((SUBLANES, LANES), F32)
    for b in range(REL_BUCKETS):
        tab_row = jnp.where(lane == b, tab_ref[b, h], tab_row)

    def lookup(bmap):
        rows = bmap.shape[0]
        tab = jnp.broadcast_to(tab_row[0:1], (rows, LANES))
        return jnp.concatenate(
            [jnp.take_along_axis(tab, bmap[:, c0:c0 + LANES], axis=1)
             for c0 in range(0, bmap.shape[1], LANES)], axis=1)

    def step(r, c):
        rows = pl.ds(pl.multiple_of(r * ATT_TILE, ATT_TILE), ATT_TILE)
        oc_ref[0, rows, :] = lookup(bc_ref[rows, :]) * LOG2_E
        return c

    lax.fori_loop(0, seq // ATT_TILE, step, 0)

    tt = lax.broadcasted_iota(jnp.int32, (ATT_TILE, ATT_TILE), 0)
    kk = lax.broadcasted_iota(jnp.int32, (ATT_TILE, ATT_TILE), 1)
    far = tab_ref[REL_BUCKETS - 1, h]
    of_ref[0] = jnp.full(of_ref.shape[1:], far * LOG2_E, F32)
    ot_ref[0, 0] = jnp.zeros((ATT_TILE, ATT_TILE), F32)
    ot_ref[1, 0] = (lookup(bt_ref[1]) - far) * LOG2_E
    ot_ref[2, 0] = jnp.where(kk <= tt, (lookup(bt_ref[0]) - far) * LOG2_E, NEG_INF)
    ot_ref[3, 0] = jnp.where(kk > tt, 0.0, NEG_INF)


def _bias_tables(rel_table, seq):
    bucket_c, bucket_t, _ = _static_maps(seq)
    return pl.pallas_call(
        functools.partial(_bias_body, seq=seq),
        out_shape=(jax.ShapeDtypeStruct((N_HEADS, seq, LANES), F32),
                   jax.ShapeDtypeStruct((4, N_HEADS, ATT_TILE, ATT_TILE), F32),
                   jax.ShapeDtypeStruct((N_HEADS, SUBLANES, LANES), F32)),
        grid=(N_HEADS,),
        in_specs=[pl.BlockSpec(memory_space=pltpu.SMEM),
                  pl.BlockSpec((seq, LANES), lambda h: (0, 0)),
                  pl.BlockSpec((2, ATT_TILE, ATT_TILE), lambda h: (0, 0, 0))],
        out_specs=(pl.BlockSpec((1, seq, LANES), lambda h: (h, 0, 0)),
                   pl.BlockSpec((4, 1, ATT_TILE, ATT_TILE), lambda h: (0, h, 0, 0)),
                   pl.BlockSpec((1, SUBLANES, LANES), lambda h: (h, 0, 0))),
        compiler_params=pltpu.CompilerParams(dimension_semantics=("arbitrary",)),
        name="rel_bias",
    )(rel_table.astype(F32), jnp.asarray(bucket_c), jnp.asarray(bucket_t))


def _compress_body(xk_ref, xv_ref, w1_ref, pe_ref, w2_ref, ok_ref, ov_ref, x_sc):
    n_grp = xk_ref.shape[1] // CMP_STRIDE

    def one(x_ref, idx, o_ref):
        x_sc[...] = x_ref[0].astype(F32)
        x = jnp.concatenate([x_sc[pl.ds(t, n_grp, stride=CMP_STRIDE), :]
                             for t in range(CMP_STRIDE)], axis=1)
        a0 = jnp.dot((x + pe_ref[idx, 0]).astype(BF16), w1_ref[idx, 0],
                     preferred_element_type=F32)
        a1 = jnp.dot((x + pe_ref[idx, 1]).astype(BF16), w1_ref[idx, 1],
                     preferred_element_type=F32)
        pre = a0 + pltpu.roll(a1, LANES - 1, 0)
        hid = jax.nn.gelu(pre).astype(BF16)
        o_ref[0, 0] = jnp.dot(hid, w2_ref[idx], preferred_element_type=F32).astype(BF16)

    one(xk_ref, 0, ok_ref)
    one(xv_ref, 1, ov_ref)


def _compress(proj, w1, pe, w2, B, S):
    G = N_GROUPS
    assert S // CMP_STRIDE == LANES
    half = CMP_STRIDE * HEAD_DIM
    q_cols = N_HEADS

    def spec_x(slot):
        return pl.BlockSpec((1, S, HEAD_DIM), lambda b, g, slot=slot: (b, 0, q_cols + slot * G + g))

    spec_o = pl.BlockSpec((1, 1, LANES, HEAD_DIM), lambda b, g: (b, g, 0, 0))
    out = jax.ShapeDtypeStruct((B, G, LANES, HEAD_DIM), BF16)
    return pl.pallas_call(
        _compress_body,
        out_shape=(out, out),
        grid=(B, G),
        in_specs=[spec_x(0), spec_x(1),
                  pl.BlockSpec((2, 2, half, HEAD_DIM), lambda b, g: (0, 0, 0, 0)),
                  pl.BlockSpec((2, 2, 1, half), lambda b, g: (0, 0, 0, 0)),
                  pl.BlockSpec((2, HEAD_DIM, HEAD_DIM), lambda b, g: (0, 0, 0))],
        out_specs=(spec_o, spec_o),
        scratch_shapes=[pltpu.VMEM((S, HEAD_DIM), F32)],
        compiler_params=pltpu.CompilerParams(dimension_semantics=("arbitrary", "arbitrary")),
        name="nsa_compress",
    )(proj, proj, w1, pe, w2)


def _nsa_body(q_ref, ks_ref, vs_ref, kw_ref, vw_ref, kc_ref, vc_ref, bc_ref, bt_ref, far_ref,
              ovl_ref, augs_ref, augw_ref, gl_ref, o_ref, kts_sc, ktw_sc, q4_sc, m_sc, acc_sc):
    R = HEADS_PER_GROUP
    tq = ATT_TILE
    qi = pl.program_id(2)
    n_tiles = ks_ref.shape[1] // tq
    nb = ovl_ref.shape[0]
    nt = (((1,), (1,)), ((), ()))

    @pl.when(qi == 0)
    def _():
        def tr(j, c):
            rows = pl.ds(pl.multiple_of(j * tq, tq), tq)
            kts_sc[j, :HEAD_DIM, :] = ks_ref[0, rows, :].astype(F32).T.astype(BF16)
            kts_sc[j, HEAD_DIM:, :] = augs_ref[j]
            ktw_sc[j, :HEAD_DIM, :] = kw_ref[0, rows, :].astype(F32).T.astype(BF16)
            ktw_sc[j, HEAD_DIM:, :] = augw_ref[0]
            return c

        lax.fori_loop(0, n_tiles, tr, 0)
        zeros = jnp.zeros((HEAD_DIM, tq), BF16)
        kts_sc[n_tiles, :HEAD_DIM, :] = zeros
        kts_sc[n_tiles, HEAD_DIM:, :] = augs_ref[n_tiles]
        ktw_sc[n_tiles, :HEAD_DIM, :] = zeros
        ktw_sc[n_tiles, HEAD_DIM:, :] = augw_ref[1]

    q = q_ref[0]
    q4 = jnp.concatenate([q[:, r * HEAD_DIM:(r + 1) * HEAD_DIM] for r in range(R)], axis=0)
    q4_sc[:, :HEAD_DIM] = q4

    pos3 = qi * tq + lax.broadcasted_iota(jnp.int32, (1, tq, 1), 1)

    sc = lax.dot_general(q4, kc_ref[0, 0], nt, preferred_element_type=F32)
    sc = sc.reshape(R, tq, LANES) + bc_ref[...]
    cidx = lax.broadcasted_iota(jnp.int32, (1, 1, LANES), 2)
    valid = (cidx * CMP_STRIDE + (CMP_BLOCK - 1) <= pos3) & (cidx < LANES - 1)
    sc = jnp.where(valid, sc, NEG_INF)
    mc = jnp.max(sc, axis=-1, keepdims=True)
    ec = jnp.exp2(sc - mc)
    pc = ec / jnp.sum(ec, axis=-1, keepdims=True)
    pc = jnp.where(pos3 >= CMP_BLOCK - 1, pc, 0.0)
    o_cmp = jnp.dot(pc.reshape(R * tq, LANES).astype(BF16), vc_ref[0, 0],
                    preferred_element_type=F32)

    psum = pc[0]
    for r in range(1, R):
        psum = psum + pc[r]
    p_hi = psum.astype(BF16)
    p_lo = (psum - p_hi.astype(F32)).astype(BF16)
    ovt = ovl_ref[...]
    imp = (lax.dot_general(ovt, p_hi, nt, preferred_element_type=F32)
           + lax.dot_general(ovt, p_lo, nt, preferred_element_type=F32))
    jb = lax.broadcasted_iota(jnp.int32, (nb, 1), 0)
    pos_t = qi * tq + lax.broadcasted_iota(jnp.int32, (1, tq), 1)
    q_blk = lax.shift_right_logical(pos_t, int(math.log2(SEL_BLOCK)))
    forced = (jb == 0) | (jb == q_blk) | (jb == q_blk - 1)
    future = jb > q_blk
    imp = jnp.where(forced, FORCE_SCORE, jnp.where(future, -1.0, imp))
    cnt = jnp.zeros((nb, tq), F32)
    for i in range(nb):
        row = imp[i:i + 1, :]
        beats = (row > imp) | ((row == imp) & (jb > i))
        cnt = cnt + jnp.where(beats, 1.0, 0.0)
    sel_t = jnp.where(cnt < float(min(SEL_TOP_N, nb)), 1.0, 0.0)
    sel = jnp.concatenate([sel_t, jnp.zeros((LANES - nb, tq), F32)], axis=0).T

    lane = lax.broadcasted_iota(jnp.int32, (1, LANES), 1)
    sel_pad = jnp.where(lane < nb, (sel - 1.0) * (-NEG_INF), 0.0)
    for r in range(R):
        far = jnp.broadcast_to(far_ref[r, 0:1, :], (tq, LANES))
        far_hi = far.astype(BF16).astype(F32)
        pad = jnp.where(lane == nb, far_hi, jnp.where(lane == nb + 1, far - far_hi, sel_pad))
        pad = jnp.where(lane == nb + 2, 1.0, pad)
        q4_sc[r * tq:(r + 1) * tq, HEAD_DIM:] = pad.astype(BF16)

    ones = jnp.ones((tq, HEAD_DIM), BF16)
    n_chunks = R * tq // ATT_ROWS
    chunks_per_head = tq // ATT_ROWS
    dead = n_tiles

    def keys(kt_sc, tiles):
        return jnp.concatenate([kt_sc[t] for t in tiles], axis=1)

    def values(v_ref, tiles):
        parts = []
        for t in tiles:
            rows = pl.ds(pl.multiple_of(t * tq, tq), tq)
            parts.append(jnp.concatenate([v_ref[0, rows, :], ones], axis=1))
        return jnp.concatenate(parts, axis=0)

    def logits(ci, kt, kinds):
        r, hh = divmod(ci, chunks_per_head)
        rs = slice(ci * ATT_ROWS, (ci + 1) * ATT_ROWS)
        qs = slice(hh * ATT_ROWS, (hh + 1) * ATT_ROWS)
        s = jnp.dot(q4_sc[rs, :], kt, preferred_element_type=F32)
        return s + jnp.concatenate([bt_ref[kd, r, qs, :] for kd in kinds], axis=1)

    def probs(s, m):
        return jnp.concatenate([jnp.exp2(s[:, k0:k0 + LANES] - m)
                                for k0 in range(0, s.shape[1], LANES)], axis=1).astype(BF16)

    m_sc[...] = jnp.full(m_sc.shape, NEG_INF, F32)
    acc_sc[...] = jnp.zeros(acc_sc.shape, F32)

    def pair(pi, c):
        ja = 2 * pi
        jb = ja + 1
        kt = keys(kts_sc, (ja, jnp.where(jb <= qi, jb, dead)))
        vv = values(vs_ref, (ja, jnp.minimum(jb, n_tiles - 1)))
        kinds = (jnp.clip(ja - qi + 2, 0, 2), jnp.clip(jb - qi + 2, 0, 2))
        for ci in range(n_chunks):
            rs = slice(ci * ATT_ROWS, (ci + 1) * ATT_ROWS)
            s = logits(ci, kt, kinds)
            m_old = m_sc[rs]
            m_new = jnp.maximum(m_old, jnp.max(s, axis=-1, keepdims=True))
            alpha = jnp.exp2(m_old - m_new)
            m_sc[rs] = m_new
            pv = jnp.dot(probs(s, m_new), vv, preferred_element_type=F32)
            acc_sc[rs] = acc_sc[rs] * jnp.concatenate([alpha, alpha], axis=1) + pv
        return c

    lax.fori_loop(0, (qi + 2) // 2, pair, 0)
    acc = acc_sc[...]
    o_sel = acc[:, :HEAD_DIM] / acc[:, HEAD_DIM:]

    n_win = WINDOW // tq
    w_tiles = [qi - n_win + t for t in range(n_win + 1)]
    kt = keys(ktw_sc, [jnp.where(t >= 0, t, dead) for t in w_tiles])
    vv = values(vw_ref, [jnp.maximum(t, 0) for t in w_tiles])
    w_kinds = (3,) + (0,) * (n_win - 2) + (1, 2)
    o_parts = []
    for ci in range(n_chunks):
        s = logits(ci, kt, w_kinds)
        pv = jnp.dot(probs(s, jnp.max(s, axis=-1, keepdims=True)), vv,
                     preferred_element_type=F32)
        o_parts.append(pv[:, :HEAD_DIM] / pv[:, HEAD_DIM:])
    o_win = jnp.concatenate(o_parts, axis=0)

    gates = jax.nn.sigmoid(gl_ref[0, 0])
    outs = []
    for r in range(R):
        hs = slice(r * tq, (r + 1) * tq)
        o_r = (gates[:, r:r + 1] * o_cmp[hs]
               + gates[:, R + r:R + r + 1] * o_sel[hs]
               + gates[:, 2 * R + r:2 * R + r + 1] * o_win[hs])
        outs.append(o_r)
    o_ref[0] = jnp.concatenate(outs, axis=1).astype(o_ref.dtype)


def _nsa_attention(proj, kcmp, vcmp, bias_c, bias_t, bias_far, gates_t, B, S):
    assert WINDOW % ATT_TILE == 0 and S % ATT_TILE == 0
    assert S // CMP_STRIDE == LANES and S // SEL_BLOCK <= LANES
    R, G, tq = HEADS_PER_GROUP, N_GROUPS, ATT_TILE
    n_tiles = S // tq
    nb = S // SEL_BLOCK
    _, _, overlap = _static_maps(S)
    overlap_t = np.ascontiguousarray(overlap.T[:nb])
    assert nb + 3 <= LANES and n_tiles % 2 == 0
    blk_of_key = (np.arange(S) // SEL_BLOCK).reshape(n_tiles, 1, tq)
    aug_sel = np.zeros((n_tiles + 1, LANES, tq), np.float32)
    aug_sel[:n_tiles] = np.arange(LANES).reshape(1, LANES, 1) == blk_of_key
    aug_sel[:n_tiles, nb:nb + 2, :] = 1.0
    aug_sel[n_tiles, nb + 2, :] = NEG_INF
    aug_win = np.zeros((2, LANES, tq), np.float32)
    aug_win[0, nb:nb + 2, :] = 1.0
    aug_win[1, nb + 2, :] = NEG_INF
    q_cols = N_HEADS

    def kv_spec(slot):
        return pl.BlockSpec((1, S, HEAD_DIM),
                            lambda b, g, i, slot=slot: (b, 0, q_cols + slot * G + g))

    cmp_spec = pl.BlockSpec((1, 1, LANES, HEAD_DIM), lambda b, g, i: (b, g, 0, 0))
    in_specs = [
        pl.BlockSpec((1, tq, R * HEAD_DIM), lambda b, g, i: (b, i, g)),
        kv_spec(2), kv_spec(3), kv_spec(4), kv_spec(5),
        cmp_spec, cmp_spec,
        pl.BlockSpec((R, tq, LANES), lambda b, g, i: (g, i, 0)),
        pl.BlockSpec((4, R, tq, tq), lambda b, g, i: (0, g, 0, 0)),
        pl.BlockSpec((R, SUBLANES, LANES), lambda b, g, i: (g, 0, 0)),
        pl.BlockSpec((nb, LANES), lambda b, g, i: (0, 0)),
        pl.BlockSpec((n_tiles + 1, LANES, tq), lambda b, g, i: (0, 0, 0)),
        pl.BlockSpec((2, LANES, tq), lambda b, g, i: (0, 0, 0)),
        pl.BlockSpec((1, 1, tq, 3 * R), lambda b, g, i: (b, g, i, 0)),
    ]
    kdim = HEAD_DIM + LANES
    est = (2 * 4 * S * HEAD_DIM * 2 + 2 * 4 * R * tq * tq * 4 + 2 * R * tq * LANES * 4
           + 2 * 2 * S * LANES * 2 + 2 * S * kdim * 2 + R * tq * (kdim * 2 + LANES * 4 + 2 * HEAD_DIM * 4)
           + 3 * R * tq * LANES * 4 + 16 * ATT_ROWS * tq * 4 + 4 * tq * R * HEAD_DIM * 2)
    return pl.pallas_call(
        _nsa_body,
        out_shape=jax.ShapeDtypeStruct((B, S, N_HEADS * HEAD_DIM), BF16),
        grid=(B, G, n_tiles),
        in_specs=in_specs,
        out_specs=pl.BlockSpec((1, tq, R * HEAD_DIM), lambda b, g, i: (b, i, g)),
        scratch_shapes=[pltpu.VMEM((n_tiles + 1, kdim, tq), BF16),
                        pltpu.VMEM((n_tiles + 1, kdim, tq), BF16),
                        pltpu.VMEM((R * tq, kdim), BF16),
                        pltpu.VMEM((R * tq, LANES), F32),
                        pltpu.VMEM((R * tq, 2 * HEAD_DIM), F32)],
        compiler_params=pltpu.CompilerParams(
            dimension_semantics=("arbitrary", "arbitrary", "arbitrary"),
            vmem_limit_bytes=_vmem_limit(est)),
        name="nsa_attention",
    )(proj, proj, proj, proj, proj, kcmp, vcmp, bias_c, bias_t, bias_far,
      jnp.asarray(overlap_t, BF16), jnp.asarray(aug_sel, BF16), jnp.asarray(aug_win, BF16),
      gates_t)


@functools.lru_cache(maxsize=None)
def _hgrn_masks():
    C = HGRN_CHUNK
    t = np.arange(C)[:, None]
    s = np.arange(C)[None, :]
    masks = [(t // HGRN_SUB == s // HGRN_SUB) & (s <= t)]
    half = C // 2
    while half >= HGRN_SUB:
        grp = 2 * half
        masks.append((t // grp == s // grp) & (t % grp >= half) & (s % grp < half))
        half //= 2
    assert np.array_equal(np.sum(masks, axis=0), (s <= t).astype(int))
    return np.stack(masks).astype(np.float32), (s <= t).astype(np.float32)


def _hgrn_body(q_ref, f_ref, i_ref, g_ref, lb_ref, gn_ref, msk_ref, tril_ref, o_ref,
               st_all, b_all, k_all, q_all, *, layer, heads):
    C = HGRN_CHUNK
    n_chunks = q_ref.shape[1] // C
    nt = (((1,), (1,)), ((), ()))

    lbp = lb_ref[...]
    e = jnp.exp(lbp - jnp.max(lbp, axis=0, keepdims=True))
    sm = e / jnp.sum(e, axis=0, keepdims=True)
    cum = sm[0:1]
    first = cum
    for d in range(1, layer + 1):
        cum = cum + sm[d:d + 1]
    lb_all = cum - first
    log_lb_all = jnp.log(lb_all)
    log_1m_all = jnp.log1p(-lb_all)
    gn_all = gn_ref[...]

    cols = lax.broadcasted_iota(jnp.int32, (1, C), 1)
    tril = tril_ref[...]

    st_all[...] = jnp.zeros(st_all.shape, F32)

    def chunk(c, carry):
        for hh in range(heads):
            one_head(c, hh)
        return carry

    def one_head(c, hh):
        sl = pl.ds(pl.multiple_of(c * C, C), C)
        hs = slice(hh * HEAD_DIM, (hh + 1) * HEAD_DIM)
        lb, log_lb, log_1m, gn = lb_all[:, hs], log_lb_all[:, hs], log_1m_all[:, hs], gn_all[:, hs]
        st_ref, b_sc, k_sc, q_sc = st_all.at[hh], b_all.at[hh], k_all.at[hh], q_all.at[hh]
        qr = q_ref[0, sl, hs].astype(F32)
        x = f_ref[0, sl, hs].astype(F32)
        v = i_ref[0, sl, hs].astype(F32)
        gr = g_ref[0, sl, hs].astype(F32)
        q = qr / (1.0 + jnp.exp2(qr * (-LOG2_E)))
        ex = jnp.exp2(jnp.abs(x) * (-LOG2_E))
        u = 1.0 + ex
        r1 = 1.0 / u
        k = (1.0 - lb) * jnp.where(x >= 0.0, ex * r1, r1)
        c2 = log_1m + (jnp.minimum(x, 0.0) - jnp.log(u))
        e2 = jnp.exp2(jnp.abs(log_lb - c2) * (-LOG2_E))
        log_f = jnp.maximum(log_lb, c2) + jnp.log(1.0 + e2)

        lf_hi = log_f.astype(BF16)
        lf_lo = (log_f - lf_hi.astype(F32)).astype(BF16)
        b = (jnp.dot(tril, lf_hi, preferred_element_type=F32)
             + jnp.dot(tril, lf_lo, preferred_element_type=F32)) * LOG2_E
        b_sc[...] = b
        k_sc[...] = k
        q_sc[...] = q

        vb = v.astype(BF16)
        st = st_ref[...]
        o = lax.dot_general((q * jnp.exp2(b)).astype(BF16), st.astype(BF16), nt,
                            preferred_element_type=F32)

        pieces = []
        for blk in range(C // HGRN_SUB):
            r0 = blk * HGRN_SUB
            bt = b_sc[r0:r0 + HGRN_SUB, :]
            qt = q_sc[r0:r0 + HGRN_SUB, :]
            arow = jnp.zeros((HGRN_SUB, C), F32)
            for s in range(HGRN_SUB):
                bs = b_sc[r0 + s:r0 + s + 1, :]
                ks = k_sc[r0 + s:r0 + s + 1, :]
                col = jnp.sum(jnp.exp2(bt - bs) * qt * ks, axis=-1, keepdims=True)
                arow = jnp.where(cols == r0 + s, col, arow)
            pieces.append(arow)
        a = jnp.where(msk_ref[0] > 0.5, jnp.concatenate(pieces, axis=0), 0.0)

        half = C // 2
        lvl = 1
        while half >= HGRN_SUB:
            grp = 2 * half
            anc = jnp.concatenate(
                [jnp.broadcast_to(b_sc[g0 + half - 1:g0 + half, :], (grp, b.shape[1]))
                 for g0 in range(0, C, grp)], axis=0)
            e = jnp.exp2(-jnp.abs(b - anc))
            al = lax.dot_general((q * e).astype(BF16), (k * e).astype(BF16), nt,
                                 preferred_element_type=F32)
            a = jnp.where(msk_ref[lvl] > 0.5, al, a)
            half //= 2
            lvl += 1

        o = o + jnp.dot(a.astype(BF16), vb, preferred_element_type=F32)

        b_last = b_sc[C - 1:C, :]
        kh = (k * jnp.exp2(b_last - b)).astype(BF16)
        st_ref[...] = st * jnp.exp2(b_last) + jnp.dot(v.T.astype(BF16), kh,
                                                     preferred_element_type=F32)

        ms = jnp.mean(o * o, axis=-1, keepdims=True)
        o = o * lax.rsqrt(ms + RMS_EPS) * gn * (gr / (1.0 + jnp.exp2(gr * (-LOG2_E))))
        o_ref[0, sl, hs] = o.astype(o_ref.dtype)

    lax.fori_loop(0, n_chunks, chunk, 0)


def _hgrn(proj, hgrn_lb, onorm, layer, B, S):
    H = N_HEADS
    depth = hgrn_lb.shape[0]

    nh = HGRN_HEADS_PER_STEP
    width = nh * HEAD_DIM
    steps = H // nh

    def spec(part):
        return pl.BlockSpec((1, S, width), lambda b, h, part=part: (b, 0, part * steps + h))

    masks, tril = _hgrn_masks()
    C = HGRN_CHUNK
    est = 2 * 4 * S * width * proj.dtype.itemsize + 2 * S * width * 2 + nh * 64 * C * C * 4
    return pl.pallas_call(
        functools.partial(_hgrn_body, layer=layer, heads=nh),
        out_shape=jax.ShapeDtypeStruct((B, S, H * HEAD_DIM), BF16),
        grid=(B, steps),
        in_specs=[spec(0), spec(1), spec(2), spec(3),
                  pl.BlockSpec((depth, width), lambda b, h: (0, h)),
                  pl.BlockSpec((1, width), lambda b, h: (0, h)),
                  pl.BlockSpec(masks.shape, lambda b, h: (0, 0, 0)),
                  pl.BlockSpec((C, C), lambda b, h: (0, 0))],
        out_specs=pl.BlockSpec((1, S, width), lambda b, h: (b, 0, h)),
        scratch_shapes=[pltpu.VMEM((nh, HEAD_DIM, HEAD_DIM), F32),
                        pltpu.VMEM((nh, C, HEAD_DIM), F32),
                        pltpu.VMEM((nh, C, HEAD_DIM), F32),
                        pltpu.VMEM((nh, C, HEAD_DIM), F32)],
        compiler_params=pltpu.CompilerParams(
            dimension_semantics=("arbitrary", "arbitrary"),
            vmem_limit_bytes=_vmem_limit(est)),
        name="hgrn2_recurrence",
    )(proj, proj, proj, proj, hgrn_lb.astype(F32), onorm.reshape(1, -1).astype(F32),
      jnp.asarray(masks), jnp.asarray(tril, BF16))


def _mlp_body(x_ref, gi_ref, wu_ref, wd_ref, go_ref, o_ref, hn_ref, *, nf, tm):
    f = pl.program_id(1)

    def row_chunks(fn):
        def step(r, c):
            fn(pl.ds(pl.multiple_of(r * NORM_ROWS, NORM_ROWS), NORM_ROWS))
            return c

        lax.fori_loop(0, tm // NORM_ROWS, step, 0)

    @pl.when(f == 0)
    def _():
        g = gi_ref[...]

        def norm_in(rows):
            xs = x_ref[rows, :]
            ms = jnp.mean(xs * xs, axis=-1, keepdims=True)
            hn_ref[rows, :] = (xs * lax.rsqrt(ms + RMS_EPS) * g).astype(BF16)

        row_chunks(norm_in)
        o_ref[...] = jnp.zeros(o_ref.shape, F32)

    hid = jnp.dot(hn_ref[...], wu_ref[...], preferred_element_type=F32)
    hid = jnp.square(jnp.maximum(hid, 0.0)).astype(BF16)
    o_ref[...] += jnp.dot(hid, wd_ref[...], preferred_element_type=F32)

    @pl.when(f == nf - 1)
    def _():
        g = go_ref[...]

        def norm_out(rows):
            y = o_ref[rows, :]
            ms = jnp.mean(y * y, axis=-1, keepdims=True)
            o_ref[rows, :] = x_ref[rows, :] + y * lax.rsqrt(ms + RMS_EPS) * g

        row_chunks(norm_out)


def _mlp(xf, g_in, g_out, w_up, w_down, layer, *, tm=512, tf=1024):
    M, D = xf.shape
    F = w_up.shape[2]
    assert M % tm == 0 and F % tf == 0 and tm % NORM_ROWS == 0
    nf = F // tf
    est = (2 * tm * D * 4 + 2 * tm * D * 4 + tm * D * 2 + 2 * 2 * D * tf * 2
           + tm * tf * 6 + 2 * tm * D * 4)
    return pl.pallas_call(
        functools.partial(_mlp_body, nf=nf, tm=tm),
        out_shape=jax.ShapeDtypeStruct((M, D), F32),
        grid=(M // tm, nf),
        in_specs=[pl.BlockSpec((tm, D), lambda i, f: (i, 0)),
                  pl.BlockSpec((1, D), lambda i, f: (0, 0)),
                  pl.BlockSpec((None, D, tf), lambda i, f: (layer, 0, f)),
                  pl.BlockSpec((None, tf, D), lambda i, f: (layer, f, 0)),
                  pl.BlockSpec((1, D), lambda i, f: (0, 0))],
        out_specs=pl.BlockSpec((tm, D), lambda i, f: (i, 0)),
        scratch_shapes=[pltpu.VMEM((tm, D), BF16)],
        compiler_params=pltpu.CompilerParams(
            dimension_semantics=("arbitrary", "arbitrary"),
            vmem_limit_bytes=_vmem_limit(est)),
        name="mlp",
    )(xf, g_in.reshape(1, D).astype(F32), w_up.astype(BF16), w_down.astype(BF16),
      g_out.reshape(1, D).astype(F32))


def _nsa_layer(xf, B, S, g_in, g_out, rel_table, w_in, cmp_pe, cmp_w1, cmp_w2, w_out):
    D = xf.shape[1]
    G, R, Dh = N_GROUPS, HEADS_PER_GROUP, HEAD_DIM
    n_main = N_HEADS * Dh + 6 * G * Dh
    n_gate = 3 * N_HEADS
    w_all = w_in.astype(BF16)
    w_gate = jnp.pad(w_in[:, n_main:], ((0, 0), (0, LANES - n_gate))).astype(BF16)
    colscale = jnp.concatenate([jnp.full((N_HEADS * Dh,), Dh ** -0.5 * LOG2_E, F32),
                                jnp.ones((6 * G * Dh,), F32)])[None]

    proj, glog = _matmul(xf, w_all, tm=1024, tn=1024, norm_g=g_in, epi="colscale",
                         colscale=colscale, side_w=w_gate, n_out=n_main, out_dtype=BF16,
                         name="nsa_proj")
    gates_t = (glog[:, :n_gate].reshape(B, S, 3, G, R).transpose(0, 3, 1, 2, 4)
               .reshape(B, G, S, 3 * R))

    proj3 = proj.reshape(B, S, n_main)
    half = CMP_STRIDE * Dh
    kcmp, vcmp = _compress(proj3, cmp_w1.reshape(2, 2, half, Dh).astype(BF16),
                           cmp_pe.reshape(2, 2, 1, half).astype(F32),
                           cmp_w2.astype(BF16), B, S)

    bias_c, bias_t, bias_far = _bias_tables(rel_table, S)
    attn = _nsa_attention(proj3, kcmp, vcmp, bias_c, bias_t, bias_far, gates_t, B, S)
    return _matmul(attn.reshape(B * S, D), w_out.astype(BF16), tm=512, tn=D,
                   epi="resnorm", res=xf, res_g=g_out, out_dtype=F32, name="nsa_out")


def _hgrn_layer(xf, B, S, layer, g_in, g_out, w_in, hgrn_lb, onorm, w_out):
    D = xf.shape[1]
    proj = _matmul(xf, w_in.astype(BF16), tm=1024, tn=1024, norm_g=g_in,
                   out_dtype=BF16, name="hgrn_proj")
    mixed = _hgrn(proj.reshape(B, S, 4 * D), hgrn_lb, onorm, layer, B, S)
    return _matmul(mixed.reshape(B * S, D), w_out.astype(BF16), tm=512, tn=D,
                   epi="resnorm", res=xf, res_g=g_out, out_dtype=F32, name="hgrn_out")


def kernel(x, norm_g, rel_table, nsa_w_in, nsa_cmp_pe, nsa_cmp_w1, nsa_cmp_w2, nsa_w_out,
           hgrn_w_in, hgrn_lb, hgrn_onorm, hgrn_w_out, mlp_w_up, mlp_w_down):
    B, S, D = x.shape
    depth = norm_g.shape[0]
    assert D == N_HEADS * HEAD_DIM and S % ATT_TILE == 0 and S % HGRN_CHUNK == 0
    xf = x.reshape(B * S, D).astype(F32)
    for layer in range(depth):
        j = layer // 2
        if layer % 2 == 0:
            xf = _nsa_layer(xf, B, S, norm_g[layer, 0], norm_g[layer, 1], rel_table,
                            nsa_w_in[j], nsa_cmp_pe[j], nsa_cmp_w1[j], nsa_cmp_w2[j],
                            nsa_w_out[j])
        else:
            xf = _hgrn_layer(xf, B, S, layer, norm_g[layer, 0], norm_g[layer, 1],
                             hgrn_w_in[j], hgrn_lb, hgrn_onorm[j], hgrn_w_out[j])
        xf = _mlp(xf, norm_g[layer, 2], norm_g[layer, 3], mlp_w_up, mlp_w_down, layer)
    return xf.reshape(B, S, D).astype(x.dtype)
```

```python
import functools
import math

import numpy as np
import jax
import jax.numpy as jnp
from jax import lax
from jax.experimental import pallas as pl
from jax.experimental.pallas import tpu as pltpu

F32 = jnp.float32
BF16 = jnp.bfloat16

N_HEADS = 16
N_GROUPS = 4
HEADS_PER_GROUP = N_HEADS // N_GROUPS
HEAD_DIM = 128
CMP_BLOCK = 32
CMP_STRIDE = 16
SEL_BLOCK = 64
SEL_TOP_N = 8
WINDOW = 512
FORCE_SCORE = 1.0e4
REL_BUCKETS = 32
REL_MAX_DIST = 128
RMS_EPS = 1e-6
NEG_INF = -1.0e30
LOG2_E = math.log2(math.e)

LANES = 128
SUBLANES = 8
VMEM_BYTES_V7X = 64 * 1024 * 1024
VMEM_LIMIT_CAP = VMEM_BYTES_V7X - 8 * 1024 * 1024

ATT_TILE = 256
ATT_ROWS = 128
HGRN_CHUNK = 128
HGRN_SUB = 8
HGRN_HEADS_PER_STEP = 8
NORM_ROWS = 256


def _vmem_limit(nbytes):
    return int(min(VMEM_LIMIT_CAP, max(32 * 1024 * 1024, nbytes)))


def _mm_body(*refs, norm, epi, side, tm):
    it = iter(refs)
    x_ref = next(it)
    g_ref = next(it) if norm else None
    w_ref = next(it)
    ws_ref = next(it) if side else None
    cs_ref = next(it) if epi == "colscale" else None
    res_ref = next(it) if epi == "resnorm" else None
    go_ref = next(it) if epi == "resnorm" else None
    o_ref = next(it)
    os_ref = next(it) if side else None
    hn_ref = next(it) if norm else None

    j = pl.program_id(1)

    if norm:
        @pl.when(j == 0)
        def _():
            g = g_ref[...]

            def step(r, c):
                rows = pl.ds(pl.multiple_of(r * NORM_ROWS, NORM_ROWS), NORM_ROWS)
                xs = x_ref[rows, :]
                ms = jnp.mean(xs * xs, axis=-1, keepdims=True)
                hn_ref[rows, :] = (xs * lax.rsqrt(ms + RMS_EPS) * g).astype(BF16)
                return c

            lax.fori_loop(0, tm // NORM_ROWS, step, 0)

        lhs = hn_ref[...]
    else:
        lhs = x_ref[...]

    if side:
        @pl.when(j == 0)
        def _():
            os_ref[...] = jnp.dot(lhs, ws_ref[...], preferred_element_type=F32)

    acc = jnp.dot(lhs, w_ref[...], preferred_element_type=F32)
    if epi == "colscale":
        acc = acc * cs_ref[...]
    elif epi == "resnorm":
        ms = jnp.mean(acc * acc, axis=-1, keepdims=True)
        acc = res_ref[...] + acc * lax.rsqrt(ms + RMS_EPS) * go_ref[...]
    o_ref[...] = acc.astype(o_ref.dtype)


def _matmul(x, w, *, tm, tn, norm_g=None, epi="none", colscale=None, res=None, res_g=None,
            side_w=None, n_out=None, out_dtype=BF16, name="mm"):
    M, K = x.shape
    N = w.shape[1] if n_out is None else n_out
    assert N <= w.shape[1]
    norm = norm_g is not None
    side = side_w is not None
    assert M % tm == 0 and N % tn == 0
    assert not (epi == "resnorm" and tn != N)

    in_specs = [pl.BlockSpec((tm, K), lambda i, j: (i, 0))]
    args = [x]
    if norm:
        in_specs.append(pl.BlockSpec((1, K), lambda i, j: (0, 0)))
        args.append(norm_g.reshape(1, K).astype(F32))
    in_specs.append(pl.BlockSpec((K, tn), lambda i, j: (0, j)))
    args.append(w)
    ns = side_w.shape[1] if side else 0
    if side:
        in_specs.append(pl.BlockSpec((K, ns), lambda i, j: (0, 0)))
        args.append(side_w)
    if epi == "colscale":
        in_specs.append(pl.BlockSpec((1, tn), lambda i, j: (0, j)))
        args.append(colscale)
    if epi == "resnorm":
        in_specs.append(pl.BlockSpec((tm, tn), lambda i, j: (i, j)))
        args.append(res)
        in_specs.append(pl.BlockSpec((1, tn), lambda i, j: (0, j)))
        args.append(res_g.reshape(1, N).astype(F32))

    out_shape = jax.ShapeDtypeStruct((M, N), out_dtype)
    out_specs = pl.BlockSpec((tm, tn), lambda i, j: (i, j))
    if side:
        out_shape = (out_shape, jax.ShapeDtypeStruct((M, ns), F32))
        out_specs = (out_specs, pl.BlockSpec((tm, ns), lambda i, j: (i, 0)))

    xb = x.dtype.itemsize
    ob = jnp.dtype(out_dtype).itemsize
    est = (2 * tm * K * xb + 2 * K * tn * 2 + 2 * tm * tn * ob + (tm * K * 2 if norm else 0)
           + (2 * tm * tn * 4 if epi == "resnorm" else 0) + 3 * tm * tn * 4
           + 2 * K * ns * 2 + 3 * tm * ns * 4)

    return pl.pallas_call(
        functools.partial(_mm_body, norm=norm, epi=epi, side=side, tm=tm),
        out_shape=out_shape,
        grid=(M // tm, N // tn),
        in_specs=in_specs,
        out_specs=out_specs,
        scratch_shapes=[pltpu.VMEM((tm, K), BF16)] if norm else [],
        compiler_params=pltpu.CompilerParams(
            dimension_semantics=("arbitrary", "arbitrary"),
            vmem_limit_bytes=_vmem_limit(est)),
        name=name,
    )(*args)


def _rel_bucket_np(dist):
    n = np.maximum(dist, 0)
    max_exact = REL_BUCKETS // 2
    nf = np.maximum(n, 1).astype(np.float32)
    ratio = np.log(nf / np.float32(max_exact)) / np.float32(math.log(REL_MAX_DIST / max_exact))
    large = max_exact + (ratio * np.float32(REL_BUCKETS - max_exact)).astype(np.int32)
    large = np.minimum(large, REL_BUCKETS - 1)
    return np.where(n < max_exact, n, large).astype(np.int32)


@functools.lru_cache(maxsize=None)
def _static_maps(seq):
    n_cmp = LANES
    pos = np.arange(seq, dtype=np.int32)[:, None]
    c_end = np.arange(n_cmp, dtype=np.int32)[None, :] * CMP_STRIDE + CMP_BLOCK - 1
    bucket_c = _rel_bucket_np(pos - c_end)
    t = np.arange(ATT_TILE, dtype=np.int32)[:, None]
    k = np.arange(ATT_TILE, dtype=np.int32)[None, :]
    bucket_t = np.stack([_rel_bucket_np(t - k), _rel_bucket_np(ATT_TILE + t - k)])
    assert _rel_bucket_np(np.array([ATT_TILE + 1]))[0] == REL_BUCKETS - 1
    nc = seq // CMP_STRIDE - CMP_BLOCK // CMP_STRIDE + 1
    nb = seq // SEL_BLOCK
    c_start = np.arange(nc)[:, None] * CMP_STRIDE
    b_start = np.arange(nb)[None, :] * SEL_BLOCK
    ov = ((c_start <= b_start + SEL_BLOCK - 1) & (c_start + CMP_BLOCK - 1 >= b_start))
    overlap = np.zeros((LANES, LANES), np.float32)
    overlap[:nc, :nb] = ov
    return bucket_c, bucket_t, overlap


def _bias_body(tab_ref, bc_ref, bt_ref, oc_ref, ot_ref, of_ref, *, seq):
    h = pl.program_id(0)

    lane = lax.broadcasted_iota(jnp.int32, (SUBLANES, LANES), 1)
    tab_row = jnp.zeros((SUBLANES, LANES), F32)
    for b in range(REL_BUCKETS):
        tab_row = jnp.where(lane == b, tab_ref[b, h], tab_row)

    def lookup(bmap):
        rows = bmap.shape[0]
        tab = jnp.broadcast_to(tab_row[0:1], (rows, LANES))
        return jnp.concatenate(
            [jnp.take_along_axis(tab, bmap[:, c0:c0 + LANES], axis=1)
             for c0 in range(0, bmap.shape[1], LANES)], axis=1)

    def step(r, c):
        rows = pl.ds(pl.multiple_of(r * ATT_TILE, ATT_TILE), ATT_TILE)
        oc_ref[0, rows, :] = lookup(bc_ref[rows, :]) * LOG2_E
        return c

    lax.fori_loop(0, seq // ATT_TILE, step, 0)

    tt = lax.broadcasted_iota(jnp.int32, (ATT_TILE, ATT_TILE), 0)
    kk = lax.broadcasted_iota(jnp.int32, (ATT_TILE, ATT_TILE), 1)
    far = tab_ref[REL_BUCKETS - 1, h]
    of_ref[0] = jnp.full(of_ref.shape[1:], far * LOG2_E, F32)
    ot_ref[0, 0] = jnp.zeros((ATT_TILE, ATT_TILE), F32)
    ot_ref[1, 0] = (lookup(bt_ref[1]) - far) * LOG2_E
    ot_ref[2, 0] = jnp.where(kk <= tt, (lookup(bt_ref[0]) - far) * LOG2_E, NEG_INF)
    ot_ref[3, 0] = jnp.where(kk > tt, 0.0, NEG_INF)


def _bias_tables(rel_table, seq):
    bucket_c, bucket_t, _ = _static_maps(seq)
    return pl.pallas_call(
        functools.partial(_bias_body, seq=seq),
        out_shape=(jax.ShapeDtypeStruct((N_HEADS, seq, LANES), F32),
                   jax.ShapeDtypeStruct((4, N_HEADS, ATT_TILE, ATT_TILE), F32),
                   jax.ShapeDtypeStruct((N_HEADS, SUBLANES, LANES), F32)),
        grid=(N_HEADS,),
        in_specs=[pl.BlockSpec(memory_space=pltpu.SMEM),
                  pl.BlockSpec((seq, LANES), lambda h: (0, 0)),
                  pl.BlockSpec((2, ATT_TILE, ATT_TILE), lambda h: (0, 0, 0))],
        out_specs=(pl.BlockSpec((1, seq, LANES), lambda h: (h, 0, 0)),
                   pl.BlockSpec((4, 1, ATT_TILE, ATT_TILE), lambda h: (0, h, 0, 0)),
                   pl.BlockSpec((1, SUBLANES, LANES), lambda h: (h, 0, 0))),
        compiler_params=pltpu.CompilerParams(dimension_semantics=("arbitrary",)),
        name="rel_bias",
    )(rel_table.astype(F32), jnp.asarray(bucket_c), jnp.asarray(bucket_t))


def _compress_body(xk_ref, xv_ref, w1_ref, pe_ref, w2_ref, ok_ref, ov_ref, x_sc):
    n_grp = xk_ref.shape[1] // CMP_STRIDE

    def one(x_ref, idx, o_ref):
        x_sc[...] = x_ref[0].astype(F32)
        x = jnp.concatenate([x_sc[pl.ds(t, n_grp, stride=CMP_STRIDE), :]
                             for t in range(CMP_STRIDE)], axis=1)
        a0 = jnp.dot((x + pe_ref[idx, 0]).astype(BF16), w1_ref[idx, 0],
                     preferred_element_type=F32)
        a1 = jnp.dot((x + pe_ref[idx, 1]).astype(BF16), w1_ref[idx, 1],
                     preferred_element_type=F32)
        pre = a0 + pltpu.roll(a1, LANES - 1, 0)
        hid = jax.nn.gelu(pre).astype(BF16)
        o_ref[0, 0] = jnp.dot(hid, w2_ref[idx], preferred_element_type=F32).astype(BF16)

    one(xk_ref, 0, ok_ref)
    one(xv_ref, 1, ov_ref)


def _compress(proj, w1, pe, w2, B, S):
    G = N_GROUPS
    assert S // CMP_STRIDE == LANES
    half = CMP_STRIDE * HEAD_DIM
    q_cols = N_HEADS

    def spec_x(slot):
        return pl.BlockSpec((1, S, HEAD_DIM), lambda b, g, slot=slot: (b, 0, q_cols + slot * G + g))

    spec_o = pl.BlockSpec((1, 1, LANES, HEAD_DIM), lambda b, g: (b, g, 0, 0))
    out = jax.ShapeDtypeStruct((B, G, LANES, HEAD_DIM), BF16)
    return pl.pallas_call(
        _compress_body,
        out_shape=(out, out),
        grid=(B, G),
        in_specs=[spec_x(0), spec_x(1),
                  pl.BlockSpec((2, 2, half, HEAD_DIM), lambda b, g: (0, 0, 0, 0)),
                  pl.BlockSpec((2, 2, 1, half), lambda b, g: (0, 0, 0, 0)),
                  pl.BlockSpec((2, HEAD_DIM, HEAD_DIM), lambda b, g: (0, 0, 0))],
        out_specs=(spec_o, spec_o),
        scratch_shapes=[pltpu.VMEM((S, HEAD_DIM), F32)],
        compiler_params=pltpu.CompilerParams(dimension_semantics=("arbitrary", "arbitrary")),
        name="nsa_compress",
    )(proj, proj, w1, pe, w2)


def _nsa_body(q_ref, ks_ref, vs_ref, kw_ref, vw_ref, kc_ref, vc_ref, bc_ref, bt_ref, far_ref,
              ovl_ref, augs_ref, augw_ref, gl_ref, o_ref, kts_sc, ktw_sc, q4_sc, m_sc, acc_sc):
    R = HEADS_PER_GROUP
    tq = ATT_TILE
    qi = pl.program_id(2)
    n_tiles = ks_ref.shape[1] // tq
    nb = ovl_ref.shape[0]
    nt = (((1,), (1,)), ((), ()))

    @pl.when(qi == 0)
    def _():
        def tr(j, c):
            rows = pl.ds(pl.multiple_of(j * tq, tq), tq)
            kts_sc[j, :HEAD_DIM, :] = ks_ref[0, rows, :].astype(F32).T.astype(BF16)
            kts_sc[j, HEAD_DIM:, :] = augs_ref[j]
            ktw_sc[j, :HEAD_DIM, :] = kw_ref[0, rows, :].astype(F32).T.astype(BF16)
            ktw_sc[j, HEAD_DIM:, :] = augw_ref[0]
            return c

        lax.fori_loop(0, n_tiles, tr, 0)
        zeros = jnp.zeros((HEAD_DIM, tq), BF16)
        kts_sc[n_tiles, :HEAD_DIM, :] = zeros
        kts_sc[n_tiles, HEAD_DIM:, :] = augs_ref[n_tiles]
        ktw_sc[n_tiles, :HEAD_DIM, :] = zeros
        ktw_sc[n_tiles, HEAD_DIM:, :] = augw_ref[1]

    q = q_ref[0]
    q4 = jnp.concatenate([q[:, r * HEAD_DIM:(r + 1) * HEAD_DIM] for r in range(R)], axis=0)
    q4_sc[:, :HEAD_DIM] = q4

    pos3 = qi * tq + lax.broadcasted_iota(jnp.int32, (1, tq, 1), 1)

    sc = lax.dot_general(q4, kc_ref[0, 0], nt, preferred_element_type=F32)
    sc = sc.reshape(R, tq, LANES) + bc_ref[...]
    cidx = lax.broadcasted_iota(jnp.int32, (1, 1, LANES), 2)
    valid = (cidx * CMP_STRIDE + (CMP_BLOCK - 1) <= pos3) & (cidx < LANES - 1)
    sc = jnp.where(valid, sc, NEG_INF)
    mc = jnp.max(sc, axis=-1, keepdims=True)
    ec = jnp.exp2(sc - mc)
    pc = ec / jnp.sum(ec, axis=-1, keepdims=True)
    pc = jnp.where(pos3 >= CMP_BLOCK - 1, pc, 0.0)
    o_cmp = jnp.dot(pc.reshape(R * tq, LANES).astype(BF16), vc_ref[0, 0],
                    preferred_element_type=F32)

    psum = pc[0]
    for r in range(1, R):
        psum = psum + pc[r]
    p_hi = psum.astype(BF16)
    p_lo = (psum - p_hi.astype(F32)).astype(BF16)
    ovt = ovl_ref[...]
    imp = (lax.dot_general(ovt, p_hi, nt, preferred_element_type=F32)
           + lax.dot_general(ovt, p_lo, nt, preferred_element_type=F32))
    jb = lax.broadcasted_iota(jnp.int32, (nb, 1), 0)
    pos_t = qi * tq + lax.broadcasted_iota(jnp.int32, (1, tq), 1)
    q_blk = lax.shift_right_logical(pos_t, int(math.log2(SEL_BLOCK)))
    forced = (jb == 0) | (jb == q_blk) | (jb == q_blk - 1)
    future = jb > q_blk
    imp = jnp.where(forced, FORCE_SCORE, jnp.where(future, -1.0, imp))
    cnt = jnp.zeros((nb, tq), F32)
    for i in range(nb):
        row = imp[i:i + 1, :]
        beats = (row > imp) | ((row == imp) & (jb > i))
        cnt = cnt + jnp.where(beats, 1.0, 0.0)
    sel_t = jnp.where(cnt < float(min(SEL_TOP_N, nb)), 1.0, 0.0)
    sel = jnp.concatenate([sel_t, jnp.zeros((LANES - nb, tq), F32)], axis=0).T

    lane = lax.broadcasted_iota(jnp.int32, (1, LANES), 1)
    sel_pad = jnp.where(lane < nb, (sel - 1.0) * (-NEG_INF), 0.0)
    for r in range(R):
        far = jnp.broadcast_to(far_ref[r, 0:1, :], (tq, LANES))
        far_hi = far.astype(BF16).astype(F32)
        pad = jnp.where(lane == nb, far_hi, jnp.where(lane == nb + 1, far - far_hi, sel_pad))
        pad = jnp.where(lane == nb + 2, 1.0, pad)
        q4_sc[r * tq:(r + 1) * tq, HEAD_DIM:] = pad.astype(BF16)

    ones = jnp.ones((tq, HEAD_DIM), BF16)
    n_chunks = R * tq // ATT_ROWS
    chunks_per_head = tq // ATT_ROWS
    dead = n_tiles

    def keys(kt_sc, tiles):
        return jnp.concatenate([kt_sc[t] for t in tiles], axis=1)

    def values(v_ref, tiles):
        parts = []
        for t in tiles:
            rows = pl.ds(pl.multiple_of(t * tq, tq), tq)
            parts.append(jnp.concatenate([v_ref[0, rows, :], ones], axis=1))
        return jnp.concatenate(parts, axis=0)

    def logits(ci, kt, kinds):
        r, hh = divmod(ci, chunks_per_head)
        rs = slice(ci * ATT_ROWS, (ci + 1) * ATT_ROWS)
        qs = slice(hh * ATT_ROWS, (hh + 1) * ATT_ROWS)
        s = jnp.dot(q4_sc[rs, :], kt, preferred_element_type=F32)
        return s + jnp.concatenate([bt_ref[kd, r, qs, :] for kd in kinds], axis=1)

    def probs(s, m):
        return jnp.concatenate([jnp.exp2(s[:, k0:k0 + LANES] - m)
                                for k0 in range(0, s.shape[1], LANES)], axis=1).astype(BF16)

    m_sc[...] = jnp.full(m_sc.shape, NEG_INF, F32)
    acc_sc[...] = jnp.zeros(acc_sc.shape, F32)

    def pair(pi, c):
        ja = 2 * pi
        jb = ja + 1
        kt = keys(kts_sc, (ja, jnp.where(jb <= qi, jb, dead)))
        vv = values(vs_ref, (ja, jnp.minimum(jb, n_tiles - 1)))
        kinds = (jnp.clip(ja - qi + 2, 0, 2), jnp.clip(jb - qi + 2, 0, 2))
        for ci in range(n_chunks):
            rs = slice(ci * ATT_ROWS, (ci + 1) * ATT_ROWS)
            s = logits(ci, kt, kinds)
            m_old = m_sc[rs]
            m_new = jnp.maximum(m_old, jnp.max(s, axis=-1, keepdims=True))
            alpha = jnp.exp2(m_old - m_new)
            m_sc[rs] = m_new
            pv = jnp.dot(probs(s, m_new), vv, preferred_element_type=F32)
            acc_sc[rs] = acc_sc[rs] * jnp.concatenate([alpha, alpha], axis=1) + pv
        return c

    lax.fori_loop(0, (qi + 2) // 2, pair, 0)
    acc = acc_sc[...]
    o_sel = acc[:, :HEAD_DIM] / acc[:, HEAD_DIM:]

    n_win = WINDOW // tq
    w_tiles = [qi - n_win + t for t in range(n_win + 1)]
    kt = keys(ktw_sc, [jnp.where(t >= 0, t, dead) for t in w_tiles])
    vv = values(vw_ref, [jnp.maximum(t, 0) for t in w_tiles])
    w_kinds = (3,) + (0,) * (n_win - 2) + (1, 2)
    o_parts = []
    for ci in range(n_chunks):
        s = logits(ci, kt, w_kinds)
        pv = jnp.dot(probs(s, jnp.max(s, axis=-1, keepdims=True)), vv,
                     preferred_element_type=F32)
        o_parts.append(pv[:, :HEAD_DIM] / pv[:, HEAD_DIM:])
    o_win = jnp.concatenate(o_parts, axis=0)

    gates = jax.nn.sigmoid(gl_ref[0, 0])
    outs = []
    for r in range(R):
        hs = slice(r * tq, (r + 1) * tq)
        o_r = (gates[:, r:r + 1] * o_cmp[hs]
               + gates[:, R + r:R + r + 1] * o_sel[hs]
               + gates[:, 2 * R + r:2 * R + r + 1] * o_win[hs])
        outs.append(o_r)
    o_ref[0] = jnp.concatenate(outs, axis=1).astype(o_ref.dtype)


def _nsa_attention(proj, kcmp, vcmp, bias_c, bias_t, bias_far, gates_t, B, S):
    assert WINDOW % ATT_TILE == 0 and S % ATT_TILE == 0
    assert S // CMP_STRIDE == LANES and S // SEL_BLOCK <= LANES
    R, G, tq = HEADS_PER_GROUP, N_GROUPS, ATT_TILE
    n_tiles = S // tq
    nb = S // SEL_BLOCK
    _, _, overlap = _static_maps(S)
    overlap_t = np.ascontiguousarray(overlap.T[:nb])
    assert nb + 3 <= LANES and n_tiles % 2 == 0
    blk_of_key = (np.arange(S) // SEL_BLOCK).reshape(n_tiles, 1, tq)
    aug_sel = np.zeros((n_tiles + 1, LANES, tq), np.float32)
    aug_sel[:n_tiles] = np.arange(LANES).reshape(1, LANES, 1) == blk_of_key
    aug_sel[:n_tiles, nb:nb + 2, :] = 1.0
    aug_sel[n_tiles, nb + 2, :] = NEG_INF
    aug_win = np.zeros((2, LANES, tq), np.float32)
    aug_win[0, nb:nb + 2, :] = 1.0
    aug_win[1, nb + 2, :] = NEG_INF
    q_cols = N_HEADS

    def kv_spec(slot):
        return pl.BlockSpec((1, S, HEAD_DIM),
                            lambda b, g, i, slot=slot: (b, 0, q_cols + slot * G + g))

    cmp_spec = pl.BlockSpec((1, 1, LANES, HEAD_DIM), lambda b, g, i: (b, g, 0, 0))
    in_specs = [
        pl.BlockSpec((1, tq, R * HEAD_DIM), lambda b, g, i: (b, i, g)),
        kv_spec(2), kv_spec(3), kv_spec(4), kv_spec(5),
        cmp_spec, cmp_spec,
        pl.BlockSpec((R, tq, LANES), lambda b, g, i: (g, i, 0)),
        pl.BlockSpec((4, R, tq, tq), lambda b, g, i: (0, g, 0, 0)),
        pl.BlockSpec((R, SUBLANES, LANES), lambda b, g, i: (g, 0, 0)),
        pl.BlockSpec((nb, LANES), lambda b, g, i: (0, 0)),
        pl.BlockSpec((n_tiles + 1, LANES, tq), lambda b, g, i: (0, 0, 0)),
        pl.BlockSpec((2, LANES, tq), lambda b, g, i: (0, 0, 0)),
        pl.BlockSpec((1, 1, tq, 3 * R), lambda b, g, i: (b, g, i, 0)),
    ]
    kdim = HEAD_DIM + LANES
    est = (2 * 4 * S * HEAD_DIM * 2 + 2 * 4 * R * tq * tq * 4 + 2 * R * tq * LANES * 4
           + 2 * 2 * S * LANES * 2 + 2 * S * kdim * 2 + R * tq * (kdim * 2 + LANES * 4 + 2 * HEAD_DIM * 4)
           + 3 * R * tq * LANES * 4 + 16 * ATT_ROWS * tq * 4 + 4 * tq * R * HEAD_DIM * 2)
    assert est <= VMEM_LIMIT_CAP
    return pl.pallas_call(
        _nsa_body,
        out_shape=jax.ShapeDtypeStruct((B, S, N_HEADS * HEAD_DIM), BF16),
        grid=(B, G, n_tiles),
        in_specs=in_specs,
        out_specs=pl.BlockSpec((1, tq, R * HEAD_DIM), lambda b, g, i: (b, i, g)),
        scratch_shapes=[pltpu.VMEM((n_tiles + 1, kdim, tq), BF16),
                        pltpu.VMEM((n_tiles + 1, kdim, tq), BF16),
                        pltpu.VMEM((R * tq, kdim), BF16),
                        pltpu.VMEM((R * tq, LANES), F32),
                        pltpu.VMEM((R * tq, 2 * HEAD_DIM), F32)],
        compiler_params=pltpu.CompilerParams(
            dimension_semantics=("arbitrary", "arbitrary", "arbitrary"),
            vmem_limit_bytes=VMEM_LIMIT_CAP),
        name="nsa_attention",
    )(proj, proj, proj, proj, proj, kcmp, vcmp, bias_c, bias_t, bias_far,
      jnp.asarray(overlap_t, BF16), jnp.asarray(aug_sel, BF16), jnp.asarray(aug_win, BF16),
      gates_t)


@functools.lru_cache(maxsize=None)
def _hgrn_masks():
    C = HGRN_CHUNK
    t = np.arange(C)[:, None]
    s = np.arange(C)[None, :]
    masks = [(t // HGRN_SUB == s // HGRN_SUB) & (s <= t)]
    half = C // 2
    while half >= HGRN_SUB:
        grp = 2 * half
        masks.append((t // grp == s // grp) & (t % grp >= half) & (s % grp < half))
        half //= 2
    assert np.array_equal(np.sum(masks, axis=0), (s <= t).astype(int))
    return np.stack(masks).astype(np.float32), (s <= t).astype(np.float32)


def _hgrn_body(q_ref, f_ref, i_ref, g_ref, lb_ref, gn_ref, msk_ref, tril_ref, o_ref,
               st_all, b_all, k_all, q_all, *, layer, heads):
    C = HGRN_CHUNK
    n_chunks = q_ref.shape[1] // C
    nt = (((1,), (1,)), ((), ()))

    lbp = lb_ref[...]
    e = jnp.exp(lbp - jnp.max(lbp, axis=0, keepdims=True))
    sm = e / jnp.sum(e, axis=0, keepdims=True)
    cum = sm[0:1]
    first = cum
    for d in range(1, layer + 1):
        cum = cum + sm[d:d + 1]
    lb_all = cum - first
    log_lb_all = jnp.log(lb_all)
    log_1m_all = jnp.log1p(-lb_all)
    gn_all = gn_ref[...]

    cols = lax.broadcasted_iota(jnp.int32, (1, C), 1)
    tril = tril_ref[...]

    st_all[...] = jnp.zeros(st_all.shape, F32)

    def chunk(c, carry):
        for hh in range(heads):
            one_head(c, hh)
        return carry

    def one_head(c, hh):
        sl = pl.ds(pl.multiple_of(c * C, C), C)
        hs = slice(hh * HEAD_DIM, (hh + 1) * HEAD_DIM)
        lb, log_lb, log_1m, gn = lb_all[:, hs], log_lb_all[:, hs], log_1m_all[:, hs], gn_all[:, hs]
        st_ref, b_sc, k_sc, q_sc = st_all.at[hh], b_all.at[hh], k_all.at[hh], q_all.at[hh]
        qr = q_ref[0, sl, hs].astype(F32)
        x = f_ref[0, sl, hs].astype(F32)
        v = i_ref[0, sl, hs].astype(F32)
        gr = g_ref[0, sl, hs].astype(F32)
        q = qr / (1.0 + jnp.exp2(qr * (-LOG2_E)))
        ex = jnp.exp2(jnp.abs(x) * (-LOG2_E))
        u = 1.0 + ex
        r1 = 1.0 / u
        k = (1.0 - lb) * jnp.where(x >= 0.0, ex * r1, r1)
        c2 = log_1m + (jnp.minimum(x, 0.0) - jnp.log(u))
        e2 = jnp.exp2(jnp.abs(log_lb - c2) * (-LOG2_E))
        log_f = jnp.maximum(log_lb, c2) + jnp.log(1.0 + e2)

        lf_hi = log_f.astype(BF16)
        lf_lo = (log_f - lf_hi.astype(F32)).astype(BF16)
        b = (jnp.dot(tril, lf_hi, preferred_element_type=F32)
             + jnp.dot(tril, lf_lo, preferred_element_type=F32)) * LOG2_E
        b_sc[...] = b
        k_sc[...] = k
        q_sc[...] = q

        vb = v.astype(BF16)
        st = st_ref[...]
        o = lax.dot_general((q * jnp.exp2(b)).astype(BF16), st.astype(BF16), nt,
                            preferred_element_type=F32)

        pieces = []
        for blk in range(C // HGRN_SUB):
            r0 = blk * HGRN_SUB
            bt = b_sc[r0:r0 + HGRN_SUB, :]
            qt = q_sc[r0:r0 + HGRN_SUB, :]
            arow = jnp.zeros((HGRN_SUB, C), F32)
            for s in range(HGRN_SUB):
                bs = b_sc[r0 + s:r0 + s + 1, :]
                ks = k_sc[r0 + s:r0 + s + 1, :]
                col = jnp.sum(jnp.exp2(bt - bs) * qt * ks, axis=-1, keepdims=True)
                arow = jnp.where(cols == r0 + s, col, arow)
            pieces.append(arow)
        a = jnp.where(msk_ref[0] > 0.5, jnp.concatenate(pieces, axis=0), 0.0)

        half = C // 2
        lvl = 1
        while half >= HGRN_SUB:
            grp = 2 * half
            anc = jnp.concatenate(
                [jnp.broadcast_to(b_sc[g0 + half - 1:g0 + half, :], (grp, b.shape[1]))
                 for g0 in range(0, C, grp)], axis=0)
            e = jnp.exp2(-jnp.abs(b - anc))
            al = lax.dot_general((q * e).astype(BF16), (k * e).astype(BF16), nt,
                                 preferred_element_type=F32)
            a = jnp.where(msk_ref[lvl] > 0.5, al, a)
            half //= 2
            lvl += 1

        o = o + jnp.dot(a.astype(BF16), vb, preferred_element_type=F32)

        b_last = b_sc[C - 1:C, :]
        kh = (k * jnp.exp2(b_last - b)).astype(BF16)
        st_ref[...] = st * jnp.exp2(b_last) + jnp.dot(v.T.astype(BF16), kh,
                                                     preferred_element_type=F32)

        ms = jnp.mean(o * o, axis=-1, keepdims=True)
        o = o * lax.rsqrt(ms + RMS_EPS) * gn * (gr / (1.0 + jnp.exp2(gr * (-LOG2_E))))
        o_ref[0, sl, hs] = o.astype(o_ref.dtype)

    lax.fori_loop(0, n_chunks, chunk, 0)


def _hgrn(proj, hgrn_lb, onorm, layer, B, S):
    H = N_HEADS
    depth = hgrn_lb.shape[0]

    nh = HGRN_HEADS_PER_STEP
    width = nh * HEAD_DIM
    steps = H // nh

    def spec(part):
        return pl.BlockSpec((1, S, width), lambda b, h, part=part: (b, 0, part * steps + h))

    masks, tril = _hgrn_masks()
    C = HGRN_CHUNK
    est = 2 * 4 * S * width * proj.dtype.itemsize + 2 * S * width * 2 + nh * 64 * C * C * 4
    return pl.pallas_call(
        functools.partial(_hgrn_body, layer=layer, heads=nh),
        out_shape=jax.ShapeDtypeStruct((B, S, H * HEAD_DIM), BF16),
        grid=(B, steps),
        in_specs=[spec(0), spec(1), spec(2), spec(3),
                  pl.BlockSpec((depth, width), lambda b, h: (0, h)),
                  pl.BlockSpec((1, width), lambda b, h: (0, h)),
                  pl.BlockSpec(masks.shape, lambda b, h: (0, 0, 0)),
                  pl.BlockSpec((C, C), lambda b, h: (0, 0))],
        out_specs=pl.BlockSpec((1, S, width), lambda b, h: (b, 0, h)),
        scratch_shapes=[pltpu.VMEM((nh, HEAD_DIM, HEAD_DIM), F32),
                        pltpu.VMEM((nh, C, HEAD_DIM), F32),
                        pltpu.VMEM((nh, C, HEAD_DIM), F32),
                        pltpu.VMEM((nh, C, HEAD_DIM), F32)],
        compiler_params=pltpu.CompilerParams(
            dimension_semantics=("arbitrary", "arbitrary"),
            vmem_limit_bytes=_vmem_limit(est)),
        name="hgrn2_recurrence",
    )(proj, proj, proj, proj, hgrn_lb.astype(F32), onorm.reshape(1, -1).astype(F32),
      jnp.asarray(masks), jnp.asarray(tril, BF16))


def _mlp_body(x_ref, gi_ref, wu_ref, wd_ref, go_ref, o_ref, hn_ref, *, nf, tm):
    f = pl.program_id(1)

    def row_chunks(fn):
        def step(r, c):
            fn(pl.ds(pl.multiple_of(r * NORM_ROWS, NORM_ROWS), NORM_ROWS))
            return c

        lax.fori_loop(0, tm // NORM_ROWS, step, 0)

    @pl.when(f == 0)
    def _():
        g = gi_ref[...]

        def norm_in(rows):
            xs = x_ref[rows, :]
            ms = jnp.mean(xs * xs, axis=-1, keepdims=True)
            hn_ref[rows, :] = (xs * lax.rsqrt(ms + RMS_EPS) * g).astype(BF16)

        row_chunks(norm_in)
        o_ref[...] = jnp.zeros(o_ref.shape, F32)

    hid = jnp.dot(hn_ref[...], wu_ref[...], preferred_element_type=F32)
    hid = jnp.square(jnp.maximum(hid, 0.0)).astype(BF16)
    o_ref[...] += jnp.dot(hid, wd_ref[...], preferred_element_type=F32)

    @pl.when(f == nf - 1)
    def _():
        g = go_ref[...]

        def norm_out(rows):
            y = o_ref[rows, :]
            ms = jnp.mean(y * y, axis=-1, keepdims=True)
            o_ref[rows, :] = x_ref[rows, :] + y * lax.rsqrt(ms + RMS_EPS) * g

        row_chunks(norm_out)


def _mlp(xf, g_in, g_out, w_up, w_down, layer, *, tm=512, tf=1024):
    M, D = xf.shape
    F = w_up.shape[2]
    assert M % tm == 0 and F % tf == 0 and tm % NORM_ROWS == 0
    nf = F // tf
    est = (2 * tm * D * 4 + 2 * tm * D * 4 + tm * D * 2 + 2 * 2 * D * tf * 2
           + tm * tf * 6 + 2 * tm * D * 4)
    return pl.pallas_call(
        functools.partial(_mlp_body, nf=nf, tm=tm),
        out_shape=jax.ShapeDtypeStruct((M, D), F32),
        grid=(M // tm, nf),
        in_specs=[pl.BlockSpec((tm, D), lambda i, f: (i, 0)),
                  pl.BlockSpec((1, D), lambda i, f: (0, 0)),
                  pl.BlockSpec((None, D, tf), lambda i, f: (layer, 0, f)),
                  pl.BlockSpec((None, tf, D), lambda i, f: (layer, f, 0)),
                  pl.BlockSpec((1, D), lambda i, f: (0, 0))],
        out_specs=pl.BlockSpec((tm, D), lambda i, f: (i, 0)),
        scratch_shapes=[pltpu.VMEM((tm, D), BF16)],
        compiler_params=pltpu.CompilerParams(
            dimension_semantics=("arbitrary", "arbitrary"),
            vmem_limit_bytes=_vmem_limit(est)),
        name="mlp",
    )(xf, g_in.reshape(1, D).astype(F32), w_up.astype(BF16), w_down.astype(BF16),
      g_out.reshape(1, D).astype(F32))


def _nsa_layer(xf, B, S, g_in, g_out, rel_table, w_in, cmp_pe, cmp_w1, cmp_w2, w_out):
    D = xf.shape[1]
    G, R, Dh = N_GROUPS, HEADS_PER_GROUP, HEAD_DIM
    n_main = N_HEADS * Dh + 6 * G * Dh
    n_gate = 3 * N_HEADS
    w_all = w_in.astype(BF16)
    w_gate = jnp.pad(w_in[:, n_main:], ((0, 0), (0, LANES - n_gate))).astype(BF16)
    colscale = jnp.concatenate([jnp.full((N_HEADS * Dh,), Dh ** -0.5 * LOG2_E, F32),
                                jnp.ones((6 * G * Dh,), F32)])[None]

    proj, glog = _matmul(xf, w_all, tm=1024, tn=1024, norm_g=g_in, epi="colscale",
                         colscale=colscale, side_w=w_gate, n_out=n_main, out_dtype=BF16,
                         name="nsa_proj")
    gates_t = (glog[:, :n_gate].reshape(B, S, 3, G, R).transpose(0, 3, 1, 2, 4)
               .reshape(B, G, S, 3 * R))

    proj3 = proj.reshape(B, S, n_main)
    half = CMP_STRIDE * Dh
    kcmp, vcmp = _compress(proj3, cmp_w1.reshape(2, 2, half, Dh).astype(BF16),
                           cmp_pe.reshape(2, 2, 1, half).astype(F32),
                           cmp_w2.astype(BF16), B, S)

    bias_c, bias_t, bias_far = _bias_tables(rel_table, S)
    attn = _nsa_attention(proj3, kcmp, vcmp, bias_c, bias_t, bias_far, gates_t, B, S)
    return _matmul(attn.reshape(B * S, D), w_out.astype(BF16), tm=512, tn=D,
                   epi="resnorm", res=xf, res_g=g_out, out_dtype=F32, name="nsa_out")


def _hgrn_layer(xf, B, S, layer, g_in, g_out, w_in, hgrn_lb, onorm, w_out):
    D = xf.shape[1]
    proj = _matmul(xf, w_in.astype(BF16), tm=1024, tn=1024, norm_g=g_in,
                   out_dtype=BF16, name="hgrn_proj")
    mixed = _hgrn(proj.reshape(B, S, 4 * D), hgrn_lb, onorm, layer, B, S)
    return _matmul(mixed.reshape(B * S, D), w_out.astype(BF16), tm=512, tn=D,
                   epi="resnorm", res=xf, res_g=g_out, out_dtype=F32, name="hgrn_out")


def kernel(x, norm_g, rel_table, nsa_w_in, nsa_cmp_pe, nsa_cmp_w1, nsa_cmp_w2, nsa_w_out,
           hgrn_w_in, hgrn_lb, hgrn_onorm, hgrn_w_out, mlp_w_up, mlp_w_down):
    B, S, D = x.shape
    depth = norm_g.shape[0]
    assert D == N_HEADS * HEAD_DIM and S % ATT_TILE == 0 and S % HGRN_CHUNK == 0
    xf = x.reshape(B * S, D).astype(F32)
    for layer in range(depth):
        j = layer // 2
        if layer % 2 == 0:
            xf = _nsa_layer(xf, B, S, norm_g[layer, 0], norm_g[layer, 1], rel_table,
                            nsa_w_in[j], nsa_cmp_pe[j], nsa_cmp_w1[j], nsa_cmp_w2[j],
                            nsa_w_out[j])
        else:
            xf = _hgrn_layer(xf, B, S, layer, norm_g[layer, 0], norm_g[layer, 1],
                             hgrn_w_in[j], hgrn_lb, hgrn_onorm[j], hgrn_w_out[j])
        xf = _mlp(xf, norm_g[layer, 2], norm_g[layer, 3], mlp_w_up, mlp_w_down, layer)
    return xf.reshape(B, S, D).astype(x.dtype)
```

```python
import functools
import math

import numpy as np
import jax
import jax.numpy as jnp
from jax import lax
from jax.experimental import pallas as pl
from jax.experimental.pallas import tpu as pltpu

F32 = jnp.float32
BF16 = jnp.bfloat16

N_HEADS = 16
N_GROUPS = 4
HEADS_PER_GROUP = N_HEADS // N_GROUPS
HEAD_DIM = 128
CMP_BLOCK = 32
CMP_STRIDE = 16
SEL_BLOCK = 64
SEL_TOP_N = 8
WINDOW = 512
FORCE_SCORE = 1.0e4
REL_BUCKETS = 32
REL_MAX_DIST = 128
RMS_EPS = 1e-6
NEG_INF = -1.0e30
LOG2_E = math.log2(math.e)

LANES = 128
SUBLANES = 8
VMEM_BYTES_V7X = 64 * 1024 * 1024
VMEM_LIMIT_CAP = VMEM_BYTES_V7X - 8 * 1024 * 1024

ATT_TILE = 256
ATT_ROWS = 128
HGRN_CHUNK = 128
HGRN_SUB = 8
HGRN_HEADS_PER_STEP = 8
NORM_ROWS = 256


VMEM_LIMIT_FLOOR = 32 * 1024 * 1024


def _vmem_limit(nbytes):
    return int(min(VMEM_LIMIT_CAP, max(VMEM_LIMIT_FLOOR, nbytes)))


def _mm_body(*refs, norm, epi, side, tm):
    it = iter(refs)
    x_ref = next(it)
    g_ref = next(it) if norm else None
    w_ref = next(it)
    ws_ref = next(it) if side else None
    cs_ref = next(it) if epi == "colscale" else None
    res_ref = next(it) if epi == "resnorm" else None
    go_ref = next(it) if epi == "resnorm" else None
    o_ref = next(it)
    os_ref = next(it) if side else None
    hn_ref = next(it) if norm else None

    j = pl.program_id(1)

    if norm:
        @pl.when(j == 0)
        def _():
            g = g_ref[...]

            def step(r, c):
                rows = pl.ds(pl.multiple_of(r * NORM_ROWS, NORM_ROWS), NORM_ROWS)
                xs = x_ref[rows, :]
                ms = jnp.mean(xs * xs, axis=-1, keepdims=True)
                hn_ref[rows, :] = (xs * lax.rsqrt(ms + RMS_EPS) * g).astype(BF16)
                return c

            lax.fori_loop(0, tm // NORM_ROWS, step, 0)

        lhs = hn_ref[...]
    else:
        lhs = x_ref[...]

    if side:
        @pl.when(j == 0)
        def _():
            os_ref[...] = jnp.dot(lhs, ws_ref[...], preferred_element_type=F32)

    acc = jnp.dot(lhs, w_ref[...], preferred_element_type=F32)
    if epi == "colscale":
        acc = acc * cs_ref[...]
    elif epi == "resnorm":
        ms = jnp.mean(acc * acc, axis=-1, keepdims=True)
        acc = res_ref[...] + acc * lax.rsqrt(ms + RMS_EPS) * go_ref[...]
    o_ref[...] = acc.astype(o_ref.dtype)


def _matmul(x, w, *, tm, tn, norm_g=None, epi="none", colscale=None, res=None, res_g=None,
            side_w=None, n_out=None, out_dtype=BF16, name="mm"):
    M, K = x.shape
    N = w.shape[1] if n_out is None else n_out
    assert N <= w.shape[1]
    norm = norm_g is not None
    side = side_w is not None
    assert M % tm == 0 and N % tn == 0
    assert not (epi == "resnorm" and tn != N)

    in_specs = [pl.BlockSpec((tm, K), lambda i, j: (i, 0))]
    args = [x]
    if norm:
        in_specs.append(pl.BlockSpec((1, K), lambda i, j: (0, 0)))
        args.append(norm_g.reshape(1, K).astype(F32))
    in_specs.append(pl.BlockSpec((K, tn), lambda i, j: (0, j)))
    args.append(w)
    ns = side_w.shape[1] if side else 0
    if side:
        in_specs.append(pl.BlockSpec((K, ns), lambda i, j: (0, 0)))
        args.append(side_w)
    if epi == "colscale":
        in_specs.append(pl.BlockSpec((1, tn), lambda i, j: (0, j)))
        args.append(colscale)
    if epi == "resnorm":
        in_specs.append(pl.BlockSpec((tm, tn), lambda i, j: (i, j)))
        args.append(res)
        in_specs.append(pl.BlockSpec((1, tn), lambda i, j: (0, j)))
        args.append(res_g.reshape(1, N).astype(F32))

    out_shape = jax.ShapeDtypeStruct((M, N), out_dtype)
    out_specs = pl.BlockSpec((tm, tn), lambda i, j: (i, j))
    if side:
        out_shape = (out_shape, jax.ShapeDtypeStruct((M, ns), F32))
        out_specs = (out_specs, pl.BlockSpec((tm, ns), lambda i, j: (i, 0)))

    xb = x.dtype.itemsize
    ob = jnp.dtype(out_dtype).itemsize
    est = (2 * tm * K * xb + 2 * K * tn * 2 + 2 * tm * tn * ob + (tm * K * 2 if norm else 0)
           + (2 * tm * tn * 4 if epi == "resnorm" else 0) + 3 * tm * tn * 4
           + 2 * K * ns * 2 + 3 * tm * ns * 4)

    return pl.pallas_call(
        functools.partial(_mm_body, norm=norm, epi=epi, side=side, tm=tm),
        out_shape=out_shape,
        grid=(M // tm, N // tn),
        in_specs=in_specs,
        out_specs=out_specs,
        scratch_shapes=[pltpu.VMEM((tm, K), BF16)] if norm else [],
        compiler_params=pltpu.CompilerParams(
            dimension_semantics=("arbitrary", "arbitrary"),
            vmem_limit_bytes=_vmem_limit(est)),
        name=name,
    )(*args)


def _rel_bucket_np(dist):
    n = np.maximum(dist, 0)
    max_exact = REL_BUCKETS // 2
    nf = np.maximum(n, 1).astype(np.float32)
    ratio = np.log(nf / np.float32(max_exact)) / np.float32(math.log(REL_MAX_DIST / max_exact))
    large = max_exact + (ratio * np.float32(REL_BUCKETS - max_exact)).astype(np.int32)
    large = np.minimum(large, REL_BUCKETS - 1)
    return np.where(n < max_exact, n, large).astype(np.int32)


@functools.lru_cache(maxsize=None)
def _static_maps(seq):
    n_cmp = LANES
    pos = np.arange(seq, dtype=np.int32)[:, None]
    c_end = np.arange(n_cmp, dtype=np.int32)[None, :] * CMP_STRIDE + CMP_BLOCK - 1
    bucket_c = _rel_bucket_np(pos - c_end)
    t = np.arange(ATT_TILE, dtype=np.int32)[:, None]
    k = np.arange(ATT_TILE, dtype=np.int32)[None, :]
    bucket_t = np.stack([_rel_bucket_np(t - k), _rel_bucket_np(ATT_TILE + t - k)])
    assert _rel_bucket_np(np.array([ATT_TILE + 1]))[0] == REL_BUCKETS - 1
    nc = seq // CMP_STRIDE - CMP_BLOCK // CMP_STRIDE + 1
    nb = seq // SEL_BLOCK
    c_start = np.arange(nc)[:, None] * CMP_STRIDE
    b_start = np.arange(nb)[None, :] * SEL_BLOCK
    ov = ((c_start <= b_start + SEL_BLOCK - 1) & (c_start + CMP_BLOCK - 1 >= b_start))
    overlap = np.zeros((LANES, LANES), np.float32)
    overlap[:nc, :nb] = ov
    return bucket_c, bucket_t, overlap


def _bias_body(tab_ref, bc_ref, bt_ref, oc_ref, ot_ref, of_ref, *, seq):
    h = pl.program_id(0)

    lane = lax.broadcasted_iota(jnp.int32, (SUBLANES, LANES), 1)
    tab_row = jnp.zeros((SUBLANES, LANES), F32)
    for b in range(REL_BUCKETS):
        tab_row = jnp.where(lane == b, tab_ref[b, h], tab_row)

    def lookup(bmap):
        rows = bmap.shape[0]
        tab = jnp.broadcast_to(tab_row[0:1], (rows, LANES))
        return jnp.concatenate(
            [jnp.take_along_axis(tab, bmap[:, c0:c0 + LANES], axis=1)
             for c0 in range(0, bmap.shape[1], LANES)], axis=1)

    def step(r, c):
        rows = pl.ds(pl.multiple_of(r * ATT_TILE, ATT_TILE), ATT_TILE)
        oc_ref[0, rows, :] = lookup(bc_ref[rows, :]) * LOG2_E
        return c

    lax.fori_loop(0, seq // ATT_TILE, step, 0)

    tt = lax.broadcasted_iota(jnp.int32, (ATT_TILE, ATT_TILE), 0)
    kk = lax.broadcasted_iota(jnp.int32, (ATT_TILE, ATT_TILE), 1)
    far = tab_ref[REL_BUCKETS - 1, h]
    of_ref[0] = jnp.full(of_ref.shape[1:], far * LOG2_E, F32)
    ot_ref[0, 0] = jnp.zeros((ATT_TILE, ATT_TILE), F32)
    ot_ref[1, 0] = (lookup(bt_ref[1]) - far) * LOG2_E
    ot_ref[2, 0] = jnp.where(kk <= tt, (lookup(bt_ref[0]) - far) * LOG2_E, NEG_INF)
    ot_ref[3, 0] = jnp.where(kk > tt, 0.0, NEG_INF)


def _bias_tables(rel_table, seq):
    bucket_c, bucket_t, _ = _static_maps(seq)
    return pl.pallas_call(
        functools.partial(_bias_body, seq=seq),
        out_shape=(jax.ShapeDtypeStruct((N_HEADS, seq, LANES), F32),
                   jax.ShapeDtypeStruct((4, N_HEADS, ATT_TILE, ATT_TILE), F32),
                   jax.ShapeDtypeStruct((N_HEADS, SUBLANES, LANES), F32)),
        grid=(N_HEADS,),
        in_specs=[pl.BlockSpec(memory_space=pltpu.SMEM),
                  pl.BlockSpec((seq, LANES), lambda h: (0, 0)),
                  pl.BlockSpec((2, ATT_TILE, ATT_TILE), lambda h: (0, 0, 0))],
        out_specs=(pl.BlockSpec((1, seq, LANES), lambda h: (h, 0, 0)),
                   pl.BlockSpec((4, 1, ATT_TILE, ATT_TILE), lambda h: (0, h, 0, 0)),
                   pl.BlockSpec((1, SUBLANES, LANES), lambda h: (h, 0, 0))),
        compiler_params=pltpu.CompilerParams(dimension_semantics=("arbitrary",)),
        name="rel_bias",
    )(rel_table.astype(F32), jnp.asarray(bucket_c), jnp.asarray(bucket_t))


def _compress_body(xk_ref, xv_ref, w1_ref, pe_ref, w2_ref, ok_ref, ov_ref, x_sc):
    n_grp = xk_ref.shape[1] // CMP_STRIDE

    def one(x_ref, idx, o_ref):
        x_sc[...] = x_ref[0].astype(F32)
        x = jnp.concatenate([x_sc[pl.ds(t, n_grp, stride=CMP_STRIDE), :]
                             for t in range(CMP_STRIDE)], axis=1)
        a0 = jnp.dot((x + pe_ref[idx, 0]).astype(BF16), w1_ref[idx, 0],
                     preferred_element_type=F32)
        a1 = jnp.dot((x + pe_ref[idx, 1]).astype(BF16), w1_ref[idx, 1],
                     preferred_element_type=F32)
        pre = a0 + pltpu.roll(a1, LANES - 1, 0)
        hid = jax.nn.gelu(pre).astype(BF16)
        o_ref[0, 0] = jnp.dot(hid, w2_ref[idx], preferred_element_type=F32).astype(BF16)

    one(xk_ref, 0, ok_ref)
    one(xv_ref, 1, ov_ref)


def _compress(proj, w1, pe, w2, B, S):
    G = N_GROUPS
    assert S // CMP_STRIDE == LANES
    half = CMP_STRIDE * HEAD_DIM
    q_cols = N_HEADS

    def spec_x(slot):
        return pl.BlockSpec((1, S, HEAD_DIM), lambda b, g, slot=slot: (b, 0, q_cols + slot * G + g))

    spec_o = pl.BlockSpec((1, 1, LANES, HEAD_DIM), lambda b, g: (b, g, 0, 0))
    out = jax.ShapeDtypeStruct((B, G, LANES, HEAD_DIM), BF16)
    return pl.pallas_call(
        _compress_body,
        out_shape=(out, out),
        grid=(B, G),
        in_specs=[spec_x(0), spec_x(1),
                  pl.BlockSpec((2, 2, half, HEAD_DIM), lambda b, g: (0, 0, 0, 0)),
                  pl.BlockSpec((2, 2, 1, half), lambda b, g: (0, 0, 0, 0)),
                  pl.BlockSpec((2, HEAD_DIM, HEAD_DIM), lambda b, g: (0, 0, 0))],
        out_specs=(spec_o, spec_o),
        scratch_shapes=[pltpu.VMEM((S, HEAD_DIM), F32)],
        compiler_params=pltpu.CompilerParams(dimension_semantics=("arbitrary", "arbitrary")),
        name="nsa_compress",
    )(proj, proj, w1, pe, w2)


def _nsa_body(q_ref, ks_ref, vs_ref, kw_ref, vw_ref, kc_ref, vc_ref, bc_ref, bt_ref, far_ref,
              ovl_ref, augs_ref, augw_ref, gl_ref, o_ref, kts_sc, ktw_sc, q4_sc, m_sc, acc_sc):
    R = HEADS_PER_GROUP
    tq = ATT_TILE
    qi = pl.program_id(2)
    n_tiles = ks_ref.shape[1] // tq
    nb = ovl_ref.shape[0]
    nt = (((1,), (1,)), ((), ()))

    @pl.when(qi == 0)
    def _():
        def tr(j, c):
            rows = pl.ds(pl.multiple_of(j * tq, tq), tq)
            kts_sc[j, :HEAD_DIM, :] = ks_ref[0, rows, :].T
            kts_sc[j, HEAD_DIM:, :] = augs_ref[j]
            ktw_sc[j, :HEAD_DIM, :] = kw_ref[0, rows, :].T
            ktw_sc[j, HEAD_DIM:, :] = augw_ref[0]
            return c

        lax.fori_loop(0, n_tiles, tr, 0)
        zeros = jnp.zeros((HEAD_DIM, tq), BF16)
        kts_sc[n_tiles, :HEAD_DIM, :] = zeros
        kts_sc[n_tiles, HEAD_DIM:, :] = augs_ref[n_tiles]
        ktw_sc[n_tiles, :HEAD_DIM, :] = zeros
        ktw_sc[n_tiles, HEAD_DIM:, :] = augw_ref[1]

    q = q_ref[0]
    q4 = jnp.concatenate([q[:, r * HEAD_DIM:(r + 1) * HEAD_DIM] for r in range(R)], axis=0)
    q4_sc[:, :HEAD_DIM] = q4

    pos3 = qi * tq + lax.broadcasted_iota(jnp.int32, (1, tq, 1), 1)

    sc = lax.dot_general(q4, kc_ref[0, 0], nt, preferred_element_type=F32)
    sc = sc.reshape(R, tq, LANES) + bc_ref[...]
    cidx = lax.broadcasted_iota(jnp.int32, (1, 1, LANES), 2)
    valid = (cidx * CMP_STRIDE + (CMP_BLOCK - 1) <= pos3) & (cidx < LANES - 1)
    sc = jnp.where(valid, sc, NEG_INF)
    mc = jnp.max(sc, axis=-1, keepdims=True)
    ec = jnp.exp2(sc - mc)
    pc = ec / jnp.sum(ec, axis=-1, keepdims=True)
    pc = jnp.where(pos3 >= CMP_BLOCK - 1, pc, 0.0)
    o_cmp = jnp.dot(pc.reshape(R * tq, LANES).astype(BF16), vc_ref[0, 0],
                    preferred_element_type=F32)

    psum = pc[0]
    for r in range(1, R):
        psum = psum + pc[r]
    p_hi = psum.astype(BF16)
    p_lo = (psum - p_hi.astype(F32)).astype(BF16)
    ovt = ovl_ref[...]
    imp = (lax.dot_general(ovt, p_hi, nt, preferred_element_type=F32)
           + lax.dot_general(ovt, p_lo, nt, preferred_element_type=F32))
    jb = lax.broadcasted_iota(jnp.int32, (nb, 1), 0)
    pos_t = qi * tq + lax.broadcasted_iota(jnp.int32, (1, tq), 1)
    q_blk = lax.shift_right_logical(pos_t, int(math.log2(SEL_BLOCK)))
    forced = (jb == 0) | (jb == q_blk) | (jb == q_blk - 1)
    future = jb > q_blk
    imp = jnp.where(forced, FORCE_SCORE, jnp.where(future, -1.0, imp))
    cnt = jnp.zeros((nb, tq), F32)
    for i in range(nb):
        row = imp[i:i + 1, :]
        beats = (row > imp) | ((row == imp) & (jb > i))
        cnt = cnt + jnp.where(beats, 1.0, 0.0)
    sel_t = jnp.where(cnt < float(min(SEL_TOP_N, nb)), 1.0, 0.0)
    sel = jnp.concatenate([sel_t, jnp.zeros((LANES - nb, tq), F32)], axis=0).T

    lane = lax.broadcasted_iota(jnp.int32, (1, LANES), 1)
    sel_pad = jnp.where(lane < nb, (sel - 1.0) * (-NEG_INF), 0.0)
    for r in range(R):
        far = jnp.broadcast_to(far_ref[r, 0:1, :], (tq, LANES))
        far_hi = far.astype(BF16).astype(F32)
        pad = jnp.where(lane == nb, far_hi, jnp.where(lane == nb + 1, far - far_hi, sel_pad))
        pad = jnp.where(lane == nb + 2, 1.0, pad)
        q4_sc[r * tq:(r + 1) * tq, HEAD_DIM:] = pad.astype(BF16)

    ones = jnp.ones((tq, HEAD_DIM), BF16)
    n_chunks = R * tq // ATT_ROWS
    chunks_per_head = tq // ATT_ROWS
    dead = n_tiles

    def keys(kt_sc, tiles):
        return jnp.concatenate([kt_sc[t] for t in tiles], axis=1)

    def values(v_ref, tiles):
        parts = []
        for t in tiles:
            rows = pl.ds(pl.multiple_of(t * tq, tq), tq)
            parts.append(jnp.concatenate([v_ref[0, rows, :], ones], axis=1))
        return jnp.concatenate(parts, axis=0)

    def logits(ci, kt, kinds):
        r, hh = divmod(ci, chunks_per_head)
        rs = slice(ci * ATT_ROWS, (ci + 1) * ATT_ROWS)
        qs = slice(hh * ATT_ROWS, (hh + 1) * ATT_ROWS)
        s = jnp.dot(q4_sc[rs, :], kt, preferred_element_type=F32)
        return s + jnp.concatenate([bt_ref[kd, r, qs, :] for kd in kinds], axis=1)

    def probs(s, m):
        return jnp.concatenate([jnp.exp2(s[:, k0:k0 + LANES] - m)
                                for k0 in range(0, s.shape[1], LANES)], axis=1).astype(BF16)

    m_sc[...] = jnp.full(m_sc.shape, NEG_INF, F32)
    acc_sc[...] = jnp.zeros(acc_sc.shape, F32)

    def pair(pi, c):
        ja = 2 * pi
        jb = ja + 1
        kt = keys(kts_sc, (ja, jnp.where(jb <= qi, jb, dead)))
        vv = values(vs_ref, (ja, jnp.minimum(jb, n_tiles - 1)))
        kinds = (jnp.clip(ja - qi + 2, 0, 2), jnp.clip(jb - qi + 2, 0, 2))
        for ci in range(n_chunks):
            rs = slice(ci * ATT_ROWS, (ci + 1) * ATT_ROWS)
            s = logits(ci, kt, kinds)
            m_old = m_sc[rs]
            m_new = jnp.maximum(m_old, jnp.max(s, axis=-1, keepdims=True))
            alpha = jnp.exp2(m_old - m_new)
            m_sc[rs] = m_new
            pv = jnp.dot(probs(s, m_new), vv, preferred_element_type=F32)
            acc_sc[rs] = acc_sc[rs] * jnp.concatenate([alpha, alpha], axis=1) + pv
        return c

    lax.fori_loop(0, (qi + 2) // 2, pair, 0)
    acc = acc_sc[...]
    o_sel = acc[:, :HEAD_DIM] / acc[:, HEAD_DIM:]

    n_win = WINDOW // tq
    w_tiles = [qi - n_win + t for t in range(n_win + 1)]
    kt = keys(ktw_sc, [jnp.where(t >= 0, t, dead) for t in w_tiles])
    vv = values(vw_ref, [jnp.maximum(t, 0) for t in w_tiles])
    w_kinds = (3,) + (0,) * (n_win - 2) + (1, 2)
    o_parts = []
    for ci in range(n_chunks):
        s = logits(ci, kt, w_kinds)
        pv = jnp.dot(probs(s, jnp.max(s, axis=-1, keepdims=True)), vv,
                     preferred_element_type=F32)
        o_parts.append(pv[:, :HEAD_DIM] / pv[:, HEAD_DIM:])
    o_win = jnp.concatenate(o_parts, axis=0)

    gates = jax.nn.sigmoid(gl_ref[0, 0])
    outs = []
    for r in range(R):
        hs = slice(r * tq, (r + 1) * tq)
        o_r = (gates[:, r:r + 1] * o_cmp[hs]
               + gates[:, R + r:R + r + 1] * o_sel[hs]
               + gates[:, 2 * R + r:2 * R + r + 1] * o_win[hs])
        outs.append(o_r)
    o_ref[0] = jnp.concatenate(outs, axis=1).astype(o_ref.dtype)


def _nsa_attention(proj, kcmp, vcmp, bias_c, bias_t, bias_far, gates_t, B, S):
    assert WINDOW % ATT_TILE == 0 and S % ATT_TILE == 0
    assert S // CMP_STRIDE == LANES and S // SEL_BLOCK <= LANES
    R, G, tq = HEADS_PER_GROUP, N_GROUPS, ATT_TILE
    n_tiles = S // tq
    nb = S // SEL_BLOCK
    _, _, overlap = _static_maps(S)
    overlap_t = np.ascontiguousarray(overlap.T[:nb])
    assert nb + 3 <= LANES and n_tiles % 2 == 0
    blk_of_key = (np.arange(S) // SEL_BLOCK).reshape(n_tiles, 1, tq)
    aug_sel = np.zeros((n_tiles + 1, LANES, tq), np.float32)
    aug_sel[:n_tiles] = np.arange(LANES).reshape(1, LANES, 1) == blk_of_key
    aug_sel[:n_tiles, nb:nb + 2, :] = 1.0
    aug_sel[n_tiles, nb + 2, :] = NEG_INF
    aug_win = np.zeros((2, LANES, tq), np.float32)
    aug_win[0, nb:nb + 2, :] = 1.0
    aug_win[1, nb + 2, :] = NEG_INF
    q_cols = N_HEADS

    def kv_spec(slot):
        return pl.BlockSpec((1, S, HEAD_DIM),
                            lambda b, g, i, slot=slot: (b, 0, q_cols + slot * G + g))

    cmp_spec = pl.BlockSpec((1, 1, LANES, HEAD_DIM), lambda b, g, i: (b, g, 0, 0))
    in_specs = [
        pl.BlockSpec((1, tq, R * HEAD_DIM), lambda b, g, i: (b, i, g)),
        kv_spec(2), kv_spec(3), kv_spec(4), kv_spec(5),
        cmp_spec, cmp_spec,
        pl.BlockSpec((R, tq, LANES), lambda b, g, i: (g, i, 0)),
        pl.BlockSpec((4, R, tq, tq), lambda b, g, i: (0, g, 0, 0)),
        pl.BlockSpec((R, SUBLANES, LANES), lambda b, g, i: (g, 0, 0)),
        pl.BlockSpec((nb, LANES), lambda b, g, i: (0, 0)),
        pl.BlockSpec((n_tiles + 1, LANES, tq), lambda b, g, i: (0, 0, 0)),
        pl.BlockSpec((2, LANES, tq), lambda b, g, i: (0, 0, 0)),
        pl.BlockSpec((1, 1, tq, 3 * R), lambda b, g, i: (b, g, i, 0)),
    ]
    kdim = HEAD_DIM + LANES
    est = (2 * 4 * S * HEAD_DIM * 2 + 2 * 4 * R * tq * tq * 4 + 2 * R * tq * LANES * 4
           + 2 * 2 * S * LANES * 2 + 2 * S * kdim * 2 + R * tq * (kdim * 2 + LANES * 4 + 2 * HEAD_DIM * 4)
           + 3 * R * tq * LANES * 4 + 16 * ATT_ROWS * tq * 4 + 4 * tq * R * HEAD_DIM * 2)
    assert est <= VMEM_LIMIT_CAP
    return pl.pallas_call(
        _nsa_body,
        out_shape=jax.ShapeDtypeStruct((B, S, N_HEADS * HEAD_DIM), BF16),
        grid=(B, G, n_tiles),
        in_specs=in_specs,
        out_specs=pl.BlockSpec((1, tq, R * HEAD_DIM), lambda b, g, i: (b, i, g)),
        scratch_shapes=[pltpu.VMEM((n_tiles + 1, kdim, tq), BF16),
                        pltpu.VMEM((n_tiles + 1, kdim, tq), BF16),
                        pltpu.VMEM((R * tq, kdim), BF16),
                        pltpu.VMEM((R * tq, LANES), F32),
                        pltpu.VMEM((R * tq, 2 * HEAD_DIM), F32)],
        compiler_params=pltpu.CompilerParams(
            dimension_semantics=("arbitrary", "arbitrary", "arbitrary"),
            vmem_limit_bytes=VMEM_LIMIT_CAP),
        name="nsa_attention",
    )(proj, proj, proj, proj, proj, kcmp, vcmp, bias_c, bias_t, bias_far,
      jnp.asarray(overlap_t, BF16), jnp.asarray(aug_sel, BF16), jnp.asarray(aug_win, BF16),
      gates_t)


@functools.lru_cache(maxsize=None)
def _hgrn_masks():
    C = HGRN_CHUNK
    t = np.arange(C)[:, None]
    s = np.arange(C)[None, :]
    masks = [(t // HGRN_SUB == s // HGRN_SUB) & (s <= t)]
    half = C // 2
    while half >= HGRN_SUB:
        grp = 2 * half
        masks.append((t // grp == s // grp) & (t % grp >= half) & (s % grp < half))
        half //= 2
    assert np.array_equal(np.sum(masks, axis=0), (s <= t).astype(int))
    return np.stack(masks).astype(np.float32), (s <= t).astype(np.float32)


def _hgrn_body(q_ref, f_ref, i_ref, g_ref, lb_ref, gn_ref, msk_ref, tril_ref, o_ref,
               st_all, b_all, k_all, q_all, *, layer, heads):
    C = HGRN_CHUNK
    n_chunks = q_ref.shape[1] // C
    nt = (((1,), (1,)), ((), ()))

    lbp = lb_ref[...]
    e = jnp.exp(lbp - jnp.max(lbp, axis=0, keepdims=True))
    sm = e / jnp.sum(e, axis=0, keepdims=True)
    cum = sm[0:1]
    first = cum
    for d in range(1, layer + 1):
        cum = cum + sm[d:d + 1]
    lb_all = cum - first
    log_lb_all = jnp.log(lb_all)
    log_1m_all = jnp.log1p(-lb_all)
    gn_all = gn_ref[...]

    cols = lax.broadcasted_iota(jnp.int32, (1, C), 1)
    tril = tril_ref[...]

    st_all[...] = jnp.zeros(st_all.shape, F32)

    def chunk(c, carry):
        for hh in range(heads):
            one_head(c, hh)
        return carry

    def one_head(c, hh):
        sl = pl.ds(pl.multiple_of(c * C, C), C)
        hs = slice(hh * HEAD_DIM, (hh + 1) * HEAD_DIM)
        lb, log_lb, log_1m, gn = lb_all[:, hs], log_lb_all[:, hs], log_1m_all[:, hs], gn_all[:, hs]
        st_ref, b_sc, k_sc, q_sc = st_all.at[hh], b_all.at[hh], k_all.at[hh], q_all.at[hh]
        qr = q_ref[0, sl, hs].astype(F32)
        x = f_ref[0, sl, hs].astype(F32)
        v = i_ref[0, sl, hs].astype(F32)
        gr = g_ref[0, sl, hs].astype(F32)
        q = qr / (1.0 + jnp.exp2(qr * (-LOG2_E)))
        ex = jnp.exp2(jnp.abs(x) * (-LOG2_E))
        u = 1.0 + ex
        r1 = 1.0 / u
        k = (1.0 - lb) * jnp.where(x >= 0.0, ex * r1, r1)
        c2 = log_1m + (jnp.minimum(x, 0.0) - jnp.log(u))
        e2 = jnp.exp2(jnp.abs(log_lb - c2) * (-LOG2_E))
        log_f = jnp.maximum(log_lb, c2) + jnp.log(1.0 + e2)

        lf_hi = log_f.astype(BF16)
        lf_lo = (log_f - lf_hi.astype(F32)).astype(BF16)
        b = (jnp.dot(tril, lf_hi, preferred_element_type=F32)
             + jnp.dot(tril, lf_lo, preferred_element_type=F32)) * LOG2_E
        b_sc[...] = b
        k_sc[...] = k
        q_sc[...] = q

        vb = v.astype(BF16)
        qb = q.astype(BF16)
        kb = k.astype(BF16)
        st = st_ref[...]
        o = lax.dot_general((q * jnp.exp2(b)).astype(BF16), st.astype(BF16), nt,
                            preferred_element_type=F32)

        pieces = []
        for blk in range(C // HGRN_SUB):
            r0 = blk * HGRN_SUB
            bt = b_sc[r0:r0 + HGRN_SUB, :]
            qt = q_sc[r0:r0 + HGRN_SUB, :]
            arow = jnp.zeros((HGRN_SUB, C), F32)
            for s in range(HGRN_SUB):
                bs = b_sc[r0 + s:r0 + s + 1, :]
                ks = k_sc[r0 + s:r0 + s + 1, :]
                col = jnp.sum(jnp.exp2(bt - bs) * qt * ks, axis=-1, keepdims=True)
                arow = jnp.where(cols == r0 + s, col, arow)
            pieces.append(arow)
        a = jnp.where(msk_ref[0] > 0.5, jnp.concatenate(pieces, axis=0), 0.0)

        half = C // 2
        lvl = 1
        while half >= HGRN_SUB:
            grp = 2 * half
            anc = jnp.concatenate(
                [jnp.broadcast_to(b_sc[g0 + half - 1:g0 + half, :], (grp, b.shape[1]))
                 for g0 in range(0, C, grp)], axis=0)
            e = jnp.exp2(-jnp.abs(b - anc)).astype(BF16)
            al = lax.dot_general(qb * e, kb * e, nt, preferred_element_type=F32)
            a = jnp.where(msk_ref[lvl] > 0.5, al, a)
            half //= 2
            lvl += 1

        o = o + jnp.dot(a.astype(BF16), vb, preferred_element_type=F32)

        b_last = b_sc[C - 1:C, :]
        kh = (k * jnp.exp2(b_last - b)).astype(BF16)
        st_ref[...] = st * jnp.exp2(b_last) + jnp.dot(v.T.astype(BF16), kh,
                                                     preferred_element_type=F32)

        ms = jnp.mean(o * o, axis=-1, keepdims=True)
        o = o * lax.rsqrt(ms + RMS_EPS) * gn * (gr / (1.0 + jnp.exp2(gr * (-LOG2_E))))
        o_ref[0, sl, hs] = o.astype(o_ref.dtype)

    lax.fori_loop(0, n_chunks, chunk, 0)


def _hgrn(proj, hgrn_lb, onorm, layer, B, S):
    H = N_HEADS
    depth = hgrn_lb.shape[0]

    nh = HGRN_HEADS_PER_STEP
    width = nh * HEAD_DIM
    steps = H // nh

    def spec(part):
        return pl.BlockSpec((1, S, width), lambda b, h, part=part: (b, 0, part * steps + h))

    masks, tril = _hgrn_masks()
    C = HGRN_CHUNK
    est = 2 * 4 * S * width * proj.dtype.itemsize + 2 * S * width * 2 + nh * 64 * C * C * 4
    return pl.pallas_call(
        functools.partial(_hgrn_body, layer=layer, heads=nh),
        out_shape=jax.ShapeDtypeStruct((B, S, H * HEAD_DIM), BF16),
        grid=(B, steps),
        in_specs=[spec(0), spec(1), spec(2), spec(3),
                  pl.BlockSpec((depth, width), lambda b, h: (0, h)),
                  pl.BlockSpec((1, width), lambda b, h: (0, h)),
                  pl.BlockSpec(masks.shape, lambda b, h: (0, 0, 0)),
                  pl.BlockSpec((C, C), lambda b, h: (0, 0))],
        out_specs=pl.BlockSpec((1, S, width), lambda b, h: (b, 0, h)),
        scratch_shapes=[pltpu.VMEM((nh, HEAD_DIM, HEAD_DIM), F32),
                        pltpu.VMEM((nh, C, HEAD_DIM), F32),
                        pltpu.VMEM((nh, C, HEAD_DIM), F32),
                        pltpu.VMEM((nh, C, HEAD_DIM), F32)],
        compiler_params=pltpu.CompilerParams(
            dimension_semantics=("arbitrary", "arbitrary"),
            vmem_limit_bytes=_vmem_limit(est)),
        name="hgrn2_recurrence",
    )(proj, proj, proj, proj, hgrn_lb.astype(F32), onorm.reshape(1, -1).astype(F32),
      jnp.asarray(masks), jnp.asarray(tril, BF16))


def _mlp_body(x_ref, gi_ref, wu_ref, wd_ref, go_ref, o_ref, hn_ref, *, nf, tm):
    f = pl.program_id(1)

    def row_chunks(fn):
        def step(r, c):
            fn(pl.ds(pl.multiple_of(r * NORM_ROWS, NORM_ROWS), NORM_ROWS))
            return c

        lax.fori_loop(0, tm // NORM_ROWS, step, 0)

    @pl.when(f == 0)
    def _():
        g = gi_ref[...]

        def norm_in(rows):
            xs = x_ref[rows, :]
            ms = jnp.mean(xs * xs, axis=-1, keepdims=True)
            hn_ref[rows, :] = (xs * lax.rsqrt(ms + RMS_EPS) * g).astype(BF16)

        row_chunks(norm_in)
        o_ref[...] = jnp.zeros(o_ref.shape, F32)

    hid = jnp.dot(hn_ref[...], wu_ref[...], preferred_element_type=F32)
    hid = jnp.square(jnp.maximum(hid, 0.0)).astype(BF16)
    o_ref[...] += jnp.dot(hid, wd_ref[...], preferred_element_type=F32)

    @pl.when(f == nf - 1)
    def _():
        g = go_ref[...]

        def norm_out(rows):
            y = o_ref[rows, :]
            ms = jnp.mean(y * y, axis=-1, keepdims=True)
            o_ref[rows, :] = x_ref[rows, :] + y * lax.rsqrt(ms + RMS_EPS) * g

        row_chunks(norm_out)


def _mlp(xf, g_in, g_out, w_up, w_down, layer, *, tm=512, tf=1024):
    M, D = xf.shape
    F = w_up.shape[2]
    assert M % tm == 0 and F % tf == 0 and tm % NORM_ROWS == 0
    nf = F // tf
    est = (2 * tm * D * 4 + 2 * tm * D * 4 + tm * D * 2 + 2 * 2 * D * tf * 2
           + tm * tf * 6 + 2 * tm * D * 4)
    return pl.pallas_call(
        functools.partial(_mlp_body, nf=nf, tm=tm),
        out_shape=jax.ShapeDtypeStruct((M, D), F32),
        grid=(M // tm, nf),
        in_specs=[pl.BlockSpec((tm, D), lambda i, f: (i, 0)),
                  pl.BlockSpec((1, D), lambda i, f: (0, 0)),
                  pl.BlockSpec((None, D, tf), lambda i, f: (layer, 0, f)),
                  pl.BlockSpec((None, tf, D), lambda i, f: (layer, f, 0)),
                  pl.BlockSpec((1, D), lambda i, f: (0, 0))],
        out_specs=pl.BlockSpec((tm, D), lambda i, f: (i, 0)),
        scratch_shapes=[pltpu.VMEM((tm, D), BF16)],
        compiler_params=pltpu.CompilerParams(
            dimension_semantics=("arbitrary", "arbitrary"),
            vmem_limit_bytes=_vmem_limit(est)),
        name="mlp",
    )(xf, g_in.reshape(1, D).astype(F32), w_up.astype(BF16), w_down.astype(BF16),
      g_out.reshape(1, D).astype(F32))


def _nsa_layer(xf, B, S, g_in, g_out, rel_table, w_in, cmp_pe, cmp_w1, cmp_w2, w_out):
    D = xf.shape[1]
    G, R, Dh = N_GROUPS, HEADS_PER_GROUP, HEAD_DIM
    n_main = N_HEADS * Dh + 6 * G * Dh
    n_gate = 3 * N_HEADS
    w_all = w_in.astype(BF16)
    w_gate = jnp.pad(w_in[:, n_main:], ((0, 0), (0, LANES - n_gate))).astype(BF16)
    colscale = jnp.concatenate([jnp.full((N_HEADS * Dh,), Dh ** -0.5 * LOG2_E, F32),
                                jnp.ones((6 * G * Dh,), F32)])[None]

    proj, glog = _matmul(xf, w_all, tm=1024, tn=1024, norm_g=g_in, epi="colscale",
                         colscale=colscale, side_w=w_gate, n_out=n_main, out_dtype=BF16,
                         name="nsa_proj")
    gates_t = (glog[:, :n_gate].reshape(B, S, 3, G, R).transpose(0, 3, 1, 2, 4)
               .reshape(B, G, S, 3 * R))

    proj3 = proj.reshape(B, S, n_main)
    half = CMP_STRIDE * Dh
    kcmp, vcmp = _compress(proj3, cmp_w1.reshape(2, 2, half, Dh).astype(BF16),
                           cmp_pe.reshape(2, 2, 1, half).astype(F32),
                           cmp_w2.astype(BF16), B, S)

    bias_c, bias_t, bias_far = _bias_tables(rel_table, S)
    attn = _nsa_attention(proj3, kcmp, vcmp, bias_c, bias_t, bias_far, gates_t, B, S)
    return _matmul(attn.reshape(B * S, D), w_out.astype(BF16), tm=512, tn=D,
                   epi="resnorm", res=xf, res_g=g_out, out_dtype=F32, name="nsa_out")


def _hgrn_layer(xf, B, S, layer, g_in, g_out, w_in, hgrn_lb, onorm, w_out):
    D = xf.shape[1]
    proj = _matmul(xf, w_in.astype(BF16), tm=1024, tn=1024, norm_g=g_in,
                   out_dtype=BF16, name="hgrn_proj")
    mixed = _hgrn(proj.reshape(B, S, 4 * D), hgrn_lb, onorm, layer, B, S)
    return _matmul(mixed.reshape(B * S, D), w_out.astype(BF16), tm=512, tn=D,
                   epi="resnorm", res=xf, res_g=g_out, out_dtype=F32, name="hgrn_out")


def kernel(x, norm_g, rel_table, nsa_w_in, nsa_cmp_pe, nsa_cmp_w1, nsa_cmp_w2, nsa_w_out,
           hgrn_w_in, hgrn_lb, hgrn_onorm, hgrn_w_out, mlp_w_up, mlp_w_down):
    B, S, D = x.shape
    depth = norm_g.shape[0]
    assert D == N_HEADS * HEAD_DIM and S % ATT_TILE == 0 and S % HGRN_CHUNK == 0
    xf = x.reshape(B * S, D).astype(F32)
    for layer in range(depth):
        j = layer // 2
        if layer % 2 == 0:
            xf = _nsa_layer(xf, B, S, norm_g[layer, 0], norm_g[layer, 1], rel_table,
                            nsa_w_in[j], nsa_cmp_pe[j], nsa_cmp_w1[j], nsa_cmp_w2[j],
                            nsa_w_out[j])
        else:
            xf = _hgrn_layer(xf, B, S, layer, norm_g[layer, 0], norm_g[layer, 1],
                             hgrn_w_in[j], hgrn_lb, hgrn_onorm[j], hgrn_w_out[j])
        xf = _mlp(xf, norm_g[layer, 2], norm_g[layer, 3], mlp_w_up, mlp_w_down, layer)
    return xf.reshape(B, S, D).astype(x.dtype)
```

```python
import functools
import math

import numpy as np
import jax
import jax.numpy as jnp
from jax import lax
from jax.experimental import pallas as pl
from jax.experimental.pallas import tpu as pltpu

F32 = jnp.float32
BF16 = jnp.bfloat16

N_HEADS = 16
N_GROUPS = 4
HEADS_PER_GROUP = N_HEADS // N_GROUPS
HEAD_DIM = 128
CMP_BLOCK = 32
CMP_STRIDE = 16
SEL_BLOCK = 64
SEL_TOP_N = 8
WINDOW = 512
FORCE_SCORE = 1.0e4
REL_BUCKETS = 32
REL_MAX_DIST = 128
RMS_EPS = 1e-6
NEG_INF = -1.0e30
LOG2_E = math.log2(math.e)

LANES = 128
SUBLANES = 8
VMEM_BYTES_V7X = 64 * 1024 * 1024
VMEM_LIMIT_CAP = VMEM_BYTES_V7X - 8 * 1024 * 1024

ATT_TILE = 256
ATT_ROWS = 128
SEL_TILES_PER_STEP = 4
HGRN_CHUNK = 128
HGRN_SUB = 8
HGRN_HEADS_PER_STEP = 8
NORM_ROWS = 256


VMEM_LIMIT_FLOOR = 32 * 1024 * 1024


def _vmem_limit(nbytes):
    return int(min(VMEM_LIMIT_CAP, max(VMEM_LIMIT_FLOOR, nbytes)))


def _mm_body(*refs, norm, epi, side, tm):
    it = iter(refs)
    x_ref = next(it)
    g_ref = next(it) if norm else None
    w_ref = next(it)
    ws_ref = next(it) if side else None
    cs_ref = next(it) if epi == "colscale" else None
    res_ref = next(it) if epi == "resnorm" else None
    go_ref = next(it) if epi == "resnorm" else None
    o_ref = next(it)
    os_ref = next(it) if side else None
    hn_ref = next(it) if norm else None

    j = pl.program_id(1)

    if norm:
        @pl.when(j == 0)
        def _():
            g = g_ref[...]

            def step(r, c):
                rows = pl.ds(pl.multiple_of(r * NORM_ROWS, NORM_ROWS), NORM_ROWS)
                xs = x_ref[rows, :]
                ms = jnp.mean(xs * xs, axis=-1, keepdims=True)
                hn_ref[rows, :] = (xs * lax.rsqrt(ms + RMS_EPS) * g).astype(BF16)
                return c

            lax.fori_loop(0, tm // NORM_ROWS, step, 0)

        lhs = hn_ref[...]
    else:
        lhs = x_ref[...]

    if side:
        @pl.when(j == 0)
        def _():
            os_ref[...] = jnp.dot(lhs, ws_ref[...], preferred_element_type=F32)

    acc = jnp.dot(lhs, w_ref[...], preferred_element_type=F32)
    if epi == "colscale":
        acc = acc * cs_ref[...]
    elif epi == "resnorm":
        ms = jnp.mean(acc * acc, axis=-1, keepdims=True)
        acc = res_ref[...] + acc * lax.rsqrt(ms + RMS_EPS) * go_ref[...]
    o_ref[...] = acc.astype(o_ref.dtype)


def _matmul(x, w, *, tm, tn, norm_g=None, epi="none", colscale=None, res=None, res_g=None,
            side_w=None, n_out=None, out_dtype=BF16, name="mm"):
    M, K = x.shape
    N = w.shape[1] if n_out is None else n_out
    assert N <= w.shape[1]
    norm = norm_g is not None
    side = side_w is not None
    assert M % tm == 0 and N % tn == 0
    assert not (epi == "resnorm" and tn != N)

    in_specs = [pl.BlockSpec((tm, K), lambda i, j: (i, 0))]
    args = [x]
    if norm:
        in_specs.append(pl.BlockSpec((1, K), lambda i, j: (0, 0)))
        args.append(norm_g.reshape(1, K).astype(F32))
    in_specs.append(pl.BlockSpec((K, tn), lambda i, j: (0, j)))
    args.append(w)
    ns = side_w.shape[1] if side else 0
    if side:
        in_specs.append(pl.BlockSpec((K, ns), lambda i, j: (0, 0)))
        args.append(side_w)
    if epi == "colscale":
        in_specs.append(pl.BlockSpec((1, tn), lambda i, j: (0, j)))
        args.append(colscale)
    if epi == "resnorm":
        in_specs.append(pl.BlockSpec((tm, tn), lambda i, j: (i, j)))
        args.append(res)
        in_specs.append(pl.BlockSpec((1, tn), lambda i, j: (0, j)))
        args.append(res_g.reshape(1, N).astype(F32))

    out_shape = jax.ShapeDtypeStruct((M, N), out_dtype)
    out_specs = pl.BlockSpec((tm, tn), lambda i, j: (i, j))
    if side:
        out_shape = (out_shape, jax.ShapeDtypeStruct((M, ns), F32))
        out_specs = (out_specs, pl.BlockSpec((tm, ns), lambda i, j: (i, 0)))

    xb = x.dtype.itemsize
    ob = jnp.dtype(out_dtype).itemsize
    est = (2 * tm * K * xb + 2 * K * tn * 2 + 2 * tm * tn * ob + (tm * K * 2 if norm else 0)
           + (2 * tm * tn * 4 if epi == "resnorm" else 0) + 3 * tm * tn * 4
           + 2 * K * ns * 2 + 3 * tm * ns * 4)

    return pl.pallas_call(
        functools.partial(_mm_body, norm=norm, epi=epi, side=side, tm=tm),
        out_shape=out_shape,
        grid=(M // tm, N // tn),
        in_specs=in_specs,
        out_specs=out_specs,
        scratch_shapes=[pltpu.VMEM((tm, K), BF16)] if norm else [],
        compiler_params=pltpu.CompilerParams(
            dimension_semantics=("arbitrary", "arbitrary"),
            vmem_limit_bytes=_vmem_limit(est)),
        name=name,
    )(*args)


def _rel_bucket_np(dist):
    n = np.maximum(dist, 0)
    max_exact = REL_BUCKETS // 2
    nf = np.maximum(n, 1).astype(np.float32)
    ratio = np.log(nf / np.float32(max_exact)) / np.float32(math.log(REL_MAX_DIST / max_exact))
    large = max_exact + (ratio * np.float32(REL_BUCKETS - max_exact)).astype(np.int32)
    large = np.minimum(large, REL_BUCKETS - 1)
    return np.where(n < max_exact, n, large).astype(np.int32)


@functools.lru_cache(maxsize=None)
def _static_maps(seq):
    n_cmp = LANES
    pos = np.arange(seq, dtype=np.int32)[:, None]
    c_end = np.arange(n_cmp, dtype=np.int32)[None, :] * CMP_STRIDE + CMP_BLOCK - 1
    bucket_c = _rel_bucket_np(pos - c_end)
    t = np.arange(ATT_TILE, dtype=np.int32)[:, None]
    k = np.arange(ATT_TILE, dtype=np.int32)[None, :]
    bucket_t = np.stack([_rel_bucket_np(t - k), _rel_bucket_np(ATT_TILE + t - k)])
    assert _rel_bucket_np(np.array([ATT_TILE + 1]))[0] == REL_BUCKETS - 1
    nc = seq // CMP_STRIDE - CMP_BLOCK // CMP_STRIDE + 1
    nb = seq // SEL_BLOCK
    c_start = np.arange(nc)[:, None] * CMP_STRIDE
    b_start = np.arange(nb)[None, :] * SEL_BLOCK
    ov = ((c_start <= b_start + SEL_BLOCK - 1) & (c_start + CMP_BLOCK - 1 >= b_start))
    overlap = np.zeros((LANES, LANES), np.float32)
    overlap[:nc, :nb] = ov
    return bucket_c, bucket_t, overlap


def _bias_body(tab_ref, bc_ref, bt_ref, oc_ref, ot_ref, of_ref, *, seq):
    h = pl.program_id(0)

    lane = lax.broadcasted_iota(jnp.int32, (SUBLANES, LANES), 1)
    tab_row = jnp.zeros((SUBLANES, LANES), F32)
    for b in range(REL_BUCKETS):
        tab_row = jnp.where(lane == b, tab_ref[b, h], tab_row)

    def lookup(bmap):
        rows = bmap.shape[0]
        tab = jnp.broadcast_to(tab_row[0:1], (rows, LANES))
        return jnp.concatenate(
            [jnp.take_along_axis(tab, bmap[:, c0:c0 + LANES], axis=1)
             for c0 in range(0, bmap.shape[1], LANES)], axis=1)

    def step(r, c):
        rows = pl.ds(pl.multiple_of(r * ATT_TILE, ATT_TILE), ATT_TILE)
        oc_ref[0, rows, :] = lookup(bc_ref[rows, :]) * LOG2_E
        return c

    lax.fori_loop(0, seq // ATT_TILE, step, 0)

    tt = lax.broadcasted_iota(jnp.int32, (ATT_TILE, ATT_TILE), 0)
    kk = lax.broadcasted_iota(jnp.int32, (ATT_TILE, ATT_TILE), 1)
    far = tab_ref[REL_BUCKETS - 1, h]
    of_ref[0] = jnp.full(of_ref.shape[1:], far * LOG2_E, F32)
    ot_ref[0, 0] = jnp.zeros((ATT_TILE, ATT_TILE), F32)
    ot_ref[1, 0] = (lookup(bt_ref[1]) - far) * LOG2_E
    ot_ref[2, 0] = jnp.where(kk <= tt, (lookup(bt_ref[0]) - far) * LOG2_E, NEG_INF)
    ot_ref[3, 0] = jnp.where(kk > tt, 0.0, NEG_INF)


def _bias_tables(rel_table, seq):
    bucket_c, bucket_t, _ = _static_maps(seq)
    return pl.pallas_call(
        functools.partial(_bias_body, seq=seq),
        out_shape=(jax.ShapeDtypeStruct((N_HEADS, seq, LANES), F32),
                   jax.ShapeDtypeStruct((4, N_HEADS, ATT_TILE, ATT_TILE), F32),
                   jax.ShapeDtypeStruct((N_HEADS, SUBLANES, LANES), F32)),
        grid=(N_HEADS,),
        in_specs=[pl.BlockSpec(memory_space=pltpu.SMEM),
                  pl.BlockSpec((seq, LANES), lambda h: (0, 0)),
                  pl.BlockSpec((2, ATT_TILE, ATT_TILE), lambda h: (0, 0, 0))],
        out_specs=(pl.BlockSpec((1, seq, LANES), lambda h: (h, 0, 0)),
                   pl.BlockSpec((4, 1, ATT_TILE, ATT_TILE), lambda h: (0, h, 0, 0)),
                   pl.BlockSpec((1, SUBLANES, LANES), lambda h: (h, 0, 0))),
        compiler_params=pltpu.CompilerParams(dimension_semantics=("arbitrary",)),
        name="rel_bias",
    )(rel_table.astype(F32), jnp.asarray(bucket_c), jnp.asarray(bucket_t))


def _compress_body(xk_ref, xv_ref, w1_ref, pe_ref, w2_ref, ok_ref, ov_ref, x_sc):
    n_grp = xk_ref.shape[1] // CMP_STRIDE

    def one(x_ref, idx, o_ref):
        x_sc[...] = x_ref[0].astype(F32)
        x = jnp.concatenate([x_sc[pl.ds(t, n_grp, stride=CMP_STRIDE), :]
                             for t in range(CMP_STRIDE)], axis=1)
        a0 = jnp.dot((x + pe_ref[idx, 0]).astype(BF16), w1_ref[idx, 0],
                     preferred_element_type=F32)
        a1 = jnp.dot((x + pe_ref[idx, 1]).astype(BF16), w1_ref[idx, 1],
                     preferred_element_type=F32)
        pre = a0 + pltpu.roll(a1, LANES - 1, 0)
        hid = jax.nn.gelu(pre).astype(BF16)
        o_ref[0, 0] = jnp.dot(hid, w2_ref[idx], preferred_element_type=F32).astype(BF16)

    one(xk_ref, 0, ok_ref)
    one(xv_ref, 1, ov_ref)


def _compress(proj, w1, pe, w2, B, S):
    G = N_GROUPS
    assert S // CMP_STRIDE == LANES
    half = CMP_STRIDE * HEAD_DIM
    q_cols = N_HEADS

    def spec_x(slot):
        return pl.BlockSpec((1, S, HEAD_DIM), lambda b, g, slot=slot: (b, 0, q_cols + slot * G + g))

    spec_o = pl.BlockSpec((1, 1, LANES, HEAD_DIM), lambda b, g: (b, g, 0, 0))
    out = jax.ShapeDtypeStruct((B, G, LANES, HEAD_DIM), BF16)
    return pl.pallas_call(
        _compress_body,
        out_shape=(out, out),
        grid=(B, G),
        in_specs=[spec_x(0), spec_x(1),
                  pl.BlockSpec((2, 2, half, HEAD_DIM), lambda b, g: (0, 0, 0, 0)),
                  pl.BlockSpec((2, 2, 1, half), lambda b, g: (0, 0, 0, 0)),
                  pl.BlockSpec((2, HEAD_DIM, HEAD_DIM), lambda b, g: (0, 0, 0))],
        out_specs=(spec_o, spec_o),
        scratch_shapes=[pltpu.VMEM((S, HEAD_DIM), F32)],
        compiler_params=pltpu.CompilerParams(dimension_semantics=("arbitrary", "arbitrary")),
        name="nsa_compress",
    )(proj, proj, w1, pe, w2)


def _nsa_body(q_ref, ks_ref, vs_ref, kw_ref, vw_ref, kc_ref, vc_ref, bc_ref, bt_ref, far_ref,
              ovl_ref, augs_ref, augw_ref, gl_ref, o_ref, kts_sc, ktw_sc, qa_sc, oc_sc, m_sc,
              acc_sc):
    R = HEADS_PER_GROUP
    tq = ATT_TILE
    qi = pl.program_id(2)
    n_tiles = ks_ref.shape[1] // tq
    nb = ovl_ref.shape[0]
    nt = (((1,), (1,)), ((), ()))

    cidx = lax.broadcasted_iota(jnp.int32, (1, 1, LANES), 2)
    jb = lax.broadcasted_iota(jnp.int32, (nb, 1), 0)
    lane = lax.broadcasted_iota(jnp.int32, (1, LANES), 1)

    def select_tile(t):
        rows = pl.ds(pl.multiple_of(t * tq, tq), tq)
        q = q_ref[0, rows, :]
        q4 = jnp.concatenate([q[:, r * HEAD_DIM:(r + 1) * HEAD_DIM] for r in range(R)], axis=0)
        qa_sc[t, :, :HEAD_DIM] = q4
        pos3 = t * tq + lax.broadcasted_iota(jnp.int32, (1, tq, 1), 1)

        sc = lax.dot_general(q4, kc_ref[0, 0], nt, preferred_element_type=F32)
        sc = sc.reshape(R, tq, LANES) + bc_ref[:, rows, :]
        valid = (cidx * CMP_STRIDE + (CMP_BLOCK - 1) <= pos3) & (cidx < LANES - 1)
        sc = jnp.where(valid, sc, NEG_INF)
        mc = jnp.max(sc, axis=-1, keepdims=True)
        ec = jnp.exp2(sc - mc)
        pc = ec / jnp.sum(ec, axis=-1, keepdims=True)
        pc = jnp.where(pos3 >= CMP_BLOCK - 1, pc, 0.0)
        oc_sc[t] = jnp.dot(pc.reshape(R * tq, LANES).astype(BF16), vc_ref[0, 0],
                           preferred_element_type=F32)

        psum = pc[0]
        for r in range(1, R):
            psum = psum + pc[r]
        p_hi = psum.astype(BF16)
        p_lo = (psum - p_hi.astype(F32)).astype(BF16)
        ovt = ovl_ref[...]
        imp = (lax.dot_general(ovt, p_hi, nt, preferred_element_type=F32)
               + lax.dot_general(ovt, p_lo, nt, preferred_element_type=F32))
        pos_t = t * tq + lax.broadcasted_iota(jnp.int32, (1, tq), 1)
        q_blk = lax.shift_right_logical(pos_t, int(math.log2(SEL_BLOCK)))
        forced = (jb == 0) | (jb == q_blk) | (jb == q_blk - 1)
        future = jb > q_blk
        imp = jnp.where(forced, FORCE_SCORE, jnp.where(future, -1.0, imp))
        cnt = jnp.zeros((nb, tq), F32)
        for i in range(nb):
            row = imp[i:i + 1, :]
            beats = (row > imp) | ((row == imp) & (jb > i))
            cnt = cnt + jnp.where(beats, 1.0, 0.0)
        sel_t = jnp.where(cnt < float(min(SEL_TOP_N, nb)), 1.0, 0.0)
        sel = jnp.concatenate([sel_t, jnp.zeros((LANES - nb, tq), F32)], axis=0).T

        sel_pad = jnp.where(lane < nb, (sel - 1.0) * (-NEG_INF), 0.0)
        for r in range(R):
            far = jnp.broadcast_to(far_ref[r, 0:1, :], (tq, LANES))
            far_hi = far.astype(BF16).astype(F32)
            pad = jnp.where(lane == nb, far_hi, jnp.where(lane == nb + 1, far - far_hi, sel_pad))
            pad = jnp.where(lane == nb + 2, 1.0, pad)
            qa_sc[t, r * tq:(r + 1) * tq, HEAD_DIM:] = pad.astype(BF16)

    @pl.when(qi == 0)
    def _():
        def tr(j, c):
            rows = pl.ds(pl.multiple_of(j * tq, tq), tq)
            kts_sc[j, :HEAD_DIM, :] = ks_ref[0, rows, :].T
            kts_sc[j, HEAD_DIM:, :] = augs_ref[j]
            ktw_sc[j, :HEAD_DIM, :] = kw_ref[0, rows, :].T
            ktw_sc[j, HEAD_DIM:, :] = augw_ref[0]
            return c

        lax.fori_loop(0, n_tiles, tr, 0)
        zeros = jnp.zeros((HEAD_DIM, tq), BF16)
        kts_sc[n_tiles, :HEAD_DIM, :] = zeros
        kts_sc[n_tiles, HEAD_DIM:, :] = augs_ref[n_tiles]
        ktw_sc[n_tiles, :HEAD_DIM, :] = zeros
        ktw_sc[n_tiles, HEAD_DIM:, :] = augw_ref[1]

        def sel_group(gi, c):
            for u in range(SEL_TILES_PER_STEP):
                select_tile(gi * SEL_TILES_PER_STEP + u)
            return c

        lax.fori_loop(0, n_tiles // SEL_TILES_PER_STEP, sel_group, 0)

    q4_sc = qa_sc.at[qi]
    o_cmp = oc_sc[qi]

    ones = jnp.ones((tq, HEAD_DIM), BF16)
    n_chunks = R * tq // ATT_ROWS
    chunks_per_head = tq // ATT_ROWS
    dead = n_tiles

    def keys(kt_sc, tiles):
        return jnp.concatenate([kt_sc[t] for t in tiles], axis=1)

    def values(v_ref, tiles):
        parts = []
        for t in tiles:
            rows = pl.ds(pl.multiple_of(t * tq, tq), tq)
            parts.append(jnp.concatenate([v_ref[0, rows, :], ones], axis=1))
        return jnp.concatenate(parts, axis=0)

    def logits(ci, kt, kinds):
        r, hh = divmod(ci, chunks_per_head)
        rs = slice(ci * ATT_ROWS, (ci + 1) * ATT_ROWS)
        qs = slice(hh * ATT_ROWS, (hh + 1) * ATT_ROWS)
        s = jnp.dot(q4_sc[rs, :], kt, preferred_element_type=F32)
        return s + jnp.concatenate([bt_ref[kd, r, qs, :] for kd in kinds], axis=1)

    def probs(s, m):
        return jnp.concatenate([jnp.exp2(s[:, k0:k0 + LANES] - m)
                                for k0 in range(0, s.shape[1], LANES)], axis=1).astype(BF16)

    m_sc[...] = jnp.full(m_sc.shape, NEG_INF, F32)
    acc_sc[...] = jnp.zeros(acc_sc.shape, F32)

    def pair(pi, c):
        ja = 2 * pi
        jb = ja + 1
        kt = keys(kts_sc, (ja, jnp.where(jb <= qi, jb, dead)))
        vv = values(vs_ref, (ja, jnp.minimum(jb, n_tiles - 1)))
        kinds = (jnp.clip(ja - qi + 2, 0, 2), jnp.clip(jb - qi + 2, 0, 2))
        for ci in range(n_chunks):
            rs = slice(ci * ATT_ROWS, (ci + 1) * ATT_ROWS)
            s = logits(ci, kt, kinds)
            m_old = m_sc[rs]
            m_new = jnp.maximum(m_old, jnp.max(s, axis=-1, keepdims=True))
            alpha = jnp.exp2(m_old - m_new)
            m_sc[rs] = m_new
            pv = jnp.dot(probs(s, m_new), vv, preferred_element_type=F32)
            acc_sc[rs] = acc_sc[rs] * jnp.concatenate([alpha, alpha], axis=1) + pv
        return c

    lax.fori_loop(0, (qi + 2) // 2, pair, 0)
    acc = acc_sc[...]
    o_sel = acc[:, :HEAD_DIM] / acc[:, HEAD_DIM:]

    n_win = WINDOW // tq
    w_tiles = [qi - n_win + t for t in range(n_win + 1)]
    kt = keys(ktw_sc, [jnp.where(t >= 0, t, dead) for t in w_tiles])
    vv = values(vw_ref, [jnp.maximum(t, 0) for t in w_tiles])
    w_kinds = (3,) + (0,) * (n_win - 2) + (1, 2)
    o_parts = []
    for ci in range(n_chunks):
        s = logits(ci, kt, w_kinds)
        pv = jnp.dot(probs(s, jnp.max(s, axis=-1, keepdims=True)), vv,
                     preferred_element_type=F32)
        o_parts.append(pv[:, :HEAD_DIM] / pv[:, HEAD_DIM:])
    o_win = jnp.concatenate(o_parts, axis=0)

    gates = jax.nn.sigmoid(gl_ref[0, 0])
    outs = []
    for r in range(R):
        hs = slice(r * tq, (r + 1) * tq)
        o_r = (gates[:, r:r + 1] * o_cmp[hs]
               + gates[:, R + r:R + r + 1] * o_sel[hs]
               + gates[:, 2 * R + r:2 * R + r + 1] * o_win[hs])
        outs.append(o_r)
    o_ref[0] = jnp.concatenate(outs, axis=1).astype(o_ref.dtype)


def _nsa_attention(proj, kcmp, vcmp, bias_c, bias_t, bias_far, gates_t, B, S):
    assert WINDOW % ATT_TILE == 0 and S % ATT_TILE == 0
    assert S // CMP_STRIDE == LANES and S // SEL_BLOCK <= LANES
    R, G, tq = HEADS_PER_GROUP, N_GROUPS, ATT_TILE
    n_tiles = S // tq
    nb = S // SEL_BLOCK
    _, _, overlap = _static_maps(S)
    overlap_t = np.ascontiguousarray(overlap.T[:nb])
    assert nb + 3 <= LANES and n_tiles % 2 == 0
    blk_of_key = (np.arange(S) // SEL_BLOCK).reshape(n_tiles, 1, tq)
    aug_sel = np.zeros((n_tiles + 1, LANES, tq), np.float32)
    aug_sel[:n_tiles] = np.arange(LANES).reshape(1, LANES, 1) == blk_of_key
    aug_sel[:n_tiles, nb:nb + 2, :] = 1.0
    aug_sel[n_tiles, nb + 2, :] = NEG_INF
    aug_win = np.zeros((2, LANES, tq), np.float32)
    aug_win[0, nb:nb + 2, :] = 1.0
    aug_win[1, nb + 2, :] = NEG_INF
    q_cols = N_HEADS

    def kv_spec(slot):
        return pl.BlockSpec((1, S, HEAD_DIM),
                            lambda b, g, i, slot=slot: (b, 0, q_cols + slot * G + g))

    cmp_spec = pl.BlockSpec((1, 1, LANES, HEAD_DIM), lambda b, g, i: (b, g, 0, 0))
    in_specs = [
        pl.BlockSpec((1, S, R * HEAD_DIM), lambda b, g, i: (b, 0, g)),
        kv_spec(2), kv_spec(3), kv_spec(4), kv_spec(5),
        cmp_spec, cmp_spec,
        pl.BlockSpec((R, S, LANES), lambda b, g, i: (g, 0, 0)),
        pl.BlockSpec((4, R, tq, tq), lambda b, g, i: (0, g, 0, 0)),
        pl.BlockSpec((R, SUBLANES, LANES), lambda b, g, i: (g, 0, 0)),
        pl.BlockSpec((nb, LANES), lambda b, g, i: (0, 0)),
        pl.BlockSpec((n_tiles + 1, LANES, tq), lambda b, g, i: (0, 0, 0)),
        pl.BlockSpec((2, LANES, tq), lambda b, g, i: (0, 0, 0)),
        pl.BlockSpec((1, 1, tq, 3 * R), lambda b, g, i: (b, g, i, 0)),
    ]
    kdim = HEAD_DIM + LANES
    assert n_tiles % SEL_TILES_PER_STEP == 0
    est = (2 * 4 * S * HEAD_DIM * 2 + 2 * 4 * R * tq * tq * 4 + 2 * R * S * LANES * 4
           + 2 * S * R * HEAD_DIM * 2 + 2 * 2 * S * LANES * 2 + 2 * S * kdim * 2
           + R * S * (kdim * 2 + HEAD_DIM * 4) + R * tq * (LANES * 4 + 2 * HEAD_DIM * 4)
           + 3 * R * tq * LANES * 4 + 16 * ATT_ROWS * tq * 4 + 4 * tq * R * HEAD_DIM * 2)
    assert est <= VMEM_LIMIT_CAP
    return pl.pallas_call(
        _nsa_body,
        out_shape=jax.ShapeDtypeStruct((B, S, N_HEADS * HEAD_DIM), BF16),
        grid=(B, G, n_tiles),
        in_specs=in_specs,
        out_specs=pl.BlockSpec((1, tq, R * HEAD_DIM), lambda b, g, i: (b, i, g)),
        scratch_shapes=[pltpu.VMEM((n_tiles + 1, kdim, tq), BF16),
                        pltpu.VMEM((n_tiles + 1, kdim, tq), BF16),
                        pltpu.VMEM((n_tiles, R * tq, kdim), BF16),
                        pltpu.VMEM((n_tiles, R * tq, HEAD_DIM), F32),
                        pltpu.VMEM((R * tq, LANES), F32),
                        pltpu.VMEM((R * tq, 2 * HEAD_DIM), F32)],
        compiler_params=pltpu.CompilerParams(
            dimension_semantics=("arbitrary", "arbitrary", "arbitrary"),
            vmem_limit_bytes=VMEM_LIMIT_CAP),
        name="nsa_attention",
    )(proj, proj, proj, proj, proj, kcmp, vcmp, bias_c, bias_t, bias_far,
      jnp.asarray(overlap_t, BF16), jnp.asarray(aug_sel, BF16), jnp.asarray(aug_win, BF16),
      gates_t)


@functools.lru_cache(maxsize=None)
def _hgrn_masks():
    C = HGRN_CHUNK
    t = np.arange(C)[:, None]
    s = np.arange(C)[None, :]
    masks = [(t // HGRN_SUB == s // HGRN_SUB) & (s <= t)]
    half = C // 2
    while half >= HGRN_SUB:
        grp = 2 * half
        masks.append((t // grp == s // grp) & (t % grp >= half) & (s % grp < half))
        half //= 2
    assert np.array_equal(np.sum(masks, axis=0), (s <= t).astype(int))
    return np.stack(masks).astype(np.float32), (s <= t).astype(np.float32)


def _hgrn_body(q_ref, f_ref, i_ref, g_ref, lb_ref, gn_ref, msk_ref, tril_ref, o_ref,
               st_all, b_all, k_all, q_all, *, layer, heads):
    C = HGRN_CHUNK
    n_chunks = q_ref.shape[1] // C
    nt = (((1,), (1,)), ((), ()))

    lbp = lb_ref[...]
    e = jnp.exp(lbp - jnp.max(lbp, axis=0, keepdims=True))
    sm = e / jnp.sum(e, axis=0, keepdims=True)
    cum = sm[0:1]
    first = cum
    for d in range(1, layer + 1):
        cum = cum + sm[d:d + 1]
    lb_all = cum - first
    log_lb_all = jnp.log(lb_all)
    log_1m_all = jnp.log1p(-lb_all)
    gn_all = gn_ref[...]

    cols = lax.broadcasted_iota(jnp.int32, (1, C), 1)
    tril = tril_ref[...]

    st_all[...] = jnp.zeros(st_all.shape, F32)

    def chunk(c, carry):
        for hh in range(heads):
            one_head(c, hh)
        return carry

    def one_head(c, hh):
        sl = pl.ds(pl.multiple_of(c * C, C), C)
        hs = slice(hh * HEAD_DIM, (hh + 1) * HEAD_DIM)
        lb, log_lb, log_1m, gn = lb_all[:, hs], log_lb_all[:, hs], log_1m_all[:, hs], gn_all[:, hs]
        st_ref, b_sc, k_sc, q_sc = st_all.at[hh], b_all.at[hh], k_all.at[hh], q_all.at[hh]
        qr = q_ref[0, sl, hs].astype(F32)
        x = f_ref[0, sl, hs].astype(F32)
        v = i_ref[0, sl, hs].astype(F32)
        gr = g_ref[0, sl, hs].astype(F32)
        q = qr / (1.0 + jnp.exp2(qr * (-LOG2_E)))
        ex = jnp.exp2(jnp.abs(x) * (-LOG2_E))
        u = 1.0 + ex
        r1 = 1.0 / u
        k = (1.0 - lb) * jnp.where(x >= 0.0, ex * r1, r1)
        c2 = log_1m + (jnp.minimum(x, 0.0) - jnp.log(u))
        e2 = jnp.exp2(jnp.abs(log_lb - c2) * (-LOG2_E))
        log_f = jnp.maximum(log_lb, c2) + jnp.log(1.0 + e2)

        lf_hi = log_f.astype(BF16)
        lf_lo = (log_f - lf_hi.astype(F32)).astype(BF16)
        b = (jnp.dot(tril, lf_hi, preferred_element_type=F32)
             + jnp.dot(tril, lf_lo, preferred_element_type=F32)) * LOG2_E
        b_sc[...] = b
        k_sc[...] = k
        q_sc[...] = q

        vb = v.astype(BF16)
        qb = q.astype(BF16)
        kb = k.astype(BF16)
        st = st_ref[...]
        o = lax.dot_general((q * jnp.exp2(b)).astype(BF16), st.astype(BF16), nt,
                            preferred_element_type=F32)

        pieces = []
        for blk in range(C // HGRN_SUB):
            r0 = blk * HGRN_SUB
            bt = b_sc[r0:r0 + HGRN_SUB, :]
            qt = q_sc[r0:r0 + HGRN_SUB, :]
            arow = jnp.zeros((HGRN_SUB, C), F32)
            for s in range(HGRN_SUB):
                bs = b_sc[r0 + s:r0 + s + 1, :]
                ks = k_sc[r0 + s:r0 + s + 1, :]
                col = jnp.sum(jnp.exp2(bt - bs) * qt * ks, axis=-1, keepdims=True)
                arow = jnp.where(cols == r0 + s, col, arow)
            pieces.append(arow)
        a = jnp.where(msk_ref[0] > 0.5, jnp.concatenate(pieces, axis=0), 0.0)

        half = C // 2
        lvl = 1
        while half >= HGRN_SUB:
            grp = 2 * half
            anc = jnp.concatenate(
                [jnp.broadcast_to(b_sc[g0 + half - 1:g0 + half, :], (grp, b.shape[1]))
                 for g0 in range(0, C, grp)], axis=0)
            e = jnp.exp2(-jnp.abs(b - anc)).astype(BF16)
            al = lax.dot_general(qb * e, kb * e, nt, preferred_element_type=F32)
            a = jnp.where(msk_ref[lvl] > 0.5, al, a)
            half //= 2
            lvl += 1

        o = o + jnp.dot(a.astype(BF16), vb, preferred_element_type=F32)

        b_last = b_sc[C - 1:C, :]
        kh = (k * jnp.exp2(b_last - b)).astype(BF16)
        st_ref[...] = st * jnp.exp2(b_last) + jnp.dot(v.T.astype(BF16), kh,
                                                     preferred_element_type=F32)

        ms = jnp.mean(o * o, axis=-1, keepdims=True)
        o = o * lax.rsqrt(ms + RMS_EPS) * gn * (gr / (1.0 + jnp.exp2(gr * (-LOG2_E))))
        o_ref[0, sl, hs] = o.astype(o_ref.dtype)

    lax.fori_loop(0, n_chunks, chunk, 0)


def _hgrn(proj, hgrn_lb, onorm, layer, B, S):
    H = N_HEADS
    depth = hgrn_lb.shape[0]

    nh = HGRN_HEADS_PER_STEP
    width = nh * HEAD_DIM
    steps = H // nh

    def spec(part):
        return pl.BlockSpec((1, S, width), lambda b, h, part=part: (b, 0, part * steps + h))

    masks, tril = _hgrn_masks()
    C = HGRN_CHUNK
    est = 2 * 4 * S * width * proj.dtype.itemsize + 2 * S * width * 2 + nh * 64 * C * C * 4
    return pl.pallas_call(
        functools.partial(_hgrn_body, layer=layer, heads=nh),
        out_shape=jax.ShapeDtypeStruct((B, S, H * HEAD_DIM), BF16),
        grid=(B, steps),
        in_specs=[spec(0), spec(1), spec(2), spec(3),
                  pl.BlockSpec((depth, width), lambda b, h: (0, h)),
                  pl.BlockSpec((1, width), lambda b, h: (0, h)),
                  pl.BlockSpec(masks.shape, lambda b, h: (0, 0, 0)),
                  pl.BlockSpec((C, C), lambda b, h: (0, 0))],
        out_specs=pl.BlockSpec((1, S, width), lambda b, h: (b, 0, h)),
        scratch_shapes=[pltpu.VMEM((nh, HEAD_DIM, HEAD_DIM), F32),
                        pltpu.VMEM((nh, C, HEAD_DIM), F32),
                        pltpu.VMEM((nh, C, HEAD_DIM), F32),
                        pltpu.VMEM((nh, C, HEAD_DIM), F32)],
        compiler_params=pltpu.CompilerParams(
            dimension_semantics=("arbitrary", "arbitrary"),
            vmem_limit_bytes=_vmem_limit(est)),
        name="hgrn2_recurrence",
    )(proj, proj, proj, proj, hgrn_lb.astype(F32), onorm.reshape(1, -1).astype(F32),
      jnp.asarray(masks), jnp.asarray(tril, BF16))


def _mlp_body(x_ref, gi_ref, wu_ref, wd_ref, go_ref, o_ref, hn_ref, *, nf, tm):
    f = pl.program_id(1)

    def row_chunks(fn):
        def step(r, c):
            fn(pl.ds(pl.multiple_of(r * NORM_ROWS, NORM_ROWS), NORM_ROWS))
            return c

        lax.fori_loop(0, tm // NORM_ROWS, step, 0)

    @pl.when(f == 0)
    def _():
        g = gi_ref[...]

        def norm_in(rows):
            xs = x_ref[rows, :]
            ms = jnp.mean(xs * xs, axis=-1, keepdims=True)
            hn_ref[rows, :] = (xs * lax.rsqrt(ms + RMS_EPS) * g).astype(BF16)

        row_chunks(norm_in)
        o_ref[...] = jnp.zeros(o_ref.shape, F32)

    hid = jnp.dot(hn_ref[...], wu_ref[...], preferred_element_type=F32)
    hid = jnp.square(jnp.maximum(hid, 0.0)).astype(BF16)
    o_ref[...] += jnp.dot(hid, wd_ref[...], preferred_element_type=F32)

    @pl.when(f == nf - 1)
    def _():
        g = go_ref[...]

        def norm_out(rows):
            y = o_ref[rows, :]
            ms = jnp.mean(y * y, axis=-1, keepdims=True)
            o_ref[rows, :] = x_ref[rows, :] + y * lax.rsqrt(ms + RMS_EPS) * g

        row_chunks(norm_out)


def _mlp(xf, g_in, g_out, w_up, w_down, layer, *, tm=512, tf=1024):
    M, D = xf.shape
    F = w_up.shape[2]
    assert M % tm == 0 and F % tf == 0 and tm % NORM_ROWS == 0
    nf = F // tf
    est = (2 * tm * D * 4 + 2 * tm * D * 4 + tm * D * 2 + 2 * 2 * D * tf * 2
           + tm * tf * 6 + 2 * tm * D * 4)
    return pl.pallas_call(
        functools.partial(_mlp_body, nf=nf, tm=tm),
        out_shape=jax.ShapeDtypeStruct((M, D), F32),
        grid=(M // tm, nf),
        in_specs=[pl.BlockSpec((tm, D), lambda i, f: (i, 0)),
                  pl.BlockSpec((1, D), lambda i, f: (0, 0)),
                  pl.BlockSpec((None, D, tf), lambda i, f: (layer, 0, f)),
                  pl.BlockSpec((None, tf, D), lambda i, f: (layer, f, 0)),
                  pl.BlockSpec((1, D), lambda i, f: (0, 0))],
        out_specs=pl.BlockSpec((tm, D), lambda i, f: (i, 0)),
        scratch_shapes=[pltpu.VMEM((tm, D), BF16)],
        compiler_params=pltpu.CompilerParams(
            dimension_semantics=("arbitrary", "arbitrary"),
            vmem_limit_bytes=_vmem_limit(est)),
        name="mlp",
    )(xf, g_in.reshape(1, D).astype(F32), w_up.astype(BF16), w_down.astype(BF16),
      g_out.reshape(1, D).astype(F32))


def _nsa_layer(xf, B, S, g_in, g_out, rel_table, w_in, cmp_pe, cmp_w1, cmp_w2, w_out):
    D = xf.shape[1]
    G, R, Dh = N_GROUPS, HEADS_PER_GROUP, HEAD_DIM
    n_main = N_HEADS * Dh + 6 * G * Dh
    n_gate = 3 * N_HEADS
    w_all = w_in.astype(BF16)
    w_gate = jnp.pad(w_in[:, n_main:], ((0, 0), (0, LANES - n_gate))).astype(BF16)
    colscale = jnp.concatenate([jnp.full((N_HEADS * Dh,), Dh ** -0.5 * LOG2_E, F32),
                                jnp.ones((6 * G * Dh,), F32)])[None]

    proj, glog = _matmul(xf, w_all, tm=1024, tn=1024, norm_g=g_in, epi="colscale",
                         colscale=colscale, side_w=w_gate, n_out=n_main, out_dtype=BF16,
                         name="nsa_proj")
    gates_t = (glog[:, :n_gate].reshape(B, S, 3, G, R).transpose(0, 3, 1, 2, 4)
               .reshape(B, G, S, 3 * R))

    proj3 = proj.reshape(B, S, n_main)
    half = CMP_STRIDE * Dh
    kcmp, vcmp = _compress(proj3, cmp_w1.reshape(2, 2, half, Dh).astype(BF16),
                           cmp_pe.reshape(2, 2, 1, half).astype(F32),
                           cmp_w2.astype(BF16), B, S)

    bias_c, bias_t, bias_far = _bias_tables(rel_table, S)
    attn = _nsa_attention(proj3, kcmp, vcmp, bias_c, bias_t, bias_far, gates_t, B, S)
    return _matmul(attn.reshape(B * S, D), w_out.astype(BF16), tm=512, tn=D,
                   epi="resnorm", res=xf, res_g=g_out, out_dtype=F32, name="nsa_out")


def _hgrn_layer(xf, B, S, layer, g_in, g_out, w_in, hgrn_lb, onorm, w_out):
    D = xf.shape[1]
    proj = _matmul(xf, w_in.astype(BF16), tm=1024, tn=1024, norm_g=g_in,
                   out_dtype=BF16, name="hgrn_proj")
    mixed = _hgrn(proj.reshape(B, S, 4 * D), hgrn_lb, onorm, layer, B, S)
    return _matmul(mixed.reshape(B * S, D), w_out.astype(BF16), tm=512, tn=D,
                   epi="resnorm", res=xf, res_g=g_out, out_dtype=F32, name="hgrn_out")


def kernel(x, norm_g, rel_table, nsa_w_in, nsa_cmp_pe, nsa_cmp_w1, nsa_cmp_w2, nsa_w_out,
           hgrn_w_in, hgrn_lb, hgrn_onorm, hgrn_w_out, mlp_w_up, mlp_w_down):
    B, S, D = x.shape
    depth = norm_g.shape[0]
    assert D == N_HEADS * HEAD_DIM and S % ATT_TILE == 0 and S % HGRN_CHUNK == 0
    xf = x.reshape(B * S, D).astype(F32)
    for layer in range(depth):
        j = layer // 2
        if layer % 2 == 0:
            xf = _nsa_layer(xf, B, S, norm_g[layer, 0], norm_g[layer, 1], rel_table,
                            nsa_w_in[j], nsa_cmp_pe[j], nsa_cmp_w1[j], nsa_cmp_w2[j],
                            nsa_w_out[j])
        else:
            xf = _hgrn_layer(xf, B, S, layer, norm_g[layer, 0], norm_g[layer, 1],
                             hgrn_w_in[j], hgrn_lb, hgrn_onorm[j], hgrn_w_out[j])
        xf = _mlp(xf, norm_g[layer, 2], norm_g[layer, 3], mlp_w_up, mlp_w_down, layer)
    return xf.reshape(B, S, D).astype(x.dtype)
```

```python
import functools
import math

import numpy as np
import jax
import jax.numpy as jnp
from jax import lax
from jax.experimental import pallas as pl
from jax.experimental.pallas import tpu as pltpu

F32 = jnp.float32
BF16 = jnp.bfloat16

N_HEADS = 16
N_GROUPS = 4
HEADS_PER_GROUP = N_HEADS // N_GROUPS
HEAD_DIM = 128
CMP_BLOCK = 32
CMP_STRIDE = 16
SEL_BLOCK = 64
SEL_TOP_N = 8
WINDOW = 512
FORCE_SCORE = 1.0e4
REL_BUCKETS = 32
REL_MAX_DIST = 128
RMS_EPS = 1e-6
NEG_INF = -1.0e30
LOG2_E = math.log2(math.e)

LANES = 128
SUBLANES = 8
VMEM_BYTES_V7X = 64 * 1024 * 1024
VMEM_LIMIT_CAP = VMEM_BYTES_V7X - 8 * 1024 * 1024

ATT_TILE = 256
ATT_ROWS = 128
SEL_TILES_PER_STEP = 4
ATT_TILES_PER_STEP = 2
HGRN_CHUNK = 128
HGRN_SUB = 8
HGRN_HEADS_PER_STEP = 8
NORM_ROWS = 256


VMEM_LIMIT_FLOOR = 32 * 1024 * 1024


def _vmem_limit(nbytes):
    return int(min(VMEM_LIMIT_CAP, max(VMEM_LIMIT_FLOOR, nbytes)))


def _mm_body(*refs, norm, epi, side, tm):
    it = iter(refs)
    x_ref = next(it)
    g_ref = next(it) if norm else None
    w_ref = next(it)
    ws_ref = next(it) if side else None
    cs_ref = next(it) if epi == "colscale" else None
    res_ref = next(it) if epi == "resnorm" else None
    go_ref = next(it) if epi == "resnorm" else None
    o_ref = next(it)
    os_ref = next(it) if side else None
    hn_ref = next(it) if norm else None

    j = pl.program_id(1)

    if norm:
        @pl.when(j == 0)
        def _():
            g = g_ref[...]

            def step(r, c):
                rows = pl.ds(pl.multiple_of(r * NORM_ROWS, NORM_ROWS), NORM_ROWS)
                xs = x_ref[rows, :]
                ms = jnp.mean(xs * xs, axis=-1, keepdims=True)
                hn_ref[rows, :] = (xs * lax.rsqrt(ms + RMS_EPS) * g).astype(BF16)
                return c

            lax.fori_loop(0, tm // NORM_ROWS, step, 0)

        lhs = hn_ref[...]
    else:
        lhs = x_ref[...]

    if side:
        @pl.when(j == 0)
        def _():
            os_ref[...] = jnp.dot(lhs, ws_ref[...], preferred_element_type=F32)

    acc = jnp.dot(lhs, w_ref[...], preferred_element_type=F32)
    if epi == "colscale":
        acc = acc * cs_ref[...]
    elif epi == "resnorm":
        ms = jnp.mean(acc * acc, axis=-1, keepdims=True)
        acc = res_ref[...] + acc * lax.rsqrt(ms + RMS_EPS) * go_ref[...]
    o_ref[...] = acc.astype(o_ref.dtype)


def _matmul(x, w, *, tm, tn, norm_g=None, epi="none", colscale=None, res=None, res_g=None,
            side_w=None, n_out=None, out_dtype=BF16, name="mm"):
    M, K = x.shape
    N = w.shape[1] if n_out is None else n_out
    assert N <= w.shape[1]
    norm = norm_g is not None
    side = side_w is not None
    assert M % tm == 0 and N % tn == 0
    assert not (epi == "resnorm" and tn != N)

    in_specs = [pl.BlockSpec((tm, K), lambda i, j: (i, 0))]
    args = [x]
    if norm:
        in_specs.append(pl.BlockSpec((1, K), lambda i, j: (0, 0)))
        args.append(norm_g.reshape(1, K).astype(F32))
    in_specs.append(pl.BlockSpec((K, tn), lambda i, j: (0, j)))
    args.append(w)
    ns = side_w.shape[1] if side else 0
    if side:
        in_specs.append(pl.BlockSpec((K, ns), lambda i, j: (0, 0)))
        args.append(side_w)
    if epi == "colscale":
        in_specs.append(pl.BlockSpec((1, tn), lambda i, j: (0, j)))
        args.append(colscale)
    if epi == "resnorm":
        in_specs.append(pl.BlockSpec((tm, tn), lambda i, j: (i, j)))
        args.append(res)
        in_specs.append(pl.BlockSpec((1, tn), lambda i, j: (0, j)))
        args.append(res_g.reshape(1, N).astype(F32))

    out_shape = jax.ShapeDtypeStruct((M, N), out_dtype)
    out_specs = pl.BlockSpec((tm, tn), lambda i, j: (i, j))
    if side:
        out_shape = (out_shape, jax.ShapeDtypeStruct((M, ns), F32))
        out_specs = (out_specs, pl.BlockSpec((tm, ns), lambda i, j: (i, 0)))

    xb = x.dtype.itemsize
    ob = jnp.dtype(out_dtype).itemsize
    est = (2 * tm * K * xb + 2 * K * tn * 2 + 2 * tm * tn * ob + (tm * K * 2 if norm else 0)
           + (2 * tm * tn * 4 if epi == "resnorm" else 0) + 3 * tm * tn * 4
           + 2 * K * ns * 2 + 3 * tm * ns * 4)

    return pl.pallas_call(
        functools.partial(_mm_body, norm=norm, epi=epi, side=side, tm=tm),
        out_shape=out_shape,
        grid=(M // tm, N // tn),
        in_specs=in_specs,
        out_specs=out_specs,
        scratch_shapes=[pltpu.VMEM((tm, K), BF16)] if norm else [],
        compiler_params=pltpu.CompilerParams(
            dimension_semantics=("arbitrary", "arbitrary"),
            vmem_limit_bytes=_vmem_limit(est)),
        name=name,
    )(*args)


def _rel_bucket_np(dist):
    n = np.maximum(dist, 0)
    max_exact = REL_BUCKETS // 2
    nf = np.maximum(n, 1).astype(np.float32)
    ratio = np.log(nf / np.float32(max_exact)) / np.float32(math.log(REL_MAX_DIST / max_exact))
    large = max_exact + (ratio * np.float32(REL_BUCKETS - max_exact)).astype(np.int32)
    large = np.minimum(large, REL_BUCKETS - 1)
    return np.where(n < max_exact, n, large).astype(np.int32)


@functools.lru_cache(maxsize=None)
def _static_maps(seq):
    n_cmp = LANES
    pos = np.arange(seq, dtype=np.int32)[:, None]
    c_end = np.arange(n_cmp, dtype=np.int32)[None, :] * CMP_STRIDE + CMP_BLOCK - 1
    bucket_c = _rel_bucket_np(pos - c_end)
    t = np.arange(ATT_TILE, dtype=np.int32)[:, None]
    k = np.arange(ATT_TILE, dtype=np.int32)[None, :]
    bucket_t = np.stack([_rel_bucket_np(t - k), _rel_bucket_np(ATT_TILE + t - k)])
    assert _rel_bucket_np(np.array([ATT_TILE + 1]))[0] == REL_BUCKETS - 1
    nc = seq // CMP_STRIDE - CMP_BLOCK // CMP_STRIDE + 1
    nb = seq // SEL_BLOCK
    c_start = np.arange(nc)[:, None] * CMP_STRIDE
    b_start = np.arange(nb)[None, :] * SEL_BLOCK
    ov = ((c_start <= b_start + SEL_BLOCK - 1) & (c_start + CMP_BLOCK - 1 >= b_start))
    overlap = np.zeros((LANES, LANES), np.float32)
    overlap[:nc, :nb] = ov
    return bucket_c, bucket_t, overlap


def _bias_body(tab_ref, bc_ref, bt_ref, oc_ref, ot_ref, of_ref, *, seq):
    h = pl.program_id(0)

    lane = lax.broadcasted_iota(jnp.int32, (SUBLANES, LANES), 1)
    tab_row = jnp.zeros((SUBLANES, LANES), F32)
    for b in range(REL_BUCKETS):
        tab_row = jnp.where(lane == b, tab_ref[b, h], tab_row)

    def lookup(bmap):
        rows = bmap.shape[0]
        tab = jnp.broadcast_to(tab_row[0:1], (rows, LANES))
        return jnp.concatenate(
            [jnp.take_along_axis(tab, bmap[:, c0:c0 + LANES], axis=1)
             for c0 in range(0, bmap.shape[1], LANES)], axis=1)

    def step(r, c):
        rows = pl.ds(pl.multiple_of(r * ATT_TILE, ATT_TILE), ATT_TILE)
        oc_ref[0, rows, :] = lookup(bc_ref[rows, :]) * LOG2_E
        return c

    lax.fori_loop(0, seq // ATT_TILE, step, 0)

    tt = lax.broadcasted_iota(jnp.int32, (ATT_TILE, ATT_TILE), 0)
    kk = lax.broadcasted_iota(jnp.int32, (ATT_TILE, ATT_TILE), 1)
    far = tab_ref[REL_BUCKETS - 1, h]
    of_ref[0] = jnp.full(of_ref.shape[1:], far * LOG2_E, F32)
    ot_ref[0, 0] = jnp.zeros((ATT_TILE, ATT_TILE), F32)
    ot_ref[1, 0] = (lookup(bt_ref[1]) - far) * LOG2_E
    ot_ref[2, 0] = jnp.where(kk <= tt, (lookup(bt_ref[0]) - far) * LOG2_E, NEG_INF)
    ot_ref[3, 0] = jnp.where(kk > tt, 0.0, NEG_INF)
    ot_ref[4, 0] = jnp.full((ATT_TILE, ATT_TILE), NEG_INF, F32)


def _bias_tables(rel_table, seq):
    bucket_c, bucket_t, _ = _static_maps(seq)
    return pl.pallas_call(
        functools.partial(_bias_body, seq=seq),
        out_shape=(jax.ShapeDtypeStruct((N_HEADS, seq, LANES), F32),
                   jax.ShapeDtypeStruct((5, N_HEADS, ATT_TILE, ATT_TILE), F32),
                   jax.ShapeDtypeStruct((N_HEADS, SUBLANES, LANES), F32)),
        grid=(N_HEADS,),
        in_specs=[pl.BlockSpec(memory_space=pltpu.SMEM),
                  pl.BlockSpec((seq, LANES), lambda h: (0, 0)),
                  pl.BlockSpec((2, ATT_TILE, ATT_TILE), lambda h: (0, 0, 0))],
        out_specs=(pl.BlockSpec((1, seq, LANES), lambda h: (h, 0, 0)),
                   pl.BlockSpec((5, 1, ATT_TILE, ATT_TILE), lambda h: (0, h, 0, 0)),
                   pl.BlockSpec((1, SUBLANES, LANES), lambda h: (h, 0, 0))),
        compiler_params=pltpu.CompilerParams(dimension_semantics=("arbitrary",)),
        name="rel_bias",
    )(rel_table.astype(F32), jnp.asarray(bucket_c), jnp.asarray(bucket_t))


def _compress_body(xk_ref, xv_ref, w1_ref, pe_ref, w2_ref, ok_ref, ov_ref, x_sc):
    n_grp = xk_ref.shape[1] // CMP_STRIDE

    def one(x_ref, idx, o_ref):
        x_sc[...] = x_ref[0].astype(F32)
        x = jnp.concatenate([x_sc[pl.ds(t, n_grp, stride=CMP_STRIDE), :]
                             for t in range(CMP_STRIDE)], axis=1)
        a0 = jnp.dot((x + pe_ref[idx, 0]).astype(BF16), w1_ref[idx, 0],
                     preferred_element_type=F32)
        a1 = jnp.dot((x + pe_ref[idx, 1]).astype(BF16), w1_ref[idx, 1],
                     preferred_element_type=F32)
        pre = a0 + pltpu.roll(a1, LANES - 1, 0)
        hid = jax.nn.gelu(pre).astype(BF16)
        o_ref[0, 0] = jnp.dot(hid, w2_ref[idx], preferred_element_type=F32).astype(BF16)

    one(xk_ref, 0, ok_ref)
    one(xv_ref, 1, ov_ref)


def _compress(proj, w1, pe, w2, B, S):
    G = N_GROUPS
    assert S // CMP_STRIDE == LANES
    half = CMP_STRIDE * HEAD_DIM
    q_cols = N_HEADS

    def spec_x(slot):
        return pl.BlockSpec((1, S, HEAD_DIM), lambda b, g, slot=slot: (b, 0, q_cols + slot * G + g))

    spec_o = pl.BlockSpec((1, 1, LANES, HEAD_DIM), lambda b, g: (b, g, 0, 0))
    out = jax.ShapeDtypeStruct((B, G, LANES, HEAD_DIM), BF16)
    return pl.pallas_call(
        _compress_body,
        out_shape=(out, out),
        grid=(B, G),
        in_specs=[spec_x(0), spec_x(1),
                  pl.BlockSpec((2, 2, half, HEAD_DIM), lambda b, g: (0, 0, 0, 0)),
                  pl.BlockSpec((2, 2, 1, half), lambda b, g: (0, 0, 0, 0)),
                  pl.BlockSpec((2, HEAD_DIM, HEAD_DIM), lambda b, g: (0, 0, 0))],
        out_specs=(spec_o, spec_o),
        scratch_shapes=[pltpu.VMEM((S, HEAD_DIM), F32)],
        compiler_params=pltpu.CompilerParams(dimension_semantics=("arbitrary", "arbitrary")),
        name="nsa_compress",
    )(proj, proj, w1, pe, w2)


def _nsa_body(q_ref, ks_ref, vs_ref, kw_ref, vw_ref, kc_ref, vc_ref, bc_ref, bt_ref, far_ref,
              ovl_ref, augs_ref, augw_ref, gl_ref, o_ref, kts_sc, ktw_sc, qa_sc, oc_sc, m_sc,
              acc_sc):
    R = HEADS_PER_GROUP
    tq = ATT_TILE
    qi = pl.program_id(2)
    n_tiles = ks_ref.shape[1] // tq
    nb = ovl_ref.shape[0]
    nt = (((1,), (1,)), ((), ()))

    cidx = lax.broadcasted_iota(jnp.int32, (1, 1, LANES), 2)
    jb = lax.broadcasted_iota(jnp.int32, (nb, 1), 0)
    lane = lax.broadcasted_iota(jnp.int32, (1, LANES), 1)

    def select_tile(t):
        rows = pl.ds(pl.multiple_of(t * tq, tq), tq)
        q = q_ref[0, rows, :]
        q4 = jnp.concatenate([q[:, r * HEAD_DIM:(r + 1) * HEAD_DIM] for r in range(R)], axis=0)
        qa_sc[t, :, :HEAD_DIM] = q4
        pos3 = t * tq + lax.broadcasted_iota(jnp.int32, (1, tq, 1), 1)

        sc = lax.dot_general(q4, kc_ref[0, 0], nt, preferred_element_type=F32)
        sc = sc.reshape(R, tq, LANES) + bc_ref[:, rows, :]
        valid = (cidx * CMP_STRIDE + (CMP_BLOCK - 1) <= pos3) & (cidx < LANES - 1)
        sc = jnp.where(valid, sc, NEG_INF)
        mc = jnp.max(sc, axis=-1, keepdims=True)
        ec = jnp.exp2(sc - mc)
        pc = ec / jnp.sum(ec, axis=-1, keepdims=True)
        pc = jnp.where(pos3 >= CMP_BLOCK - 1, pc, 0.0)
        oc_sc[t] = jnp.dot(pc.reshape(R * tq, LANES).astype(BF16), vc_ref[0, 0],
                           preferred_element_type=F32)

        psum = pc[0]
        for r in range(1, R):
            psum = psum + pc[r]
        p_hi = psum.astype(BF16)
        p_lo = (psum - p_hi.astype(F32)).astype(BF16)
        ovt = ovl_ref[...]
        imp = (lax.dot_general(ovt, p_hi, nt, preferred_element_type=F32)
               + lax.dot_general(ovt, p_lo, nt, preferred_element_type=F32))
        pos_t = t * tq + lax.broadcasted_iota(jnp.int32, (1, tq), 1)
        q_blk = lax.shift_right_logical(pos_t, int(math.log2(SEL_BLOCK)))
        forced = (jb == 0) | (jb == q_blk) | (jb == q_blk - 1)
        future = jb > q_blk
        imp = jnp.where(forced, FORCE_SCORE, jnp.where(future, -1.0, imp))
        cnt = jnp.zeros((nb, tq), F32)
        for i in range(nb):
            row = imp[i:i + 1, :]
            beats = (row > imp) | ((row == imp) & (jb > i))
            cnt = cnt + jnp.where(beats, 1.0, 0.0)
        sel_t = jnp.where(cnt < float(min(SEL_TOP_N, nb)), 1.0, 0.0)
        sel = jnp.concatenate([sel_t, jnp.zeros((LANES - nb, tq), F32)], axis=0).T

        sel_pad = jnp.where(lane < nb, (sel - 1.0) * (-NEG_INF), 0.0)
        for r in range(R):
            far = jnp.broadcast_to(far_ref[r, 0:1, :], (tq, LANES))
            far_hi = far.astype(BF16).astype(F32)
            pad = jnp.where(lane == nb, far_hi, jnp.where(lane == nb + 1, far - far_hi, sel_pad))
            pad = jnp.where(lane == nb + 2, 1.0, pad)
            qa_sc[t, r * tq:(r + 1) * tq, HEAD_DIM:] = pad.astype(BF16)

    @pl.when(qi == 0)
    def _():
        def tr(j, c):
            rows = pl.ds(pl.multiple_of(j * tq, tq), tq)
            kts_sc[j, :HEAD_DIM, :] = ks_ref[0, rows, :].T
            kts_sc[j, HEAD_DIM:, :] = augs_ref[j]
            ktw_sc[j, :HEAD_DIM, :] = kw_ref[0, rows, :].T
            ktw_sc[j, HEAD_DIM:, :] = augw_ref[0]
            return c

        lax.fori_loop(0, n_tiles, tr, 0)
        zeros = jnp.zeros((HEAD_DIM, tq), BF16)
        kts_sc[n_tiles, :HEAD_DIM, :] = zeros
        kts_sc[n_tiles, HEAD_DIM:, :] = augs_ref[n_tiles]
        ktw_sc[n_tiles, :HEAD_DIM, :] = zeros
        ktw_sc[n_tiles, HEAD_DIM:, :] = augw_ref[1]

        def sel_group(gi, c):
            for u in range(SEL_TILES_PER_STEP):
                select_tile(gi * SEL_TILES_PER_STEP + u)
            return c

        lax.fori_loop(0, n_tiles // SEL_TILES_PER_STEP, sel_group, 0)

    ones = jnp.ones((tq, HEAD_DIM), BF16)
    n_chunks = R * tq // ATT_ROWS
    chunks_per_head = tq // ATT_ROWS
    dead = n_tiles
    step_tiles = [ATT_TILES_PER_STEP * qi + u for u in range(ATT_TILES_PER_STEP)]

    def keys(kt_sc, tiles):
        return jnp.concatenate([kt_sc[t] for t in tiles], axis=1)

    def values(v_ref, tiles):
        parts = []
        for t in tiles:
            rows = pl.ds(pl.multiple_of(t * tq, tq), tq)
            parts.append(jnp.concatenate([v_ref[0, rows, :], ones], axis=1))
        return jnp.concatenate(parts, axis=0)

    def logits(q_sc, ci, kt, kinds):
        r, hh = divmod(ci, chunks_per_head)
        rs = slice(ci * ATT_ROWS, (ci + 1) * ATT_ROWS)
        qs = slice(hh * ATT_ROWS, (hh + 1) * ATT_ROWS)
        s = jnp.dot(q_sc[rs, :], kt, preferred_element_type=F32)
        return s + jnp.concatenate([bt_ref[kd, r, qs, :] for kd in kinds], axis=1)

    def probs(s, m):
        return jnp.concatenate([jnp.exp2(s[:, k0:k0 + LANES] - m)
                                for k0 in range(0, s.shape[1], LANES)], axis=1).astype(BF16)

    m_sc[...] = jnp.full(m_sc.shape, NEG_INF, F32)
    acc_sc[...] = jnp.zeros(acc_sc.shape, F32)

    def kind(j, t):
        return jnp.where(j > t, 4, jnp.clip(j - t + 2, 0, 2))

    def pair(pi, c):
        ja = 2 * pi
        jb = ja + 1
        kt = keys(kts_sc, (ja, jb))
        vv = values(vs_ref, (ja, jb))
        for u, t in enumerate(step_tiles):
            kinds = (kind(ja, t), kind(jb, t))
            for ci in range(n_chunks):
                rs = slice((u * n_chunks + ci) * ATT_ROWS, (u * n_chunks + ci + 1) * ATT_ROWS)
                s = logits(qa_sc.at[t], ci, kt, kinds)
                m_old = m_sc[rs]
                m_new = jnp.maximum(m_old, jnp.max(s, axis=-1, keepdims=True))
                alpha = jnp.exp2(m_old - m_new)
                m_sc[rs] = m_new
                pv = jnp.dot(probs(s, m_new), vv, preferred_element_type=F32)
                acc_sc[rs] = acc_sc[rs] * jnp.concatenate([alpha, alpha], axis=1) + pv
        return c

    assert ATT_TILES_PER_STEP == 2
    lax.fori_loop(0, qi + 1, pair, 0)

    n_win = WINDOW // tq
    w_kinds = (3,) + (0,) * (n_win - 2) + (1, 2)
    gates = jax.nn.sigmoid(gl_ref[0, 0])
    for u, t in enumerate(step_tiles):
        acc = acc_sc[u * R * tq:(u + 1) * R * tq, :]
        o_sel = acc[:, :HEAD_DIM] / acc[:, HEAD_DIM:]

        w_tiles = [t - n_win + w for w in range(n_win + 1)]
        kt = keys(ktw_sc, [jnp.where(w >= 0, w, dead) for w in w_tiles])
        vv = values(vw_ref, [jnp.maximum(w, 0) for w in w_tiles])
        o_parts = []
        for ci in range(n_chunks):
            s = logits(qa_sc.at[t], ci, kt, w_kinds)
            pv = jnp.dot(probs(s, jnp.max(s, axis=-1, keepdims=True)), vv,
                         preferred_element_type=F32)
            o_parts.append(pv[:, :HEAD_DIM] / pv[:, HEAD_DIM:])
        o_win = jnp.concatenate(o_parts, axis=0)

        o_cmp = oc_sc[t]
        g = gates[u * tq:(u + 1) * tq]
        outs = []
        for r in range(R):
            hs = slice(r * tq, (r + 1) * tq)
            outs.append(g[:, r:r + 1] * o_cmp[hs] + g[:, R + r:R + r + 1] * o_sel[hs]
                        + g[:, 2 * R + r:2 * R + r + 1] * o_win[hs])
        o_ref[0, u * tq:(u + 1) * tq, :] = jnp.concatenate(outs, axis=1).astype(o_ref.dtype)


def _nsa_attention(proj, kcmp, vcmp, bias_c, bias_t, bias_far, gates_t, B, S):
    assert WINDOW % ATT_TILE == 0 and S % ATT_TILE == 0
    assert S // CMP_STRIDE == LANES and S // SEL_BLOCK <= LANES
    R, G, tq = HEADS_PER_GROUP, N_GROUPS, ATT_TILE
    n_tiles = S // tq
    ts = ATT_TILES_PER_STEP
    assert n_tiles % ts == 0
    nb = S // SEL_BLOCK
    _, _, overlap = _static_maps(S)
    overlap_t = np.ascontiguousarray(overlap.T[:nb])
    assert nb + 3 <= LANES and n_tiles % 2 == 0
    blk_of_key = (np.arange(S) // SEL_BLOCK).reshape(n_tiles, 1, tq)
    aug_sel = np.zeros((n_tiles + 1, LANES, tq), np.float32)
    aug_sel[:n_tiles] = np.arange(LANES).reshape(1, LANES, 1) == blk_of_key
    aug_sel[:n_tiles, nb:nb + 2, :] = 1.0
    aug_sel[n_tiles, nb + 2, :] = NEG_INF
    aug_win = np.zeros((2, LANES, tq), np.float32)
    aug_win[0, nb:nb + 2, :] = 1.0
    aug_win[1, nb + 2, :] = NEG_INF
    q_cols = N_HEADS

    def kv_spec(slot):
        return pl.BlockSpec((1, S, HEAD_DIM),
                            lambda b, g, i, slot=slot: (b, 0, q_cols + slot * G + g))

    cmp_spec = pl.BlockSpec((1, 1, LANES, HEAD_DIM), lambda b, g, i: (b, g, 0, 0))
    in_specs = [
        pl.BlockSpec((1, S, R * HEAD_DIM), lambda b, g, i: (b, 0, g)),
        kv_spec(2), kv_spec(3), kv_spec(4), kv_spec(5),
        cmp_spec, cmp_spec,
        pl.BlockSpec((R, S, LANES), lambda b, g, i: (g, 0, 0)),
        pl.BlockSpec((5, R, tq, tq), lambda b, g, i: (0, g, 0, 0)),
        pl.BlockSpec((R, SUBLANES, LANES), lambda b, g, i: (g, 0, 0)),
        pl.BlockSpec((nb, LANES), lambda b, g, i: (0, 0)),
        pl.BlockSpec((n_tiles + 1, LANES, tq), lambda b, g, i: (0, 0, 0)),
        pl.BlockSpec((2, LANES, tq), lambda b, g, i: (0, 0, 0)),
        pl.BlockSpec((1, 1, ts * tq, 3 * R), lambda b, g, i: (b, g, i, 0)),
    ]
    kdim = HEAD_DIM + LANES
    assert n_tiles % SEL_TILES_PER_STEP == 0
    est = (2 * 4 * S * HEAD_DIM * 2 + 2 * 5 * R * tq * tq * 4 + 2 * R * S * LANES * 4
           + 2 * S * R * HEAD_DIM * 2 + 2 * 2 * S * LANES * 2 + 2 * S * kdim * 2
           + R * S * (kdim * 2 + HEAD_DIM * 4) + ts * R * tq * (LANES * 4 + 2 * HEAD_DIM * 4)
           + 3 * R * tq * LANES * 4 + 16 * ATT_ROWS * tq * 4 + 4 * tq * R * HEAD_DIM * 2)
    assert est <= VMEM_LIMIT_CAP
    return pl.pallas_call(
        _nsa_body,
        out_shape=jax.ShapeDtypeStruct((B, S, N_HEADS * HEAD_DIM), BF16),
        grid=(B, G, n_tiles // ts),
        in_specs=in_specs,
        out_specs=pl.BlockSpec((1, ts * tq, R * HEAD_DIM), lambda b, g, i: (b, i, g)),
        scratch_shapes=[pltpu.VMEM((n_tiles + 1, kdim, tq), BF16),
                        pltpu.VMEM((n_tiles + 1, kdim, tq), BF16),
                        pltpu.VMEM((n_tiles, R * tq, kdim), BF16),
                        pltpu.VMEM((n_tiles, R * tq, HEAD_DIM), F32),
                        pltpu.VMEM((ts * R * tq, LANES), F32),
                        pltpu.VMEM((ts * R * tq, 2 * HEAD_DIM), F32)],
        compiler_params=pltpu.CompilerParams(
            dimension_semantics=("arbitrary", "arbitrary", "arbitrary"),
            vmem_limit_bytes=VMEM_LIMIT_CAP),
        name="nsa_attention",
    )(proj, proj, proj, proj, proj, kcmp, vcmp, bias_c, bias_t, bias_far,
      jnp.asarray(overlap_t, BF16), jnp.asarray(aug_sel, BF16), jnp.asarray(aug_win, BF16),
      gates_t)


@functools.lru_cache(maxsize=None)
def _hgrn_masks():
    C = HGRN_CHUNK
    t = np.arange(C)[:, None]
    s = np.arange(C)[None, :]
    masks = [(t // HGRN_SUB == s // HGRN_SUB) & (s <= t)]
    half = C // 2
    while half >= HGRN_SUB:
        grp = 2 * half
        masks.append((t // grp == s // grp) & (t % grp >= half) & (s % grp < half))
        half //= 2
    assert np.array_equal(np.sum(masks, axis=0), (s <= t).astype(int))
    return np.stack(masks).astype(np.float32), (s <= t).astype(np.float32)


def _hgrn_body(q_ref, f_ref, i_ref, g_ref, lb_ref, gn_ref, msk_ref, tril_ref, o_ref,
               st_all, b_all, k_all, q_all, *, layer, heads):
    C = HGRN_CHUNK
    n_chunks = q_ref.shape[1] // C
    nt = (((1,), (1,)), ((), ()))

    lbp = lb_ref[...]
    e = jnp.exp(lbp - jnp.max(lbp, axis=0, keepdims=True))
    sm = e / jnp.sum(e, axis=0, keepdims=True)
    cum = sm[0:1]
    first = cum
    for d in range(1, layer + 1):
        cum = cum + sm[d:d + 1]
    lb_all = cum - first
    log_lb_all = jnp.log(lb_all)
    log_1m_all = jnp.log1p(-lb_all)
    gn_all = gn_ref[...]

    cols = lax.broadcasted_iota(jnp.int32, (1, C), 1)
    tril = tril_ref[...]

    st_all[...] = jnp.zeros(st_all.shape, F32)

    def chunk(c, carry):
        for hh in range(heads):
            one_head(c, hh)
        return carry

    def one_head(c, hh):
        sl = pl.ds(pl.multiple_of(c * C, C), C)
        hs = slice(hh * HEAD_DIM, (hh + 1) * HEAD_DIM)
        lb, log_lb, log_1m, gn = lb_all[:, hs], log_lb_all[:, hs], log_1m_all[:, hs], gn_all[:, hs]
        st_ref, b_sc, k_sc, q_sc = st_all.at[hh], b_all.at[hh], k_all.at[hh], q_all.at[hh]
        qr = q_ref[0, sl, hs].astype(F32)
        x = f_ref[0, sl, hs].astype(F32)
        v = i_ref[0, sl, hs].astype(F32)
        gr = g_ref[0, sl, hs].astype(F32)
        q = qr / (1.0 + jnp.exp2(qr * (-LOG2_E)))
        ex = jnp.exp2(jnp.abs(x) * (-LOG2_E))
        u = 1.0 + ex
        r1 = 1.0 / u
        k = (1.0 - lb) * jnp.where(x >= 0.0, ex * r1, r1)
        c2 = log_1m + (jnp.minimum(x, 0.0) - jnp.log(u))
        e2 = jnp.exp2(jnp.abs(log_lb - c2) * (-LOG2_E))
        log_f = jnp.maximum(log_lb, c2) + jnp.log(1.0 + e2)

        lf_hi = log_f.astype(BF16)
        lf_lo = (log_f - lf_hi.astype(F32)).astype(BF16)
        b = (jnp.dot(tril, lf_hi, preferred_element_type=F32)
             + jnp.dot(tril, lf_lo, preferred_element_type=F32)) * LOG2_E
        b_sc[...] = b
        k_sc[...] = k
        q_sc[...] = q

        vb = v.astype(BF16)
        qb = q.astype(BF16)
        kb = k.astype(BF16)
        st = st_ref[...]
        o = lax.dot_general((q * jnp.exp2(b)).astype(BF16), st.astype(BF16), nt,
                            preferred_element_type=F32)

        pieces = []
        for blk in range(C // HGRN_SUB):
            r0 = blk * HGRN_SUB
            bt = b_sc[r0:r0 + HGRN_SUB, :]
            qt = q_sc[r0:r0 + HGRN_SUB, :]
            arow = jnp.zeros((HGRN_SUB, C), F32)
            for s in range(HGRN_SUB):
                bs = b_sc[r0 + s:r0 + s + 1, :]
                ks = k_sc[r0 + s:r0 + s + 1, :]
                col = jnp.sum(jnp.exp2(bt - bs) * qt * ks, axis=-1, keepdims=True)
                arow = jnp.where(cols == r0 + s, col, arow)
            pieces.append(arow)
        a = jnp.where(msk_ref[0] > 0.5, jnp.concatenate(pieces, axis=0), 0.0)

        half = C // 2
        lvl = 1
        while half >= HGRN_SUB:
            grp = 2 * half
            anc = jnp.concatenate(
                [jnp.broadcast_to(b_sc[g0 + half - 1:g0 + half, :], (grp, b.shape[1]))
                 for g0 in range(0, C, grp)], axis=0)
            e = jnp.exp2(-jnp.abs(b - anc)).astype(BF16)
            al = lax.dot_general(qb * e, kb * e, nt, preferred_element_type=F32)
            a = jnp.where(msk_ref[lvl] > 0.5, al, a)
            half //= 2
            lvl += 1

        o = o + jnp.dot(a.astype(BF16), vb, preferred_element_type=F32)

        b_last = b_sc[C - 1:C, :]
        kh = (k * jnp.exp2(b_last - b)).astype(BF16)
        st_ref[...] = st * jnp.exp2(b_last) + jnp.dot(v.T.astype(BF16), kh,
                                                     preferred_element_type=F32)

        ms = jnp.mean(o * o, axis=-1, keepdims=True)
        o = o * lax.rsqrt(ms + RMS_EPS) * gn * (gr / (1.0 + jnp.exp2(gr * (-LOG2_E))))
        o_ref[0, sl, hs] = o.astype(o_ref.dtype)

    lax.fori_loop(0, n_chunks, chunk, 0)


def _hgrn(proj, hgrn_lb, onorm, layer, B, S):
    H = N_HEADS
    depth = hgrn_lb.shape[0]

    nh = HGRN_HEADS_PER_STEP
    width = nh * HEAD_DIM
    steps = H // nh

    def spec(part):
        return pl.BlockSpec((1, S, width), lambda b, h, part=part: (b, 0, part * steps + h))

    masks, tril = _hgrn_masks()
    C = HGRN_CHUNK
    est = 2 * 4 * S * width * proj.dtype.itemsize + 2 * S * width * 2 + nh * 64 * C * C * 4
    return pl.pallas_call(
        functools.partial(_hgrn_body, layer=layer, heads=nh),
        out_shape=jax.ShapeDtypeStruct((B, S, H * HEAD_DIM), BF16),
        grid=(B, steps),
        in_specs=[spec(0), spec(1), spec(2), spec(3),
                  pl.BlockSpec((depth, width), lambda b, h: (0, h)),
                  pl.BlockSpec((1, width), lambda b, h: (0, h)),
                  pl.BlockSpec(masks.shape, lambda b, h: (0, 0, 0)),
                  pl.BlockSpec((C, C), lambda b, h: (0, 0))],
        out_specs=pl.BlockSpec((1, S, width), lambda b, h: (b, 0, h)),
        scratch_shapes=[pltpu.VMEM((nh, HEAD_DIM, HEAD_DIM), F32),
                        pltpu.VMEM((nh, C, HEAD_DIM), F32),
                        pltpu.VMEM((nh, C, HEAD_DIM), F32),
                        pltpu.VMEM((nh, C, HEAD_DIM), F32)],
        compiler_params=pltpu.CompilerParams(
            dimension_semantics=("arbitrary", "arbitrary"),
            vmem_limit_bytes=_vmem_limit(est)),
        name="hgrn2_recurrence",
    )(proj, proj, proj, proj, hgrn_lb.astype(F32), onorm.reshape(1, -1).astype(F32),
      jnp.asarray(masks), jnp.asarray(tril, BF16))


def _mlp_body(x_ref, gi_ref, wu_ref, wd_ref, go_ref, o_ref, hn_ref, *, nf, tm):
    f = pl.program_id(1)

    def row_chunks(fn):
        def step(r, c):
            fn(pl.ds(pl.multiple_of(r * NORM_ROWS, NORM_ROWS), NORM_ROWS))
            return c

        lax.fori_loop(0, tm // NORM_ROWS, step, 0)

    @pl.when(f == 0)
    def _():
        g = gi_ref[...]

        def norm_in(rows):
            xs = x_ref[rows, :]
            ms = jnp.mean(xs * xs, axis=-1, keepdims=True)
            hn_ref[rows, :] = (xs * lax.rsqrt(ms + RMS_EPS) * g).astype(BF16)

        row_chunks(norm_in)
        o_ref[...] = jnp.zeros(o_ref.shape, F32)

    hid = jnp.dot(hn_ref[...], wu_ref[...], preferred_element_type=F32)
    hid = jnp.square(jnp.maximum(hid, 0.0)).astype(BF16)
    o_ref[...] += jnp.dot(hid, wd_ref[...], preferred_element_type=F32)

    @pl.when(f == nf - 1)
    def _():
        g = go_ref[...]

        def norm_out(rows):
            y = o_ref[rows, :]
            ms = jnp.mean(y * y, axis=-1, keepdims=True)
            o_ref[rows, :] = x_ref[rows, :] + y * lax.rsqrt(ms + RMS_EPS) * g

        row_chunks(norm_out)


def _mlp(xf, g_in, g_out, w_up, w_down, layer, *, tm=512, tf=1024):
    M, D = xf.shape
    F = w_up.shape[2]
    assert M % tm == 0 and F % tf == 0 and tm % NORM_ROWS == 0
    nf = F // tf
    est = (2 * tm * D * 4 + 2 * tm * D * 4 + tm * D * 2 + 2 * 2 * D * tf * 2
           + tm * tf * 6 + 2 * tm * D * 4)
    return pl.pallas_call(
        functools.partial(_mlp_body, nf=nf, tm=tm),
        out_shape=jax.ShapeDtypeStruct((M, D), F32),
        grid=(M // tm, nf),
        in_specs=[pl.BlockSpec((tm, D), lambda i, f: (i, 0)),
                  pl.BlockSpec((1, D), lambda i, f: (0, 0)),
                  pl.BlockSpec((None, D, tf), lambda i, f: (layer, 0, f)),
                  pl.BlockSpec((None, tf, D), lambda i, f: (layer, f, 0)),
                  pl.BlockSpec((1, D), lambda i, f: (0, 0))],
        out_specs=pl.BlockSpec((tm, D), lambda i, f: (i, 0)),
        scratch_shapes=[pltpu.VMEM((tm, D), BF16)],
        compiler_params=pltpu.CompilerParams(
            dimension_semantics=("arbitrary", "arbitrary"),
            vmem_limit_bytes=_vmem_limit(est)),
        name="mlp",
    )(xf, g_in.reshape(1, D).astype(F32), w_up.astype(BF16), w_down.astype(BF16),
      g_out.reshape(1, D).astype(F32))


def _nsa_layer(xf, B, S, g_in, g_out, rel_table, w_in, cmp_pe, cmp_w1, cmp_w2, w_out):
    D = xf.shape[1]
    G, R, Dh = N_GROUPS, HEADS_PER_GROUP, HEAD_DIM
    n_main = N_HEADS * Dh + 6 * G * Dh
    n_gate = 3 * N_HEADS
    w_all = w_in.astype(BF16)
    w_gate = jnp.pad(w_in[:, n_main:], ((0, 0), (0, LANES - n_gate))).astype(BF16)
    colscale = jnp.concatenate([jnp.full((N_HEADS * Dh,), Dh ** -0.5 * LOG2_E, F32),
                                jnp.ones((6 * G * Dh,), F32)])[None]

    proj, glog = _matmul(xf, w_all, tm=1024, tn=1024, norm_g=g_in, epi="colscale",
                         colscale=colscale, side_w=w_gate, n_out=n_main, out_dtype=BF16,
                         name="nsa_proj")
    gates_t = (glog[:, :n_gate].reshape(B, S, 3, G, R).transpose(0, 3, 1, 2, 4)
               .reshape(B, G, S, 3 * R))

    proj3 = proj.reshape(B, S, n_main)
    half = CMP_STRIDE * Dh
    kcmp, vcmp = _compress(proj3, cmp_w1.reshape(2, 2, half, Dh).astype(BF16),
                           cmp_pe.reshape(2, 2, 1, half).astype(F32),
                           cmp_w2.astype(BF16), B, S)

    bias_c, bias_t, bias_far = _bias_tables(rel_table, S)
    attn = _nsa_attention(proj3, kcmp, vcmp, bias_c, bias_t, bias_far, gates_t, B, S)
    return _matmul(attn.reshape(B * S, D), w_out.astype(BF16), tm=512, tn=D,
                   epi="resnorm", res=xf, res_g=g_out, out_dtype=F32, name="nsa_out")


def _hgrn_layer(xf, B, S, layer, g_in, g_out, w_in, hgrn_lb, onorm, w_out):
    D = xf.shape[1]
    proj = _matmul(xf, w_in.astype(BF16), tm=1024, tn=1024, norm_g=g_in,
                   out_dtype=BF16, name="hgrn_proj")
    mixed = _hgrn(proj.reshape(B, S, 4 * D), hgrn_lb, onorm, layer, B, S)
    return _matmul(mixed.reshape(B * S, D), w_out.astype(BF16), tm=512, tn=D,
                   epi="resnorm", res=xf, res_g=g_out, out_dtype=F32, name="hgrn_out")


def kernel(x, norm_g, rel_table, nsa_w_in, nsa_cmp_pe, nsa_cmp_w1, nsa_cmp_w2, nsa_w_out,
           hgrn_w_in, hgrn_lb, hgrn_onorm, hgrn_w_out, mlp_w_up, mlp_w_down):
    B, S, D = x.shape
    depth = norm_g.shape[0]
    assert D == N_HEADS * HEAD_DIM and S % ATT_TILE == 0 and S % HGRN_CHUNK == 0
    xf = x.reshape(B * S, D).astype(F32)
    for layer in range(depth):
        j = layer // 2
        if layer % 2 == 0:
            xf = _nsa_layer(xf, B, S, norm_g[layer, 0], norm_g[layer, 1], rel_table,
                            nsa_w_in[j], nsa_cmp_pe[j], nsa_cmp_w1[j], nsa_cmp_w2[j],
                            nsa_w_out[j])
        else:
            xf = _hgrn_layer(xf, B, S, layer, norm_g[layer, 0], norm_g[layer, 1],
                             hgrn_w_in[j], hgrn_lb, hgrn_onorm[j], hgrn_w_out[j])
        xf = _mlp(xf, norm_g[layer, 2], norm_g[layer, 3], mlp_w_up, mlp_w_down, layer)
    return xf.reshape(B, S, D).astype(x.dtype)
```

```python
import functools
import math

import numpy as np
import jax
import jax.numpy as jnp
from jax import lax
from jax.experimental import pallas as pl
from jax.experimental.pallas import tpu as pltpu

F32 = jnp.float32
BF16 = jnp.bfloat16

N_HEADS = 16
N_GROUPS = 4
HEADS_PER_GROUP = N_HEADS // N_GROUPS
HEAD_DIM = 128
CMP_BLOCK = 32
CMP_STRIDE = 16
SEL_BLOCK = 64
SEL_TOP_N = 8
WINDOW = 512
FORCE_SCORE = 1.0e4
REL_BUCKETS = 32
REL_MAX_DIST = 128
RMS_EPS = 1e-6
NEG_INF = -1.0e30
LOG2_E = math.log2(math.e)

LANES = 128
SUBLANES = 8
VMEM_BYTES_V7X = 64 * 1024 * 1024
VMEM_LIMIT_CAP = VMEM_BYTES_V7X - 8 * 1024 * 1024

ATT_TILE = 256
ATT_ROWS = 128
SEL_TILES_PER_STEP = 4
ATT_TILES_PER_STEP = 2
PREV_TILE, DIAG_TILE, EDGE_TILE = 0, 1, 2
N_BIAS_TILES = 3
HGRN_CHUNK = 128
HGRN_SUB = 8
HGRN_HEADS_PER_STEP = 8
NORM_ROWS = 256


VMEM_LIMIT_FLOOR = 32 * 1024 * 1024


def _vmem_limit(nbytes):
    return int(min(VMEM_LIMIT_CAP, max(VMEM_LIMIT_FLOOR, nbytes)))


def _mm_body(*refs, norm, epi, side, tm):
    it = iter(refs)
    x_ref = next(it)
    g_ref = next(it) if norm else None
    w_ref = next(it)
    ws_ref = next(it) if side else None
    cs_ref = next(it) if epi == "colscale" else None
    res_ref = next(it) if epi == "resnorm" else None
    go_ref = next(it) if epi == "resnorm" else None
    o_ref = next(it)
    os_ref = next(it) if side else None
    hn_ref = next(it) if norm else None

    j = pl.program_id(1)

    if norm:
        @pl.when(j == 0)
        def _():
            g = g_ref[...]

            def step(r, c):
                rows = pl.ds(pl.multiple_of(r * NORM_ROWS, NORM_ROWS), NORM_ROWS)
                xs = x_ref[rows, :]
                ms = jnp.mean(xs * xs, axis=-1, keepdims=True)
                hn_ref[rows, :] = (xs * lax.rsqrt(ms + RMS_EPS) * g).astype(BF16)
                return c

            lax.fori_loop(0, tm // NORM_ROWS, step, 0)

        lhs = hn_ref[...]
    else:
        lhs = x_ref[...]

    if side:
        @pl.when(j == 0)
        def _():
            os_ref[...] = jnp.dot(lhs, ws_ref[...], preferred_element_type=F32)

    acc = jnp.dot(lhs, w_ref[...], preferred_element_type=F32)
    if epi == "colscale":
        acc = acc * cs_ref[...]
    elif epi == "resnorm":
        ms = jnp.mean(acc * acc, axis=-1, keepdims=True)
        acc = res_ref[...] + acc * lax.rsqrt(ms + RMS_EPS) * go_ref[...]
    o_ref[...] = acc.astype(o_ref.dtype)


def _matmul(x, w, *, tm, tn, norm_g=None, epi="none", colscale=None, res=None, res_g=None,
            side_w=None, n_out=None, out_dtype=BF16, name="mm"):
    M, K = x.shape
    N = w.shape[1] if n_out is None else n_out
    assert N <= w.shape[1]
    norm = norm_g is not None
    side = side_w is not None
    assert M % tm == 0 and N % tn == 0
    assert not (epi == "resnorm" and tn != N)

    in_specs = [pl.BlockSpec((tm, K), lambda i, j: (i, 0))]
    args = [x]
    if norm:
        in_specs.append(pl.BlockSpec((1, K), lambda i, j: (0, 0)))
        args.append(norm_g.reshape(1, K).astype(F32))
    in_specs.append(pl.BlockSpec((K, tn), lambda i, j: (0, j)))
    args.append(w)
    ns = side_w.shape[1] if side else 0
    if side:
        in_specs.append(pl.BlockSpec((K, ns), lambda i, j: (0, 0)))
        args.append(side_w)
    if epi == "colscale":
        in_specs.append(pl.BlockSpec((1, tn), lambda i, j: (0, j)))
        args.append(colscale)
    if epi == "resnorm":
        in_specs.append(pl.BlockSpec((tm, tn), lambda i, j: (i, j)))
        args.append(res)
        in_specs.append(pl.BlockSpec((1, tn), lambda i, j: (0, j)))
        args.append(res_g.reshape(1, N).astype(F32))

    out_shape = jax.ShapeDtypeStruct((M, N), out_dtype)
    out_specs = pl.BlockSpec((tm, tn), lambda i, j: (i, j))
    if side:
        out_shape = (out_shape, jax.ShapeDtypeStruct((M, ns), F32))
        out_specs = (out_specs, pl.BlockSpec((tm, ns), lambda i, j: (i, 0)))

    xb = x.dtype.itemsize
    ob = jnp.dtype(out_dtype).itemsize
    est = (2 * tm * K * xb + 2 * K * tn * 2 + 2 * tm * tn * ob + (tm * K * 2 if norm else 0)
           + (2 * tm * tn * 4 if epi == "resnorm" else 0) + 3 * tm * tn * 4
           + 2 * K * ns * 2 + 3 * tm * ns * 4)

    return pl.pallas_call(
        functools.partial(_mm_body, norm=norm, epi=epi, side=side, tm=tm),
        out_shape=out_shape,
        grid=(M // tm, N // tn),
        in_specs=in_specs,
        out_specs=out_specs,
        scratch_shapes=[pltpu.VMEM((tm, K), BF16)] if norm else [],
        compiler_params=pltpu.CompilerParams(
            dimension_semantics=("arbitrary", "arbitrary"),
            vmem_limit_bytes=_vmem_limit(est)),
        name=name,
    )(*args)


def _rel_bucket_np(dist):
    n = np.maximum(dist, 0)
    max_exact = REL_BUCKETS // 2
    nf = np.maximum(n, 1).astype(np.float32)
    ratio = np.log(nf / np.float32(max_exact)) / np.float32(math.log(REL_MAX_DIST / max_exact))
    large = max_exact + (ratio * np.float32(REL_BUCKETS - max_exact)).astype(np.int32)
    large = np.minimum(large, REL_BUCKETS - 1)
    return np.where(n < max_exact, n, large).astype(np.int32)


@functools.lru_cache(maxsize=None)
def _static_maps(seq):
    n_cmp = LANES
    pos = np.arange(seq, dtype=np.int32)[:, None]
    c_end = np.arange(n_cmp, dtype=np.int32)[None, :] * CMP_STRIDE + CMP_BLOCK - 1
    bucket_c = _rel_bucket_np(pos - c_end)
    t = np.arange(ATT_TILE, dtype=np.int32)[:, None]
    k = np.arange(ATT_TILE, dtype=np.int32)[None, :]
    bucket_t = np.stack([_rel_bucket_np(t - k), _rel_bucket_np(ATT_TILE + t - k)])
    assert _rel_bucket_np(np.array([ATT_TILE + 1]))[0] == REL_BUCKETS - 1
    nc = seq // CMP_STRIDE - CMP_BLOCK // CMP_STRIDE + 1
    nb = seq // SEL_BLOCK
    c_start = np.arange(nc)[:, None] * CMP_STRIDE
    b_start = np.arange(nb)[None, :] * SEL_BLOCK
    ov = ((c_start <= b_start + SEL_BLOCK - 1) & (c_start + CMP_BLOCK - 1 >= b_start))
    overlap = np.zeros((LANES, LANES), np.float32)
    overlap[:nc, :nb] = ov
    return bucket_c, bucket_t, overlap


def _bias_body(tab_ref, bc_ref, bt_ref, oc_ref, ot_ref, of_ref, *, seq):
    h = pl.program_id(0)

    lane = lax.broadcasted_iota(jnp.int32, (SUBLANES, LANES), 1)
    tab_row = jnp.zeros((SUBLANES, LANES), F32)
    for b in range(REL_BUCKETS):
        tab_row = jnp.where(lane == b, tab_ref[b, h], tab_row)

    def lookup(bmap):
        rows = bmap.shape[0]
        tab = jnp.broadcast_to(tab_row[0:1], (rows, LANES))
        return jnp.concatenate(
            [jnp.take_along_axis(tab, bmap[:, c0:c0 + LANES], axis=1)
             for c0 in range(0, bmap.shape[1], LANES)], axis=1)

    def step(r, c):
        rows = pl.ds(pl.multiple_of(r * ATT_TILE, ATT_TILE), ATT_TILE)
        oc_ref[0, rows, :] = lookup(bc_ref[rows, :]) * LOG2_E
        return c

    lax.fori_loop(0, seq // ATT_TILE, step, 0)

    tt = lax.broadcasted_iota(jnp.int32, (ATT_TILE, ATT_TILE), 0)
    kk = lax.broadcasted_iota(jnp.int32, (ATT_TILE, ATT_TILE), 1)
    far = tab_ref[REL_BUCKETS - 1, h]
    of_ref[0] = jnp.full(of_ref.shape[1:], far * LOG2_E, F32)
    ot_ref[PREV_TILE, 0] = (lookup(bt_ref[1]) - far) * LOG2_E
    ot_ref[DIAG_TILE, 0] = jnp.where(kk <= tt, (lookup(bt_ref[0]) - far) * LOG2_E, NEG_INF)
    ot_ref[EDGE_TILE, 0] = jnp.where(kk > tt, 0.0, NEG_INF)


def _bias_tables(rel_table, seq):
    bucket_c, bucket_t, _ = _static_maps(seq)
    return pl.pallas_call(
        functools.partial(_bias_body, seq=seq),
        out_shape=(jax.ShapeDtypeStruct((N_HEADS, seq, LANES), F32),
                   jax.ShapeDtypeStruct((N_BIAS_TILES, N_HEADS, ATT_TILE, ATT_TILE), F32),
                   jax.ShapeDtypeStruct((N_HEADS, SUBLANES, LANES), F32)),
        grid=(N_HEADS,),
        in_specs=[pl.BlockSpec(memory_space=pltpu.SMEM),
                  pl.BlockSpec((seq, LANES), lambda h: (0, 0)),
                  pl.BlockSpec((2, ATT_TILE, ATT_TILE), lambda h: (0, 0, 0))],
        out_specs=(pl.BlockSpec((1, seq, LANES), lambda h: (h, 0, 0)),
                   pl.BlockSpec((N_BIAS_TILES, 1, ATT_TILE, ATT_TILE), lambda h: (0, h, 0, 0)),
                   pl.BlockSpec((1, SUBLANES, LANES), lambda h: (h, 0, 0))),
        compiler_params=pltpu.CompilerParams(dimension_semantics=("arbitrary",)),
        name="rel_bias",
    )(rel_table.astype(F32), jnp.asarray(bucket_c), jnp.asarray(bucket_t))


def _compress_body(xk_ref, xv_ref, w1_ref, pe_ref, w2_ref, ok_ref, ov_ref, x_sc):
    n_grp = xk_ref.shape[1] // CMP_STRIDE

    def one(x_ref, idx, o_ref):
        x_sc[...] = x_ref[0].astype(F32)
        x = jnp.concatenate([x_sc[pl.ds(t, n_grp, stride=CMP_STRIDE), :]
                             for t in range(CMP_STRIDE)], axis=1)
        a0 = jnp.dot((x + pe_ref[idx, 0]).astype(BF16), w1_ref[idx, 0],
                     preferred_element_type=F32)
        a1 = jnp.dot((x + pe_ref[idx, 1]).astype(BF16), w1_ref[idx, 1],
                     preferred_element_type=F32)
        pre = a0 + pltpu.roll(a1, LANES - 1, 0)
        hid = jax.nn.gelu(pre).astype(BF16)
        o_ref[0, 0] = jnp.dot(hid, w2_ref[idx], preferred_element_type=F32).astype(BF16)

    one(xk_ref, 0, ok_ref)
    one(xv_ref, 1, ov_ref)


def _compress(proj, w1, pe, w2, B, S):
    G = N_GROUPS
    assert S // CMP_STRIDE == LANES
    half = CMP_STRIDE * HEAD_DIM
    q_cols = N_HEADS

    def spec_x(slot):
        return pl.BlockSpec((1, S, HEAD_DIM), lambda b, g, slot=slot: (b, 0, q_cols + slot * G + g))

    spec_o = pl.BlockSpec((1, 1, LANES, HEAD_DIM), lambda b, g: (b, g, 0, 0))
    out = jax.ShapeDtypeStruct((B, G, LANES, HEAD_DIM), BF16)
    return pl.pallas_call(
        _compress_body,
        out_shape=(out, out),
        grid=(B, G),
        in_specs=[spec_x(0), spec_x(1),
                  pl.BlockSpec((2, 2, half, HEAD_DIM), lambda b, g: (0, 0, 0, 0)),
                  pl.BlockSpec((2, 2, 1, half), lambda b, g: (0, 0, 0, 0)),
                  pl.BlockSpec((2, HEAD_DIM, HEAD_DIM), lambda b, g: (0, 0, 0))],
        out_specs=(spec_o, spec_o),
        scratch_shapes=[pltpu.VMEM((S, HEAD_DIM), F32)],
        compiler_params=pltpu.CompilerParams(dimension_semantics=("arbitrary", "arbitrary")),
        name="nsa_compress",
    )(proj, proj, w1, pe, w2)


def _nsa_body(q_ref, ks_ref, vs_ref, kw_ref, vw_ref, kc_ref, vc_ref, bc_ref, bt_ref, far_ref,
              ovl_ref, augs_ref, augw_ref, gl_ref, o_ref, kts_sc, ktw_sc, qa_sc, oc_sc, m_sc,
              acc_sc):
    R = HEADS_PER_GROUP
    tq = ATT_TILE
    qi = pl.program_id(2)
    n_tiles = ks_ref.shape[1] // tq
    nb = ovl_ref.shape[0]
    nt = (((1,), (1,)), ((), ()))

    cidx = lax.broadcasted_iota(jnp.int32, (1, 1, LANES), 2)
    jb = lax.broadcasted_iota(jnp.int32, (nb, 1), 0)
    lane = lax.broadcasted_iota(jnp.int32, (1, LANES), 1)

    def select_tile(t):
        rows = pl.ds(pl.multiple_of(t * tq, tq), tq)
        q = q_ref[0, rows, :]
        q4 = jnp.concatenate([q[:, r * HEAD_DIM:(r + 1) * HEAD_DIM] for r in range(R)], axis=0)
        qa_sc[t, :, :HEAD_DIM] = q4
        pos3 = t * tq + lax.broadcasted_iota(jnp.int32, (1, tq, 1), 1)

        sc = lax.dot_general(q4, kc_ref[0, 0], nt, preferred_element_type=F32)
        sc = sc.reshape(R, tq, LANES) + bc_ref[:, rows, :]
        valid = (cidx * CMP_STRIDE + (CMP_BLOCK - 1) <= pos3) & (cidx < LANES - 1)
        sc = jnp.where(valid, sc, NEG_INF)
        mc = jnp.max(sc, axis=-1, keepdims=True)
        ec = jnp.exp2(sc - mc)
        pc = ec / jnp.sum(ec, axis=-1, keepdims=True)
        pc = jnp.where(pos3 >= CMP_BLOCK - 1, pc, 0.0)
        oc_sc[t] = jnp.dot(pc.reshape(R * tq, LANES).astype(BF16), vc_ref[0, 0],
                           preferred_element_type=F32)

        psum = pc[0]
        for r in range(1, R):
            psum = psum + pc[r]
        p_hi = psum.astype(BF16)
        p_lo = (psum - p_hi.astype(F32)).astype(BF16)
        ovt = ovl_ref[...]
        imp = (lax.dot_general(ovt, p_hi, nt, preferred_element_type=F32)
               + lax.dot_general(ovt, p_lo, nt, preferred_element_type=F32))
        pos_t = t * tq + lax.broadcasted_iota(jnp.int32, (1, tq), 1)
        q_blk = lax.shift_right_logical(pos_t, int(math.log2(SEL_BLOCK)))
        forced = (jb == 0) | (jb == q_blk) | (jb == q_blk - 1)
        future = jb > q_blk
        imp = jnp.where(forced, FORCE_SCORE, jnp.where(future, -1.0, imp))
        cnt = jnp.zeros((nb, tq), F32)
        for i in range(nb):
            row = imp[i:i + 1, :]
            beats = (row > imp) | ((row == imp) & (jb > i))
            cnt = cnt + jnp.where(beats, 1.0, 0.0)
        sel_t = jnp.where(cnt < float(min(SEL_TOP_N, nb)), 1.0, 0.0)
        sel = jnp.concatenate([sel_t, jnp.zeros((LANES - nb, tq), F32)], axis=0).T

        sel_pad = jnp.where(lane < nb, (sel - 1.0) * (-NEG_INF), 0.0)
        for r in range(R):
            far = jnp.broadcast_to(far_ref[r, 0:1, :], (tq, LANES))
            far_hi = far.astype(BF16).astype(F32)
            pad = jnp.where(lane == nb, far_hi, jnp.where(lane == nb + 1, far - far_hi, sel_pad))
            pad = jnp.where(lane == nb + 2, 1.0, pad)
            qa_sc[t, r * tq:(r + 1) * tq, HEAD_DIM:] = pad.astype(BF16)

    @pl.when(qi == 0)
    def _():
        def tr(j, c):
            rows = pl.ds(pl.multiple_of(j * tq, tq), tq)
            kts_sc[j, :HEAD_DIM, :] = ks_ref[0, rows, :].T
            kts_sc[j, HEAD_DIM:, :] = augs_ref[j]
            ktw_sc[j, :HEAD_DIM, :] = kw_ref[0, rows, :].T
            ktw_sc[j, HEAD_DIM:, :] = augw_ref[0]
            return c

        lax.fori_loop(0, n_tiles, tr, 0)
        zeros = jnp.zeros((HEAD_DIM, tq), BF16)
        kts_sc[n_tiles, :HEAD_DIM, :] = zeros
        kts_sc[n_tiles, HEAD_DIM:, :] = augs_ref[n_tiles]
        ktw_sc[n_tiles, :HEAD_DIM, :] = zeros
        ktw_sc[n_tiles, HEAD_DIM:, :] = augw_ref[1]

        def sel_group(gi, c):
            for u in range(SEL_TILES_PER_STEP):
                select_tile(gi * SEL_TILES_PER_STEP + u)
            return c

        lax.fori_loop(0, n_tiles // SEL_TILES_PER_STEP, sel_group, 0)

    ones = jnp.ones((tq, HEAD_DIM), BF16)
    n_chunks = R * tq // ATT_ROWS
    chunks_per_head = tq // ATT_ROWS
    dead = n_tiles
    step_tiles = [ATT_TILES_PER_STEP * qi + u for u in range(ATT_TILES_PER_STEP)]

    def keys(kt_sc, tiles):
        return jnp.concatenate([kt_sc[t] for t in tiles], axis=1)

    def values(v_ref, tiles):
        parts = []
        for t in tiles:
            rows = pl.ds(pl.multiple_of(t * tq, tq), tq)
            parts.append(jnp.concatenate([v_ref[0, rows, :], ones], axis=1))
        return jnp.concatenate(parts, axis=0)

    def logits(q_sc, ci, kt, kinds):
        r, hh = divmod(ci, chunks_per_head)
        rs = slice(ci * ATT_ROWS, (ci + 1) * ATT_ROWS)
        qs = slice(hh * ATT_ROWS, (hh + 1) * ATT_ROWS)
        s = jnp.dot(q_sc[rs, :], kt, preferred_element_type=F32)
        if all(kd is None for kd in kinds):
            return s
        parts = [s[:, i * tq:(i + 1) * tq] for i in range(len(kinds))]
        return jnp.concatenate([p if kd is None else p + bt_ref[kd, r, qs, :]
                                for p, kd in zip(parts, kinds)], axis=1)

    def probs(s, m):
        return jnp.concatenate([jnp.exp2(s[:, k0:k0 + LANES] - m)
                                for k0 in range(0, s.shape[1], LANES)], axis=1).astype(BF16)

    m_sc[...] = jnp.full(m_sc.shape, NEG_INF, F32)
    acc_sc[...] = jnp.zeros(acc_sc.shape, F32)

    def sel_update(u, t, kt, vv, kinds):
        for ci in range(n_chunks):
            rs = slice((u * n_chunks + ci) * ATT_ROWS, (u * n_chunks + ci + 1) * ATT_ROWS)
            s = logits(qa_sc.at[t], ci, kt, kinds)
            m_old = m_sc[rs]
            m_new = jnp.maximum(m_old, jnp.max(s, axis=-1, keepdims=True))
            alpha = jnp.exp2(m_old - m_new)
            m_sc[rs] = m_new
            pv = jnp.dot(probs(s, m_new), vv, preferred_element_type=F32)
            acc_sc[rs] = acc_sc[rs] * jnp.concatenate([alpha, alpha], axis=1) + pv

    assert ATT_TILES_PER_STEP == 2
    ta, tb = step_tiles

    def far_pair(pi, c):
        pair_tiles = (2 * pi, 2 * pi + 1)
        kt, vv = keys(kts_sc, pair_tiles), values(vs_ref, pair_tiles)
        sel_update(0, ta, kt, vv, (None, None))
        sel_update(1, tb, kt, vv, (None, None))
        return c

    lax.fori_loop(0, jnp.maximum(qi - 1, 0), far_pair, 0)

    @pl.when(qi >= 1)
    def _():
        pair_tiles = (ta - 2, ta - 1)
        kt, vv = keys(kts_sc, pair_tiles), values(vs_ref, pair_tiles)
        sel_update(0, ta, kt, vv, (None, PREV_TILE))
        sel_update(1, tb, kt, vv, (None, None))

    sel_update(0, ta, kts_sc[ta], values(vs_ref, (ta,)), (DIAG_TILE,))
    sel_update(1, tb, keys(kts_sc, (ta, tb)), values(vs_ref, (ta, tb)), (PREV_TILE, DIAG_TILE))

    n_win = WINDOW // tq
    w_kinds = (EDGE_TILE,) + (None,) * (n_win - 2) + (PREV_TILE, DIAG_TILE)
    gates = jax.nn.sigmoid(gl_ref[0, 0])
    for u, t in enumerate(step_tiles):
        acc = acc_sc[u * R * tq:(u + 1) * R * tq, :]
        o_sel = acc[:, :HEAD_DIM] / acc[:, HEAD_DIM:]

        w_tiles = [t - n_win + w for w in range(n_win + 1)]
        kt = keys(ktw_sc, [jnp.where(w >= 0, w, dead) for w in w_tiles])
        vv = values(vw_ref, [jnp.maximum(w, 0) for w in w_tiles])
        o_parts = []
        for ci in range(n_chunks):
            s = logits(qa_sc.at[t], ci, kt, w_kinds)
            pv = jnp.dot(probs(s, jnp.max(s, axis=-1, keepdims=True)), vv,
                         preferred_element_type=F32)
            o_parts.append(pv[:, :HEAD_DIM] / pv[:, HEAD_DIM:])
        o_win = jnp.concatenate(o_parts, axis=0)

        o_cmp = oc_sc[t]
        g = gates[u * tq:(u + 1) * tq]
        outs = []
        for r in range(R):
            hs = slice(r * tq, (r + 1) * tq)
            outs.append(g[:, r:r + 1] * o_cmp[hs] + g[:, R + r:R + r + 1] * o_sel[hs]
                        + g[:, 2 * R + r:2 * R + r + 1] * o_win[hs])
        o_ref[0, u * tq:(u + 1) * tq, :] = jnp.concatenate(outs, axis=1).astype(o_ref.dtype)


def _nsa_attention(proj, kcmp, vcmp, bias_c, bias_t, bias_far, gates_t, B, S):
    assert WINDOW % ATT_TILE == 0 and S % ATT_TILE == 0
    assert S // CMP_STRIDE == LANES and S // SEL_BLOCK <= LANES
    R, G, tq = HEADS_PER_GROUP, N_GROUPS, ATT_TILE
    n_tiles = S // tq
    ts = ATT_TILES_PER_STEP
    assert n_tiles % ts == 0
    nb = S // SEL_BLOCK
    _, _, overlap = _static_maps(S)
    overlap_t = np.ascontiguousarray(overlap.T[:nb])
    assert nb + 3 <= LANES and n_tiles % 2 == 0
    blk_of_key = (np.arange(S) // SEL_BLOCK).reshape(n_tiles, 1, tq)
    aug_sel = np.zeros((n_tiles + 1, LANES, tq), np.float32)
    aug_sel[:n_tiles] = np.arange(LANES).reshape(1, LANES, 1) == blk_of_key
    aug_sel[:n_tiles, nb:nb + 2, :] = 1.0
    aug_sel[n_tiles, nb + 2, :] = NEG_INF
    aug_win = np.zeros((2, LANES, tq), np.float32)
    aug_win[0, nb:nb + 2, :] = 1.0
    aug_win[1, nb + 2, :] = NEG_INF
    q_cols = N_HEADS

    def kv_spec(slot):
        return pl.BlockSpec((1, S, HEAD_DIM),
                            lambda b, g, i, slot=slot: (b, 0, q_cols + slot * G + g))

    cmp_spec = pl.BlockSpec((1, 1, LANES, HEAD_DIM), lambda b, g, i: (b, g, 0, 0))
    in_specs = [
        pl.BlockSpec((1, S, R * HEAD_DIM), lambda b, g, i: (b, 0, g)),
        kv_spec(2), kv_spec(3), kv_spec(4), kv_spec(5),
        cmp_spec, cmp_spec,
        pl.BlockSpec((R, S, LANES), lambda b, g, i: (g, 0, 0)),
        pl.BlockSpec((N_BIAS_TILES, R, tq, tq), lambda b, g, i: (0, g, 0, 0)),
        pl.BlockSpec((R, SUBLANES, LANES), lambda b, g, i: (g, 0, 0)),
        pl.BlockSpec((nb, LANES), lambda b, g, i: (0, 0)),
        pl.BlockSpec((n_tiles + 1, LANES, tq), lambda b, g, i: (0, 0, 0)),
        pl.BlockSpec((2, LANES, tq), lambda b, g, i: (0, 0, 0)),
        pl.BlockSpec((1, 1, ts * tq, 3 * R), lambda b, g, i: (b, g, i, 0)),
    ]
    kdim = HEAD_DIM + LANES
    assert n_tiles % SEL_TILES_PER_STEP == 0
    est = (2 * 4 * S * HEAD_DIM * 2 + 2 * N_BIAS_TILES * R * tq * tq * 4 + 2 * R * S * LANES * 4
           + 2 * S * R * HEAD_DIM * 2 + 2 * 2 * S * LANES * 2 + 2 * S * kdim * 2
           + R * S * (kdim * 2 + HEAD_DIM * 4) + ts * R * tq * (LANES * 4 + 2 * HEAD_DIM * 4)
           + 3 * R * tq * LANES * 4 + 16 * ATT_ROWS * tq * 4 + 4 * tq * R * HEAD_DIM * 2)
    assert est <= VMEM_LIMIT_CAP
    return pl.pallas_call(
        _nsa_body,
        out_shape=jax.ShapeDtypeStruct((B, S, N_HEADS * HEAD_DIM), BF16),
        grid=(B, G, n_tiles // ts),
        in_specs=in_specs,
        out_specs=pl.BlockSpec((1, ts * tq, R * HEAD_DIM), lambda b, g, i: (b, i, g)),
        scratch_shapes=[pltpu.VMEM((n_tiles + 1, kdim, tq), BF16),
                        pltpu.VMEM((n_tiles + 1, kdim, tq), BF16),
                        pltpu.VMEM((n_tiles, R * tq, kdim), BF16),
                        pltpu.VMEM((n_tiles, R * tq, HEAD_DIM), F32),
                        pltpu.VMEM((ts * R * tq, LANES), F32),
                        pltpu.VMEM((ts * R * tq, 2 * HEAD_DIM), F32)],
        compiler_params=pltpu.CompilerParams(
            dimension_semantics=("arbitrary", "arbitrary", "arbitrary"),
            vmem_limit_bytes=VMEM_LIMIT_CAP),
        name="nsa_attention",
    )(proj, proj, proj, proj, proj, kcmp, vcmp, bias_c, bias_t, bias_far,
      jnp.asarray(overlap_t, BF16), jnp.asarray(aug_sel, BF16), jnp.asarray(aug_win, BF16),
      gates_t)


@functools.lru_cache(maxsize=None)
def _hgrn_masks():
    C = HGRN_CHUNK
    t = np.arange(C)[:, None]
    s = np.arange(C)[None, :]
    masks = [(t // HGRN_SUB == s // HGRN_SUB) & (s <= t)]
    half = C // 2
    while half >= HGRN_SUB:
        grp = 2 * half
        masks.append((t // grp == s // grp) & (t % grp >= half) & (s % grp < half))
        half //= 2
    assert np.array_equal(np.sum(masks, axis=0), (s <= t).astype(int))
    return np.stack(masks).astype(np.float32), (s <= t).astype(np.float32)


def _hgrn_body(q_ref, f_ref, i_ref, g_ref, lb_ref, gn_ref, msk_ref, tril_ref, o_ref,
               st_all, b_all, k_all, q_all, *, layer, heads):
    C = HGRN_CHUNK
    n_chunks = q_ref.shape[1] // C
    nt = (((1,), (1,)), ((), ()))

    lbp = lb_ref[...]
    e = jnp.exp(lbp - jnp.max(lbp, axis=0, keepdims=True))
    sm = e / jnp.sum(e, axis=0, keepdims=True)
    cum = sm[0:1]
    first = cum
    for d in range(1, layer + 1):
        cum = cum + sm[d:d + 1]
    lb_all = cum - first
    log_lb_all = jnp.log(lb_all)
    log_1m_all = jnp.log1p(-lb_all)
    gn_all = gn_ref[...]

    cols = lax.broadcasted_iota(jnp.int32, (1, C), 1)
    tril = tril_ref[...]

    st_all[...] = jnp.zeros(st_all.shape, F32)

    def chunk(c, carry):
        for hh in range(heads):
            one_head(c, hh)
        return carry

    def one_head(c, hh):
        sl = pl.ds(pl.multiple_of(c * C, C), C)
        hs = slice(hh * HEAD_DIM, (hh + 1) * HEAD_DIM)
        lb, log_lb, log_1m, gn = lb_all[:, hs], log_lb_all[:, hs], log_1m_all[:, hs], gn_all[:, hs]
        st_ref, b_sc, k_sc, q_sc = st_all.at[hh], b_all.at[hh], k_all.at[hh], q_all.at[hh]
        qr = q_ref[0, sl, hs].astype(F32)
        x = f_ref[0, sl, hs].astype(F32)
        v = i_ref[0, sl, hs].astype(F32)
        gr = g_ref[0, sl, hs].astype(F32)
        q = qr / (1.0 + jnp.exp2(qr * (-LOG2_E)))
        ex = jnp.exp2(jnp.abs(x) * (-LOG2_E))
        u = 1.0 + ex
        r1 = 1.0 / u
        k = (1.0 - lb) * jnp.where(x >= 0.0, ex * r1, r1)
        c2 = log_1m + (jnp.minimum(x, 0.0) - jnp.log(u))
        e2 = jnp.exp2(jnp.abs(log_lb - c2) * (-LOG2_E))
        log_f = jnp.maximum(log_lb, c2) + jnp.log(1.0 + e2)

        lf_hi = log_f.astype(BF16)
        lf_lo = (log_f - lf_hi.astype(F32)).astype(BF16)
        b = (jnp.dot(tril, lf_hi, preferred_element_type=F32)
             + jnp.dot(tril, lf_lo, preferred_element_type=F32)) * LOG2_E
        b_sc[...] = b
        k_sc[...] = k
        q_sc[...] = q

        vb = v.astype(BF16)
        qb = q.astype(BF16)
        kb = k.astype(BF16)
        st = st_ref[...]
        o = lax.dot_general((q * jnp.exp2(b)).astype(BF16), st.astype(BF16), nt,
                            preferred_element_type=F32)

        pieces = []
        for blk in range(C // HGRN_SUB):
            r0 = blk * HGRN_SUB
            bt = b_sc[r0:r0 + HGRN_SUB, :]
            qt = q_sc[r0:r0 + HGRN_SUB, :]
            arow = jnp.zeros((HGRN_SUB, C), F32)
            for s in range(HGRN_SUB):
                bs = b_sc[r0 + s:r0 + s + 1, :]
                ks = k_sc[r0 + s:r0 + s + 1, :]
                col = jnp.sum(jnp.exp2(bt - bs) * qt * ks, axis=-1, keepdims=True)
                arow = jnp.where(cols == r0 + s, col, arow)
            pieces.append(arow)
        a = jnp.where(msk_ref[0] > 0.5, jnp.concatenate(pieces, axis=0), 0.0)

        half = C // 2
        lvl = 1
        while half >= HGRN_SUB:
            grp = 2 * half
            anc = jnp.concatenate(
                [jnp.broadcast_to(b_sc[g0 + half - 1:g0 + half, :], (grp, b.shape[1]))
                 for g0 in range(0, C, grp)], axis=0)
            e = jnp.exp2(-jnp.abs(b - anc)).astype(BF16)
            al = lax.dot_general(qb * e, kb * e, nt, preferred_element_type=F32)
            a = jnp.where(msk_ref[lvl] > 0.5, al, a)
            half //= 2
            lvl += 1

        o = o + jnp.dot(a.astype(BF16), vb, preferred_element_type=F32)

        b_last = b_sc[C - 1:C, :]
        kh = (k * jnp.exp2(b_last - b)).astype(BF16)
        st_ref[...] = st * jnp.exp2(b_last) + jnp.dot(v.T.astype(BF16), kh,
                                                     preferred_element_type=F32)

        ms = jnp.mean(o * o, axis=-1, keepdims=True)
        o = o * lax.rsqrt(ms + RMS_EPS) * gn * (gr / (1.0 + jnp.exp2(gr * (-LOG2_E))))
        o_ref[0, sl, hs] = o.astype(o_ref.dtype)

    lax.fori_loop(0, n_chunks, chunk, 0)


def _hgrn(proj, hgrn_lb, onorm, layer, B, S):
    H = N_HEADS
    depth = hgrn_lb.shape[0]

    nh = HGRN_HEADS_PER_STEP
    width = nh * HEAD_DIM
    steps = H // nh

    def spec(part):
        return pl.BlockSpec((1, S, width), lambda b, h, part=part: (b, 0, part * steps + h))

    masks, tril = _hgrn_masks()
    C = HGRN_CHUNK
    est = 2 * 4 * S * width * proj.dtype.itemsize + 2 * S * width * 2 + nh * 64 * C * C * 4
    return pl.pallas_call(
        functools.partial(_hgrn_body, layer=layer, heads=nh),
        out_shape=jax.ShapeDtypeStruct((B, S, H * HEAD_DIM), BF16),
        grid=(B, steps),
        in_specs=[spec(0), spec(1), spec(2), spec(3),
                  pl.BlockSpec((depth, width), lambda b, h: (0, h)),
                  pl.BlockSpec((1, width), lambda b, h: (0, h)),
                  pl.BlockSpec(masks.shape, lambda b, h: (0, 0, 0)),
                  pl.BlockSpec((C, C), lambda b, h: (0, 0))],
        out_specs=pl.BlockSpec((1, S, width), lambda b, h: (b, 0, h)),
        scratch_shapes=[pltpu.VMEM((nh, HEAD_DIM, HEAD_DIM), F32),
                        pltpu.VMEM((nh, C, HEAD_DIM), F32),
                        pltpu.VMEM((nh, C, HEAD_DIM), F32),
                        pltpu.VMEM((nh, C, HEAD_DIM), F32)],
        compiler_params=pltpu.CompilerParams(
            dimension_semantics=("arbitrary", "arbitrary"),
            vmem_limit_bytes=_vmem_limit(est)),
        name="hgrn2_recurrence",
    )(proj, proj, proj, proj, hgrn_lb.astype(F32), onorm.reshape(1, -1).astype(F32),
      jnp.asarray(masks), jnp.asarray(tril, BF16))


def _mlp_body(x_ref, gi_ref, wu_ref, wd_ref, go_ref, o_ref, hn_ref, *, nf, tm):
    f = pl.program_id(1)

    def row_chunks(fn):
        def step(r, c):
            fn(pl.ds(pl.multiple_of(r * NORM_ROWS, NORM_ROWS), NORM_ROWS))
            return c

        lax.fori_loop(0, tm // NORM_ROWS, step, 0)

    @pl.when(f == 0)
    def _():
        g = gi_ref[...]

        def norm_in(rows):
            xs = x_ref[rows, :]
            ms = jnp.mean(xs * xs, axis=-1, keepdims=True)
            hn_ref[rows, :] = (xs * lax.rsqrt(ms + RMS_EPS) * g).astype(BF16)

        row_chunks(norm_in)
        o_ref[...] = jnp.zeros(o_ref.shape, F32)

    hid = jnp.dot(hn_ref[...], wu_ref[...], preferred_element_type=F32)
    hid = jnp.square(jnp.maximum(hid, 0.0)).astype(BF16)
    o_ref[...] += jnp.dot(hid, wd_ref[...], preferred_element_type=F32)

    @pl.when(f == nf - 1)
    def _():
        g = go_ref[...]

        def norm_out(rows):
            y = o_ref[rows, :]
            ms = jnp.mean(y * y, axis=-1, keepdims=True)
            o_ref[rows, :] = x_ref[rows, :] + y * lax.rsqrt(ms + RMS_EPS) * g

        row_chunks(norm_out)


def _mlp(xf, g_in, g_out, w_up, w_down, layer, *, tm=512, tf=1024):
    M, D = xf.shape
    F = w_up.shape[2]
    assert M % tm == 0 and F % tf == 0 and tm % NORM_ROWS == 0
    nf = F // tf
    est = (2 * tm * D * 4 + 2 * tm * D * 4 + tm * D * 2 + 2 * 2 * D * tf * 2
           + tm * tf * 6 + 2 * tm * D * 4)
    return pl.pallas_call(
        functools.partial(_mlp_body, nf=nf, tm=tm),
        out_shape=jax.ShapeDtypeStruct((M, D), F32),
        grid=(M // tm, nf),
        in_specs=[pl.BlockSpec((tm, D), lambda i, f: (i, 0)),
                  pl.BlockSpec((1, D), lambda i, f: (0, 0)),
                  pl.BlockSpec((None, D, tf), lambda i, f: (layer, 0, f)),
                  pl.BlockSpec((None, tf, D), lambda i, f: (layer, f, 0)),
                  pl.BlockSpec((1, D), lambda i, f: (0, 0))],
        out_specs=pl.BlockSpec((tm, D), lambda i, f: (i, 0)),
        scratch_shapes=[pltpu.VMEM((tm, D), BF16)],
        compiler_params=pltpu.CompilerParams(
            dimension_semantics=("arbitrary", "arbitrary"),
            vmem_limit_bytes=_vmem_limit(est)),
        name="mlp",
    )(xf, g_in.reshape(1, D).astype(F32), w_up.astype(BF16), w_down.astype(BF16),
      g_out.reshape(1, D).astype(F32))


def _nsa_layer(xf, B, S, g_in, g_out, rel_table, w_in, cmp_pe, cmp_w1, cmp_w2, w_out):
    D = xf.shape[1]
    G, R, Dh = N_GROUPS, HEADS_PER_GROUP, HEAD_DIM
    n_main = N_HEADS * Dh + 6 * G * Dh
    n_gate = 3 * N_HEADS
    w_all = w_in.astype(BF16)
    w_gate = jnp.pad(w_in[:, n_main:], ((0, 0), (0, LANES - n_gate))).astype(BF16)
    colscale = jnp.concatenate([jnp.full((N_HEADS * Dh,), Dh ** -0.5 * LOG2_E, F32),
                                jnp.ones((6 * G * Dh,), F32)])[None]

    proj, glog = _matmul(xf, w_all, tm=1024, tn=1024, norm_g=g_in, epi="colscale",
                         colscale=colscale, side_w=w_gate, n_out=n_main, out_dtype=BF16,
                         name="nsa_proj")
    gates_t = (glog[:, :n_gate].reshape(B, S, 3, G, R).transpose(0, 3, 1, 2, 4)
               .reshape(B, G, S, 3 * R))

    proj3 = proj.reshape(B, S, n_main)
    half = CMP_STRIDE * Dh
    kcmp, vcmp = _compress(proj3, cmp_w1.reshape(2, 2, half, Dh).astype(BF16),
                           cmp_pe.reshape(2, 2, 1, half).astype(F32),
                           cmp_w2.astype(BF16), B, S)

    bias_c, bias_t, bias_far = _bias_tables(rel_table, S)
    attn = _nsa_attention(proj3, kcmp, vcmp, bias_c, bias_t, bias_far, gates_t, B, S)
    return _matmul(attn.reshape(B * S, D), w_out.astype(BF16), tm=512, tn=D,
                   epi="resnorm", res=xf, res_g=g_out, out_dtype=F32, name="nsa_out")


def _hgrn_layer(xf, B, S, layer, g_in, g_out, w_in, hgrn_lb, onorm, w_out):
    D = xf.shape[1]
    proj = _matmul(xf, w_in.astype(BF16), tm=1024, tn=1024, norm_g=g_in,
                   out_dtype=BF16, name="hgrn_proj")
    mixed = _hgrn(proj.reshape(B, S, 4 * D), hgrn_lb, onorm, layer, B, S)
    return _matmul(mixed.reshape(B * S, D), w_out.astype(BF16), tm=512, tn=D,
                   epi="resnorm", res=xf, res_g=g_out, out_dtype=F32, name="hgrn_out")


def kernel(x, norm_g, rel_table, nsa_w_in, nsa_cmp_pe, nsa_cmp_w1, nsa_cmp_w2, nsa_w_out,
           hgrn_w_in, hgrn_lb, hgrn_onorm, hgrn_w_out, mlp_w_up, mlp_w_down):
    B, S, D = x.shape
    depth = norm_g.shape[0]
    assert D == N_HEADS * HEAD_DIM and S % ATT_TILE == 0 and S % HGRN_CHUNK == 0
    xf = x.reshape(B * S, D).astype(F32)
    for layer in range(depth):
        j = layer // 2
        if layer % 2 == 0:
            xf = _nsa_layer(xf, B, S, norm_g[layer, 0], norm_g[layer, 1], rel_table,
                            nsa_w_in[j], nsa_cmp_pe[j], nsa_cmp_w1[j], nsa_cmp_w2[j],
                            nsa_w_out[j])
        else:
            xf = _hgrn_layer(xf, B, S, layer, norm_g[layer, 0], norm_g[layer, 1],
                             hgrn_w_in[j], hgrn_lb, hgrn_onorm[j], hgrn_w_out[j])
        xf = _mlp(xf, norm_g[layer, 2], norm_g[layer, 3], mlp_w_up, mlp_w_down, layer)
    return xf.reshape(B, S, D).astype(x.dtype)
```

```python
import functools
import math

import numpy as np
import jax
import jax.numpy as jnp
from jax import lax
from jax.experimental import pallas as pl
from jax.experimental.pallas import tpu as pltpu

F32 = jnp.float32
BF16 = jnp.bfloat16

N_HEADS = 16
N_GROUPS = 4
HEADS_PER_GROUP = N_HEADS // N_GROUPS
HEAD_DIM = 128
CMP_BLOCK = 32
CMP_STRIDE = 16
SEL_BLOCK = 64
SEL_TOP_N = 8
WINDOW = 512
FORCE_SCORE = 1.0e4
REL_BUCKETS = 32
REL_MAX_DIST = 128
RMS_EPS = 1e-6
NEG_INF = -1.0e30
LOG2_E = math.log2(math.e)

LANES = 128
SUBLANES = 8
VMEM_BYTES_V7X = 64 * 1024 * 1024
VMEM_LIMIT_CAP = VMEM_BYTES_V7X - 8 * 1024 * 1024

ATT_TILE = 256
ATT_ROWS = 128
SEL_TILES_PER_STEP = 4
ATT_TILES_PER_STEP = 2
PREV_TILE, DIAG_TILE, EDGE_TILE = 0, 1, 2
N_BIAS_TILES = 3
HGRN_CHUNK = 128
HGRN_SUB = 8
HGRN_HEADS_PER_STEP = 8
NORM_ROWS = 256


VMEM_LIMIT_FLOOR = 32 * 1024 * 1024


def _vmem_limit(nbytes):
    return int(min(VMEM_LIMIT_CAP, max(VMEM_LIMIT_FLOOR, nbytes)))


def _mm_body(*refs, norm, epi, side, tm):
    it = iter(refs)
    x_ref = next(it)
    g_ref = next(it) if norm else None
    w_ref = next(it)
    ws_ref = next(it) if side else None
    cs_ref = next(it) if epi == "colscale" else None
    res_ref = next(it) if epi == "resnorm" else None
    go_ref = next(it) if epi == "resnorm" else None
    o_ref = next(it)
    os_ref = next(it) if side else None
    hn_ref = next(it) if norm else None

    j = pl.program_id(1)

    if norm:
        @pl.when(j == 0)
        def _():
            g = g_ref[...]

            def step(r, c):
                rows = pl.ds(pl.multiple_of(r * NORM_ROWS, NORM_ROWS), NORM_ROWS)
                xs = x_ref[rows, :]
                ms = jnp.mean(xs * xs, axis=-1, keepdims=True)
                hn_ref[rows, :] = (xs * lax.rsqrt(ms + RMS_EPS) * g).astype(BF16)
                return c

            lax.fori_loop(0, tm // NORM_ROWS, step, 0)

        lhs = hn_ref[...]
    else:
        lhs = x_ref[...]

    if side:
        @pl.when(j == 0)
        def _():
            os_ref[...] = jnp.dot(lhs, ws_ref[...], preferred_element_type=F32)

    acc = jnp.dot(lhs, w_ref[...].astype(BF16), preferred_element_type=F32)
    if epi == "colscale":
        acc = acc * cs_ref[...]
    elif epi == "resnorm":
        ms = jnp.mean(acc * acc, axis=-1, keepdims=True)
        acc = res_ref[...] + acc * lax.rsqrt(ms + RMS_EPS) * go_ref[...]
    o_ref[...] = acc.astype(o_ref.dtype)


def _matmul(x, w, *, tm, tn, norm_g=None, epi="none", colscale=None, res=None, res_g=None,
            side_w=None, n_out=None, out_dtype=BF16, name="mm"):
    M, K = x.shape
    N = w.shape[1] if n_out is None else n_out
    assert N <= w.shape[1]
    norm = norm_g is not None
    side = side_w is not None
    assert M % tm == 0 and N % tn == 0
    assert not (epi == "resnorm" and tn != N)

    in_specs = [pl.BlockSpec((tm, K), lambda i, j: (i, 0))]
    args = [x]
    if norm:
        in_specs.append(pl.BlockSpec((1, K), lambda i, j: (0, 0)))
        args.append(norm_g.reshape(1, K).astype(F32))
    in_specs.append(pl.BlockSpec((K, tn), lambda i, j: (0, j)))
    args.append(w)
    ns = side_w.shape[1] if side else 0
    if side:
        in_specs.append(pl.BlockSpec((K, ns), lambda i, j: (0, 0)))
        args.append(side_w)
    if epi == "colscale":
        in_specs.append(pl.BlockSpec((1, tn), lambda i, j: (0, j)))
        args.append(colscale)
    if epi == "resnorm":
        in_specs.append(pl.BlockSpec((tm, tn), lambda i, j: (i, j)))
        args.append(res)
        in_specs.append(pl.BlockSpec((1, tn), lambda i, j: (0, j)))
        args.append(res_g.reshape(1, N).astype(F32))

    out_shape = jax.ShapeDtypeStruct((M, N), out_dtype)
    out_specs = pl.BlockSpec((tm, tn), lambda i, j: (i, j))
    if side:
        out_shape = (out_shape, jax.ShapeDtypeStruct((M, ns), F32))
        out_specs = (out_specs, pl.BlockSpec((tm, ns), lambda i, j: (i, 0)))

    xb = x.dtype.itemsize
    ob = jnp.dtype(out_dtype).itemsize
    wb = w.dtype.itemsize
    est = (2 * tm * K * xb + 2 * K * tn * wb + (K * tn * 2 if wb > 2 else 0) + 2 * tm * tn * ob
           + (tm * K * 2 if norm else 0)
           + (2 * tm * tn * 4 if epi == "resnorm" else 0) + 3 * tm * tn * 4
           + 2 * K * ns * 2 + 3 * tm * ns * 4)

    return pl.pallas_call(
        functools.partial(_mm_body, norm=norm, epi=epi, side=side, tm=tm),
        out_shape=out_shape,
        grid=(M // tm, N // tn),
        in_specs=in_specs,
        out_specs=out_specs,
        scratch_shapes=[pltpu.VMEM((tm, K), BF16)] if norm else [],
        compiler_params=pltpu.CompilerParams(
            dimension_semantics=("arbitrary", "arbitrary"),
            vmem_limit_bytes=_vmem_limit(est)),
        name=name,
    )(*args)


def _rel_bucket_np(dist):
    n = np.maximum(dist, 0)
    max_exact = REL_BUCKETS // 2
    nf = np.maximum(n, 1).astype(np.float32)
    ratio = np.log(nf / np.float32(max_exact)) / np.float32(math.log(REL_MAX_DIST / max_exact))
    large = max_exact + (ratio * np.float32(REL_BUCKETS - max_exact)).astype(np.int32)
    large = np.minimum(large, REL_BUCKETS - 1)
    return np.where(n < max_exact, n, large).astype(np.int32)


@functools.lru_cache(maxsize=None)
def _static_maps(seq):
    n_cmp = LANES
    pos = np.arange(seq, dtype=np.int32)[:, None]
    c_end = np.arange(n_cmp, dtype=np.int32)[None, :] * CMP_STRIDE + CMP_BLOCK - 1
    bucket_c = _rel_bucket_np(pos - c_end)
    t = np.arange(ATT_TILE, dtype=np.int32)[:, None]
    k = np.arange(ATT_TILE, dtype=np.int32)[None, :]
    bucket_t = np.stack([_rel_bucket_np(t - k), _rel_bucket_np(ATT_TILE + t - k)])
    assert _rel_bucket_np(np.array([ATT_TILE + 1]))[0] == REL_BUCKETS - 1
    nc = seq // CMP_STRIDE - CMP_BLOCK // CMP_STRIDE + 1
    nb = seq // SEL_BLOCK
    c_start = np.arange(nc)[:, None] * CMP_STRIDE
    b_start = np.arange(nb)[None, :] * SEL_BLOCK
    ov = ((c_start <= b_start + SEL_BLOCK - 1) & (c_start + CMP_BLOCK - 1 >= b_start))
    overlap = np.zeros((LANES, LANES), np.float32)
    overlap[:nc, :nb] = ov
    return bucket_c, bucket_t, overlap


def _bias_body(tab_ref, bc_ref, bt_ref, oc_ref, ot_ref, of_ref, *, seq):
    h = pl.program_id(0)

    lane = lax.broadcasted_iota(jnp.int32, (SUBLANES, LANES), 1)
    tab_row = jnp.zeros((SUBLANES, LANES), F32)
    for b in range(REL_BUCKETS):
        tab_row = jnp.where(lane == b, tab_ref[b, h], tab_row)

    def lookup(bmap):
        rows = bmap.shape[0]
        tab = jnp.broadcast_to(tab_row[0:1], (rows, LANES))
        return jnp.concatenate(
            [jnp.take_along_axis(tab, bmap[:, c0:c0 + LANES], axis=1)
             for c0 in range(0, bmap.shape[1], LANES)], axis=1)

    def step(r, c):
        rows = pl.ds(pl.multiple_of(r * ATT_TILE, ATT_TILE), ATT_TILE)
        oc_ref[0, rows, :] = lookup(bc_ref[rows, :]) * LOG2_E
        return c

    lax.fori_loop(0, seq // ATT_TILE, step, 0)

    tt = lax.broadcasted_iota(jnp.int32, (ATT_TILE, ATT_TILE), 0)
    kk = lax.broadcasted_iota(jnp.int32, (ATT_TILE, ATT_TILE), 1)
    far = tab_ref[REL_BUCKETS - 1, h]
    of_ref[0] = jnp.full(of_ref.shape[1:], far * LOG2_E, F32)
    ot_ref[PREV_TILE, 0] = (lookup(bt_ref[1]) - far) * LOG2_E
    ot_ref[DIAG_TILE, 0] = jnp.where(kk <= tt, (lookup(bt_ref[0]) - far) * LOG2_E, NEG_INF)
    ot_ref[EDGE_TILE, 0] = jnp.where(kk > tt, 0.0, NEG_INF)


def _bias_tables(rel_table, seq):
    bucket_c, bucket_t, _ = _static_maps(seq)
    return pl.pallas_call(
        functools.partial(_bias_body, seq=seq),
        out_shape=(jax.ShapeDtypeStruct((N_HEADS, seq, LANES), F32),
                   jax.ShapeDtypeStruct((N_BIAS_TILES, N_HEADS, ATT_TILE, ATT_TILE), F32),
                   jax.ShapeDtypeStruct((N_HEADS, SUBLANES, LANES), F32)),
        grid=(N_HEADS,),
        in_specs=[pl.BlockSpec(memory_space=pltpu.SMEM),
                  pl.BlockSpec((seq, LANES), lambda h: (0, 0)),
                  pl.BlockSpec((2, ATT_TILE, ATT_TILE), lambda h: (0, 0, 0))],
        out_specs=(pl.BlockSpec((1, seq, LANES), lambda h: (h, 0, 0)),
                   pl.BlockSpec((N_BIAS_TILES, 1, ATT_TILE, ATT_TILE), lambda h: (0, h, 0, 0)),
                   pl.BlockSpec((1, SUBLANES, LANES), lambda h: (h, 0, 0))),
        compiler_params=pltpu.CompilerParams(dimension_semantics=("arbitrary",)),
        name="rel_bias",
    )(rel_table.astype(F32), jnp.asarray(bucket_c), jnp.asarray(bucket_t))


def _compress_body(xk_ref, xv_ref, w1_ref, pe_ref, w2_ref, ok_ref, ov_ref, x_sc):
    n_grp = xk_ref.shape[1] // CMP_STRIDE

    def one(x_ref, idx, o_ref):
        x_sc[...] = x_ref[0].astype(F32)
        x = jnp.concatenate([x_sc[pl.ds(t, n_grp, stride=CMP_STRIDE), :]
                             for t in range(CMP_STRIDE)], axis=1)
        a0 = jnp.dot((x + pe_ref[idx, 0]).astype(BF16), w1_ref[idx, 0],
                     preferred_element_type=F32)
        a1 = jnp.dot((x + pe_ref[idx, 1]).astype(BF16), w1_ref[idx, 1],
                     preferred_element_type=F32)
        pre = a0 + pltpu.roll(a1, LANES - 1, 0)
        hid = jax.nn.gelu(pre).astype(BF16)
        o_ref[0, 0] = jnp.dot(hid, w2_ref[idx], preferred_element_type=F32).astype(BF16)

    one(xk_ref, 0, ok_ref)
    one(xv_ref, 1, ov_ref)


def _compress(proj, w1, pe, w2, B, S):
    G = N_GROUPS
    assert S // CMP_STRIDE == LANES
    half = CMP_STRIDE * HEAD_DIM
    q_cols = N_HEADS

    def spec_x(slot):
        return pl.BlockSpec((1, S, HEAD_DIM), lambda b, g, slot=slot: (b, 0, q_cols + slot * G + g))

    spec_o = pl.BlockSpec((1, 1, LANES, HEAD_DIM), lambda b, g: (b, g, 0, 0))
    out = jax.ShapeDtypeStruct((B, G, LANES, HEAD_DIM), BF16)
    return pl.pallas_call(
        _compress_body,
        out_shape=(out, out),
        grid=(B, G),
        in_specs=[spec_x(0), spec_x(1),
                  pl.BlockSpec((2, 2, half, HEAD_DIM), lambda b, g: (0, 0, 0, 0)),
                  pl.BlockSpec((2, 2, 1, half), lambda b, g: (0, 0, 0, 0)),
                  pl.BlockSpec((2, HEAD_DIM, HEAD_DIM), lambda b, g: (0, 0, 0))],
        out_specs=(spec_o, spec_o),
        scratch_shapes=[pltpu.VMEM((S, HEAD_DIM), F32)],
        compiler_params=pltpu.CompilerParams(dimension_semantics=("arbitrary", "arbitrary")),
        name="nsa_compress",
    )(proj, proj, w1, pe, w2)


def _nsa_body(q_ref, ks_ref, vs_ref, kw_ref, vw_ref, kc_ref, vc_ref, bc_ref, bt_ref, far_ref,
              ovl_ref, augs_ref, augw_ref, gl_ref, o_ref, kts_sc, ktw_sc, qa_sc, oc_sc, m_sc,
              acc_sc):
    R = HEADS_PER_GROUP
    tq = ATT_TILE
    qi = pl.program_id(2)
    n_tiles = ks_ref.shape[1] // tq
    nb = ovl_ref.shape[0]
    nt = (((1,), (1,)), ((), ()))

    cidx = lax.broadcasted_iota(jnp.int32, (1, 1, LANES), 2)
    jb = lax.broadcasted_iota(jnp.int32, (nb, 1), 0)
    lane = lax.broadcasted_iota(jnp.int32, (1, LANES), 1)

    def select_tile(t):
        rows = pl.ds(pl.multiple_of(t * tq, tq), tq)
        q = q_ref[0, rows, :]
        q4 = jnp.concatenate([q[:, r * HEAD_DIM:(r + 1) * HEAD_DIM] for r in range(R)], axis=0)
        qa_sc[t, :, :HEAD_DIM] = q4
        pos3 = t * tq + lax.broadcasted_iota(jnp.int32, (1, tq, 1), 1)

        sc = lax.dot_general(q4, kc_ref[0, 0], nt, preferred_element_type=F32)
        sc = sc.reshape(R, tq, LANES) + bc_ref[:, rows, :]
        valid = (cidx * CMP_STRIDE + (CMP_BLOCK - 1) <= pos3) & (cidx < LANES - 1)
        sc = jnp.where(valid, sc, NEG_INF)
        mc = jnp.max(sc, axis=-1, keepdims=True)
        ec = jnp.exp2(sc - mc)
        pc = ec / jnp.sum(ec, axis=-1, keepdims=True)
        pc = jnp.where(pos3 >= CMP_BLOCK - 1, pc, 0.0)
        oc_sc[t] = jnp.dot(pc.reshape(R * tq, LANES).astype(BF16), vc_ref[0, 0],
                           preferred_element_type=F32)

        psum = pc[0]
        for r in range(1, R):
            psum = psum + pc[r]
        p_hi = psum.astype(BF16)
        p_lo = (psum - p_hi.astype(F32)).astype(BF16)
        ovt = ovl_ref[...]
        imp = (lax.dot_general(ovt, p_hi, nt, preferred_element_type=F32)
               + lax.dot_general(ovt, p_lo, nt, preferred_element_type=F32))
        pos_t = t * tq + lax.broadcasted_iota(jnp.int32, (1, tq), 1)
        q_blk = lax.shift_right_logical(pos_t, int(math.log2(SEL_BLOCK)))
        forced = (jb == 0) | (jb == q_blk) | (jb == q_blk - 1)
        future = jb > q_blk
        imp = jnp.where(forced, FORCE_SCORE, jnp.where(future, -1.0, imp))
        cnt = jnp.zeros((nb, tq), F32)
        for i in range(nb):
            row = imp[i:i + 1, :]
            beats = (row > imp) | ((row == imp) & (jb > i))
            cnt = cnt + jnp.where(beats, 1.0, 0.0)
        sel_t = jnp.where(cnt < float(min(SEL_TOP_N, nb)), 1.0, 0.0)
        sel = jnp.concatenate([sel_t, jnp.zeros((LANES - nb, tq), F32)], axis=0).T

        sel_pad = jnp.where(lane < nb, (sel - 1.0) * (-NEG_INF), 0.0)
        for r in range(R):
            far = jnp.broadcast_to(far_ref[r, 0:1, :], (tq, LANES))
            far_hi = far.astype(BF16).astype(F32)
            pad = jnp.where(lane == nb, far_hi, jnp.where(lane == nb + 1, far - far_hi, sel_pad))
            pad = jnp.where(lane == nb + 2, 1.0, pad)
            qa_sc[t, r * tq:(r + 1) * tq, HEAD_DIM:] = pad.astype(BF16)

    @pl.when(qi == 0)
    def _():
        def tr(j, c):
            rows = pl.ds(pl.multiple_of(j * tq, tq), tq)
            kts_sc[j, :HEAD_DIM, :] = ks_ref[0, rows, :].T
            kts_sc[j, HEAD_DIM:, :] = augs_ref[j]
            ktw_sc[j, :HEAD_DIM, :] = kw_ref[0, rows, :].T
            ktw_sc[j, HEAD_DIM:, :] = augw_ref[0]
            return c

        lax.fori_loop(0, n_tiles, tr, 0)
        zeros = jnp.zeros((HEAD_DIM, tq), BF16)
        kts_sc[n_tiles, :HEAD_DIM, :] = zeros
        kts_sc[n_tiles, HEAD_DIM:, :] = augs_ref[n_tiles]
        ktw_sc[n_tiles, :HEAD_DIM, :] = zeros
        ktw_sc[n_tiles, HEAD_DIM:, :] = augw_ref[1]

        def sel_group(gi, c):
            for u in range(SEL_TILES_PER_STEP):
                select_tile(gi * SEL_TILES_PER_STEP + u)
            return c

        lax.fori_loop(0, n_tiles // SEL_TILES_PER_STEP, sel_group, 0)

    ones = jnp.ones((tq, HEAD_DIM), BF16)
    n_chunks = R * tq // ATT_ROWS
    chunks_per_head = tq // ATT_ROWS
    dead = n_tiles
    step_tiles = [ATT_TILES_PER_STEP * qi + u for u in range(ATT_TILES_PER_STEP)]

    def keys(kt_sc, tiles):
        return jnp.concatenate([kt_sc[t] for t in tiles], axis=1)

    def values(v_ref, tiles):
        parts = []
        for t in tiles:
            rows = pl.ds(pl.multiple_of(t * tq, tq), tq)
            parts.append(jnp.concatenate([v_ref[0, rows, :], ones], axis=1))
        return jnp.concatenate(parts, axis=0)

    def logits(q_sc, ci, kt, kinds):
        r, hh = divmod(ci, chunks_per_head)
        rs = slice(ci * ATT_ROWS, (ci + 1) * ATT_ROWS)
        qs = slice(hh * ATT_ROWS, (hh + 1) * ATT_ROWS)
        s = jnp.dot(q_sc[rs, :], kt, preferred_element_type=F32)
        if all(kd is None for kd in kinds):
            return s
        parts = [s[:, i * tq:(i + 1) * tq] for i in range(len(kinds))]
        return jnp.concatenate([p if kd is None else p + bt_ref[kd, r, qs, :]
                                for p, kd in zip(parts, kinds)], axis=1)

    def probs(s, m):
        return jnp.concatenate([jnp.exp2(s[:, k0:k0 + LANES] - m)
                                for k0 in range(0, s.shape[1], LANES)], axis=1).astype(BF16)

    m_sc[...] = jnp.full(m_sc.shape, NEG_INF, F32)
    acc_sc[...] = jnp.zeros(acc_sc.shape, F32)

    def sel_update(u, t, kt, vv, kinds):
        for ci in range(n_chunks):
            rs = slice((u * n_chunks + ci) * ATT_ROWS, (u * n_chunks + ci + 1) * ATT_ROWS)
            s = logits(qa_sc.at[t], ci, kt, kinds)
            m_old = m_sc[rs]
            m_new = jnp.maximum(m_old, jnp.max(s, axis=-1, keepdims=True))
            alpha = jnp.exp2(m_old - m_new)
            m_sc[rs] = m_new
            pv = jnp.dot(probs(s, m_new), vv, preferred_element_type=F32)
            acc_sc[rs] = acc_sc[rs] * jnp.concatenate([alpha, alpha], axis=1) + pv

    assert ATT_TILES_PER_STEP == 2
    ta, tb = step_tiles

    def far_pair(pi, c):
        pair_tiles = (2 * pi, 2 * pi + 1)
        kt, vv = keys(kts_sc, pair_tiles), values(vs_ref, pair_tiles)
        sel_update(0, ta, kt, vv, (None, None))
        sel_update(1, tb, kt, vv, (None, None))
        return c

    lax.fori_loop(0, jnp.maximum(qi - 1, 0), far_pair, 0)

    @pl.when(qi >= 1)
    def _():
        pair_tiles = (ta - 2, ta - 1)
        kt, vv = keys(kts_sc, pair_tiles), values(vs_ref, pair_tiles)
        sel_update(0, ta, kt, vv, (None, PREV_TILE))
        sel_update(1, tb, kt, vv, (None, None))

    sel_update(0, ta, kts_sc[ta], values(vs_ref, (ta,)), (DIAG_TILE,))
    sel_update(1, tb, keys(kts_sc, (ta, tb)), values(vs_ref, (ta, tb)), (PREV_TILE, DIAG_TILE))

    n_win = WINDOW // tq
    w_kinds = (EDGE_TILE,) + (None,) * (n_win - 2) + (PREV_TILE, DIAG_TILE)
    gates = jax.nn.sigmoid(gl_ref[0, 0])
    for u, t in enumerate(step_tiles):
        acc = acc_sc[u * R * tq:(u + 1) * R * tq, :]
        o_sel = acc[:, :HEAD_DIM] / acc[:, HEAD_DIM:]

        w_tiles = [t - n_win + w for w in range(n_win + 1)]
        kt = keys(ktw_sc, [jnp.where(w >= 0, w, dead) for w in w_tiles])
        vv = values(vw_ref, [jnp.maximum(w, 0) for w in w_tiles])
        o_parts = []
        for ci in range(n_chunks):
            s = logits(qa_sc.at[t], ci, kt, w_kinds)
            pv = jnp.dot(probs(s, jnp.max(s, axis=-1, keepdims=True)), vv,
                         preferred_element_type=F32)
            o_parts.append(pv[:, :HEAD_DIM] / pv[:, HEAD_DIM:])
        o_win = jnp.concatenate(o_parts, axis=0)

        o_cmp = oc_sc[t]
        g = gates[u * tq:(u + 1) * tq]
        outs = []
        for r in range(R):
            hs = slice(r * tq, (r + 1) * tq)
            outs.append(g[:, r:r + 1] * o_cmp[hs] + g[:, R + r:R + r + 1] * o_sel[hs]
                        + g[:, 2 * R + r:2 * R + r + 1] * o_win[hs])
        o_ref[0, u * tq:(u + 1) * tq, :] = jnp.concatenate(outs, axis=1).astype(o_ref.dtype)


def _nsa_attention(proj, kcmp, vcmp, bias_c, bias_t, bias_far, gates_t, B, S):
    assert WINDOW % ATT_TILE == 0 and S % ATT_TILE == 0
    assert S // CMP_STRIDE == LANES and S // SEL_BLOCK <= LANES
    R, G, tq = HEADS_PER_GROUP, N_GROUPS, ATT_TILE
    n_tiles = S // tq
    ts = ATT_TILES_PER_STEP
    assert n_tiles % ts == 0
    nb = S // SEL_BLOCK
    _, _, overlap = _static_maps(S)
    overlap_t = np.ascontiguousarray(overlap.T[:nb])
    assert nb + 3 <= LANES and n_tiles % 2 == 0
    blk_of_key = (np.arange(S) // SEL_BLOCK).reshape(n_tiles, 1, tq)
    aug_sel = np.zeros((n_tiles + 1, LANES, tq), np.float32)
    aug_sel[:n_tiles] = np.arange(LANES).reshape(1, LANES, 1) == blk_of_key
    aug_sel[:n_tiles, nb:nb + 2, :] = 1.0
    aug_sel[n_tiles, nb + 2, :] = NEG_INF
    aug_win = np.zeros((2, LANES, tq), np.float32)
    aug_win[0, nb:nb + 2, :] = 1.0
    aug_win[1, nb + 2, :] = NEG_INF
    q_cols = N_HEADS

    def kv_spec(slot):
        return pl.BlockSpec((1, S, HEAD_DIM),
                            lambda b, g, i, slot=slot: (b, 0, q_cols + slot * G + g))

    cmp_spec = pl.BlockSpec((1, 1, LANES, HEAD_DIM), lambda b, g, i: (b, g, 0, 0))
    in_specs = [
        pl.BlockSpec((1, S, R * HEAD_DIM), lambda b, g, i: (b, 0, g)),
        kv_spec(2), kv_spec(3), kv_spec(4), kv_spec(5),
        cmp_spec, cmp_spec,
        pl.BlockSpec((R, S, LANES), lambda b, g, i: (g, 0, 0)),
        pl.BlockSpec((N_BIAS_TILES, R, tq, tq), lambda b, g, i: (0, g, 0, 0)),
        pl.BlockSpec((R, SUBLANES, LANES), lambda b, g, i: (g, 0, 0)),
        pl.BlockSpec((nb, LANES), lambda b, g, i: (0, 0)),
        pl.BlockSpec((n_tiles + 1, LANES, tq), lambda b, g, i: (0, 0, 0)),
        pl.BlockSpec((2, LANES, tq), lambda b, g, i: (0, 0, 0)),
        pl.BlockSpec((1, 1, ts * tq, 3 * R), lambda b, g, i: (b, g, i, 0)),
    ]
    kdim = HEAD_DIM + LANES
    assert n_tiles % SEL_TILES_PER_STEP == 0
    est = (2 * 4 * S * HEAD_DIM * 2 + 2 * N_BIAS_TILES * R * tq * tq * 4 + 2 * R * S * LANES * 4
           + 2 * S * R * HEAD_DIM * 2 + 2 * 2 * S * LANES * 2 + 2 * S * kdim * 2
           + R * S * (kdim * 2 + HEAD_DIM * 4) + ts * R * tq * (LANES * 4 + 2 * HEAD_DIM * 4)
           + 3 * R * tq * LANES * 4 + 16 * ATT_ROWS * tq * 4 + 4 * tq * R * HEAD_DIM * 2)
    assert est <= VMEM_LIMIT_CAP
    return pl.pallas_call(
        _nsa_body,
        out_shape=jax.ShapeDtypeStruct((B, S, N_HEADS * HEAD_DIM), BF16),
        grid=(B, G, n_tiles // ts),
        in_specs=in_specs,
        out_specs=pl.BlockSpec((1, ts * tq, R * HEAD_DIM), lambda b, g, i: (b, i, g)),
        scratch_shapes=[pltpu.VMEM((n_tiles + 1, kdim, tq), BF16),
                        pltpu.VMEM((n_tiles + 1, kdim, tq), BF16),
                        pltpu.VMEM((n_tiles, R * tq, kdim), BF16),
                        pltpu.VMEM((n_tiles, R * tq, HEAD_DIM), F32),
                        pltpu.VMEM((ts * R * tq, LANES), F32),
                        pltpu.VMEM((ts * R * tq, 2 * HEAD_DIM), F32)],
        compiler_params=pltpu.CompilerParams(
            dimension_semantics=("arbitrary", "arbitrary", "arbitrary"),
            vmem_limit_bytes=VMEM_LIMIT_CAP),
        name="nsa_attention",
    )(proj, proj, proj, proj, proj, kcmp, vcmp, bias_c, bias_t, bias_far,
      jnp.asarray(overlap_t, BF16), jnp.asarray(aug_sel, BF16), jnp.asarray(aug_win, BF16),
      gates_t)


@functools.lru_cache(maxsize=None)
def _hgrn_masks():
    C = HGRN_CHUNK
    t = np.arange(C)[:, None]
    s = np.arange(C)[None, :]
    masks = [(t // HGRN_SUB == s // HGRN_SUB) & (s <= t)]
    half = C // 2
    while half >= HGRN_SUB:
        grp = 2 * half
        masks.append((t // grp == s // grp) & (t % grp >= half) & (s % grp < half))
        half //= 2
    assert np.array_equal(np.sum(masks, axis=0), (s <= t).astype(int))
    return np.stack(masks).astype(np.float32), (s <= t).astype(np.float32)


def _hgrn_body(q_ref, f_ref, i_ref, g_ref, lb_ref, gn_ref, msk_ref, tril_ref, o_ref,
               st_all, b_all, k_all, q_all, *, layer, heads):
    C = HGRN_CHUNK
    n_chunks = q_ref.shape[1] // C
    nt = (((1,), (1,)), ((), ()))

    lbp = lb_ref[...]
    e = jnp.exp(lbp - jnp.max(lbp, axis=0, keepdims=True))
    sm = e / jnp.sum(e, axis=0, keepdims=True)
    cum = sm[0:1]
    first = cum
    for d in range(1, layer + 1):
        cum = cum + sm[d:d + 1]
    lb_all = cum - first
    log_lb_all = jnp.log(lb_all)
    log_1m_all = jnp.log1p(-lb_all)
    gn_all = gn_ref[...]

    cols = lax.broadcasted_iota(jnp.int32, (1, C), 1)
    tril = tril_ref[...]

    st_all[...] = jnp.zeros(st_all.shape, F32)

    def chunk(c, carry):
        for hh in range(heads):
            one_head(c, hh)
        return carry

    def one_head(c, hh):
        sl = pl.ds(pl.multiple_of(c * C, C), C)
        hs = slice(hh * HEAD_DIM, (hh + 1) * HEAD_DIM)
        lb, log_lb, log_1m, gn = lb_all[:, hs], log_lb_all[:, hs], log_1m_all[:, hs], gn_all[:, hs]
        st_ref, b_sc, k_sc, q_sc = st_all.at[hh], b_all.at[hh], k_all.at[hh], q_all.at[hh]
        qr = q_ref[0, sl, hs].astype(F32)
        x = f_ref[0, sl, hs].astype(F32)
        v = i_ref[0, sl, hs].astype(F32)
        gr = g_ref[0, sl, hs].astype(F32)
        q = qr / (1.0 + jnp.exp2(qr * (-LOG2_E)))
        ex = jnp.exp2(jnp.abs(x) * (-LOG2_E))
        u = 1.0 + ex
        r1 = 1.0 / u
        k = (1.0 - lb) * jnp.where(x >= 0.0, ex * r1, r1)
        c2 = log_1m + (jnp.minimum(x, 0.0) - jnp.log(u))
        e2 = jnp.exp2(jnp.abs(log_lb - c2) * (-LOG2_E))
        log_f = jnp.maximum(log_lb, c2) + jnp.log(1.0 + e2)

        lf_hi = log_f.astype(BF16)
        lf_lo = (log_f - lf_hi.astype(F32)).astype(BF16)
        b = (jnp.dot(tril, lf_hi, preferred_element_type=F32)
             + jnp.dot(tril, lf_lo, preferred_element_type=F32)) * LOG2_E
        b_sc[...] = b
        k_sc[...] = k
        q_sc[...] = q

        vb = v.astype(BF16)
        qb = q.astype(BF16)
        kb = k.astype(BF16)
        st = st_ref[...]
        o = lax.dot_general((q * jnp.exp2(b)).astype(BF16), st.astype(BF16), nt,
                            preferred_element_type=F32)

        pieces = []
        for blk in range(C // HGRN_SUB):
            r0 = blk * HGRN_SUB
            bt = b_sc[r0:r0 + HGRN_SUB, :]
            qt = q_sc[r0:r0 + HGRN_SUB, :]
            arow = jnp.zeros((HGRN_SUB, C), F32)
            for s in range(HGRN_SUB):
                bs = b_sc[r0 + s:r0 + s + 1, :]
                ks = k_sc[r0 + s:r0 + s + 1, :]
                col = jnp.sum(jnp.exp2(bt - bs) * qt * ks, axis=-1, keepdims=True)
                arow = jnp.where(cols == r0 + s, col, arow)
            pieces.append(arow)
        a = jnp.where(msk_ref[0] > 0.5, jnp.concatenate(pieces, axis=0), 0.0)

        half = C // 2
        lvl = 1
        while half >= HGRN_SUB:
            grp = 2 * half
            anc = jnp.concatenate(
                [jnp.broadcast_to(b_sc[g0 + half - 1:g0 + half, :], (grp, b.shape[1]))
                 for g0 in range(0, C, grp)], axis=0)
            e = jnp.exp2(-jnp.abs(b - anc)).astype(BF16)
            al = lax.dot_general(qb * e, kb * e, nt, preferred_element_type=F32)
            a = jnp.where(msk_ref[lvl] > 0.5, al, a)
            half //= 2
            lvl += 1

        o = o + jnp.dot(a.astype(BF16), vb, preferred_element_type=F32)

        b_last = b_sc[C - 1:C, :]
        kh = (k * jnp.exp2(b_last - b)).astype(BF16)
        st_ref[...] = st * jnp.exp2(b_last) + jnp.dot(v.T.astype(BF16), kh,
                                                     preferred_element_type=F32)

        ms = jnp.mean(o * o, axis=-1, keepdims=True)
        o = o * lax.rsqrt(ms + RMS_EPS) * gn * (gr / (1.0 + jnp.exp2(gr * (-LOG2_E))))
        o_ref[0, sl, hs] = o.astype(o_ref.dtype)

    lax.fori_loop(0, n_chunks, chunk, 0)


def _hgrn(proj, hgrn_lb, onorm, layer, B, S):
    H = N_HEADS
    depth = hgrn_lb.shape[0]

    nh = HGRN_HEADS_PER_STEP
    width = nh * HEAD_DIM
    steps = H // nh

    def spec(part):
        return pl.BlockSpec((1, S, width), lambda b, h, part=part: (b, 0, part * steps + h))

    masks, tril = _hgrn_masks()
    C = HGRN_CHUNK
    est = 2 * 4 * S * width * proj.dtype.itemsize + 2 * S * width * 2 + nh * 64 * C * C * 4
    return pl.pallas_call(
        functools.partial(_hgrn_body, layer=layer, heads=nh),
        out_shape=jax.ShapeDtypeStruct((B, S, H * HEAD_DIM), BF16),
        grid=(B, steps),
        in_specs=[spec(0), spec(1), spec(2), spec(3),
                  pl.BlockSpec((depth, width), lambda b, h: (0, h)),
                  pl.BlockSpec((1, width), lambda b, h: (0, h)),
                  pl.BlockSpec(masks.shape, lambda b, h: (0, 0, 0)),
                  pl.BlockSpec((C, C), lambda b, h: (0, 0))],
        out_specs=pl.BlockSpec((1, S, width), lambda b, h: (b, 0, h)),
        scratch_shapes=[pltpu.VMEM((nh, HEAD_DIM, HEAD_DIM), F32),
                        pltpu.VMEM((nh, C, HEAD_DIM), F32),
                        pltpu.VMEM((nh, C, HEAD_DIM), F32),
                        pltpu.VMEM((nh, C, HEAD_DIM), F32)],
        compiler_params=pltpu.CompilerParams(
            dimension_semantics=("arbitrary", "arbitrary"),
            vmem_limit_bytes=_vmem_limit(est)),
        name="hgrn2_recurrence",
    )(proj, proj, proj, proj, hgrn_lb.astype(F32), onorm.reshape(1, -1).astype(F32),
      jnp.asarray(masks), jnp.asarray(tril, BF16))


def _mlp_body(x_ref, gi_ref, wu_ref, wd_ref, go_ref, o_ref, hn_ref, *, nf, tm):
    f = pl.program_id(1)

    def row_chunks(fn):
        def step(r, c):
            fn(pl.ds(pl.multiple_of(r * NORM_ROWS, NORM_ROWS), NORM_ROWS))
            return c

        lax.fori_loop(0, tm // NORM_ROWS, step, 0)

    @pl.when(f == 0)
    def _():
        g = gi_ref[...]

        def norm_in(rows):
            xs = x_ref[rows, :]
            ms = jnp.mean(xs * xs, axis=-1, keepdims=True)
            hn_ref[rows, :] = (xs * lax.rsqrt(ms + RMS_EPS) * g).astype(BF16)

        row_chunks(norm_in)
        o_ref[...] = jnp.zeros(o_ref.shape, F32)

    hid = jnp.dot(hn_ref[...], wu_ref[...], preferred_element_type=F32)
    hid = jnp.square(jnp.maximum(hid, 0.0)).astype(BF16)
    o_ref[...] += jnp.dot(hid, wd_ref[...], preferred_element_type=F32)

    @pl.when(f == nf - 1)
    def _():
        g = go_ref[...]

        def norm_out(rows):
            y = o_ref[rows, :]
            ms = jnp.mean(y * y, axis=-1, keepdims=True)
            o_ref[rows, :] = x_ref[rows, :] + y * lax.rsqrt(ms + RMS_EPS) * g

        row_chunks(norm_out)


def _mlp(xf, g_in, g_out, w_up, w_down, layer, *, tm=512, tf=1024):
    M, D = xf.shape
    F = w_up.shape[2]
    assert M % tm == 0 and F % tf == 0 and tm % NORM_ROWS == 0
    nf = F // tf
    est = (2 * tm * D * 4 + 2 * tm * D * 4 + tm * D * 2 + 2 * 2 * D * tf * 2
           + tm * tf * 6 + 2 * tm * D * 4)
    return pl.pallas_call(
        functools.partial(_mlp_body, nf=nf, tm=tm),
        out_shape=jax.ShapeDtypeStruct((M, D), F32),
        grid=(M // tm, nf),
        in_specs=[pl.BlockSpec((tm, D), lambda i, f: (i, 0)),
                  pl.BlockSpec((1, D), lambda i, f: (0, 0)),
                  pl.BlockSpec((None, D, tf), lambda i, f: (layer, 0, f)),
                  pl.BlockSpec((None, tf, D), lambda i, f: (layer, f, 0)),
                  pl.BlockSpec((1, D), lambda i, f: (0, 0))],
        out_specs=pl.BlockSpec((tm, D), lambda i, f: (i, 0)),
        scratch_shapes=[pltpu.VMEM((tm, D), BF16)],
        compiler_params=pltpu.CompilerParams(
            dimension_semantics=("arbitrary", "arbitrary"),
            vmem_limit_bytes=_vmem_limit(est)),
        name="mlp",
    )(xf, g_in.reshape(1, D).astype(F32), w_up.astype(BF16), w_down.astype(BF16),
      g_out.reshape(1, D).astype(F32))


def _nsa_layer(xf, B, S, g_in, g_out, rel_table, w_in, cmp_pe, cmp_w1, cmp_w2, w_out):
    D = xf.shape[1]
    G, R, Dh = N_GROUPS, HEADS_PER_GROUP, HEAD_DIM
    n_main = N_HEADS * Dh + 6 * G * Dh
    n_gate = 3 * N_HEADS
    w_all = w_in
    w_gate = jnp.pad(w_in[:, n_main:], ((0, 0), (0, LANES - n_gate))).astype(BF16)
    colscale = jnp.concatenate([jnp.full((N_HEADS * Dh,), Dh ** -0.5 * LOG2_E, F32),
                                jnp.ones((6 * G * Dh,), F32)])[None]

    proj, glog = _matmul(xf, w_all, tm=1024, tn=1024, norm_g=g_in, epi="colscale",
                         colscale=colscale, side_w=w_gate, n_out=n_main, out_dtype=BF16,
                         name="nsa_proj")
    gates_t = (glog[:, :n_gate].reshape(B, S, 3, G, R).transpose(0, 3, 1, 2, 4)
               .reshape(B, G, S, 3 * R))

    proj3 = proj.reshape(B, S, n_main)
    half = CMP_STRIDE * Dh
    kcmp, vcmp = _compress(proj3, cmp_w1.reshape(2, 2, half, Dh).astype(BF16),
                           cmp_pe.reshape(2, 2, 1, half).astype(F32),
                           cmp_w2.astype(BF16), B, S)

    bias_c, bias_t, bias_far = _bias_tables(rel_table, S)
    attn = _nsa_attention(proj3, kcmp, vcmp, bias_c, bias_t, bias_far, gates_t, B, S)
    return _matmul(attn.reshape(B * S, D), w_out.astype(BF16), tm=512, tn=D,
                   epi="resnorm", res=xf, res_g=g_out, out_dtype=F32, name="nsa_out")


def _hgrn_layer(xf, B, S, layer, g_in, g_out, w_in, hgrn_lb, onorm, w_out):
    D = xf.shape[1]
    proj = _matmul(xf, w_in, tm=1024, tn=1024, norm_g=g_in,
                   out_dtype=BF16, name="hgrn_proj")
    mixed = _hgrn(proj.reshape(B, S, 4 * D), hgrn_lb, onorm, layer, B, S)
    return _matmul(mixed.reshape(B * S, D), w_out.astype(BF16), tm=512, tn=D,
                   epi="resnorm", res=xf, res_g=g_out, out_dtype=F32, name="hgrn_out")


def kernel(x, norm_g, rel_table, nsa_w_in, nsa_cmp_pe, nsa_cmp_w1, nsa_cmp_w2, nsa_w_out,
           hgrn_w_in, hgrn_lb, hgrn_onorm, hgrn_w_out, mlp_w_up, mlp_w_down):
    B, S, D = x.shape
    depth = norm_g.shape[0]
    assert D == N_HEADS * HEAD_DIM and S % ATT_TILE == 0 and S % HGRN_CHUNK == 0
    xf = x.reshape(B * S, D).astype(F32)
    for layer in range(depth):
        j = layer // 2
        if layer % 2 == 0:
            xf = _nsa_layer(xf, B, S, norm_g[layer, 0], norm_g[layer, 1], rel_table,
                            nsa_w_in[j], nsa_cmp_pe[j], nsa_cmp_w1[j], nsa_cmp_w2[j],
                            nsa_w_out[j])
        else:
            xf = _hgrn_layer(xf, B, S, layer, norm_g[layer, 0], norm_g[layer, 1],
                             hgrn_w_in[j], hgrn_lb, hgrn_onorm[j], hgrn_w_out[j])
        xf = _mlp(xf, norm_g[layer, 2], norm_g[layer, 3], mlp_w_up, mlp_w_down, layer)
    return xf.reshape(B, S, D).astype(x.dtype)
```

```python
import functools
import math

import numpy as np
import jax
import jax.numpy as jnp
from jax import lax
from jax.experimental import pallas as pl
from jax.experimental.pallas import tpu as pltpu

F32 = jnp.float32
BF16 = jnp.bfloat16

N_HEADS = 16
N_GROUPS = 4
HEADS_PER_GROUP = N_HEADS // N_GROUPS
HEAD_DIM = 128
CMP_BLOCK = 32
CMP_STRIDE = 16
SEL_BLOCK = 64
SEL_TOP_N = 8
WINDOW = 512
FORCE_SCORE = 1.0e4
REL_BUCKETS = 32
REL_MAX_DIST = 128
RMS_EPS = 1e-6
NEG_INF = -1.0e30
LOG2_E = math.log2(math.e)

LANES = 128
SUBLANES = 8
VMEM_BYTES_V7X = 64 * 1024 * 1024
VMEM_LIMIT_CAP = VMEM_BYTES_V7X - 8 * 1024 * 1024

ATT_TILE = 256
ATT_ROWS = 128
SEL_TILES_PER_STEP = 4
ATT_TILES_PER_STEP = 2
PREV_TILE, DIAG_TILE, EDGE_TILE = 0, 1, 2
N_BIAS_TILES = 3
HGRN_CHUNK = 128
HGRN_SUB = 8
HGRN_HEADS_PER_STEP = 8
NORM_ROWS = 256


VMEM_LIMIT_FLOOR = 32 * 1024 * 1024


def _vmem_limit(nbytes):
    return int(min(VMEM_LIMIT_CAP, max(VMEM_LIMIT_FLOOR, nbytes)))


def _mm_body(*refs, norm, epi, side, tm):
    it = iter(refs)
    x_ref = next(it)
    g_ref = next(it) if norm else None
    w_ref = next(it)
    ws_ref = next(it) if side else None
    cs_ref = next(it) if epi == "colscale" else None
    res_ref = next(it) if epi == "resnorm" else None
    go_ref = next(it) if epi == "resnorm" else None
    o_ref = next(it)
    os_ref = next(it) if side else None
    hn_ref = next(it) if norm else None

    j = pl.program_id(1)

    if norm:
        @pl.when(j == 0)
        def _():
            g = g_ref[...]

            def step(r, c):
                rows = pl.ds(pl.multiple_of(r * NORM_ROWS, NORM_ROWS), NORM_ROWS)
                xs = x_ref[rows, :]
                ms = jnp.mean(xs * xs, axis=-1, keepdims=True)
                hn_ref[rows, :] = (xs * lax.rsqrt(ms + RMS_EPS) * g).astype(BF16)
                return c

            lax.fori_loop(0, tm // NORM_ROWS, step, 0)

        lhs = hn_ref[...]
    else:
        lhs = x_ref[...]

    if side:
        @pl.when(j == 0)
        def _():
            os_ref[...] = jnp.dot(lhs, ws_ref[...], preferred_element_type=F32)

    acc = jnp.dot(lhs, w_ref[...].astype(BF16), preferred_element_type=F32)
    if epi == "colscale":
        acc = acc * cs_ref[...]
    elif epi == "resnorm":
        ms = jnp.mean(acc * acc, axis=-1, keepdims=True)
        acc = res_ref[...] + acc * lax.rsqrt(ms + RMS_EPS) * go_ref[...]
    o_ref[...] = acc.astype(o_ref.dtype)


def _matmul(x, w, *, tm, tn, norm_g=None, epi="none", colscale=None, res=None, res_g=None,
            side_w=None, n_out=None, out_dtype=BF16, name="mm"):
    M, K = x.shape
    N = w.shape[1] if n_out is None else n_out
    assert N <= w.shape[1]
    norm = norm_g is not None
    side = side_w is not None
    assert M % tm == 0 and N % tn == 0
    assert not (epi == "resnorm" and tn != N)

    in_specs = [pl.BlockSpec((tm, K), lambda i, j: (i, 0))]
    args = [x]
    if norm:
        in_specs.append(pl.BlockSpec((1, K), lambda i, j: (0, 0)))
        args.append(norm_g.reshape(1, K).astype(F32))
    in_specs.append(pl.BlockSpec((K, tn), lambda i, j: (0, j)))
    args.append(w)
    ns = side_w.shape[1] if side else 0
    if side:
        in_specs.append(pl.BlockSpec((K, ns), lambda i, j: (0, 0)))
        args.append(side_w)
    if epi == "colscale":
        in_specs.append(pl.BlockSpec((1, tn), lambda i, j: (0, j)))
        args.append(colscale)
    if epi == "resnorm":
        in_specs.append(pl.BlockSpec((tm, tn), lambda i, j: (i, j)))
        args.append(res)
        in_specs.append(pl.BlockSpec((1, tn), lambda i, j: (0, j)))
        args.append(res_g.reshape(1, N).astype(F32))

    out_shape = jax.ShapeDtypeStruct((M, N), out_dtype)
    out_specs = pl.BlockSpec((tm, tn), lambda i, j: (i, j))
    if side:
        out_shape = (out_shape, jax.ShapeDtypeStruct((M, ns), F32))
        out_specs = (out_specs, pl.BlockSpec((tm, ns), lambda i, j: (i, 0)))

    xb = x.dtype.itemsize
    ob = jnp.dtype(out_dtype).itemsize
    wb = w.dtype.itemsize
    est = (2 * tm * K * xb + 2 * K * tn * wb + (K * tn * 2 if wb > 2 else 0) + 2 * tm * tn * ob
           + (tm * K * 2 if norm else 0)
           + (2 * tm * tn * 4 if epi == "resnorm" else 0) + 3 * tm * tn * 4
           + 2 * K * ns * 2 + 3 * tm * ns * 4)

    return pl.pallas_call(
        functools.partial(_mm_body, norm=norm, epi=epi, side=side, tm=tm),
        out_shape=out_shape,
        grid=(M // tm, N // tn),
        in_specs=in_specs,
        out_specs=out_specs,
        scratch_shapes=[pltpu.VMEM((tm, K), BF16)] if norm else [],
        compiler_params=pltpu.CompilerParams(
            dimension_semantics=("arbitrary", "arbitrary"),
            vmem_limit_bytes=_vmem_limit(est)),
        name=name,
    )(*args)


def _rel_bucket_np(dist):
    n = np.maximum(dist, 0)
    max_exact = REL_BUCKETS // 2
    nf = np.maximum(n, 1).astype(np.float32)
    ratio = np.log(nf / np.float32(max_exact)) / np.float32(math.log(REL_MAX_DIST / max_exact))
    large = max_exact + (ratio * np.float32(REL_BUCKETS - max_exact)).astype(np.int32)
    large = np.minimum(large, REL_BUCKETS - 1)
    return np.where(n < max_exact, n, large).astype(np.int32)


@functools.lru_cache(maxsize=None)
def _static_maps(seq):
    n_cmp = LANES
    pos = np.arange(seq, dtype=np.int32)[:, None]
    c_end = np.arange(n_cmp, dtype=np.int32)[None, :] * CMP_STRIDE + CMP_BLOCK - 1
    bucket_c = _rel_bucket_np(pos - c_end)
    t = np.arange(ATT_TILE, dtype=np.int32)[:, None]
    k = np.arange(ATT_TILE, dtype=np.int32)[None, :]
    bucket_t = np.stack([_rel_bucket_np(t - k), _rel_bucket_np(ATT_TILE + t - k)])
    assert _rel_bucket_np(np.array([ATT_TILE + 1]))[0] == REL_BUCKETS - 1
    nc = seq // CMP_STRIDE - CMP_BLOCK // CMP_STRIDE + 1
    nb = seq // SEL_BLOCK
    c_start = np.arange(nc)[:, None] * CMP_STRIDE
    b_start = np.arange(nb)[None, :] * SEL_BLOCK
    ov = ((c_start <= b_start + SEL_BLOCK - 1) & (c_start + CMP_BLOCK - 1 >= b_start))
    overlap = np.zeros((LANES, LANES), np.float32)
    overlap[:nc, :nb] = ov
    return bucket_c, bucket_t, overlap


def _bias_body(tab_ref, bc_ref, bt_ref, oc_ref, ot_ref, of_ref, *, seq):
    h = pl.program_id(0)

    lane = lax.broadcasted_iota(jnp.int32, (SUBLANES, LANES), 1)
    tab_row = jnp.zeros((SUBLANES, LANES), F32)
    for b in range(REL_BUCKETS):
        tab_row = jnp.where(lane == b, tab_ref[b, h], tab_row)

    def lookup(bmap):
        rows = bmap.shape[0]
        tab = jnp.broadcast_to(tab_row[0:1], (rows, LANES))
        return jnp.concatenate(
            [jnp.take_along_axis(tab, bmap[:, c0:c0 + LANES], axis=1)
             for c0 in range(0, bmap.shape[1], LANES)], axis=1)

    def step(r, c):
        rows = pl.ds(pl.multiple_of(r * ATT_TILE, ATT_TILE), ATT_TILE)
        oc_ref[0, rows, :] = lookup(bc_ref[rows, :]) * LOG2_E
        return c

    lax.fori_loop(0, seq // ATT_TILE, step, 0)

    tt = lax.broadcasted_iota(jnp.int32, (ATT_TILE, ATT_TILE), 0)
    kk = lax.broadcasted_iota(jnp.int32, (ATT_TILE, ATT_TILE), 1)
    far = tab_ref[REL_BUCKETS - 1, h]
    of_ref[0] = jnp.full(of_ref.shape[1:], far * LOG2_E, F32)
    ot_ref[PREV_TILE, 0] = (lookup(bt_ref[1]) - far) * LOG2_E
    ot_ref[DIAG_TILE, 0] = jnp.where(kk <= tt, (lookup(bt_ref[0]) - far) * LOG2_E, NEG_INF)
    ot_ref[EDGE_TILE, 0] = jnp.where(kk > tt, 0.0, NEG_INF)


def _bias_tables(rel_table, seq):
    bucket_c, bucket_t, _ = _static_maps(seq)
    return pl.pallas_call(
        functools.partial(_bias_body, seq=seq),
        out_shape=(jax.ShapeDtypeStruct((N_HEADS, seq, LANES), F32),
                   jax.ShapeDtypeStruct((N_BIAS_TILES, N_HEADS, ATT_TILE, ATT_TILE), F32),
                   jax.ShapeDtypeStruct((N_HEADS, SUBLANES, LANES), F32)),
        grid=(N_HEADS,),
        in_specs=[pl.BlockSpec(memory_space=pltpu.SMEM),
                  pl.BlockSpec((seq, LANES), lambda h: (0, 0)),
                  pl.BlockSpec((2, ATT_TILE, ATT_TILE), lambda h: (0, 0, 0))],
        out_specs=(pl.BlockSpec((1, seq, LANES), lambda h: (h, 0, 0)),
                   pl.BlockSpec((N_BIAS_TILES, 1, ATT_TILE, ATT_TILE), lambda h: (0, h, 0, 0)),
                   pl.BlockSpec((1, SUBLANES, LANES), lambda h: (h, 0, 0))),
        compiler_params=pltpu.CompilerParams(dimension_semantics=("arbitrary",)),
        name="rel_bias",
    )(rel_table.astype(F32), jnp.asarray(bucket_c), jnp.asarray(bucket_t))


def _compress_body(xk_ref, xv_ref, w1_ref, pe_ref, w2_ref, ok_ref, ov_ref, x_sc):
    n_grp = xk_ref.shape[1] // CMP_STRIDE

    def one(x_ref, idx, o_ref):
        x_sc[...] = x_ref[0].astype(F32)
        x = jnp.concatenate([x_sc[pl.ds(t, n_grp, stride=CMP_STRIDE), :]
                             for t in range(CMP_STRIDE)], axis=1)
        a0 = jnp.dot((x + pe_ref[idx, 0]).astype(BF16), w1_ref[idx, 0],
                     preferred_element_type=F32)
        a1 = jnp.dot((x + pe_ref[idx, 1]).astype(BF16), w1_ref[idx, 1],
                     preferred_element_type=F32)
        pre = a0 + pltpu.roll(a1, LANES - 1, 0)
        hid = jax.nn.gelu(pre).astype(BF16)
        o_ref[0, 0] = jnp.dot(hid, w2_ref[idx], preferred_element_type=F32).astype(BF16)

    one(xk_ref, 0, ok_ref)
    one(xv_ref, 1, ov_ref)


def _compress(proj, w1, pe, w2, B, S):
    G = N_GROUPS
    assert S // CMP_STRIDE == LANES
    half = CMP_STRIDE * HEAD_DIM
    q_cols = N_HEADS

    def spec_x(slot):
        return pl.BlockSpec((1, S, HEAD_DIM), lambda b, g, slot=slot: (b, 0, q_cols + slot * G + g))

    spec_o = pl.BlockSpec((1, 1, LANES, HEAD_DIM), lambda b, g: (b, g, 0, 0))
    out = jax.ShapeDtypeStruct((B, G, LANES, HEAD_DIM), BF16)
    return pl.pallas_call(
        _compress_body,
        out_shape=(out, out),
        grid=(B, G),
        in_specs=[spec_x(0), spec_x(1),
                  pl.BlockSpec((2, 2, half, HEAD_DIM), lambda b, g: (0, 0, 0, 0)),
                  pl.BlockSpec((2, 2, 1, half), lambda b, g: (0, 0, 0, 0)),
                  pl.BlockSpec((2, HEAD_DIM, HEAD_DIM), lambda b, g: (0, 0, 0))],
        out_specs=(spec_o, spec_o),
        scratch_shapes=[pltpu.VMEM((S, HEAD_DIM), F32)],
        compiler_params=pltpu.CompilerParams(dimension_semantics=("arbitrary", "arbitrary")),
        name="nsa_compress",
    )(proj, proj, w1, pe, w2)


def _nsa_body(q_ref, ks_ref, vs_ref, kw_ref, vw_ref, kc_ref, vc_ref, bc_ref, bt_ref, far_ref,
              ovl_ref, augs_ref, augw_ref, gl_ref, o_ref, kts_sc, ktw_sc, qa_sc, oc_sc, m_sc,
              acc_sc):
    R = HEADS_PER_GROUP
    tq = ATT_TILE
    qi = pl.program_id(2)
    n_tiles = ks_ref.shape[1] // tq
    nb = ovl_ref.shape[0]
    nt = (((1,), (1,)), ((), ()))

    cidx = lax.broadcasted_iota(jnp.int32, (1, 1, LANES), 2)
    jb = lax.broadcasted_iota(jnp.int32, (nb, 1), 0)
    lane = lax.broadcasted_iota(jnp.int32, (1, LANES), 1)

    def select_tile(t):
        rows = pl.ds(pl.multiple_of(t * tq, tq), tq)
        q = q_ref[0, rows, :]
        q4 = jnp.concatenate([q[:, r * HEAD_DIM:(r + 1) * HEAD_DIM] for r in range(R)], axis=0)
        qa_sc[t, :, :HEAD_DIM] = q4
        pos3 = t * tq + lax.broadcasted_iota(jnp.int32, (1, tq, 1), 1)

        sc = lax.dot_general(q4, kc_ref[0, 0], nt, preferred_element_type=F32)
        sc = sc.reshape(R, tq, LANES) + bc_ref[:, rows, :]
        valid = (cidx * CMP_STRIDE + (CMP_BLOCK - 1) <= pos3) & (cidx < LANES - 1)
        sc = jnp.where(valid, sc, NEG_INF)
        mc = jnp.max(sc, axis=-1, keepdims=True)
        ec = jnp.exp2(sc - mc)
        pc = ec / jnp.sum(ec, axis=-1, keepdims=True)
        pc = jnp.where(pos3 >= CMP_BLOCK - 1, pc, 0.0)
        oc_sc[t] = jnp.dot(pc.reshape(R * tq, LANES).astype(BF16), vc_ref[0, 0],
                           preferred_element_type=F32)

        psum = pc[0]
        for r in range(1, R):
            psum = psum + pc[r]
        p_hi = psum.astype(BF16)
        p_lo = (psum - p_hi.astype(F32)).astype(BF16)
        ovt = ovl_ref[...]
        imp = (lax.dot_general(ovt, p_hi, nt, preferred_element_type=F32)
               + lax.dot_general(ovt, p_lo, nt, preferred_element_type=F32))
        pos_t = t * tq + lax.broadcasted_iota(jnp.int32, (1, tq), 1)
        q_blk = lax.shift_right_logical(pos_t, int(math.log2(SEL_BLOCK)))
        forced = (jb == 0) | (jb == q_blk) | (jb == q_blk - 1)
        future = jb > q_blk
        imp = jnp.where(forced, FORCE_SCORE, jnp.where(future, -1.0, imp))
        cnt = jnp.zeros((nb, tq), F32)
        for i in range(nb):
            row = imp[i:i + 1, :]
            beats = (row > imp) | ((row == imp) & (jb > i))
            cnt = cnt + jnp.where(beats, 1.0, 0.0)
        sel_t = jnp.where(cnt < float(min(SEL_TOP_N, nb)), 1.0, 0.0)
        sel = jnp.concatenate([sel_t, jnp.zeros((LANES - nb, tq), F32)], axis=0).T

        sel_pad = jnp.where(lane < nb, (sel - 1.0) * (-NEG_INF), 0.0)
        for r in range(R):
            far = jnp.broadcast_to(far_ref[r, 0:1, :], (tq, LANES))
            far_hi = far.astype(BF16).astype(F32)
            pad = jnp.where(lane == nb, far_hi, jnp.where(lane == nb + 1, far - far_hi, sel_pad))
            pad = jnp.where(lane == nb + 2, 1.0, pad)
            qa_sc[t, r * tq:(r + 1) * tq, HEAD_DIM:] = pad.astype(BF16)

    @pl.when(qi == 0)
    def _():
        def tr(j, c):
            rows = pl.ds(pl.multiple_of(j * tq, tq), tq)
            kts_sc[j, :HEAD_DIM, :] = ks_ref[0, rows, :].T
            kts_sc[j, HEAD_DIM:, :] = augs_ref[j]
            ktw_sc[j, :HEAD_DIM, :] = kw_ref[0, rows, :].T
            ktw_sc[j, HEAD_DIM:, :] = augw_ref[0]
            return c

        lax.fori_loop(0, n_tiles, tr, 0)
        zeros = jnp.zeros((HEAD_DIM, tq), BF16)
        kts_sc[n_tiles, :HEAD_DIM, :] = zeros
        kts_sc[n_tiles, HEAD_DIM:, :] = augs_ref[n_tiles]
        ktw_sc[n_tiles, :HEAD_DIM, :] = zeros
        ktw_sc[n_tiles, HEAD_DIM:, :] = augw_ref[1]

        def sel_group(gi, c):
            for u in range(SEL_TILES_PER_STEP):
                select_tile(gi * SEL_TILES_PER_STEP + u)
            return c

        lax.fori_loop(0, n_tiles // SEL_TILES_PER_STEP, sel_group, 0)

    ones = jnp.ones((tq, HEAD_DIM), BF16)
    n_chunks = R * tq // ATT_ROWS
    chunks_per_head = tq // ATT_ROWS
    dead = n_tiles
    step_tiles = [ATT_TILES_PER_STEP * qi + u for u in range(ATT_TILES_PER_STEP)]

    def keys(kt_sc, tiles):
        return jnp.concatenate([kt_sc[t] for t in tiles], axis=1)

    def values(v_ref, tiles):
        parts = []
        for t in tiles:
            rows = pl.ds(pl.multiple_of(t * tq, tq), tq)
            parts.append(jnp.concatenate([v_ref[0, rows, :], ones], axis=1))
        return jnp.concatenate(parts, axis=0)

    def logits(q_sc, ci, kt, kinds):
        r, hh = divmod(ci, chunks_per_head)
        rs = slice(ci * ATT_ROWS, (ci + 1) * ATT_ROWS)
        qs = slice(hh * ATT_ROWS, (hh + 1) * ATT_ROWS)
        s = jnp.dot(q_sc[rs, :], kt, preferred_element_type=F32)
        if all(kd is None for kd in kinds):
            return s
        parts = [s[:, i * tq:(i + 1) * tq] for i in range(len(kinds))]
        return jnp.concatenate([p if kd is None else p + bt_ref[kd, r, qs, :]
                                for p, kd in zip(parts, kinds)], axis=1)

    def probs(s, m):
        return jnp.concatenate([jnp.exp2(s[:, k0:k0 + LANES] - m)
                                for k0 in range(0, s.shape[1], LANES)], axis=1).astype(BF16)

    m_sc[...] = jnp.full(m_sc.shape, NEG_INF, F32)
    acc_sc[...] = jnp.zeros(acc_sc.shape, F32)

    def sel_update(u, t, kt, vv, kinds):
        for ci in range(n_chunks):
            rs = slice((u * n_chunks + ci) * ATT_ROWS, (u * n_chunks + ci + 1) * ATT_ROWS)
            s = logits(qa_sc.at[t], ci, kt, kinds)
            m_old = m_sc[rs]
            m_new = jnp.maximum(m_old, jnp.max(s, axis=-1, keepdims=True))
            alpha = jnp.exp2(m_old - m_new)
            m_sc[rs] = m_new
            pv = jnp.dot(probs(s, m_new), vv, preferred_element_type=F32)
            acc_sc[rs] = acc_sc[rs] * jnp.concatenate([alpha, alpha], axis=1) + pv

    assert ATT_TILES_PER_STEP == 2
    ta, tb = step_tiles

    def far_pair(pi, c):
        pair_tiles = (2 * pi, 2 * pi + 1)
        kt, vv = keys(kts_sc, pair_tiles), values(vs_ref, pair_tiles)
        sel_update(0, ta, kt, vv, (None, None))
        sel_update(1, tb, kt, vv, (None, None))
        return c

    lax.fori_loop(0, jnp.maximum(qi - 1, 0), far_pair, 0)

    @pl.when(qi >= 1)
    def _():
        pair_tiles = (ta - 2, ta - 1)
        kt, vv = keys(kts_sc, pair_tiles), values(vs_ref, pair_tiles)
        sel_update(0, ta, kt, vv, (None, PREV_TILE))
        sel_update(1, tb, kt, vv, (None, None))

    sel_update(0, ta, kts_sc[ta], values(vs_ref, (ta,)), (DIAG_TILE,))
    sel_update(1, tb, keys(kts_sc, (ta, tb)), values(vs_ref, (ta, tb)), (PREV_TILE, DIAG_TILE))

    n_win = WINDOW // tq
    w_kinds = (EDGE_TILE,) + (None,) * (n_win - 2) + (PREV_TILE, DIAG_TILE)
    gates = jax.nn.sigmoid(gl_ref[0, 0])
    for u, t in enumerate(step_tiles):
        acc = acc_sc[u * R * tq:(u + 1) * R * tq, :]
        o_sel = acc[:, :HEAD_DIM] / acc[:, HEAD_DIM:]

        w_tiles = [t - n_win + w for w in range(n_win + 1)]
        kt = keys(ktw_sc, [jnp.where(w >= 0, w, dead) for w in w_tiles])
        vv = values(vw_ref, [jnp.maximum(w, 0) for w in w_tiles])
        o_parts = []
        for ci in range(n_chunks):
            s = logits(qa_sc.at[t], ci, kt, w_kinds)
            pv = jnp.dot(probs(s, jnp.max(s, axis=-1, keepdims=True)), vv,
                         preferred_element_type=F32)
            o_parts.append(pv[:, :HEAD_DIM] / pv[:, HEAD_DIM:])
        o_win = jnp.concatenate(o_parts, axis=0)

        o_cmp = oc_sc[t]
        g = gates[u * tq:(u + 1) * tq]
        outs = []
        for r in range(R):
            hs = slice(r * tq, (r + 1) * tq)
            outs.append(g[:, r:r + 1] * o_cmp[hs] + g[:, R + r:R + r + 1] * o_sel[hs]
                        + g[:, 2 * R + r:2 * R + r + 1] * o_win[hs])
        o_ref[0, u * tq:(u + 1) * tq, :] = jnp.concatenate(outs, axis=1).astype(o_ref.dtype)


def _nsa_attention(proj, kcmp, vcmp, bias_c, bias_t, bias_far, gates_t, B, S):
    assert WINDOW % ATT_TILE == 0 and S % ATT_TILE == 0
    assert S // CMP_STRIDE == LANES and S // SEL_BLOCK <= LANES
    R, G, tq = HEADS_PER_GROUP, N_GROUPS, ATT_TILE
    n_tiles = S // tq
    ts = ATT_TILES_PER_STEP
    assert n_tiles % ts == 0
    nb = S // SEL_BLOCK
    _, _, overlap = _static_maps(S)
    overlap_t = np.ascontiguousarray(overlap.T[:nb])
    assert nb + 3 <= LANES and n_tiles % 2 == 0
    blk_of_key = (np.arange(S) // SEL_BLOCK).reshape(n_tiles, 1, tq)
    aug_sel = np.zeros((n_tiles + 1, LANES, tq), np.float32)
    aug_sel[:n_tiles] = np.arange(LANES).reshape(1, LANES, 1) == blk_of_key
    aug_sel[:n_tiles, nb:nb + 2, :] = 1.0
    aug_sel[n_tiles, nb + 2, :] = NEG_INF
    aug_win = np.zeros((2, LANES, tq), np.float32)
    aug_win[0, nb:nb + 2, :] = 1.0
    aug_win[1, nb + 2, :] = NEG_INF
    q_cols = N_HEADS

    def kv_spec(slot):
        return pl.BlockSpec((1, S, HEAD_DIM),
                            lambda b, g, i, slot=slot: (b, 0, q_cols + slot * G + g))

    cmp_spec = pl.BlockSpec((1, 1, LANES, HEAD_DIM), lambda b, g, i: (b, g, 0, 0))
    in_specs = [
        pl.BlockSpec((1, S, R * HEAD_DIM), lambda b, g, i: (b, 0, g)),
        kv_spec(2), kv_spec(3), kv_spec(4), kv_spec(5),
        cmp_spec, cmp_spec,
        pl.BlockSpec((R, S, LANES), lambda b, g, i: (g, 0, 0)),
        pl.BlockSpec((N_BIAS_TILES, R, tq, tq), lambda b, g, i: (0, g, 0, 0)),
        pl.BlockSpec((R, SUBLANES, LANES), lambda b, g, i: (g, 0, 0)),
        pl.BlockSpec((nb, LANES), lambda b, g, i: (0, 0)),
        pl.BlockSpec((n_tiles + 1, LANES, tq), lambda b, g, i: (0, 0, 0)),
        pl.BlockSpec((2, LANES, tq), lambda b, g, i: (0, 0, 0)),
        pl.BlockSpec((1, 1, ts * tq, 3 * R), lambda b, g, i: (b, g, i, 0)),
    ]
    kdim = HEAD_DIM + LANES
    assert n_tiles % SEL_TILES_PER_STEP == 0
    est = (2 * 4 * S * HEAD_DIM * 2 + 2 * N_BIAS_TILES * R * tq * tq * 4 + 2 * R * S * LANES * 4
           + 2 * S * R * HEAD_DIM * 2 + 2 * 2 * S * LANES * 2 + 2 * S * kdim * 2
           + R * S * (kdim * 2 + HEAD_DIM * 4) + ts * R * tq * (LANES * 4 + 2 * HEAD_DIM * 4)
           + 3 * R * tq * LANES * 4 + 16 * ATT_ROWS * tq * 4 + 4 * tq * R * HEAD_DIM * 2)
    assert est <= VMEM_LIMIT_CAP
    return pl.pallas_call(
        _nsa_body,
        out_shape=jax.ShapeDtypeStruct((B, S, N_HEADS * HEAD_DIM), BF16),
        grid=(B, G, n_tiles // ts),
        in_specs=in_specs,
        out_specs=pl.BlockSpec((1, ts * tq, R * HEAD_DIM), lambda b, g, i: (b, i, g)),
        scratch_shapes=[pltpu.VMEM((n_tiles + 1, kdim, tq), BF16),
                        pltpu.VMEM((n_tiles + 1, kdim, tq), BF16),
                        pltpu.VMEM((n_tiles, R * tq, kdim), BF16),
                        pltpu.VMEM((n_tiles, R * tq, HEAD_DIM), F32),
                        pltpu.VMEM((ts * R * tq, LANES), F32),
                        pltpu.VMEM((ts * R * tq, 2 * HEAD_DIM), F32)],
        compiler_params=pltpu.CompilerParams(
            dimension_semantics=("arbitrary", "arbitrary", "arbitrary"),
            vmem_limit_bytes=VMEM_LIMIT_CAP),
        name="nsa_attention",
    )(proj, proj, proj, proj, proj, kcmp, vcmp, bias_c, bias_t, bias_far,
      jnp.asarray(overlap_t, BF16), jnp.asarray(aug_sel, BF16), jnp.asarray(aug_win, BF16),
      gates_t)


@functools.lru_cache(maxsize=None)
def _hgrn_masks():
    C = HGRN_CHUNK
    t = np.arange(C)[:, None]
    s = np.arange(C)[None, :]
    masks = [(t // HGRN_SUB == s // HGRN_SUB) & (s <= t)]
    half = C // 2
    while half >= HGRN_SUB:
        grp = 2 * half
        masks.append((t // grp == s // grp) & (t % grp >= half) & (s % grp < half))
        half //= 2
    assert np.array_equal(np.sum(masks, axis=0), (s <= t).astype(int))
    return np.stack(masks).astype(np.float32), (s <= t).astype(np.float32)


def _hgrn_body(q_ref, f_ref, i_ref, g_ref, lb_ref, gn_ref, msk_ref, tril_ref, o_ref,
               st_all, b_all, k_all, q_all, *, layer, heads):
    C = HGRN_CHUNK
    n_chunks = q_ref.shape[1] // C
    nt = (((1,), (1,)), ((), ()))

    lbp = lb_ref[...]
    e = jnp.exp(lbp - jnp.max(lbp, axis=0, keepdims=True))
    sm = e / jnp.sum(e, axis=0, keepdims=True)
    cum = sm[0:1]
    first = cum
    for d in range(1, layer + 1):
        cum = cum + sm[d:d + 1]
    lb_all = cum - first
    log_lb_all = jnp.log(lb_all)
    log_1m_all = jnp.log1p(-lb_all)
    gn_all = gn_ref[...]

    cols = lax.broadcasted_iota(jnp.int32, (1, C), 1)
    tril = tril_ref[...]

    st_all[...] = jnp.zeros(st_all.shape, F32)

    def chunk(c, carry):
        for hh in range(heads):
            one_head(c, hh)
        return carry

    def one_head(c, hh):
        sl = pl.ds(pl.multiple_of(c * C, C), C)
        hs = slice(hh * HEAD_DIM, (hh + 1) * HEAD_DIM)
        lb, log_lb, log_1m, gn = lb_all[:, hs], log_lb_all[:, hs], log_1m_all[:, hs], gn_all[:, hs]
        st_ref, b_sc, k_sc, q_sc = st_all.at[hh], b_all.at[hh], k_all.at[hh], q_all.at[hh]
        qr = q_ref[0, sl, hs].astype(F32)
        x = f_ref[0, sl, hs].astype(F32)
        v = i_ref[0, sl, hs].astype(F32)
        gr = g_ref[0, sl, hs].astype(F32)
        q = qr / (1.0 + jnp.exp2(qr * (-LOG2_E)))
        ex = jnp.exp2(jnp.abs(x) * (-LOG2_E))
        u = 1.0 + ex
        r1 = 1.0 / u
        k = (1.0 - lb) * jnp.where(x >= 0.0, ex * r1, r1)
        c2 = log_1m + (jnp.minimum(x, 0.0) - jnp.log(u))
        e2 = jnp.exp2(jnp.abs(log_lb - c2) * (-LOG2_E))
        log_f = jnp.maximum(log_lb, c2) + jnp.log(1.0 + e2)

        lf_hi = log_f.astype(BF16)
        lf_lo = (log_f - lf_hi.astype(F32)).astype(BF16)
        b = (jnp.dot(tril, lf_hi, preferred_element_type=F32)
             + jnp.dot(tril, lf_lo, preferred_element_type=F32)) * LOG2_E
        b_sc[...] = b
        k_sc[...] = k
        q_sc[...] = q

        vb = v.astype(BF16)
        qb = q.astype(BF16)
        kb = k.astype(BF16)
        st = st_ref[...]
        o = lax.dot_general((q * jnp.exp2(b)).astype(BF16), st.astype(BF16), nt,
                            preferred_element_type=F32)

        pieces = []
        for blk in range(C // HGRN_SUB):
            r0 = blk * HGRN_SUB
            bt = b_sc[r0:r0 + HGRN_SUB, :]
            qt = q_sc[r0:r0 + HGRN_SUB, :]
            arow = jnp.zeros((HGRN_SUB, C), F32)
            for s in range(HGRN_SUB):
                bs = b_sc[r0 + s:r0 + s + 1, :]
                ks = k_sc[r0 + s:r0 + s + 1, :]
                col = jnp.sum(jnp.exp2(bt - bs) * qt * ks, axis=-1, keepdims=True)
                arow = jnp.where(cols == r0 + s, col, arow)
            pieces.append(arow)
        a = jnp.where(msk_ref[0] > 0.5, jnp.concatenate(pieces, axis=0), 0.0)

        half = C // 2
        lvl = 1
        while half >= HGRN_SUB:
            grp = 2 * half
            anc = jnp.concatenate(
                [jnp.broadcast_to(b_sc[g0 + half - 1:g0 + half, :], (grp, b.shape[1]))
                 for g0 in range(0, C, grp)], axis=0)
            e = jnp.exp2(-jnp.abs(b - anc)).astype(BF16)
            al = lax.dot_general(qb * e, kb * e, nt, preferred_element_type=F32)
            a = jnp.where(msk_ref[lvl] > 0.5, al, a)
            half //= 2
            lvl += 1

        o = o + jnp.dot(a.astype(BF16), vb, preferred_element_type=F32)

        b_last = b_sc[C - 1:C, :]
        kh = (k * jnp.exp2(b_last - b)).astype(BF16)
        st_ref[...] = st * jnp.exp2(b_last) + jnp.dot(v.T.astype(BF16), kh,
                                                     preferred_element_type=F32)

        ms = jnp.mean(o * o, axis=-1, keepdims=True)
        o = o * lax.rsqrt(ms + RMS_EPS) * gn * (gr / (1.0 + jnp.exp2(gr * (-LOG2_E))))
        o_ref[0, sl, hs] = o.astype(o_ref.dtype)

    lax.fori_loop(0, n_chunks, chunk, 0)


def _hgrn(proj, hgrn_lb, onorm, layer, B, S):
    H = N_HEADS
    depth = hgrn_lb.shape[0]

    nh = HGRN_HEADS_PER_STEP
    width = nh * HEAD_DIM
    steps = H // nh

    def spec(part):
        return pl.BlockSpec((1, S, width), lambda b, h, part=part: (b, 0, part * steps + h))

    masks, tril = _hgrn_masks()
    C = HGRN_CHUNK
    est = 2 * 4 * S * width * proj.dtype.itemsize + 2 * S * width * 2 + nh * 64 * C * C * 4
    return pl.pallas_call(
        functools.partial(_hgrn_body, layer=layer, heads=nh),
        out_shape=jax.ShapeDtypeStruct((B, S, H * HEAD_DIM), BF16),
        grid=(B, steps),
        in_specs=[spec(0), spec(1), spec(2), spec(3),
                  pl.BlockSpec((depth, width), lambda b, h: (0, h)),
                  pl.BlockSpec((1, width), lambda b, h: (0, h)),
                  pl.BlockSpec(masks.shape, lambda b, h: (0, 0, 0)),
                  pl.BlockSpec((C, C), lambda b, h: (0, 0))],
        out_specs=pl.BlockSpec((1, S, width), lambda b, h: (b, 0, h)),
        scratch_shapes=[pltpu.VMEM((nh, HEAD_DIM, HEAD_DIM), F32),
                        pltpu.VMEM((nh, C, HEAD_DIM), F32),
                        pltpu.VMEM((nh, C, HEAD_DIM), F32),
                        pltpu.VMEM((nh, C, HEAD_DIM), F32)],
        compiler_params=pltpu.CompilerParams(
            dimension_semantics=("arbitrary", "arbitrary"),
            vmem_limit_bytes=_vmem_limit(est)),
        name="hgrn2_recurrence",
    )(proj, proj, proj, proj, hgrn_lb.astype(F32), onorm.reshape(1, -1).astype(F32),
      jnp.asarray(masks), jnp.asarray(tril, BF16))


def _mlp_body(x_ref, gi_ref, wu_ref, wd_ref, go_ref, o_ref, hn_ref, *, nf, tm):
    f = pl.program_id(1)

    def row_chunks(fn):
        def step(r, c):
            fn(pl.ds(pl.multiple_of(r * NORM_ROWS, NORM_ROWS), NORM_ROWS))
            return c

        lax.fori_loop(0, tm // NORM_ROWS, step, 0)

    @pl.when(f == 0)
    def _():
        g = gi_ref[...]

        def norm_in(rows):
            xs = x_ref[rows, :]
            ms = jnp.mean(xs * xs, axis=-1, keepdims=True)
            hn_ref[rows, :] = (xs * lax.rsqrt(ms + RMS_EPS) * g).astype(BF16)

        row_chunks(norm_in)
        o_ref[...] = jnp.zeros(o_ref.shape, F32)

    hid = jnp.dot(hn_ref[...], wu_ref[...], preferred_element_type=F32)
    hid = jnp.square(jnp.maximum(hid, 0.0)).astype(BF16)
    o_ref[...] += jnp.dot(hid, wd_ref[...], preferred_element_type=F32)

    @pl.when(f == nf - 1)
    def _():
        g = go_ref[...]

        def norm_out(rows):
            y = o_ref[rows, :]
            ms = jnp.mean(y * y, axis=-1, keepdims=True)
            o_ref[rows, :] = x_ref[rows, :] + y * lax.rsqrt(ms + RMS_EPS) * g

        row_chunks(norm_out)


def _mlp(xf, g_in, g_out, w_up, w_down, layer, *, tm=512, tf=1024):
    M, D = xf.shape
    F = w_up.shape[2]
    assert M % tm == 0 and F % tf == 0 and tm % NORM_ROWS == 0
    nf = F // tf
    est = (2 * tm * D * 4 + 2 * tm * D * 4 + tm * D * 2 + 2 * 2 * D * tf * 2
           + tm * tf * 6 + 2 * tm * D * 4)
    return pl.pallas_call(
        functools.partial(_mlp_body, nf=nf, tm=tm),
        out_shape=jax.ShapeDtypeStruct((M, D), F32),
        grid=(M // tm, nf),
        in_specs=[pl.BlockSpec((tm, D), lambda i, f: (i, 0)),
                  pl.BlockSpec((1, D), lambda i, f: (0, 0)),
                  pl.BlockSpec((None, D, tf), lambda i, f: (layer, 0, f)),
                  pl.BlockSpec((None, tf, D), lambda i, f: (layer, f, 0)),
                  pl.BlockSpec((1, D), lambda i, f: (0, 0))],
        out_specs=pl.BlockSpec((tm, D), lambda i, f: (i, 0)),
        scratch_shapes=[pltpu.VMEM((tm, D), BF16)],
        compiler_params=pltpu.CompilerParams(
            dimension_semantics=("arbitrary", "arbitrary"),
            vmem_limit_bytes=_vmem_limit(est)),
        name="mlp",
    )(xf, g_in.reshape(1, D).astype(F32), w_up.astype(BF16), w_down.astype(BF16),
      g_out.reshape(1, D).astype(F32))


def _nsa_layer(xf, B, S, g_in, g_out, rel_table, w_in, cmp_pe, cmp_w1, cmp_w2, w_out):
    D = xf.shape[1]
    G, R, Dh = N_GROUPS, HEADS_PER_GROUP, HEAD_DIM
    n_main = N_HEADS * Dh + 6 * G * Dh
    n_gate = 3 * N_HEADS
    w_all = w_in.astype(BF16)
    w_gate = jnp.pad(w_in[:, n_main:], ((0, 0), (0, LANES - n_gate))).astype(BF16)
    colscale = jnp.concatenate([jnp.full((N_HEADS * Dh,), Dh ** -0.5 * LOG2_E, F32),
                                jnp.ones((6 * G * Dh,), F32)])[None]

    proj, glog = _matmul(xf, w_all, tm=1024, tn=1024, norm_g=g_in, epi="colscale",
                         colscale=colscale, side_w=w_gate, n_out=n_main, out_dtype=BF16,
                         name="nsa_proj")
    gates_t = (glog[:, :n_gate].reshape(B, S, 3, G, R).transpose(0, 3, 1, 2, 4)
               .reshape(B, G, S, 3 * R))

    proj3 = proj.reshape(B, S, n_main)
    half = CMP_STRIDE * Dh
    kcmp, vcmp = _compress(proj3, cmp_w1.reshape(2, 2, half, Dh).astype(BF16),
                           cmp_pe.reshape(2, 2, 1, half).astype(F32),
                           cmp_w2.astype(BF16), B, S)

    bias_c, bias_t, bias_far = _bias_tables(rel_table, S)
    attn = _nsa_attention(proj3, kcmp, vcmp, bias_c, bias_t, bias_far, gates_t, B, S)
    return _matmul(attn.reshape(B * S, D), w_out.astype(BF16), tm=512, tn=D,
                   epi="resnorm", res=xf, res_g=g_out, out_dtype=F32, name="nsa_out")


def _hgrn_layer(xf, B, S, layer, g_in, g_out, w_in, hgrn_lb, onorm, w_out):
    D = xf.shape[1]
    proj = _matmul(xf, w_in, tm=1024, tn=1024, norm_g=g_in,
                   out_dtype=BF16, name="hgrn_proj")
    mixed = _hgrn(proj.reshape(B, S, 4 * D), hgrn_lb, onorm, layer, B, S)
    return _matmul(mixed.reshape(B * S, D), w_out.astype(BF16), tm=512, tn=D,
                   epi="resnorm", res=xf, res_g=g_out, out_dtype=F32, name="hgrn_out")


def kernel(x, norm_g, rel_table, nsa_w_in, nsa_cmp_pe, nsa_cmp_w1, nsa_cmp_w2, nsa_w_out,
           hgrn_w_in, hgrn_lb, hgrn_onorm, hgrn_w_out, mlp_w_up, mlp_w_down):
    B, S, D = x.shape
    depth = norm_g.shape[0]
    assert D == N_HEADS * HEAD_DIM and S % ATT_TILE == 0 and S % HGRN_CHUNK == 0
    xf = x.reshape(B * S, D).astype(F32)
    for layer in range(depth):
        j = layer // 2
        if layer % 2 == 0:
            xf = _nsa_layer(xf, B, S, norm_g[layer, 0], norm_g[layer, 1], rel_table,
                            nsa_w_in[j], nsa_cmp_pe[j], nsa_cmp_w1[j], nsa_cmp_w2[j],
                            nsa_w_out[j])
        else:
            xf = _hgrn_layer(xf, B, S, layer, norm_g[layer, 0], norm_g[layer, 1],
                             hgrn_w_in[j], hgrn_lb, hgrn_onorm[j], hgrn_w_out[j])
        xf = _mlp(xf, norm_g[layer, 2], norm_g[layer, 3], mlp_w_up, mlp_w_down, layer)
    return xf.reshape(B, S, D).astype(x.dtype)
```

```python
import functools
import math

import numpy as np
import jax
import jax.numpy as jnp
from jax import lax
from jax.experimental import pallas as pl
from jax.experimental.pallas import tpu as pltpu

F32 = jnp.float32
BF16 = jnp.bfloat16

N_HEADS = 16
N_GROUPS = 4
HEADS_PER_GROUP = N_HEADS // N_GROUPS
HEAD_DIM = 128
CMP_BLOCK = 32
CMP_STRIDE = 16
SEL_BLOCK = 64
SEL_TOP_N = 8
WINDOW = 512
FORCE_SCORE = 1.0e4
REL_BUCKETS = 32
REL_MAX_DIST = 128
RMS_EPS = 1e-6
NEG_INF = -1.0e30
LOG2_E = math.log2(math.e)

LANES = 128
SUBLANES = 8
VMEM_BYTES_V7X = 64 * 1024 * 1024
VMEM_LIMIT_CAP = VMEM_BYTES_V7X - 8 * 1024 * 1024

ATT_TILE = 256
ATT_ROWS = 128
SEL_TILES_PER_STEP = 4
ATT_TILES_PER_STEP = 2
PREV_TILE, DIAG_TILE, EDGE_TILE = 0, 1, 2
N_BIAS_TILES = 3
HGRN_CHUNK = 128
HGRN_SUB = 8
HGRN_HEADS_PER_STEP = 16
HGRN_SEQ_BLOCKS = 2
NORM_ROWS = 256


VMEM_LIMIT_FLOOR = 32 * 1024 * 1024


def _vmem_limit(nbytes):
    return int(min(VMEM_LIMIT_CAP, max(VMEM_LIMIT_FLOOR, nbytes)))


def _mm_body(*refs, norm, epi, side, tm):
    it = iter(refs)
    x_ref = next(it)
    g_ref = next(it) if norm else None
    w_ref = next(it)
    ws_ref = next(it) if side else None
    cs_ref = next(it) if epi == "colscale" else None
    res_ref = next(it) if epi == "resnorm" else None
    go_ref = next(it) if epi == "resnorm" else None
    o_ref = next(it)
    os_ref = next(it) if side else None
    hn_ref = next(it) if norm else None

    j = pl.program_id(1)

    if norm:
        @pl.when(j == 0)
        def _():
            g = g_ref[...]

            def step(r, c):
                rows = pl.ds(pl.multiple_of(r * NORM_ROWS, NORM_ROWS), NORM_ROWS)
                xs = x_ref[rows, :]
                ms = jnp.mean(xs * xs, axis=-1, keepdims=True)
                hn_ref[rows, :] = (xs * lax.rsqrt(ms + RMS_EPS) * g).astype(BF16)
                return c

            lax.fori_loop(0, tm // NORM_ROWS, step, 0)

        lhs = hn_ref[...]
    else:
        lhs = x_ref[...]

    if side:
        @pl.when(j == 0)
        def _():
            os_ref[...] = jnp.dot(lhs, ws_ref[...], preferred_element_type=F32)

    acc = jnp.dot(lhs, w_ref[...].astype(BF16), preferred_element_type=F32)
    if epi == "colscale":
        acc = acc * cs_ref[...]
    elif epi == "resnorm":
        ms = jnp.mean(acc * acc, axis=-1, keepdims=True)
        acc = res_ref[...] + acc * lax.rsqrt(ms + RMS_EPS) * go_ref[...]
    o_ref[...] = acc.astype(o_ref.dtype)


def _matmul(x, w, *, tm, tn, norm_g=None, epi="none", colscale=None, res=None, res_g=None,
            side_w=None, n_out=None, out_dtype=BF16, name="mm"):
    M, K = x.shape
    N = w.shape[1] if n_out is None else n_out
    assert N <= w.shape[1]
    norm = norm_g is not None
    side = side_w is not None
    assert M % tm == 0 and N % tn == 0
    assert not (epi == "resnorm" and tn != N)

    in_specs = [pl.BlockSpec((tm, K), lambda i, j: (i, 0))]
    args = [x]
    if norm:
        in_specs.append(pl.BlockSpec((1, K), lambda i, j: (0, 0)))
        args.append(norm_g.reshape(1, K).astype(F32))
    in_specs.append(pl.BlockSpec((K, tn), lambda i, j: (0, j)))
    args.append(w)
    ns = side_w.shape[1] if side else 0
    if side:
        in_specs.append(pl.BlockSpec((K, ns), lambda i, j: (0, 0)))
        args.append(side_w)
    if epi == "colscale":
        in_specs.append(pl.BlockSpec((1, tn), lambda i, j: (0, j)))
        args.append(colscale)
    if epi == "resnorm":
        in_specs.append(pl.BlockSpec((tm, tn), lambda i, j: (i, j)))
        args.append(res)
        in_specs.append(pl.BlockSpec((1, tn), lambda i, j: (0, j)))
        args.append(res_g.reshape(1, N).astype(F32))

    out_shape = jax.ShapeDtypeStruct((M, N), out_dtype)
    out_specs = pl.BlockSpec((tm, tn), lambda i, j: (i, j))
    if side:
        out_shape = (out_shape, jax.ShapeDtypeStruct((M, ns), F32))
        out_specs = (out_specs, pl.BlockSpec((tm, ns), lambda i, j: (i, 0)))

    xb = x.dtype.itemsize
    ob = jnp.dtype(out_dtype).itemsize
    wb = w.dtype.itemsize
    est = (2 * tm * K * xb + 2 * K * tn * wb + (K * tn * 2 if wb > 2 else 0) + 2 * tm * tn * ob
           + (tm * K * 2 if norm else 0)
           + (2 * tm * tn * 4 if epi == "resnorm" else 0) + 3 * tm * tn * 4
           + 2 * K * ns * 2 + 3 * tm * ns * 4)

    return pl.pallas_call(
        functools.partial(_mm_body, norm=norm, epi=epi, side=side, tm=tm),
        out_shape=out_shape,
        grid=(M // tm, N // tn),
        in_specs=in_specs,
        out_specs=out_specs,
        scratch_shapes=[pltpu.VMEM((tm, K), BF16)] if norm else [],
        compiler_params=pltpu.CompilerParams(
            dimension_semantics=("arbitrary", "arbitrary"),
            vmem_limit_bytes=_vmem_limit(est)),
        name=name,
    )(*args)


def _rel_bucket_np(dist):
    n = np.maximum(dist, 0)
    max_exact = REL_BUCKETS // 2
    nf = np.maximum(n, 1).astype(np.float32)
    ratio = np.log(nf / np.float32(max_exact)) / np.float32(math.log(REL_MAX_DIST / max_exact))
    large = max_exact + (ratio * np.float32(REL_BUCKETS - max_exact)).astype(np.int32)
    large = np.minimum(large, REL_BUCKETS - 1)
    return np.where(n < max_exact, n, large).astype(np.int32)


@functools.lru_cache(maxsize=None)
def _static_maps(seq):
    n_cmp = LANES
    pos = np.arange(seq, dtype=np.int32)[:, None]
    c_end = np.arange(n_cmp, dtype=np.int32)[None, :] * CMP_STRIDE + CMP_BLOCK - 1
    bucket_c = _rel_bucket_np(pos - c_end)
    t = np.arange(ATT_TILE, dtype=np.int32)[:, None]
    k = np.arange(ATT_TILE, dtype=np.int32)[None, :]
    bucket_t = np.stack([_rel_bucket_np(t - k), _rel_bucket_np(ATT_TILE + t - k)])
    assert _rel_bucket_np(np.array([ATT_TILE + 1]))[0] == REL_BUCKETS - 1
    nc = seq // CMP_STRIDE - CMP_BLOCK // CMP_STRIDE + 1
    nb = seq // SEL_BLOCK
    c_start = np.arange(nc)[:, None] * CMP_STRIDE
    b_start = np.arange(nb)[None, :] * SEL_BLOCK
    ov = ((c_start <= b_start + SEL_BLOCK - 1) & (c_start + CMP_BLOCK - 1 >= b_start))
    overlap = np.zeros((LANES, LANES), np.float32)
    overlap[:nc, :nb] = ov
    return bucket_c, bucket_t, overlap


def _bias_body(tab_ref, bc_ref, bt_ref, oc_ref, ot_ref, of_ref, *, seq):
    h = pl.program_id(0)

    lane = lax.broadcasted_iota(jnp.int32, (SUBLANES, LANES), 1)
    tab_row = jnp.zeros((SUBLANES, LANES), F32)
    for b in range(REL_BUCKETS):
        tab_row = jnp.where(lane == b, tab_ref[b, h], tab_row)

    def lookup(bmap):
        rows = bmap.shape[0]
        tab = jnp.broadcast_to(tab_row[0:1], (rows, LANES))
        return jnp.concatenate(
            [jnp.take_along_axis(tab, bmap[:, c0:c0 + LANES], axis=1)
             for c0 in range(0, bmap.shape[1], LANES)], axis=1)

    def step(r, c):
        rows = pl.ds(pl.multiple_of(r * ATT_TILE, ATT_TILE), ATT_TILE)
        oc_ref[0, rows, :] = lookup(bc_ref[rows, :]) * LOG2_E
        return c

    lax.fori_loop(0, seq // ATT_TILE, step, 0)

    tt = lax.broadcasted_iota(jnp.int32, (ATT_TILE, ATT_TILE), 0)
    kk = lax.broadcasted_iota(jnp.int32, (ATT_TILE, ATT_TILE), 1)
    far = tab_ref[REL_BUCKETS - 1, h]
    of_ref[0] = jnp.full(of_ref.shape[1:], far * LOG2_E, F32)
    ot_ref[PREV_TILE, 0] = (lookup(bt_ref[1]) - far) * LOG2_E
    ot_ref[DIAG_TILE, 0] = jnp.where(kk <= tt, (lookup(bt_ref[0]) - far) * LOG2_E, NEG_INF)
    ot_ref[EDGE_TILE, 0] = jnp.where(kk > tt, 0.0, NEG_INF)


def _bias_tables(rel_table, seq):
    bucket_c, bucket_t, _ = _static_maps(seq)
    return pl.pallas_call(
        functools.partial(_bias_body, seq=seq),
        out_shape=(jax.ShapeDtypeStruct((N_HEADS, seq, LANES), F32),
                   jax.ShapeDtypeStruct((N_BIAS_TILES, N_HEADS, ATT_TILE, ATT_TILE), F32),
                   jax.ShapeDtypeStruct((N_HEADS, SUBLANES, LANES), F32)),
        grid=(N_HEADS,),
        in_specs=[pl.BlockSpec(memory_space=pltpu.SMEM),
                  pl.BlockSpec((seq, LANES), lambda h: (0, 0)),
                  pl.BlockSpec((2, ATT_TILE, ATT_TILE), lambda h: (0, 0, 0))],
        out_specs=(pl.BlockSpec((1, seq, LANES), lambda h: (h, 0, 0)),
                   pl.BlockSpec((N_BIAS_TILES, 1, ATT_TILE, ATT_TILE), lambda h: (0, h, 0, 0)),
                   pl.BlockSpec((1, SUBLANES, LANES), lambda h: (h, 0, 0))),
        compiler_params=pltpu.CompilerParams(dimension_semantics=("arbitrary",)),
        name="rel_bias",
    )(rel_table.astype(F32), jnp.asarray(bucket_c), jnp.asarray(bucket_t))


def _compress_body(xk_ref, xv_ref, w1_ref, pe_ref, w2_ref, ok_ref, ov_ref, x_sc):
    n_grp = xk_ref.shape[1] // CMP_STRIDE

    def one(x_ref, idx, o_ref):
        x_sc[...] = x_ref[0].astype(F32)
        x = jnp.concatenate([x_sc[pl.ds(t, n_grp, stride=CMP_STRIDE), :]
                             for t in range(CMP_STRIDE)], axis=1)
        a0 = jnp.dot((x + pe_ref[idx, 0]).astype(BF16), w1_ref[idx, 0],
                     preferred_element_type=F32)
        a1 = jnp.dot((x + pe_ref[idx, 1]).astype(BF16), w1_ref[idx, 1],
                     preferred_element_type=F32)
        pre = a0 + pltpu.roll(a1, LANES - 1, 0)
        hid = jax.nn.gelu(pre).astype(BF16)
        o_ref[0, 0] = jnp.dot(hid, w2_ref[idx], preferred_element_type=F32).astype(BF16)

    one(xk_ref, 0, ok_ref)
    one(xv_ref, 1, ov_ref)


def _compress(proj, w1, pe, w2, B, S):
    G = N_GROUPS
    assert S // CMP_STRIDE == LANES
    half = CMP_STRIDE * HEAD_DIM
    q_cols = N_HEADS

    def spec_x(slot):
        return pl.BlockSpec((1, S, HEAD_DIM), lambda b, g, slot=slot: (b, 0, q_cols + slot * G + g))

    spec_o = pl.BlockSpec((1, 1, LANES, HEAD_DIM), lambda b, g: (b, g, 0, 0))
    out = jax.ShapeDtypeStruct((B, G, LANES, HEAD_DIM), BF16)
    return pl.pallas_call(
        _compress_body,
        out_shape=(out, out),
        grid=(B, G),
        in_specs=[spec_x(0), spec_x(1),
                  pl.BlockSpec((2, 2, half, HEAD_DIM), lambda b, g: (0, 0, 0, 0)),
                  pl.BlockSpec((2, 2, 1, half), lambda b, g: (0, 0, 0, 0)),
                  pl.BlockSpec((2, HEAD_DIM, HEAD_DIM), lambda b, g: (0, 0, 0))],
        out_specs=(spec_o, spec_o),
        scratch_shapes=[pltpu.VMEM((S, HEAD_DIM), F32)],
        compiler_params=pltpu.CompilerParams(dimension_semantics=("arbitrary", "arbitrary")),
        name="nsa_compress",
    )(proj, proj, w1, pe, w2)


def _nsa_body(q_ref, ks_ref, vs_ref, kw_ref, vw_ref, kc_ref, vc_ref, bc_ref, bt_ref, far_ref,
              ovl_ref, augs_ref, augw_ref, gl_ref, o_ref, kts_sc, ktw_sc, qa_sc, oc_sc, m_sc,
              acc_sc):
    R = HEADS_PER_GROUP
    tq = ATT_TILE
    qi = pl.program_id(2)
    n_tiles = ks_ref.shape[1] // tq
    nb = ovl_ref.shape[0]
    nt = (((1,), (1,)), ((), ()))

    cidx = lax.broadcasted_iota(jnp.int32, (1, 1, LANES), 2)
    jb = lax.broadcasted_iota(jnp.int32, (nb, 1), 0)
    lane = lax.broadcasted_iota(jnp.int32, (1, LANES), 1)

    def select_tile(t):
        rows = pl.ds(pl.multiple_of(t * tq, tq), tq)
        q = q_ref[0, rows, :]
        q4 = jnp.concatenate([q[:, r * HEAD_DIM:(r + 1) * HEAD_DIM] for r in range(R)], axis=0)
        qa_sc[t, :, :HEAD_DIM] = q4
        pos3 = t * tq + lax.broadcasted_iota(jnp.int32, (1, tq, 1), 1)

        sc = lax.dot_general(q4, kc_ref[0, 0], nt, preferred_element_type=F32)
        sc = sc.reshape(R, tq, LANES) + bc_ref[:, rows, :]
        valid = (cidx * CMP_STRIDE + (CMP_BLOCK - 1) <= pos3) & (cidx < LANES - 1)
        sc = jnp.where(valid, sc, NEG_INF)
        mc = jnp.max(sc, axis=-1, keepdims=True)
        ec = jnp.exp2(sc - mc)
        pc = ec / jnp.sum(ec, axis=-1, keepdims=True)
        pc = jnp.where(pos3 >= CMP_BLOCK - 1, pc, 0.0)
        oc_sc[t] = jnp.dot(pc.reshape(R * tq, LANES).astype(BF16), vc_ref[0, 0],
                           preferred_element_type=F32)

        psum = pc[0]
        for r in range(1, R):
            psum = psum + pc[r]
        p_hi = psum.astype(BF16)
        p_lo = (psum - p_hi.astype(F32)).astype(BF16)
        ovt = ovl_ref[...]
        imp = (lax.dot_general(ovt, p_hi, nt, preferred_element_type=F32)
               + lax.dot_general(ovt, p_lo, nt, preferred_element_type=F32))
        pos_t = t * tq + lax.broadcasted_iota(jnp.int32, (1, tq), 1)
        q_blk = lax.shift_right_logical(pos_t, int(math.log2(SEL_BLOCK)))
        forced = (jb == 0) | (jb == q_blk) | (jb == q_blk - 1)
        future = jb > q_blk
        imp = jnp.where(forced, FORCE_SCORE, jnp.where(future, -1.0, imp))
        cnt = jnp.zeros((nb, tq), F32)
        for i in range(nb):
            row = imp[i:i + 1, :]
            beats = (row > imp) | ((row == imp) & (jb > i))
            cnt = cnt + jnp.where(beats, 1.0, 0.0)
        sel_t = jnp.where(cnt < float(min(SEL_TOP_N, nb)), 1.0, 0.0)
        sel = jnp.concatenate([sel_t, jnp.zeros((LANES - nb, tq), F32)], axis=0).T

        sel_pad = jnp.where(lane < nb, (sel - 1.0) * (-NEG_INF), 0.0)
        for r in range(R):
            far = jnp.broadcast_to(far_ref[r, 0:1, :], (tq, LANES))
            far_hi = far.astype(BF16).astype(F32)
            pad = jnp.where(lane == nb, far_hi, jnp.where(lane == nb + 1, far - far_hi, sel_pad))
            pad = jnp.where(lane == nb + 2, 1.0, pad)
            qa_sc[t, r * tq:(r + 1) * tq, HEAD_DIM:] = pad.astype(BF16)

    @pl.when(qi == 0)
    def _():
        def tr(j, c):
            rows = pl.ds(pl.multiple_of(j * tq, tq), tq)
            kts_sc[j, :HEAD_DIM, :] = ks_ref[0, rows, :].T
            kts_sc[j, HEAD_DIM:, :] = augs_ref[j]
            ktw_sc[j, :HEAD_DIM, :] = kw_ref[0, rows, :].T
            ktw_sc[j, HEAD_DIM:, :] = augw_ref[0]
            return c

        lax.fori_loop(0, n_tiles, tr, 0)
        zeros = jnp.zeros((HEAD_DIM, tq), BF16)
        kts_sc[n_tiles, :HEAD_DIM, :] = zeros
        kts_sc[n_tiles, HEAD_DIM:, :] = augs_ref[n_tiles]
        ktw_sc[n_tiles, :HEAD_DIM, :] = zeros
        ktw_sc[n_tiles, HEAD_DIM:, :] = augw_ref[1]

        def sel_group(gi, c):
            for u in range(SEL_TILES_PER_STEP):
                select_tile(gi * SEL_TILES_PER_STEP + u)
            return c

        lax.fori_loop(0, n_tiles // SEL_TILES_PER_STEP, sel_group, 0)

    ones = jnp.ones((tq, HEAD_DIM), BF16)
    n_chunks = R * tq // ATT_ROWS
    chunks_per_head = tq // ATT_ROWS
    dead = n_tiles
    step_tiles = [ATT_TILES_PER_STEP * qi + u for u in range(ATT_TILES_PER_STEP)]

    def keys(kt_sc, tiles):
        return jnp.concatenate([kt_sc[t] for t in tiles], axis=1)

    def values(v_ref, tiles):
        parts = []
        for t in tiles:
            rows = pl.ds(pl.multiple_of(t * tq, tq), tq)
            parts.append(jnp.concatenate([v_ref[0, rows, :], ones], axis=1))
        return jnp.concatenate(parts, axis=0)

    def logits(q_sc, ci, kt, kinds):
        r, hh = divmod(ci, chunks_per_head)
        rs = slice(ci * ATT_ROWS, (ci + 1) * ATT_ROWS)
        qs = slice(hh * ATT_ROWS, (hh + 1) * ATT_ROWS)
        s = jnp.dot(q_sc[rs, :], kt, preferred_element_type=F32)
        if all(kd is None for kd in kinds):
            return s
        parts = [s[:, i * tq:(i + 1) * tq] for i in range(len(kinds))]
        return jnp.concatenate([p if kd is None else p + bt_ref[kd, r, qs, :]
                                for p, kd in zip(parts, kinds)], axis=1)

    def probs(s, m):
        return jnp.concatenate([jnp.exp2(s[:, k0:k0 + LANES] - m)
                                for k0 in range(0, s.shape[1], LANES)], axis=1).astype(BF16)

    m_sc[...] = jnp.full(m_sc.shape, NEG_INF, F32)
    acc_sc[...] = jnp.zeros(acc_sc.shape, F32)

    def sel_update(u, t, kt, vv, kinds):
        for ci in range(n_chunks):
            rs = slice((u * n_chunks + ci) * ATT_ROWS, (u * n_chunks + ci + 1) * ATT_ROWS)
            s = logits(qa_sc.at[t], ci, kt, kinds)
            m_old = m_sc[rs]
            m_new = jnp.maximum(m_old, jnp.max(s, axis=-1, keepdims=True))
            alpha = jnp.exp2(m_old - m_new)
            m_sc[rs] = m_new
            pv = jnp.dot(probs(s, m_new), vv, preferred_element_type=F32)
            acc_sc[rs] = acc_sc[rs] * jnp.concatenate([alpha, alpha], axis=1) + pv

    assert ATT_TILES_PER_STEP == 2
    ta, tb = step_tiles

    def far_pair(pi, c):
        pair_tiles = (2 * pi, 2 * pi + 1)
        kt, vv = keys(kts_sc, pair_tiles), values(vs_ref, pair_tiles)
        sel_update(0, ta, kt, vv, (None, None))
        sel_update(1, tb, kt, vv, (None, None))
        return c

    lax.fori_loop(0, jnp.maximum(qi - 1, 0), far_pair, 0)

    @pl.when(qi >= 1)
    def _():
        pair_tiles = (ta - 2, ta - 1)
        kt, vv = keys(kts_sc, pair_tiles), values(vs_ref, pair_tiles)
        sel_update(0, ta, kt, vv, (None, PREV_TILE))
        sel_update(1, tb, kt, vv, (None, None))

    sel_update(0, ta, kts_sc[ta], values(vs_ref, (ta,)), (DIAG_TILE,))
    sel_update(1, tb, keys(kts_sc, (ta, tb)), values(vs_ref, (ta, tb)), (PREV_TILE, DIAG_TILE))

    n_win = WINDOW // tq
    w_kinds = (EDGE_TILE,) + (None,) * (n_win - 2) + (PREV_TILE, DIAG_TILE)
    gates = jax.nn.sigmoid(gl_ref[0, 0])
    for u, t in enumerate(step_tiles):
        acc = acc_sc[u * R * tq:(u + 1) * R * tq, :]
        o_sel = acc[:, :HEAD_DIM] / acc[:, HEAD_DIM:]

        w_tiles = [t - n_win + w for w in range(n_win + 1)]
        kt = keys(ktw_sc, [jnp.where(w >= 0, w, dead) for w in w_tiles])
        vv = values(vw_ref, [jnp.maximum(w, 0) for w in w_tiles])
        o_parts = []
        for ci in range(n_chunks):
            s = logits(qa_sc.at[t], ci, kt, w_kinds)
            pv = jnp.dot(probs(s, jnp.max(s, axis=-1, keepdims=True)), vv,
                         preferred_element_type=F32)
            o_parts.append(pv[:, :HEAD_DIM] / pv[:, HEAD_DIM:])
        o_win = jnp.concatenate(o_parts, axis=0)

        o_cmp = oc_sc[t]
        g = gates[u * tq:(u + 1) * tq]
        outs = []
        for r in range(R):
            hs = slice(r * tq, (r + 1) * tq)
            outs.append(g[:, r:r + 1] * o_cmp[hs] + g[:, R + r:R + r + 1] * o_sel[hs]
                        + g[:, 2 * R + r:2 * R + r + 1] * o_win[hs])
        o_ref[0, u * tq:(u + 1) * tq, :] = jnp.concatenate(outs, axis=1).astype(o_ref.dtype)


def _nsa_attention(proj, kcmp, vcmp, bias_c, bias_t, bias_far, gates_t, B, S):
    assert WINDOW % ATT_TILE == 0 and S % ATT_TILE == 0
    assert S // CMP_STRIDE == LANES and S // SEL_BLOCK <= LANES
    R, G, tq = HEADS_PER_GROUP, N_GROUPS, ATT_TILE
    n_tiles = S // tq
    ts = ATT_TILES_PER_STEP
    assert n_tiles % ts == 0
    nb = S // SEL_BLOCK
    _, _, overlap = _static_maps(S)
    overlap_t = np.ascontiguousarray(overlap.T[:nb])
    assert nb + 3 <= LANES and n_tiles % 2 == 0
    blk_of_key = (np.arange(S) // SEL_BLOCK).reshape(n_tiles, 1, tq)
    aug_sel = np.zeros((n_tiles + 1, LANES, tq), np.float32)
    aug_sel[:n_tiles] = np.arange(LANES).reshape(1, LANES, 1) == blk_of_key
    aug_sel[:n_tiles, nb:nb + 2, :] = 1.0
    aug_sel[n_tiles, nb + 2, :] = NEG_INF
    aug_win = np.zeros((2, LANES, tq), np.float32)
    aug_win[0, nb:nb + 2, :] = 1.0
    aug_win[1, nb + 2, :] = NEG_INF
    q_cols = N_HEADS

    def kv_spec(slot):
        return pl.BlockSpec((1, S, HEAD_DIM),
                            lambda b, g, i, slot=slot: (b, 0, q_cols + slot * G + g))

    cmp_spec = pl.BlockSpec((1, 1, LANES, HEAD_DIM), lambda b, g, i: (b, g, 0, 0))
    in_specs = [
        pl.BlockSpec((1, S, R * HEAD_DIM), lambda b, g, i: (b, 0, g)),
        kv_spec(2), kv_spec(3), kv_spec(4), kv_spec(5),
        cmp_spec, cmp_spec,
        pl.BlockSpec((R, S, LANES), lambda b, g, i: (g, 0, 0)),
        pl.BlockSpec((N_BIAS_TILES, R, tq, tq), lambda b, g, i: (0, g, 0, 0)),
        pl.BlockSpec((R, SUBLANES, LANES), lambda b, g, i: (g, 0, 0)),
        pl.BlockSpec((nb, LANES), lambda b, g, i: (0, 0)),
        pl.BlockSpec((n_tiles + 1, LANES, tq), lambda b, g, i: (0, 0, 0)),
        pl.BlockSpec((2, LANES, tq), lambda b, g, i: (0, 0, 0)),
        pl.BlockSpec((1, 1, ts * tq, 3 * R), lambda b, g, i: (b, g, i, 0)),
    ]
    kdim = HEAD_DIM + LANES
    assert n_tiles % SEL_TILES_PER_STEP == 0
    est = (2 * 4 * S * HEAD_DIM * 2 + 2 * N_BIAS_TILES * R * tq * tq * 4 + 2 * R * S * LANES * 4
           + 2 * S * R * HEAD_DIM * 2 + 2 * 2 * S * LANES * 2 + 2 * S * kdim * 2
           + R * S * (kdim * 2 + HEAD_DIM * 4) + ts * R * tq * (LANES * 4 + 2 * HEAD_DIM * 4)
           + 3 * R * tq * LANES * 4 + 16 * ATT_ROWS * tq * 4 + 4 * tq * R * HEAD_DIM * 2)
    assert est <= VMEM_LIMIT_CAP
    return pl.pallas_call(
        _nsa_body,
        out_shape=jax.ShapeDtypeStruct((B, S, N_HEADS * HEAD_DIM), BF16),
        grid=(B, G, n_tiles // ts),
        in_specs=in_specs,
        out_specs=pl.BlockSpec((1, ts * tq, R * HEAD_DIM), lambda b, g, i: (b, i, g)),
        scratch_shapes=[pltpu.VMEM((n_tiles + 1, kdim, tq), BF16),
                        pltpu.VMEM((n_tiles + 1, kdim, tq), BF16),
                        pltpu.VMEM((n_tiles, R * tq, kdim), BF16),
                        pltpu.VMEM((n_tiles, R * tq, HEAD_DIM), F32),
                        pltpu.VMEM((ts * R * tq, LANES), F32),
                        pltpu.VMEM((ts * R * tq, 2 * HEAD_DIM), F32)],
        compiler_params=pltpu.CompilerParams(
            dimension_semantics=("arbitrary", "arbitrary", "arbitrary"),
            vmem_limit_bytes=VMEM_LIMIT_CAP),
        name="nsa_attention",
    )(proj, proj, proj, proj, proj, kcmp, vcmp, bias_c, bias_t, bias_far,
      jnp.asarray(overlap_t, BF16), jnp.asarray(aug_sel, BF16), jnp.asarray(aug_win, BF16),
      gates_t)


@functools.lru_cache(maxsize=None)
def _hgrn_masks():
    C = HGRN_CHUNK
    t = np.arange(C)[:, None]
    s = np.arange(C)[None, :]
    masks = [(t // HGRN_SUB == s // HGRN_SUB) & (s <= t)]
    half = C // 2
    while half >= HGRN_SUB:
        grp = 2 * half
        masks.append((t // grp == s // grp) & (t % grp >= half) & (s % grp < half))
        half //= 2
    assert np.array_equal(np.sum(masks, axis=0), (s <= t).astype(int))
    return np.stack(masks).astype(np.float32), (s <= t).astype(np.float32)


def _hgrn_body(q_ref, f_ref, i_ref, g_ref, lb_ref, gn_ref, msk_ref, tril_ref, o_ref,
               st_all, b_all, k_all, q_all, *, layer, heads):
    C = HGRN_CHUNK
    n_chunks = q_ref.shape[1] // C
    nt = (((1,), (1,)), ((), ()))

    lbp = lb_ref[...]
    e = jnp.exp(lbp - jnp.max(lbp, axis=0, keepdims=True))
    sm = e / jnp.sum(e, axis=0, keepdims=True)
    cum = sm[0:1]
    first = cum
    for d in range(1, layer + 1):
        cum = cum + sm[d:d + 1]
    lb_all = cum - first
    log_lb_all = jnp.log(lb_all)
    log_1m_all = jnp.log1p(-lb_all)
    gn_all = gn_ref[...]

    cols = lax.broadcasted_iota(jnp.int32, (1, C), 1)
    tril = tril_ref[...]

    @pl.when(pl.program_id(2) == 0)
    def _():
        st_all[...] = jnp.zeros(st_all.shape, F32)

    def chunk(c, carry):
        for hh in range(heads):
            one_head(c, hh)
        return carry

    def one_head(c, hh):
        sl = pl.ds(pl.multiple_of(c * C, C), C)
        hs = slice(hh * HEAD_DIM, (hh + 1) * HEAD_DIM)
        lb, log_lb, log_1m, gn = lb_all[:, hs], log_lb_all[:, hs], log_1m_all[:, hs], gn_all[:, hs]
        st_ref, b_sc, k_sc, q_sc = st_all.at[hh], b_all.at[hh], k_all.at[hh], q_all.at[hh]
        qr = q_ref[0, sl, hs].astype(F32)
        x = f_ref[0, sl, hs].astype(F32)
        v = i_ref[0, sl, hs].astype(F32)
        gr = g_ref[0, sl, hs].astype(F32)
        q = qr / (1.0 + jnp.exp2(qr * (-LOG2_E)))
        ex = jnp.exp2(jnp.abs(x) * (-LOG2_E))
        u = 1.0 + ex
        r1 = 1.0 / u
        k = (1.0 - lb) * jnp.where(x >= 0.0, ex * r1, r1)
        c2 = log_1m + (jnp.minimum(x, 0.0) - jnp.log(u))
        e2 = jnp.exp2(jnp.abs(log_lb - c2) * (-LOG2_E))
        log_f = jnp.maximum(log_lb, c2) + jnp.log(1.0 + e2)

        lf_hi = log_f.astype(BF16)
        lf_lo = (log_f - lf_hi.astype(F32)).astype(BF16)
        b = (jnp.dot(tril, lf_hi, preferred_element_type=F32)
             + jnp.dot(tril, lf_lo, preferred_element_type=F32)) * LOG2_E
        b_sc[...] = b
        k_sc[...] = k
        q_sc[...] = q

        vb = v.astype(BF16)
        qb = q.astype(BF16)
        kb = k.astype(BF16)
        st = st_ref[...]
        o = lax.dot_general((q * jnp.exp2(b)).astype(BF16), st.astype(BF16), nt,
                            preferred_element_type=F32)

        pieces = []
        for blk in range(C // HGRN_SUB):
            r0 = blk * HGRN_SUB
            bt = b_sc[r0:r0 + HGRN_SUB, :]
            qt = q_sc[r0:r0 + HGRN_SUB, :]
            arow = jnp.zeros((HGRN_SUB, C), F32)
            for s in range(HGRN_SUB):
                bs = b_sc[r0 + s:r0 + s + 1, :]
                ks = k_sc[r0 + s:r0 + s + 1, :]
                col = jnp.sum(jnp.exp2(bt - bs) * qt * ks, axis=-1, keepdims=True)
                arow = jnp.where(cols == r0 + s, col, arow)
            pieces.append(arow)
        a = jnp.where(msk_ref[0] > 0.5, jnp.concatenate(pieces, axis=0), 0.0)

        half = C // 2
        lvl = 1
        while half >= HGRN_SUB:
            grp = 2 * half
            anc = jnp.concatenate(
                [jnp.broadcast_to(b_sc[g0 + half - 1:g0 + half, :], (grp, b.shape[1]))
                 for g0 in range(0, C, grp)], axis=0)
            e = jnp.exp2(-jnp.abs(b - anc)).astype(BF16)
            al = lax.dot_general(qb * e, kb * e, nt, preferred_element_type=F32)
            a = jnp.where(msk_ref[lvl] > 0.5, al, a)
            half //= 2
            lvl += 1

        o = o + jnp.dot(a.astype(BF16), vb, preferred_element_type=F32)

        b_last = b_sc[C - 1:C, :]
        kh = (k * jnp.exp2(b_last - b)).astype(BF16)
        st_ref[...] = st * jnp.exp2(b_last) + jnp.dot(v.T.astype(BF16), kh,
                                                     preferred_element_type=F32)

        ms = jnp.mean(o * o, axis=-1, keepdims=True)
        o = o * lax.rsqrt(ms + RMS_EPS) * gn * (gr / (1.0 + jnp.exp2(gr * (-LOG2_E))))
        o_ref[0, sl, hs] = o.astype(o_ref.dtype)

    lax.fori_loop(0, n_chunks, chunk, 0)


def _hgrn(proj, hgrn_lb, onorm, layer, B, S):
    H = N_HEADS
    depth = hgrn_lb.shape[0]

    nh = HGRN_HEADS_PER_STEP
    width = nh * HEAD_DIM
    steps = H // nh

    nsb = HGRN_SEQ_BLOCKS
    sblk = S // nsb
    C = HGRN_CHUNK
    assert H % nh == 0 and S % nsb == 0 and sblk % C == 0

    def spec(part):
        return pl.BlockSpec((1, sblk, width),
                            lambda b, h, s, part=part: (b, s, part * steps + h))

    masks, tril = _hgrn_masks()
    est = (2 * 4 * sblk * width * proj.dtype.itemsize + 2 * sblk * width * 2
           + nh * 64 * C * C * 4)
    return pl.pallas_call(
        functools.partial(_hgrn_body, layer=layer, heads=nh),
        out_shape=jax.ShapeDtypeStruct((B, S, H * HEAD_DIM), BF16),
        grid=(B, steps, nsb),
        in_specs=[spec(0), spec(1), spec(2), spec(3),
                  pl.BlockSpec((depth, width), lambda b, h, s: (0, h)),
                  pl.BlockSpec((1, width), lambda b, h, s: (0, h)),
                  pl.BlockSpec(masks.shape, lambda b, h, s: (0, 0, 0)),
                  pl.BlockSpec((C, C), lambda b, h, s: (0, 0))],
        out_specs=pl.BlockSpec((1, sblk, width), lambda b, h, s: (b, s, h)),
        scratch_shapes=[pltpu.VMEM((nh, HEAD_DIM, HEAD_DIM), F32),
                        pltpu.VMEM((nh, C, HEAD_DIM), F32),
                        pltpu.VMEM((nh, C, HEAD_DIM), F32),
                        pltpu.VMEM((nh, C, HEAD_DIM), F32)],
        compiler_params=pltpu.CompilerParams(
            dimension_semantics=("arbitrary", "arbitrary", "arbitrary"),
            vmem_limit_bytes=_vmem_limit(est)),
        name="hgrn2_recurrence",
    )(proj, proj, proj, proj, hgrn_lb.astype(F32), onorm.reshape(1, -1).astype(F32),
      jnp.asarray(masks), jnp.asarray(tril, BF16))


def _mlp_body(x_ref, gi_ref, wu_ref, wd_ref, go_ref, o_ref, hn_ref, *, nf, tm):
    f = pl.program_id(1)

    def row_chunks(fn):
        def step(r, c):
            fn(pl.ds(pl.multiple_of(r * NORM_ROWS, NORM_ROWS), NORM_ROWS))
            return c

        lax.fori_loop(0, tm // NORM_ROWS, step, 0)

    @pl.when(f == 0)
    def _():
        g = gi_ref[...]

        def norm_in(rows):
            xs = x_ref[rows, :]
            ms = jnp.mean(xs * xs, axis=-1, keepdims=True)
            hn_ref[rows, :] = (xs * lax.rsqrt(ms + RMS_EPS) * g).astype(BF16)

        row_chunks(norm_in)
        o_ref[...] = jnp.zeros(o_ref.shape, F32)

    hid = jnp.dot(hn_ref[...], wu_ref[...], preferred_element_type=F32)
    hid = jnp.square(jnp.maximum(hid, 0.0)).astype(BF16)
    o_ref[...] += jnp.dot(hid, wd_ref[...], preferred_element_type=F32)

    @pl.when(f == nf - 1)
    def _():
        g = go_ref[...]

        def norm_out(rows):
            y = o_ref[rows, :]
            ms = jnp.mean(y * y, axis=-1, keepdims=True)
            o_ref[rows, :] = x_ref[rows, :] + y * lax.rsqrt(ms + RMS_EPS) * g

        row_chunks(norm_out)


def _mlp(xf, g_in, g_out, w_up, w_down, layer, *, tm=512, tf=1024):
    M, D = xf.shape
    F = w_up.shape[2]
    assert M % tm == 0 and F % tf == 0 and tm % NORM_ROWS == 0
    nf = F // tf
    est = (2 * tm * D * 4 + 2 * tm * D * 4 + tm * D * 2 + 2 * 2 * D * tf * 2
           + tm * tf * 6 + 2 * tm * D * 4)
    return pl.pallas_call(
        functools.partial(_mlp_body, nf=nf, tm=tm),
        out_shape=jax.ShapeDtypeStruct((M, D), F32),
        grid=(M // tm, nf),
        in_specs=[pl.BlockSpec((tm, D), lambda i, f: (i, 0)),
                  pl.BlockSpec((1, D), lambda i, f: (0, 0)),
                  pl.BlockSpec((None, D, tf), lambda i, f: (layer, 0, f)),
                  pl.BlockSpec((None, tf, D), lambda i, f: (layer, f, 0)),
                  pl.BlockSpec((1, D), lambda i, f: (0, 0))],
        out_specs=pl.BlockSpec((tm, D), lambda i, f: (i, 0)),
        scratch_shapes=[pltpu.VMEM((tm, D), BF16)],
        compiler_params=pltpu.CompilerParams(
            dimension_semantics=("arbitrary", "arbitrary"),
            vmem_limit_bytes=_vmem_limit(est)),
        name="mlp",
    )(xf, g_in.reshape(1, D).astype(F32), w_up.astype(BF16), w_down.astype(BF16),
      g_out.reshape(1, D).astype(F32))


def _nsa_layer(xf, B, S, g_in, g_out, rel_table, w_in, cmp_pe, cmp_w1, cmp_w2, w_out):
    D = xf.shape[1]
    G, R, Dh = N_GROUPS, HEADS_PER_GROUP, HEAD_DIM
    n_main = N_HEADS * Dh + 6 * G * Dh
    n_gate = 3 * N_HEADS
    w_all = w_in.astype(BF16)
    w_gate = jnp.pad(w_in[:, n_main:], ((0, 0), (0, LANES - n_gate))).astype(BF16)
    colscale = jnp.concatenate([jnp.full((N_HEADS * Dh,), Dh ** -0.5 * LOG2_E, F32),
                                jnp.ones((6 * G * Dh,), F32)])[None]

    proj, glog = _matmul(xf, w_all, tm=1024, tn=1024, norm_g=g_in, epi="colscale",
                         colscale=colscale, side_w=w_gate, n_out=n_main, out_dtype=BF16,
                         name="nsa_proj")
    gates_t = (glog[:, :n_gate].reshape(B, S, 3, G, R).transpose(0, 3, 1, 2, 4)
               .reshape(B, G, S, 3 * R))

    proj3 = proj.reshape(B, S, n_main)
    half = CMP_STRIDE * Dh
    kcmp, vcmp = _compress(proj3, cmp_w1.reshape(2, 2, half, Dh).astype(BF16),
                           cmp_pe.reshape(2, 2, 1, half).astype(F32),
                           cmp_w2.astype(BF16), B, S)

    bias_c, bias_t, bias_far = _bias_tables(rel_table, S)
    attn = _nsa_attention(proj3, kcmp, vcmp, bias_c, bias_t, bias_far, gates_t, B, S)
    return _matmul(attn.reshape(B * S, D), w_out.astype(BF16), tm=512, tn=D,
                   epi="resnorm", res=xf, res_g=g_out, out_dtype=F32, name="nsa_out")


def _hgrn_layer(xf, B, S, layer, g_in, g_out, w_in, hgrn_lb, onorm, w_out):
    D = xf.shape[1]
    proj = _matmul(xf, w_in, tm=1024, tn=1024, norm_g=g_in,
                   out_dtype=BF16, name="hgrn_proj")
    mixed = _hgrn(proj.reshape(B, S, 4 * D), hgrn_lb, onorm, layer, B, S)
    return _matmul(mixed.reshape(B * S, D), w_out.astype(BF16), tm=512, tn=D,
                   epi="resnorm", res=xf, res_g=g_out, out_dtype=F32, name="hgrn_out")


def kernel(x, norm_g, rel_table, nsa_w_in, nsa_cmp_pe, nsa_cmp_w1, nsa_cmp_w2, nsa_w_out,
           hgrn_w_in, hgrn_lb, hgrn_onorm, hgrn_w_out, mlp_w_up, mlp_w_down):
    B, S, D = x.shape
    depth = norm_g.shape[0]
    assert D == N_HEADS * HEAD_DIM and S % ATT_TILE == 0 and S % HGRN_CHUNK == 0
    xf = x.reshape(B * S, D).astype(F32)
    for layer in range(depth):
        j = layer // 2
        if layer % 2 == 0:
            xf = _nsa_layer(xf, B, S, norm_g[layer, 0], norm_g[layer, 1], rel_table,
                            nsa_w_in[j], nsa_cmp_pe[j], nsa_cmp_w1[j], nsa_cmp_w2[j],
                            nsa_w_out[j])
        else:
            xf = _hgrn_layer(xf, B, S, layer, norm_g[layer, 0], norm_g[layer, 1],
                             hgrn_w_in[j], hgrn_lb, hgrn_onorm[j], hgrn_w_out[j])
        xf = _mlp(xf, norm_g[layer, 2], norm_g[layer, 3], mlp_w_up, mlp_w_down, layer)
    return xf.reshape(B, S, D).astype(x.dtype)
```

```python
import functools
import math

import numpy as np
import jax
import jax.numpy as jnp
from jax import lax
from jax.experimental import pallas as pl
from jax.experimental.pallas import tpu as pltpu

F32 = jnp.float32
BF16 = jnp.bfloat16

N_HEADS = 16
N_GROUPS = 4
HEADS_PER_GROUP = N_HEADS // N_GROUPS
HEAD_DIM = 128
CMP_BLOCK = 32
CMP_STRIDE = 16
SEL_BLOCK = 64
SEL_TOP_N = 8
WINDOW = 512
FORCE_SCORE = 1.0e4
REL_BUCKETS = 32
REL_MAX_DIST = 128
RMS_EPS = 1e-6
NEG_INF = -1.0e30
LOG2_E = math.log2(math.e)

LANES = 128
SUBLANES = 8
VMEM_BYTES_V7X = 64 * 1024 * 1024
VMEM_LIMIT_CAP = VMEM_BYTES_V7X - 8 * 1024 * 1024

ATT_TILE = 256
ATT_ROWS = 128
SEL_TILES_PER_STEP = 4
ATT_TILES_PER_STEP = 2
PREV_TILE, DIAG_TILE, EDGE_TILE = 0, 1, 2
N_BIAS_TILES = 3
HGRN_CHUNK = 128
HGRN_SUB = 8
HGRN_HEADS_PER_STEP = 16
HGRN_HEADS_INTERLEAVED = 8
HGRN_SEQ_BLOCKS = 2
NORM_ROWS = 256


VMEM_LIMIT_FLOOR = 32 * 1024 * 1024


def _vmem_limit(nbytes):
    return int(min(VMEM_LIMIT_CAP, max(VMEM_LIMIT_FLOOR, nbytes)))


def _mm_body(*refs, norm, epi, side, tm):
    it = iter(refs)
    x_ref = next(it)
    g_ref = next(it) if norm else None
    w_ref = next(it)
    ws_ref = next(it) if side else None
    cs_ref = next(it) if epi == "colscale" else None
    res_ref = next(it) if epi == "resnorm" else None
    go_ref = next(it) if epi == "resnorm" else None
    o_ref = next(it)
    os_ref = next(it) if side else None
    hn_ref = next(it) if norm else None

    j = pl.program_id(1)

    if norm:
        @pl.when(j == 0)
        def _():
            g = g_ref[...]

            def step(r, c):
                rows = pl.ds(pl.multiple_of(r * NORM_ROWS, NORM_ROWS), NORM_ROWS)
                xs = x_ref[rows, :]
                ms = jnp.mean(xs * xs, axis=-1, keepdims=True)
                hn_ref[rows, :] = (xs * lax.rsqrt(ms + RMS_EPS) * g).astype(BF16)
                return c

            lax.fori_loop(0, tm // NORM_ROWS, step, 0)

        lhs = hn_ref[...]
    else:
        lhs = x_ref[...]

    if side:
        @pl.when(j == 0)
        def _():
            os_ref[...] = jnp.dot(lhs, ws_ref[...], preferred_element_type=F32)

    acc = jnp.dot(lhs, w_ref[...].astype(BF16), preferred_element_type=F32)
    if epi == "colscale":
        acc = acc * cs_ref[...]
    elif epi == "resnorm":
        ms = jnp.mean(acc * acc, axis=-1, keepdims=True)
        acc = res_ref[...] + acc * lax.rsqrt(ms + RMS_EPS) * go_ref[...]
    o_ref[...] = acc.astype(o_ref.dtype)


def _matmul(x, w, *, tm, tn, norm_g=None, epi="none", colscale=None, res=None, res_g=None,
            side_w=None, n_out=None, out_dtype=BF16, name="mm"):
    M, K = x.shape
    N = w.shape[1] if n_out is None else n_out
    assert N <= w.shape[1]
    norm = norm_g is not None
    side = side_w is not None
    assert M % tm == 0 and N % tn == 0
    assert not (epi == "resnorm" and tn != N)

    in_specs = [pl.BlockSpec((tm, K), lambda i, j: (i, 0))]
    args = [x]
    if norm:
        in_specs.append(pl.BlockSpec((1, K), lambda i, j: (0, 0)))
        args.append(norm_g.reshape(1, K).astype(F32))
    in_specs.append(pl.BlockSpec((K, tn), lambda i, j: (0, j)))
    args.append(w)
    ns = side_w.shape[1] if side else 0
    if side:
        in_specs.append(pl.BlockSpec((K, ns), lambda i, j: (0, 0)))
        args.append(side_w)
    if epi == "colscale":
        in_specs.append(pl.BlockSpec((1, tn), lambda i, j: (0, j)))
        args.append(colscale)
    if epi == "resnorm":
        in_specs.append(pl.BlockSpec((tm, tn), lambda i, j: (i, j)))
        args.append(res)
        in_specs.append(pl.BlockSpec((1, tn), lambda i, j: (0, j)))
        args.append(res_g.reshape(1, N).astype(F32))

    out_shape = jax.ShapeDtypeStruct((M, N), out_dtype)
    out_specs = pl.BlockSpec((tm, tn), lambda i, j: (i, j))
    if side:
        out_shape = (out_shape, jax.ShapeDtypeStruct((M, ns), F32))
        out_specs = (out_specs, pl.BlockSpec((tm, ns), lambda i, j: (i, 0)))

    xb = x.dtype.itemsize
    ob = jnp.dtype(out_dtype).itemsize
    wb = w.dtype.itemsize
    est = (2 * tm * K * xb + 2 * K * tn * wb + (K * tn * 2 if wb > 2 else 0) + 2 * tm * tn * ob
           + (tm * K * 2 if norm else 0)
           + (2 * tm * tn * 4 if epi == "resnorm" else 0) + 3 * tm * tn * 4
           + 2 * K * ns * 2 + 3 * tm * ns * 4)

    return pl.pallas_call(
        functools.partial(_mm_body, norm=norm, epi=epi, side=side, tm=tm),
        out_shape=out_shape,
        grid=(M // tm, N // tn),
        in_specs=in_specs,
        out_specs=out_specs,
        scratch_shapes=[pltpu.VMEM((tm, K), BF16)] if norm else [],
        compiler_params=pltpu.CompilerParams(
            dimension_semantics=("arbitrary", "arbitrary"),
            vmem_limit_bytes=_vmem_limit(est)),
        name=name,
    )(*args)


def _rel_bucket_np(dist):
    n = np.maximum(dist, 0)
    max_exact = REL_BUCKETS // 2
    nf = np.maximum(n, 1).astype(np.float32)
    ratio = np.log(nf / np.float32(max_exact)) / np.float32(math.log(REL_MAX_DIST / max_exact))
    large = max_exact + (ratio * np.float32(REL_BUCKETS - max_exact)).astype(np.int32)
    large = np.minimum(large, REL_BUCKETS - 1)
    return np.where(n < max_exact, n, large).astype(np.int32)


@functools.lru_cache(maxsize=None)
def _static_maps(seq):
    n_cmp = LANES
    pos = np.arange(seq, dtype=np.int32)[:, None]
    c_end = np.arange(n_cmp, dtype=np.int32)[None, :] * CMP_STRIDE + CMP_BLOCK - 1
    bucket_c = _rel_bucket_np(pos - c_end)
    t = np.arange(ATT_TILE, dtype=np.int32)[:, None]
    k = np.arange(ATT_TILE, dtype=np.int32)[None, :]
    bucket_t = np.stack([_rel_bucket_np(t - k), _rel_bucket_np(ATT_TILE + t - k)])
    assert _rel_bucket_np(np.array([ATT_TILE + 1]))[0] == REL_BUCKETS - 1
    nc = seq // CMP_STRIDE - CMP_BLOCK // CMP_STRIDE + 1
    nb = seq // SEL_BLOCK
    c_start = np.arange(nc)[:, None] * CMP_STRIDE
    b_start = np.arange(nb)[None, :] * SEL_BLOCK
    ov = ((c_start <= b_start + SEL_BLOCK - 1) & (c_start + CMP_BLOCK - 1 >= b_start))
    overlap = np.zeros((LANES, LANES), np.float32)
    overlap[:nc, :nb] = ov
    return bucket_c, bucket_t, overlap


def _bias_body(tab_ref, bc_ref, bt_ref, oc_ref, ot_ref, of_ref, *, seq):
    h = pl.program_id(0)

    lane = lax.broadcasted_iota(jnp.int32, (SUBLANES, LANES), 1)
    tab_row = jnp.zeros((SUBLANES, LANES), F32)
    for b in range(REL_BUCKETS):
        tab_row = jnp.where(lane == b, tab_ref[b, h], tab_row)

    def lookup(bmap):
        rows = bmap.shape[0]
        tab = jnp.broadcast_to(tab_row[0:1], (rows, LANES))
        return jnp.concatenate(
            [jnp.take_along_axis(tab, bmap[:, c0:c0 + LANES], axis=1)
             for c0 in range(0, bmap.shape[1], LANES)], axis=1)

    def step(r, c):
        rows = pl.ds(pl.multiple_of(r * ATT_TILE, ATT_TILE), ATT_TILE)
        oc_ref[0, rows, :] = lookup(bc_ref[rows, :]) * LOG2_E
        return c

    lax.fori_loop(0, seq // ATT_TILE, step, 0)

    tt = lax.broadcasted_iota(jnp.int32, (ATT_TILE, ATT_TILE), 0)
    kk = lax.broadcasted_iota(jnp.int32, (ATT_TILE, ATT_TILE), 1)
    far = tab_ref[REL_BUCKETS - 1, h]
    of_ref[0] = jnp.full(of_ref.shape[1:], far * LOG2_E, F32)
    ot_ref[PREV_TILE, 0] = (lookup(bt_ref[1]) - far) * LOG2_E
    ot_ref[DIAG_TILE, 0] = jnp.where(kk <= tt, (lookup(bt_ref[0]) - far) * LOG2_E, NEG_INF)
    ot_ref[EDGE_TILE, 0] = jnp.where(kk > tt, 0.0, NEG_INF)


def _bias_tables(rel_table, seq):
    bucket_c, bucket_t, _ = _static_maps(seq)
    return pl.pallas_call(
        functools.partial(_bias_body, seq=seq),
        out_shape=(jax.ShapeDtypeStruct((N_HEADS, seq, LANES), F32),
                   jax.ShapeDtypeStruct((N_BIAS_TILES, N_HEADS, ATT_TILE, ATT_TILE), F32),
                   jax.ShapeDtypeStruct((N_HEADS, SUBLANES, LANES), F32)),
        grid=(N_HEADS,),
        in_specs=[pl.BlockSpec(memory_space=pltpu.SMEM),
                  pl.BlockSpec((seq, LANES), lambda h: (0, 0)),
                  pl.BlockSpec((2, ATT_TILE, ATT_TILE), lambda h: (0, 0, 0))],
        out_specs=(pl.BlockSpec((1, seq, LANES), lambda h: (h, 0, 0)),
                   pl.BlockSpec((N_BIAS_TILES, 1, ATT_TILE, ATT_TILE), lambda h: (0, h, 0, 0)),
                   pl.BlockSpec((1, SUBLANES, LANES), lambda h: (h, 0, 0))),
        compiler_params=pltpu.CompilerParams(dimension_semantics=("arbitrary",)),
        name="rel_bias",
    )(rel_table.astype(F32), jnp.asarray(bucket_c), jnp.asarray(bucket_t))


def _compress_body(xk_ref, xv_ref, w1_ref, pe_ref, w2_ref, ok_ref, ov_ref, x_sc):
    n_grp = xk_ref.shape[1] // CMP_STRIDE

    def one(x_ref, idx, o_ref):
        x_sc[...] = x_ref[0].astype(F32)
        x = jnp.concatenate([x_sc[pl.ds(t, n_grp, stride=CMP_STRIDE), :]
                             for t in range(CMP_STRIDE)], axis=1)
        a0 = jnp.dot((x + pe_ref[idx, 0]).astype(BF16), w1_ref[idx, 0],
                     preferred_element_type=F32)
        a1 = jnp.dot((x + pe_ref[idx, 1]).astype(BF16), w1_ref[idx, 1],
                     preferred_element_type=F32)
        pre = a0 + pltpu.roll(a1, LANES - 1, 0)
        hid = jax.nn.gelu(pre).astype(BF16)
        o_ref[0, 0] = jnp.dot(hid, w2_ref[idx], preferred_element_type=F32).astype(BF16)

    one(xk_ref, 0, ok_ref)
    one(xv_ref, 1, ov_ref)


def _compress(proj, w1, pe, w2, B, S):
    G = N_GROUPS
    assert S // CMP_STRIDE == LANES
    half = CMP_STRIDE * HEAD_DIM
    q_cols = N_HEADS

    def spec_x(slot):
        return pl.BlockSpec((1, S, HEAD_DIM), lambda b, g, slot=slot: (b, 0, q_cols + slot * G + g))

    spec_o = pl.BlockSpec((1, 1, LANES, HEAD_DIM), lambda b, g: (b, g, 0, 0))
    out = jax.ShapeDtypeStruct((B, G, LANES, HEAD_DIM), BF16)
    return pl.pallas_call(
        _compress_body,
        out_shape=(out, out),
        grid=(B, G),
        in_specs=[spec_x(0), spec_x(1),
                  pl.BlockSpec((2, 2, half, HEAD_DIM), lambda b, g: (0, 0, 0, 0)),
                  pl.BlockSpec((2, 2, 1, half), lambda b, g: (0, 0, 0, 0)),
                  pl.BlockSpec((2, HEAD_DIM, HEAD_DIM), lambda b, g: (0, 0, 0))],
        out_specs=(spec_o, spec_o),
        scratch_shapes=[pltpu.VMEM((S, HEAD_DIM), F32)],
        compiler_params=pltpu.CompilerParams(dimension_semantics=("arbitrary", "arbitrary")),
        name="nsa_compress",
    )(proj, proj, w1, pe, w2)


def _nsa_body(q_ref, ks_ref, vs_ref, kw_ref, vw_ref, kc_ref, vc_ref, bc_ref, bt_ref, far_ref,
              ovl_ref, augs_ref, augw_ref, gl_ref, o_ref, kts_sc, ktw_sc, qa_sc, oc_sc, m_sc,
              acc_sc):
    R = HEADS_PER_GROUP
    tq = ATT_TILE
    qi = pl.program_id(2)
    n_tiles = ks_ref.shape[1] // tq
    nb = ovl_ref.shape[0]
    nt = (((1,), (1,)), ((), ()))

    cidx = lax.broadcasted_iota(jnp.int32, (1, 1, LANES), 2)
    jb = lax.broadcasted_iota(jnp.int32, (nb, 1), 0)
    lane = lax.broadcasted_iota(jnp.int32, (1, LANES), 1)

    def select_tile(t):
        rows = pl.ds(pl.multiple_of(t * tq, tq), tq)
        q = q_ref[0, rows, :]
        q4 = jnp.concatenate([q[:, r * HEAD_DIM:(r + 1) * HEAD_DIM] for r in range(R)], axis=0)
        qa_sc[t, :, :HEAD_DIM] = q4
        pos3 = t * tq + lax.broadcasted_iota(jnp.int32, (1, tq, 1), 1)

        sc = lax.dot_general(q4, kc_ref[0, 0], nt, preferred_element_type=F32)
        sc = sc.reshape(R, tq, LANES) + bc_ref[:, rows, :]
        valid = (cidx * CMP_STRIDE + (CMP_BLOCK - 1) <= pos3) & (cidx < LANES - 1)
        sc = jnp.where(valid, sc, NEG_INF)
        mc = jnp.max(sc, axis=-1, keepdims=True)
        ec = jnp.exp2(sc - mc)
        pc = ec / jnp.sum(ec, axis=-1, keepdims=True)
        pc = jnp.where(pos3 >= CMP_BLOCK - 1, pc, 0.0)
        oc_sc[t] = jnp.dot(pc.reshape(R * tq, LANES).astype(BF16), vc_ref[0, 0],
                           preferred_element_type=F32)

        psum = pc[0]
        for r in range(1, R):
            psum = psum + pc[r]
        p_hi = psum.astype(BF16)
        p_lo = (psum - p_hi.astype(F32)).astype(BF16)
        ovt = ovl_ref[...]
        imp = (lax.dot_general(ovt, p_hi, nt, preferred_element_type=F32)
               + lax.dot_general(ovt, p_lo, nt, preferred_element_type=F32))
        pos_t = t * tq + lax.broadcasted_iota(jnp.int32, (1, tq), 1)
        q_blk = lax.shift_right_logical(pos_t, int(math.log2(SEL_BLOCK)))
        forced = (jb == 0) | (jb == q_blk) | (jb == q_blk - 1)
        future = jb > q_blk
        imp = jnp.where(forced, FORCE_SCORE, jnp.where(future, -1.0, imp))
        cnt = jnp.zeros((nb, tq), F32)
        for i in range(nb):
            row = imp[i:i + 1, :]
            beats = (row > imp) | ((row == imp) & (jb > i))
            cnt = cnt + jnp.where(beats, 1.0, 0.0)
        sel_t = jnp.where(cnt < float(min(SEL_TOP_N, nb)), 1.0, 0.0)
        sel = jnp.concatenate([sel_t, jnp.zeros((LANES - nb, tq), F32)], axis=0).T

        sel_pad = jnp.where(lane < nb, (sel - 1.0) * (-NEG_INF), 0.0)
        for r in range(R):
            far = jnp.broadcast_to(far_ref[r, 0:1, :], (tq, LANES))
            far_hi = far.astype(BF16).astype(F32)
            pad = jnp.where(lane == nb, far_hi, jnp.where(lane == nb + 1, far - far_hi, sel_pad))
            pad = jnp.where(lane == nb + 2, 1.0, pad)
            qa_sc[t, r * tq:(r + 1) * tq, HEAD_DIM:] = pad.astype(BF16)

    @pl.when(qi == 0)
    def _():
        def tr(j, c):
            rows = pl.ds(pl.multiple_of(j * tq, tq), tq)
            kts_sc[j, :HEAD_DIM, :] = ks_ref[0, rows, :].T
            kts_sc[j, HEAD_DIM:, :] = augs_ref[j]
            ktw_sc[j, :HEAD_DIM, :] = kw_ref[0, rows, :].T
            ktw_sc[j, HEAD_DIM:, :] = augw_ref[0]
            return c

        lax.fori_loop(0, n_tiles, tr, 0)
        zeros = jnp.zeros((HEAD_DIM, tq), BF16)
        kts_sc[n_tiles, :HEAD_DIM, :] = zeros
        kts_sc[n_tiles, HEAD_DIM:, :] = augs_ref[n_tiles]
        ktw_sc[n_tiles, :HEAD_DIM, :] = zeros
        ktw_sc[n_tiles, HEAD_DIM:, :] = augw_ref[1]

        def sel_group(gi, c):
            for u in range(SEL_TILES_PER_STEP):
                select_tile(gi * SEL_TILES_PER_STEP + u)
            return c

        lax.fori_loop(0, n_tiles // SEL_TILES_PER_STEP, sel_group, 0)

    ones = jnp.ones((tq, HEAD_DIM), BF16)
    n_chunks = R * tq // ATT_ROWS
    chunks_per_head = tq // ATT_ROWS
    dead = n_tiles
    step_tiles = [ATT_TILES_PER_STEP * qi + u for u in range(ATT_TILES_PER_STEP)]

    def keys(kt_sc, tiles):
        return jnp.concatenate([kt_sc[t] for t in tiles], axis=1)

    def values(v_ref, tiles):
        parts = []
        for t in tiles:
            rows = pl.ds(pl.multiple_of(t * tq, tq), tq)
            parts.append(jnp.concatenate([v_ref[0, rows, :], ones], axis=1))
        return jnp.concatenate(parts, axis=0)

    def logits(q_sc, ci, kt, kinds):
        r, hh = divmod(ci, chunks_per_head)
        rs = slice(ci * ATT_ROWS, (ci + 1) * ATT_ROWS)
        qs = slice(hh * ATT_ROWS, (hh + 1) * ATT_ROWS)
        s = jnp.dot(q_sc[rs, :], kt, preferred_element_type=F32)
        if all(kd is None for kd in kinds):
            return s
        parts = [s[:, i * tq:(i + 1) * tq] for i in range(len(kinds))]
        return jnp.concatenate([p if kd is None else p + bt_ref[kd, r, qs, :]
                                for p, kd in zip(parts, kinds)], axis=1)

    def probs(s, m):
        return jnp.concatenate([jnp.exp2(s[:, k0:k0 + LANES] - m)
                                for k0 in range(0, s.shape[1], LANES)], axis=1).astype(BF16)

    m_sc[...] = jnp.full(m_sc.shape, NEG_INF, F32)
    acc_sc[...] = jnp.zeros(acc_sc.shape, F32)

    def sel_update(u, t, kt, vv, kinds):
        for ci in range(n_chunks):
            rs = slice((u * n_chunks + ci) * ATT_ROWS, (u * n_chunks + ci + 1) * ATT_ROWS)
            s = logits(qa_sc.at[t], ci, kt, kinds)
            m_old = m_sc[rs]
            m_new = jnp.maximum(m_old, jnp.max(s, axis=-1, keepdims=True))
            alpha = jnp.exp2(m_old - m_new)
            m_sc[rs] = m_new
            pv = jnp.dot(probs(s, m_new), vv, preferred_element_type=F32)
            acc_sc[rs] = acc_sc[rs] * jnp.concatenate([alpha, alpha], axis=1) + pv

    assert ATT_TILES_PER_STEP == 2
    ta, tb = step_tiles

    def far_pair(pi, c):
        pair_tiles = (2 * pi, 2 * pi + 1)
        kt, vv = keys(kts_sc, pair_tiles), values(vs_ref, pair_tiles)
        sel_update(0, ta, kt, vv, (None, None))
        sel_update(1, tb, kt, vv, (None, None))
        return c

    lax.fori_loop(0, jnp.maximum(qi - 1, 0), far_pair, 0)

    @pl.when(qi >= 1)
    def _():
        pair_tiles = (ta - 2, ta - 1)
        kt, vv = keys(kts_sc, pair_tiles), values(vs_ref, pair_tiles)
        sel_update(0, ta, kt, vv, (None, PREV_TILE))
        sel_update(1, tb, kt, vv, (None, None))

    sel_update(0, ta, kts_sc[ta], values(vs_ref, (ta,)), (DIAG_TILE,))
    sel_update(1, tb, keys(kts_sc, (ta, tb)), values(vs_ref, (ta, tb)), (PREV_TILE, DIAG_TILE))

    n_win = WINDOW // tq
    w_kinds = (EDGE_TILE,) + (None,) * (n_win - 2) + (PREV_TILE, DIAG_TILE)
    gates = jax.nn.sigmoid(gl_ref[0, 0])
    for u, t in enumerate(step_tiles):
        acc = acc_sc[u * R * tq:(u + 1) * R * tq, :]
        o_sel = acc[:, :HEAD_DIM] / acc[:, HEAD_DIM:]

        w_tiles = [t - n_win + w for w in range(n_win + 1)]
        kt = keys(ktw_sc, [jnp.where(w >= 0, w, dead) for w in w_tiles])
        vv = values(vw_ref, [jnp.maximum(w, 0) for w in w_tiles])
        o_parts = []
        for ci in range(n_chunks):
            s = logits(qa_sc.at[t], ci, kt, w_kinds)
            pv = jnp.dot(probs(s, jnp.max(s, axis=-1, keepdims=True)), vv,
                         preferred_element_type=F32)
            o_parts.append(pv[:, :HEAD_DIM] / pv[:, HEAD_DIM:])
        o_win = jnp.concatenate(o_parts, axis=0)

        o_cmp = oc_sc[t]
        g = gates[u * tq:(u + 1) * tq]
        outs = []
        for r in range(R):
            hs = slice(r * tq, (r + 1) * tq)
            outs.append(g[:, r:r + 1] * o_cmp[hs] + g[:, R + r:R + r + 1] * o_sel[hs]
                        + g[:, 2 * R + r:2 * R + r + 1] * o_win[hs])
        o_ref[0, u * tq:(u + 1) * tq, :] = jnp.concatenate(outs, axis=1).astype(o_ref.dtype)


def _nsa_attention(proj, kcmp, vcmp, bias_c, bias_t, bias_far, gates_t, B, S):
    assert WINDOW % ATT_TILE == 0 and S % ATT_TILE == 0
    assert S // CMP_STRIDE == LANES and S // SEL_BLOCK <= LANES
    R, G, tq = HEADS_PER_GROUP, N_GROUPS, ATT_TILE
    n_tiles = S // tq
    ts = ATT_TILES_PER_STEP
    assert n_tiles % ts == 0
    nb = S // SEL_BLOCK
    _, _, overlap = _static_maps(S)
    overlap_t = np.ascontiguousarray(overlap.T[:nb])
    assert nb + 3 <= LANES and n_tiles % 2 == 0
    blk_of_key = (np.arange(S) // SEL_BLOCK).reshape(n_tiles, 1, tq)
    aug_sel = np.zeros((n_tiles + 1, LANES, tq), np.float32)
    aug_sel[:n_tiles] = np.arange(LANES).reshape(1, LANES, 1) == blk_of_key
    aug_sel[:n_tiles, nb:nb + 2, :] = 1.0
    aug_sel[n_tiles, nb + 2, :] = NEG_INF
    aug_win = np.zeros((2, LANES, tq), np.float32)
    aug_win[0, nb:nb + 2, :] = 1.0
    aug_win[1, nb + 2, :] = NEG_INF
    q_cols = N_HEADS

    def kv_spec(slot):
        return pl.BlockSpec((1, S, HEAD_DIM),
                            lambda b, g, i, slot=slot: (b, 0, q_cols + slot * G + g))

    cmp_spec = pl.BlockSpec((1, 1, LANES, HEAD_DIM), lambda b, g, i: (b, g, 0, 0))
    in_specs = [
        pl.BlockSpec((1, S, R * HEAD_DIM), lambda b, g, i: (b, 0, g)),
        kv_spec(2), kv_spec(3), kv_spec(4), kv_spec(5),
        cmp_spec, cmp_spec,
        pl.BlockSpec((R, S, LANES), lambda b, g, i: (g, 0, 0)),
        pl.BlockSpec((N_BIAS_TILES, R, tq, tq), lambda b, g, i: (0, g, 0, 0)),
        pl.BlockSpec((R, SUBLANES, LANES), lambda b, g, i: (g, 0, 0)),
        pl.BlockSpec((nb, LANES), lambda b, g, i: (0, 0)),
        pl.BlockSpec((n_tiles + 1, LANES, tq), lambda b, g, i: (0, 0, 0)),
        pl.BlockSpec((2, LANES, tq), lambda b, g, i: (0, 0, 0)),
        pl.BlockSpec((1, 1, ts * tq, 3 * R), lambda b, g, i: (b, g, i, 0)),
    ]
    kdim = HEAD_DIM + LANES
    assert n_tiles % SEL_TILES_PER_STEP == 0
    est = (2 * 4 * S * HEAD_DIM * 2 + 2 * N_BIAS_TILES * R * tq * tq * 4 + 2 * R * S * LANES * 4
           + 2 * S * R * HEAD_DIM * 2 + 2 * 2 * S * LANES * 2 + 2 * S * kdim * 2
           + R * S * (kdim * 2 + HEAD_DIM * 4) + ts * R * tq * (LANES * 4 + 2 * HEAD_DIM * 4)
           + 3 * R * tq * LANES * 4 + 16 * ATT_ROWS * tq * 4 + 4 * tq * R * HEAD_DIM * 2)
    assert est <= VMEM_LIMIT_CAP
    return pl.pallas_call(
        _nsa_body,
        out_shape=jax.ShapeDtypeStruct((B, S, N_HEADS * HEAD_DIM), BF16),
        grid=(B, G, n_tiles // ts),
        in_specs=in_specs,
        out_specs=pl.BlockSpec((1, ts * tq, R * HEAD_DIM), lambda b, g, i: (b, i, g)),
        scratch_shapes=[pltpu.VMEM((n_tiles + 1, kdim, tq), BF16),
                        pltpu.VMEM((n_tiles + 1, kdim, tq), BF16),
                        pltpu.VMEM((n_tiles, R * tq, kdim), BF16),
                        pltpu.VMEM((n_tiles, R * tq, HEAD_DIM), F32),
                        pltpu.VMEM((ts * R * tq, LANES), F32),
                        pltpu.VMEM((ts * R * tq, 2 * HEAD_DIM), F32)],
        compiler_params=pltpu.CompilerParams(
            dimension_semantics=("arbitrary", "arbitrary", "arbitrary"),
            vmem_limit_bytes=VMEM_LIMIT_CAP),
        name="nsa_attention",
    )(proj, proj, proj, proj, proj, kcmp, vcmp, bias_c, bias_t, bias_far,
      jnp.asarray(overlap_t, BF16), jnp.asarray(aug_sel, BF16), jnp.asarray(aug_win, BF16),
      gates_t)


@functools.lru_cache(maxsize=None)
def _hgrn_masks():
    C = HGRN_CHUNK
    t = np.arange(C)[:, None]
    s = np.arange(C)[None, :]
    masks = [(t // HGRN_SUB == s // HGRN_SUB) & (s <= t)]
    half = C // 2
    while half >= HGRN_SUB:
        grp = 2 * half
        masks.append((t // grp == s // grp) & (t % grp >= half) & (s % grp < half))
        half //= 2
    assert np.array_equal(np.sum(masks, axis=0), (s <= t).astype(int))
    return np.stack(masks).astype(np.float32), (s <= t).astype(np.float32)


def _hgrn_body(q_ref, f_ref, i_ref, g_ref, lb_ref, gn_ref, msk_ref, tril_ref, o_ref,
               st_all, b_all, k_all, q_all, *, layer, heads):
    C = HGRN_CHUNK
    n_chunks = q_ref.shape[1] // C
    nt = (((1,), (1,)), ((), ()))

    lbp = lb_ref[...]
    e = jnp.exp(lbp - jnp.max(lbp, axis=0, keepdims=True))
    sm = e / jnp.sum(e, axis=0, keepdims=True)
    cum = sm[0:1]
    first = cum
    for d in range(1, layer + 1):
        cum = cum + sm[d:d + 1]
    lb_all = cum - first
    log_lb_all = jnp.log(lb_all)
    log_1m_all = jnp.log1p(-lb_all)
    gn_all = gn_ref[...]

    cols = lax.broadcasted_iota(jnp.int32, (1, C), 1)
    tril = tril_ref[...]

    @pl.when(pl.program_id(2) == 0)
    def _():
        st_all[...] = jnp.zeros(st_all.shape, F32)

    def chunk(c, carry):
        for h0 in range(0, heads, HGRN_HEADS_INTERLEAVED):
            stages = [one_head(c, hh) for hh in range(h0, h0 + HGRN_HEADS_INTERLEAVED)]
            for _ in zip(*stages):
                pass
        return carry

    def one_head(c, hh):
        sl = pl.ds(pl.multiple_of(c * C, C), C)
        hs = slice(hh * HEAD_DIM, (hh + 1) * HEAD_DIM)
        lb, log_lb, log_1m, gn = lb_all[:, hs], log_lb_all[:, hs], log_1m_all[:, hs], gn_all[:, hs]
        st_ref, b_sc, k_sc, q_sc = st_all.at[hh], b_all.at[hh], k_all.at[hh], q_all.at[hh]
        qr = q_ref[0, sl, hs].astype(F32)
        x = f_ref[0, sl, hs].astype(F32)
        v = i_ref[0, sl, hs].astype(F32)
        gr = g_ref[0, sl, hs].astype(F32)
        q = qr / (1.0 + jnp.exp2(qr * (-LOG2_E)))
        ex = jnp.exp2(jnp.abs(x) * (-LOG2_E))
        u = 1.0 + ex
        r1 = 1.0 / u
        k = (1.0 - lb) * jnp.where(x >= 0.0, ex * r1, r1)
        c2 = log_1m + (jnp.minimum(x, 0.0) - jnp.log(u))
        e2 = jnp.exp2(jnp.abs(log_lb - c2) * (-LOG2_E))
        log_f = jnp.maximum(log_lb, c2) + jnp.log(1.0 + e2)
        yield

        lf_hi = log_f.astype(BF16)
        lf_lo = (log_f - lf_hi.astype(F32)).astype(BF16)
        b = (jnp.dot(tril, lf_hi, preferred_element_type=F32)
             + jnp.dot(tril, lf_lo, preferred_element_type=F32)) * LOG2_E
        b_sc[...] = b
        k_sc[...] = k
        q_sc[...] = q
        yield

        vb = v.astype(BF16)
        qb = q.astype(BF16)
        kb = k.astype(BF16)
        st = st_ref[...]
        o = lax.dot_general((q * jnp.exp2(b)).astype(BF16), st.astype(BF16), nt,
                            preferred_element_type=F32)

        pieces = [jnp.broadcast_to(jnp.sum(q * k, axis=-1, keepdims=True), (C, C))] if HGRN_SUB == 1 else []
        for blk in range(C // HGRN_SUB if HGRN_SUB > 1 else 0):
            r0 = blk * HGRN_SUB
            bt = b_sc[r0:r0 + HGRN_SUB, :]
            qt = q_sc[r0:r0 + HGRN_SUB, :]
            arow = jnp.zeros((HGRN_SUB, C), F32)
            for s in range(HGRN_SUB):
                bs = b_sc[r0 + s:r0 + s + 1, :]
                ks = k_sc[r0 + s:r0 + s + 1, :]
                col = jnp.sum(jnp.exp2(bt - bs) * qt * ks, axis=-1, keepdims=True)
                arow = jnp.where(cols == r0 + s, col, arow)
            pieces.append(arow)
        a = jnp.where(msk_ref[0] > 0.5, jnp.concatenate(pieces, axis=0), 0.0)
        yield

        half = C // 2
        lvl = 1
        while half >= HGRN_SUB:
            grp = 2 * half
            anc = jnp.concatenate(
                [jnp.broadcast_to(b_sc[g0 + half - 1:g0 + half, :], (grp, b.shape[1]))
                 for g0 in range(0, C, grp)], axis=0)
            e = jnp.exp2(-jnp.abs(b - anc)).astype(BF16)
            al = lax.dot_general(qb * e, kb * e, nt, preferred_element_type=F32)
            a = jnp.where(msk_ref[lvl] > 0.5, al, a)
            half //= 2
            lvl += 1
            yield

        o = o + jnp.dot(a.astype(BF16), vb, preferred_element_type=F32)
        yield

        b_last = b_sc[C - 1:C, :]
        kh = (k * jnp.exp2(b_last - b)).astype(BF16)
        st_ref[...] = st * jnp.exp2(b_last) + jnp.dot(v.T.astype(BF16), kh,
                                                     preferred_element_type=F32)
        yield

        ms = jnp.mean(o * o, axis=-1, keepdims=True)
        o = o * lax.rsqrt(ms + RMS_EPS) * gn * (gr / (1.0 + jnp.exp2(gr * (-LOG2_E))))
        o_ref[0, sl, hs] = o.astype(o_ref.dtype)
        yield

    lax.fori_loop(0, n_chunks, chunk, 0)


def _hgrn(proj, hgrn_lb, onorm, layer, B, S):
    H = N_HEADS
    depth = hgrn_lb.shape[0]

    nh = HGRN_HEADS_PER_STEP
    width = nh * HEAD_DIM
    steps = H // nh

    nsb = HGRN_SEQ_BLOCKS
    sblk = S // nsb
    C = HGRN_CHUNK
    assert H % nh == 0 and S % nsb == 0 and sblk % C == 0

    def spec(part):
        return pl.BlockSpec((1, sblk, width),
                            lambda b, h, s, part=part: (b, s, part * steps + h))

    masks, tril = _hgrn_masks()
    est = (2 * 4 * sblk * width * proj.dtype.itemsize + 2 * sblk * width * 2
           + nh * 64 * C * C * 4)
    return pl.pallas_call(
        functools.partial(_hgrn_body, layer=layer, heads=nh),
        out_shape=jax.ShapeDtypeStruct((B, S, H * HEAD_DIM), BF16),
        grid=(B, steps, nsb),
        in_specs=[spec(0), spec(1), spec(2), spec(3),
                  pl.BlockSpec((depth, width), lambda b, h, s: (0, h)),
                  pl.BlockSpec((1, width), lambda b, h, s: (0, h)),
                  pl.BlockSpec(masks.shape, lambda b, h, s: (0, 0, 0)),
                  pl.BlockSpec((C, C), lambda b, h, s: (0, 0))],
        out_specs=pl.BlockSpec((1, sblk, width), lambda b, h, s: (b, s, h)),
        scratch_shapes=[pltpu.VMEM((nh, HEAD_DIM, HEAD_DIM), F32),
                        pltpu.VMEM((nh, C, HEAD_DIM), F32),
                        pltpu.VMEM((nh, C, HEAD_DIM), F32),
                        pltpu.VMEM((nh, C, HEAD_DIM), F32)],
        compiler_params=pltpu.CompilerParams(
            dimension_semantics=("arbitrary", "arbitrary", "arbitrary"),
            vmem_limit_bytes=_vmem_limit(est)),
        name="hgrn2_recurrence",
    )(proj, proj, proj, proj, hgrn_lb.astype(F32), onorm.reshape(1, -1).astype(F32),
      jnp.asarray(masks), jnp.asarray(tril, BF16))


def _mlp_body(x_ref, gi_ref, wu_ref, wd_ref, go_ref, o_ref, hn_ref, *, nf, tm):
    f = pl.program_id(1)

    def row_chunks(fn):
        def step(r, c):
            fn(pl.ds(pl.multiple_of(r * NORM_ROWS, NORM_ROWS), NORM_ROWS))
            return c

        lax.fori_loop(0, tm // NORM_ROWS, step, 0)

    @pl.when(f == 0)
    def _():
        g = gi_ref[...]

        def norm_in(rows):
            xs = x_ref[rows, :]
            ms = jnp.mean(xs * xs, axis=-1, keepdims=True)
            hn_ref[rows, :] = (xs * lax.rsqrt(ms + RMS_EPS) * g).astype(BF16)

        row_chunks(norm_in)
        o_ref[...] = jnp.zeros(o_ref.shape, F32)

    hid = jnp.dot(hn_ref[...], wu_ref[...], preferred_element_type=F32)
    hid = jnp.square(jnp.maximum(hid, 0.0)).astype(BF16)
    o_ref[...] += jnp.dot(hid, wd_ref[...], preferred_element_type=F32)

    @pl.when(f == nf - 1)
    def _():
        g = go_ref[...]

        def norm_out(rows):
            y = o_ref[rows, :]
            ms = jnp.mean(y * y, axis=-1, keepdims=True)
            o_ref[rows, :] = x_ref[rows, :] + y * lax.rsqrt(ms + RMS_EPS) * g

        row_chunks(norm_out)


def _mlp(xf, g_in, g_out, w_up, w_down, layer, *, tm=512, tf=1024):
    M, D = xf.shape
    F = w_up.shape[2]
    assert M % tm == 0 and F % tf == 0 and tm % NORM_ROWS == 0
    nf = F // tf
    est = (2 * tm * D * 4 + 2 * tm * D * 4 + tm * D * 2 + 2 * 2 * D * tf * 2
           + tm * tf * 6 + 2 * tm * D * 4)
    return pl.pallas_call(
        functools.partial(_mlp_body, nf=nf, tm=tm),
        out_shape=jax.ShapeDtypeStruct((M, D), F32),
        grid=(M // tm, nf),
        in_specs=[pl.BlockSpec((tm, D), lambda i, f: (i, 0)),
                  pl.BlockSpec((1, D), lambda i, f: (0, 0)),
                  pl.BlockSpec((None, D, tf), lambda i, f: (layer, 0, f)),
                  pl.BlockSpec((None, tf, D), lambda i, f: (layer, f, 0)),
                  pl.BlockSpec((1, D), lambda i, f: (0, 0))],
        out_specs=pl.BlockSpec((tm, D), lambda i, f: (i, 0)),
        scratch_shapes=[pltpu.VMEM((tm, D), BF16)],
        compiler_params=pltpu.CompilerParams(
            dimension_semantics=("arbitrary", "arbitrary"),
            vmem_limit_bytes=_vmem_limit(est)),
        name="mlp",
    )(xf, g_in.reshape(1, D).astype(F32), w_up.astype(BF16), w_down.astype(BF16),
      g_out.reshape(1, D).astype(F32))


def _nsa_layer(xf, B, S, g_in, g_out, rel_table, w_in, cmp_pe, cmp_w1, cmp_w2, w_out):
    D = xf.shape[1]
    G, R, Dh = N_GROUPS, HEADS_PER_GROUP, HEAD_DIM
    n_main = N_HEADS * Dh + 6 * G * Dh
    n_gate = 3 * N_HEADS
    w_all = w_in.astype(BF16)
    w_gate = jnp.pad(w_in[:, n_main:], ((0, 0), (0, LANES - n_gate))).astype(BF16)
    colscale = jnp.concatenate([jnp.full((N_HEADS * Dh,), Dh ** -0.5 * LOG2_E, F32),
                                jnp.ones((6 * G * Dh,), F32)])[None]

    proj, glog = _matmul(xf, w_all, tm=1024, tn=1024, norm_g=g_in, epi="colscale",
                         colscale=colscale, side_w=w_gate, n_out=n_main, out_dtype=BF16,
                         name="nsa_proj")
    gates_t = (glog[:, :n_gate].reshape(B, S, 3, G, R).transpose(0, 3, 1, 2, 4)
               .reshape(B, G, S, 3 * R))

    proj3 = proj.reshape(B, S, n_main)
    half = CMP_STRIDE * Dh
    kcmp, vcmp = _compress(proj3, cmp_w1.reshape(2, 2, half, Dh).astype(BF16),
                           cmp_pe.reshape(2, 2, 1, half).astype(F32),
                           cmp_w2.astype(BF16), B, S)

    bias_c, bias_t, bias_far = _bias_tables(rel_table, S)
    attn = _nsa_attention(proj3, kcmp, vcmp, bias_c, bias_t, bias_far, gates_t, B, S)
    return _matmul(attn.reshape(B * S, D), w_out.astype(BF16), tm=512, tn=D,
                   epi="resnorm", res=xf, res_g=g_out, out_dtype=F32, name="nsa_out")


def _hgrn_layer(xf, B, S, layer, g_in, g_out, w_in, hgrn_lb, onorm, w_out):
    D = xf.shape[1]
    proj = _matmul(xf, w_in, tm=1024, tn=1024, norm_g=g_in,
                   out_dtype=BF16, name="hgrn_proj")
    mixed = _hgrn(proj.reshape(B, S, 4 * D), hgrn_lb, onorm, layer, B, S)
    return _matmul(mixed.reshape(B * S, D), w_out.astype(BF16), tm=512, tn=D,
                   epi="resnorm", res=xf, res_g=g_out, out_dtype=F32, name="hgrn_out")


def kernel(x, norm_g, rel_table, nsa_w_in, nsa_cmp_pe, nsa_cmp_w1, nsa_cmp_w2, nsa_w_out,
           hgrn_w_in, hgrn_lb, hgrn_onorm, hgrn_w_out, mlp_w_up, mlp_w_down):
    B, S, D = x.shape
    depth = norm_g.shape[0]
    assert D == N_HEADS * HEAD_DIM and S % ATT_TILE == 0 and S % HGRN_CHUNK == 0
    xf = x.reshape(B * S, D).astype(F32)
    for layer in range(depth):
        j = layer // 2
        if layer % 2 == 0:
            xf = _nsa_layer(xf, B, S, norm_g[layer, 0], norm_g[layer, 1], rel_table,
                            nsa_w_in[j], nsa_cmp_pe[j], nsa_cmp_w1[j], nsa_cmp_w2[j],
                            nsa_w_out[j])
        else:
            xf = _hgrn_layer(xf, B, S, layer, norm_g[layer, 0], norm_g[layer, 1],
                             hgrn_w_in[j], hgrn_lb, hgrn_onorm[j], hgrn_w_out[j])
        xf = _mlp(xf, norm_g[layer, 2], norm_g[layer, 3], mlp_w_up, mlp_w_down, layer)
    return xf.reshape(B, S, D).astype(x.dtype)
```

```python
import functools
import math

import numpy as np
import jax
import jax.numpy as jnp
from jax import lax
from jax.experimental import pallas as pl
from jax.experimental.pallas import tpu as pltpu

F32 = jnp.float32
BF16 = jnp.bfloat16

N_HEADS = 16
N_GROUPS = 4
HEADS_PER_GROUP = N_HEADS // N_GROUPS
HEAD_DIM = 128
CMP_BLOCK = 32
CMP_STRIDE = 16
SEL_BLOCK = 64
SEL_TOP_N = 8
WINDOW = 512
FORCE_SCORE = 1.0e4
REL_BUCKETS = 32
REL_MAX_DIST = 128
RMS_EPS = 1e-6
NEG_INF = -1.0e30
LOG2_E = math.log2(math.e)

LANES = 128
SUBLANES = 8
VMEM_BYTES_V7X = 64 * 1024 * 1024
VMEM_LIMIT_CAP = VMEM_BYTES_V7X - 8 * 1024 * 1024

ATT_TILE = 256
ATT_ROWS = 128
SEL_TILES_PER_STEP = 4
ATT_TILES_PER_STEP = 2
PREV_TILE, DIAG_TILE, EDGE_TILE = 0, 1, 2
N_BIAS_TILES = 3
HGRN_CHUNK = 128
HGRN_HEADS_PER_STEP = 16
HGRN_HEADS_INTERLEAVED = 8
HGRN_SEQ_BLOCKS = 2
NORM_ROWS = 256


VMEM_LIMIT_FLOOR = 32 * 1024 * 1024


def _vmem_limit(nbytes):
    return int(min(VMEM_LIMIT_CAP, max(VMEM_LIMIT_FLOOR, nbytes)))


def _mm_body(*refs, norm, epi, side, tm):
    it = iter(refs)
    x_ref = next(it)
    g_ref = next(it) if norm else None
    w_ref = next(it)
    ws_ref = next(it) if side else None
    cs_ref = next(it) if epi == "colscale" else None
    res_ref = next(it) if epi == "resnorm" else None
    go_ref = next(it) if epi == "resnorm" else None
    o_ref = next(it)
    os_ref = next(it) if side else None
    hn_ref = next(it) if norm else None

    j = pl.program_id(1)

    if norm:
        @pl.when(j == 0)
        def _():
            g = g_ref[...]

            def step(r, c):
                rows = pl.ds(pl.multiple_of(r * NORM_ROWS, NORM_ROWS), NORM_ROWS)
                xs = x_ref[rows, :]
                ms = jnp.mean(xs * xs, axis=-1, keepdims=True)
                hn_ref[rows, :] = (xs * lax.rsqrt(ms + RMS_EPS) * g).astype(BF16)
                return c

            lax.fori_loop(0, tm // NORM_ROWS, step, 0)

        lhs = hn_ref[...]
    else:
        lhs = x_ref[...]

    if side:
        @pl.when(j == 0)
        def _():
            os_ref[...] = jnp.dot(lhs, ws_ref[...], preferred_element_type=F32)

    acc = jnp.dot(lhs, w_ref[...].astype(BF16), preferred_element_type=F32)
    if epi == "colscale":
        acc = acc * cs_ref[...]
    elif epi == "resnorm":
        ms = jnp.mean(acc * acc, axis=-1, keepdims=True)
        acc = res_ref[...] + acc * lax.rsqrt(ms + RMS_EPS) * go_ref[...]
    o_ref[...] = acc.astype(o_ref.dtype)


def _matmul(x, w, *, tm, tn, norm_g=None, epi="none", colscale=None, res=None, res_g=None,
            side_w=None, n_out=None, out_dtype=BF16, name="mm"):
    M, K = x.shape
    N = w.shape[1] if n_out is None else n_out
    assert N <= w.shape[1]
    norm = norm_g is not None
    side = side_w is not None
    assert M % tm == 0 and N % tn == 0
    assert not (epi == "resnorm" and tn != N)

    in_specs = [pl.BlockSpec((tm, K), lambda i, j: (i, 0))]
    args = [x]
    if norm:
        in_specs.append(pl.BlockSpec((1, K), lambda i, j: (0, 0)))
        args.append(norm_g.reshape(1, K).astype(F32))
    in_specs.append(pl.BlockSpec((K, tn), lambda i, j: (0, j)))
    args.append(w)
    ns = side_w.shape[1] if side else 0
    if side:
        in_specs.append(pl.BlockSpec((K, ns), lambda i, j: (0, 0)))
        args.append(side_w)
    if epi == "colscale":
        in_specs.append(pl.BlockSpec((1, tn), lambda i, j: (0, j)))
        args.append(colscale)
    if epi == "resnorm":
        in_specs.append(pl.BlockSpec((tm, tn), lambda i, j: (i, j)))
        args.append(res)
        in_specs.append(pl.BlockSpec((1, tn), lambda i, j: (0, j)))
        args.append(res_g.reshape(1, N).astype(F32))

    out_shape = jax.ShapeDtypeStruct((M, N), out_dtype)
    out_specs = pl.BlockSpec((tm, tn), lambda i, j: (i, j))
    if side:
        out_shape = (out_shape, jax.ShapeDtypeStruct((M, ns), F32))
        out_specs = (out_specs, pl.BlockSpec((tm, ns), lambda i, j: (i, 0)))

    xb = x.dtype.itemsize
    ob = jnp.dtype(out_dtype).itemsize
    wb = w.dtype.itemsize
    est = (2 * tm * K * xb + 2 * K * tn * wb + (K * tn * 2 if wb > 2 else 0) + 2 * tm * tn * ob
           + (tm * K * 2 if norm else 0)
           + (2 * tm * tn * 4 if epi == "resnorm" else 0) + 3 * tm * tn * 4
           + 2 * K * ns * 2 + 3 * tm * ns * 4)

    return pl.pallas_call(
        functools.partial(_mm_body, norm=norm, epi=epi, side=side, tm=tm),
        out_shape=out_shape,
        grid=(M // tm, N // tn),
        in_specs=in_specs,
        out_specs=out_specs,
        scratch_shapes=[pltpu.VMEM((tm, K), BF16)] if norm else [],
        compiler_params=pltpu.CompilerParams(
            dimension_semantics=("arbitrary", "arbitrary"),
            vmem_limit_bytes=_vmem_limit(est)),
        name=name,
    )(*args)


def _rel_bucket_np(dist):
    n = np.maximum(dist, 0)
    max_exact = REL_BUCKETS // 2
    nf = np.maximum(n, 1).astype(np.float32)
    ratio = np.log(nf / np.float32(max_exact)) / np.float32(math.log(REL_MAX_DIST / max_exact))
    large = max_exact + (ratio * np.float32(REL_BUCKETS - max_exact)).astype(np.int32)
    large = np.minimum(large, REL_BUCKETS - 1)
    return np.where(n < max_exact, n, large).astype(np.int32)


@functools.lru_cache(maxsize=None)
def _static_maps(seq):
    n_cmp = LANES
    pos = np.arange(seq, dtype=np.int32)[:, None]
    c_end = np.arange(n_cmp, dtype=np.int32)[None, :] * CMP_STRIDE + CMP_BLOCK - 1
    bucket_c = _rel_bucket_np(pos - c_end)
    t = np.arange(ATT_TILE, dtype=np.int32)[:, None]
    k = np.arange(ATT_TILE, dtype=np.int32)[None, :]
    bucket_t = np.stack([_rel_bucket_np(t - k), _rel_bucket_np(ATT_TILE + t - k)])
    assert _rel_bucket_np(np.array([ATT_TILE + 1]))[0] == REL_BUCKETS - 1
    nc = seq // CMP_STRIDE - CMP_BLOCK // CMP_STRIDE + 1
    nb = seq // SEL_BLOCK
    c_start = np.arange(nc)[:, None] * CMP_STRIDE
    b_start = np.arange(nb)[None, :] * SEL_BLOCK
    ov = ((c_start <= b_start + SEL_BLOCK - 1) & (c_start + CMP_BLOCK - 1 >= b_start))
    overlap = np.zeros((LANES, LANES), np.float32)
    overlap[:nc, :nb] = ov
    return bucket_c, bucket_t, overlap


def _bias_body(tab_ref, bc_ref, bt_ref, oc_ref, ot_ref, of_ref, *, seq):
    h = pl.program_id(0)

    lane = lax.broadcasted_iota(jnp.int32, (SUBLANES, LANES), 1)
    tab_row = jnp.zeros((SUBLANES, LANES), F32)
    for b in range(REL_BUCKETS):
        tab_row = jnp.where(lane == b, tab_ref[b, h], tab_row)

    def lookup(bmap):
        rows = bmap.shape[0]
        tab = jnp.broadcast_to(tab_row[0:1], (rows, LANES))
        return jnp.concatenate(
            [jnp.take_along_axis(tab, bmap[:, c0:c0 + LANES], axis=1)
             for c0 in range(0, bmap.shape[1], LANES)], axis=1)

    def step(r, c):
        rows = pl.ds(pl.multiple_of(r * ATT_TILE, ATT_TILE), ATT_TILE)
        oc_ref[0, rows, :] = lookup(bc_ref[rows, :]) * LOG2_E
        return c

    lax.fori_loop(0, seq // ATT_TILE, step, 0)

    tt = lax.broadcasted_iota(jnp.int32, (ATT_TILE, ATT_TILE), 0)
    kk = lax.broadcasted_iota(jnp.int32, (ATT_TILE, ATT_TILE), 1)
    far = tab_ref[REL_BUCKETS - 1, h]
    of_ref[0] = jnp.full(of_ref.shape[1:], far * LOG2_E, F32)
    ot_ref[PREV_TILE, 0] = (lookup(bt_ref[1]) - far) * LOG2_E
    ot_ref[DIAG_TILE, 0] = jnp.where(kk <= tt, (lookup(bt_ref[0]) - far) * LOG2_E, NEG_INF)
    ot_ref[EDGE_TILE, 0] = jnp.where(kk > tt, 0.0, NEG_INF)


def _bias_tables(rel_table, seq):
    bucket_c, bucket_t, _ = _static_maps(seq)
    return pl.pallas_call(
        functools.partial(_bias_body, seq=seq),
        out_shape=(jax.ShapeDtypeStruct((N_HEADS, seq, LANES), F32),
                   jax.ShapeDtypeStruct((N_BIAS_TILES, N_HEADS, ATT_TILE, ATT_TILE), F32),
                   jax.ShapeDtypeStruct((N_HEADS, SUBLANES, LANES), F32)),
        grid=(N_HEADS,),
        in_specs=[pl.BlockSpec(memory_space=pltpu.SMEM),
                  pl.BlockSpec((seq, LANES), lambda h: (0, 0)),
                  pl.BlockSpec((2, ATT_TILE, ATT_TILE), lambda h: (0, 0, 0))],
        out_specs=(pl.BlockSpec((1, seq, LANES), lambda h: (h, 0, 0)),
                   pl.BlockSpec((N_BIAS_TILES, 1, ATT_TILE, ATT_TILE), lambda h: (0, h, 0, 0)),
                   pl.BlockSpec((1, SUBLANES, LANES), lambda h: (h, 0, 0))),
        compiler_params=pltpu.CompilerParams(dimension_semantics=("arbitrary",)),
        name="rel_bias",
    )(rel_table.astype(F32), jnp.asarray(bucket_c), jnp.asarray(bucket_t))


def _compress_body(xk_ref, xv_ref, w1_ref, pe_ref, w2_ref, ok_ref, ov_ref, x_sc):
    n_grp = xk_ref.shape[1] // CMP_STRIDE

    def one(x_ref, idx, o_ref):
        x_sc[...] = x_ref[0].astype(F32)
        x = jnp.concatenate([x_sc[pl.ds(t, n_grp, stride=CMP_STRIDE), :]
                             for t in range(CMP_STRIDE)], axis=1)
        a0 = jnp.dot((x + pe_ref[idx, 0]).astype(BF16), w1_ref[idx, 0],
                     preferred_element_type=F32)
        a1 = jnp.dot((x + pe_ref[idx, 1]).astype(BF16), w1_ref[idx, 1],
                     preferred_element_type=F32)
        pre = a0 + pltpu.roll(a1, LANES - 1, 0)
        hid = jax.nn.gelu(pre).astype(BF16)
        o_ref[0, 0] = jnp.dot(hid, w2_ref[idx], preferred_element_type=F32).astype(BF16)

    one(xk_ref, 0, ok_ref)
    one(xv_ref, 1, ov_ref)


def _compress(proj, w1, pe, w2, B, S):
    G = N_GROUPS
    assert S // CMP_STRIDE == LANES
    half = CMP_STRIDE * HEAD_DIM
    q_cols = N_HEADS

    def spec_x(slot):
        return pl.BlockSpec((1, S, HEAD_DIM), lambda b, g, slot=slot: (b, 0, q_cols + slot * G + g))

    spec_o = pl.BlockSpec((1, 1, LANES, HEAD_DIM), lambda b, g: (b, g, 0, 0))
    out = jax.ShapeDtypeStruct((B, G, LANES, HEAD_DIM), BF16)
    return pl.pallas_call(
        _compress_body,
        out_shape=(out, out),
        grid=(B, G),
        in_specs=[spec_x(0), spec_x(1),
                  pl.BlockSpec((2, 2, half, HEAD_DIM), lambda b, g: (0, 0, 0, 0)),
                  pl.BlockSpec((2, 2, 1, half), lambda b, g: (0, 0, 0, 0)),
                  pl.BlockSpec((2, HEAD_DIM, HEAD_DIM), lambda b, g: (0, 0, 0))],
        out_specs=(spec_o, spec_o),
        scratch_shapes=[pltpu.VMEM((S, HEAD_DIM), F32)],
        compiler_params=pltpu.CompilerParams(dimension_semantics=("arbitrary", "arbitrary")),
        name="nsa_compress",
    )(proj, proj, w1, pe, w2)


def _nsa_body(q_ref, ks_ref, vs_ref, kw_ref, vw_ref, kc_ref, vc_ref, bc_ref, bt_ref, far_ref,
              ovl_ref, augs_ref, augw_ref, gl_ref, o_ref, kts_sc, ktw_sc, qa_sc, oc_sc, m_sc,
              acc_sc):
    R = HEADS_PER_GROUP
    tq = ATT_TILE
    qi = pl.program_id(2)
    n_tiles = ks_ref.shape[1] // tq
    nb = ovl_ref.shape[0]
    nt = (((1,), (1,)), ((), ()))

    cidx = lax.broadcasted_iota(jnp.int32, (1, 1, LANES), 2)
    jb = lax.broadcasted_iota(jnp.int32, (nb, 1), 0)
    lane = lax.broadcasted_iota(jnp.int32, (1, LANES), 1)

    def select_tile(t):
        rows = pl.ds(pl.multiple_of(t * tq, tq), tq)
        q = q_ref[0, rows, :]
        q4 = jnp.concatenate([q[:, r * HEAD_DIM:(r + 1) * HEAD_DIM] for r in range(R)], axis=0)
        qa_sc[t, :, :HEAD_DIM] = q4
        pos3 = t * tq + lax.broadcasted_iota(jnp.int32, (1, tq, 1), 1)

        sc = lax.dot_general(q4, kc_ref[0, 0], nt, preferred_element_type=F32)
        sc = sc.reshape(R, tq, LANES) + bc_ref[:, rows, :]
        valid = (cidx * CMP_STRIDE + (CMP_BLOCK - 1) <= pos3) & (cidx < LANES - 1)
        sc = jnp.where(valid, sc, NEG_INF)
        mc = jnp.max(sc, axis=-1, keepdims=True)
        ec = jnp.exp2(sc - mc)
        pc = ec / jnp.sum(ec, axis=-1, keepdims=True)
        pc = jnp.where(pos3 >= CMP_BLOCK - 1, pc, 0.0)
        oc_sc[t] = jnp.dot(pc.reshape(R * tq, LANES).astype(BF16), vc_ref[0, 0],
                           preferred_element_type=F32)

        psum = pc[0]
        for r in range(1, R):
            psum = psum + pc[r]
        p_hi = psum.astype(BF16)
        p_lo = (psum - p_hi.astype(F32)).astype(BF16)
        ovt = ovl_ref[...]
        imp = (lax.dot_general(ovt, p_hi, nt, preferred_element_type=F32)
               + lax.dot_general(ovt, p_lo, nt, preferred_element_type=F32))
        pos_t = t * tq + lax.broadcasted_iota(jnp.int32, (1, tq), 1)
        q_blk = lax.shift_right_logical(pos_t, int(math.log2(SEL_BLOCK)))
        forced = (jb == 0) | (jb == q_blk) | (jb == q_blk - 1)
        future = jb > q_blk
        imp = jnp.where(forced, FORCE_SCORE, jnp.where(future, -1.0, imp))
        cnt = jnp.zeros((nb, tq), F32)
        for i in range(nb):
            row = imp[i:i + 1, :]
            beats = (row > imp) | ((row == imp) & (jb > i))
            cnt = cnt + jnp.where(beats, 1.0, 0.0)
        sel_t = jnp.where(cnt < float(min(SEL_TOP_N, nb)), 1.0, 0.0)
        sel = jnp.concatenate([sel_t, jnp.zeros((LANES - nb, tq), F32)], axis=0).T

        sel_pad = jnp.where(lane < nb, (sel - 1.0) * (-NEG_INF), 0.0)
        for r in range(R):
            far = jnp.broadcast_to(far_ref[r, 0:1, :], (tq, LANES))
            far_hi = far.astype(BF16).astype(F32)
            pad = jnp.where(lane == nb, far_hi, jnp.where(lane == nb + 1, far - far_hi, sel_pad))
            pad = jnp.where(lane == nb + 2, 1.0, pad)
            qa_sc[t, r * tq:(r + 1) * tq, HEAD_DIM:] = pad.astype(BF16)

    @pl.when(qi == 0)
    def _():
        def tr(j, c):
            rows = pl.ds(pl.multiple_of(j * tq, tq), tq)
            kts_sc[j, :HEAD_DIM, :] = ks_ref[0, rows, :].T
            kts_sc[j, HEAD_DIM:, :] = augs_ref[j]
            ktw_sc[j, :HEAD_DIM, :] = kw_ref[0, rows, :].T
            ktw_sc[j, HEAD_DIM:, :] = augw_ref[0]
            return c

        lax.fori_loop(0, n_tiles, tr, 0)
        zeros = jnp.zeros((HEAD_DIM, tq), BF16)
        kts_sc[n_tiles, :HEAD_DIM, :] = zeros
        kts_sc[n_tiles, HEAD_DIM:, :] = augs_ref[n_tiles]
        ktw_sc[n_tiles, :HEAD_DIM, :] = zeros
        ktw_sc[n_tiles, HEAD_DIM:, :] = augw_ref[1]

        def sel_group(gi, c):
            for u in range(SEL_TILES_PER_STEP):
                select_tile(gi * SEL_TILES_PER_STEP + u)
            return c

        lax.fori_loop(0, n_tiles // SEL_TILES_PER_STEP, sel_group, 0)

    ones = jnp.ones((tq, HEAD_DIM), BF16)
    n_chunks = R * tq // ATT_ROWS
    chunks_per_head = tq // ATT_ROWS
    dead = n_tiles
    step_tiles = [ATT_TILES_PER_STEP * qi + u for u in range(ATT_TILES_PER_STEP)]

    def keys(kt_sc, tiles):
        return jnp.concatenate([kt_sc[t] for t in tiles], axis=1)

    def values(v_ref, tiles):
        parts = []
        for t in tiles:
            rows = pl.ds(pl.multiple_of(t * tq, tq), tq)
            parts.append(jnp.concatenate([v_ref[0, rows, :], ones], axis=1))
        return jnp.concatenate(parts, axis=0)

    def logits(q_sc, ci, kt, kinds):
        r, hh = divmod(ci, chunks_per_head)
        rs = slice(ci * ATT_ROWS, (ci + 1) * ATT_ROWS)
        qs = slice(hh * ATT_ROWS, (hh + 1) * ATT_ROWS)
        s = jnp.dot(q_sc[rs, :], kt, preferred_element_type=F32)
        if all(kd is None for kd in kinds):
            return s
        parts = [s[:, i * tq:(i + 1) * tq] for i in range(len(kinds))]
        return jnp.concatenate([p if kd is None else p + bt_ref[kd, r, qs, :]
                                for p, kd in zip(parts, kinds)], axis=1)

    def probs(s, m):
        return jnp.concatenate([jnp.exp2(s[:, k0:k0 + LANES] - m)
                                for k0 in range(0, s.shape[1], LANES)], axis=1).astype(BF16)

    m_sc[...] = jnp.full(m_sc.shape, NEG_INF, F32)
    acc_sc[...] = jnp.zeros(acc_sc.shape, F32)

    def sel_update(u, t, kt, vv, kinds):
        for ci in range(n_chunks):
            rs = slice((u * n_chunks + ci) * ATT_ROWS, (u * n_chunks + ci + 1) * ATT_ROWS)
            s = logits(qa_sc.at[t], ci, kt, kinds)
            m_old = m_sc[rs]
            m_new = jnp.maximum(m_old, jnp.max(s, axis=-1, keepdims=True))
            alpha = jnp.exp2(m_old - m_new)
            m_sc[rs] = m_new
            pv = jnp.dot(probs(s, m_new), vv, preferred_element_type=F32)
            acc_sc[rs] = acc_sc[rs] * jnp.concatenate([alpha, alpha], axis=1) + pv

    assert ATT_TILES_PER_STEP == 2
    ta, tb = step_tiles

    def far_pair(pi, c):
        pair_tiles = (2 * pi, 2 * pi + 1)
        kt, vv = keys(kts_sc, pair_tiles), values(vs_ref, pair_tiles)
        sel_update(0, ta, kt, vv, (None, None))
        sel_update(1, tb, kt, vv, (None, None))
        return c

    lax.fori_loop(0, jnp.maximum(qi - 1, 0), far_pair, 0)

    @pl.when(qi >= 1)
    def _():
        pair_tiles = (ta - 2, ta - 1)
        kt, vv = keys(kts_sc, pair_tiles), values(vs_ref, pair_tiles)
        sel_update(0, ta, kt, vv, (None, PREV_TILE))
        sel_update(1, tb, kt, vv, (None, None))

    sel_update(0, ta, kts_sc[ta], values(vs_ref, (ta,)), (DIAG_TILE,))
    sel_update(1, tb, keys(kts_sc, (ta, tb)), values(vs_ref, (ta, tb)), (PREV_TILE, DIAG_TILE))

    n_win = WINDOW // tq
    w_kinds = (EDGE_TILE,) + (None,) * (n_win - 2) + (PREV_TILE, DIAG_TILE)
    gates = jax.nn.sigmoid(gl_ref[0, 0])
    for u, t in enumerate(step_tiles):
        acc = acc_sc[u * R * tq:(u + 1) * R * tq, :]
        o_sel = acc[:, :HEAD_DIM] / acc[:, HEAD_DIM:]

        w_tiles = [t - n_win + w for w in range(n_win + 1)]
        kt = keys(ktw_sc, [jnp.where(w >= 0, w, dead) for w in w_tiles])
        vv = values(vw_ref, [jnp.maximum(w, 0) for w in w_tiles])
        o_parts = []
        for ci in range(n_chunks):
            s = logits(qa_sc.at[t], ci, kt, w_kinds)
            pv = jnp.dot(probs(s, jnp.max(s, axis=-1, keepdims=True)), vv,
                         preferred_element_type=F32)
            o_parts.append(pv[:, :HEAD_DIM] / pv[:, HEAD_DIM:])
        o_win = jnp.concatenate(o_parts, axis=0)

        o_cmp = oc_sc[t]
        g = gates[u * tq:(u + 1) * tq]
        outs = []
        for r in range(R):
            hs = slice(r * tq, (r + 1) * tq)
            outs.append(g[:, r:r + 1] * o_cmp[hs] + g[:, R + r:R + r + 1] * o_sel[hs]
                        + g[:, 2 * R + r:2 * R + r + 1] * o_win[hs])
        o_ref[0, u * tq:(u + 1) * tq, :] = jnp.concatenate(outs, axis=1).astype(o_ref.dtype)


def _nsa_attention(proj, kcmp, vcmp, bias_c, bias_t, bias_far, gates_t, B, S):
    assert WINDOW % ATT_TILE == 0 and S % ATT_TILE == 0
    assert S // CMP_STRIDE == LANES and S // SEL_BLOCK <= LANES
    R, G, tq = HEADS_PER_GROUP, N_GROUPS, ATT_TILE
    n_tiles = S // tq
    ts = ATT_TILES_PER_STEP
    assert n_tiles % ts == 0
    nb = S // SEL_BLOCK
    _, _, overlap = _static_maps(S)
    overlap_t = np.ascontiguousarray(overlap.T[:nb])
    assert nb + 3 <= LANES and n_tiles % 2 == 0
    blk_of_key = (np.arange(S) // SEL_BLOCK).reshape(n_tiles, 1, tq)
    aug_sel = np.zeros((n_tiles + 1, LANES, tq), np.float32)
    aug_sel[:n_tiles] = np.arange(LANES).reshape(1, LANES, 1) == blk_of_key
    aug_sel[:n_tiles, nb:nb + 2, :] = 1.0
    aug_sel[n_tiles, nb + 2, :] = NEG_INF
    aug_win = np.zeros((2, LANES, tq), np.float32)
    aug_win[0, nb:nb + 2, :] = 1.0
    aug_win[1, nb + 2, :] = NEG_INF
    q_cols = N_HEADS

    def kv_spec(slot):
        return pl.BlockSpec((1, S, HEAD_DIM),
                            lambda b, g, i, slot=slot: (b, 0, q_cols + slot * G + g))

    cmp_spec = pl.BlockSpec((1, 1, LANES, HEAD_DIM), lambda b, g, i: (b, g, 0, 0))
    in_specs = [
        pl.BlockSpec((1, S, R * HEAD_DIM), lambda b, g, i: (b, 0, g)),
        kv_spec(2), kv_spec(3), kv_spec(4), kv_spec(5),
        cmp_spec, cmp_spec,
        pl.BlockSpec((R, S, LANES), lambda b, g, i: (g, 0, 0)),
        pl.BlockSpec((N_BIAS_TILES, R, tq, tq), lambda b, g, i: (0, g, 0, 0)),
        pl.BlockSpec((R, SUBLANES, LANES), lambda b, g, i: (g, 0, 0)),
        pl.BlockSpec((nb, LANES), lambda b, g, i: (0, 0)),
        pl.BlockSpec((n_tiles + 1, LANES, tq), lambda b, g, i: (0, 0, 0)),
        pl.BlockSpec((2, LANES, tq), lambda b, g, i: (0, 0, 0)),
        pl.BlockSpec((1, 1, ts * tq, 3 * R), lambda b, g, i: (b, g, i, 0)),
    ]
    kdim = HEAD_DIM + LANES
    assert n_tiles % SEL_TILES_PER_STEP == 0
    est = (2 * 4 * S * HEAD_DIM * 2 + 2 * N_BIAS_TILES * R * tq * tq * 4 + 2 * R * S * LANES * 4
           + 2 * S * R * HEAD_DIM * 2 + 2 * 2 * S * LANES * 2 + 2 * S * kdim * 2
           + R * S * (kdim * 2 + HEAD_DIM * 4) + ts * R * tq * (LANES * 4 + 2 * HEAD_DIM * 4)
           + 3 * R * tq * LANES * 4 + 16 * ATT_ROWS * tq * 4 + 4 * tq * R * HEAD_DIM * 2)
    assert est <= VMEM_LIMIT_CAP
    return pl.pallas_call(
        _nsa_body,
        out_shape=jax.ShapeDtypeStruct((B, S, N_HEADS * HEAD_DIM), BF16),
        grid=(B, G, n_tiles // ts),
        in_specs=in_specs,
        out_specs=pl.BlockSpec((1, ts * tq, R * HEAD_DIM), lambda b, g, i: (b, i, g)),
        scratch_shapes=[pltpu.VMEM((n_tiles + 1, kdim, tq), BF16),
                        pltpu.VMEM((n_tiles + 1, kdim, tq), BF16),
                        pltpu.VMEM((n_tiles, R * tq, kdim), BF16),
                        pltpu.VMEM((n_tiles, R * tq, HEAD_DIM), F32),
                        pltpu.VMEM((ts * R * tq, LANES), F32),
                        pltpu.VMEM((ts * R * tq, 2 * HEAD_DIM), F32)],
        compiler_params=pltpu.CompilerParams(
            dimension_semantics=("arbitrary", "arbitrary", "arbitrary"),
            vmem_limit_bytes=VMEM_LIMIT_CAP),
        name="nsa_attention",
    )(proj, proj, proj, proj, proj, kcmp, vcmp, bias_c, bias_t, bias_far,
      jnp.asarray(overlap_t, BF16), jnp.asarray(aug_sel, BF16), jnp.asarray(aug_win, BF16),
      gates_t)


@functools.lru_cache(maxsize=None)
def _hgrn_masks():
    C = HGRN_CHUNK
    t = np.arange(C)[:, None]
    s = np.arange(C)[None, :]
    masks = [s == t]
    half = C // 2
    while half >= 1:
        grp = 2 * half
        masks.append((t // grp == s // grp) & (t % grp >= half) & (s % grp < half))
        half //= 2
    assert np.array_equal(np.sum(masks, axis=0), (s <= t).astype(int))
    return np.stack(masks).astype(np.float32), (s <= t).astype(np.float32)


def _hgrn_body(q_ref, f_ref, i_ref, g_ref, lb_ref, gn_ref, msk_ref, tril_ref, o_ref,
               st_all, b_all, *, layer, heads):
    C = HGRN_CHUNK
    n_chunks = q_ref.shape[1] // C
    nt = (((1,), (1,)), ((), ()))

    lbp = lb_ref[...]
    e = jnp.exp(lbp - jnp.max(lbp, axis=0, keepdims=True))
    sm = e / jnp.sum(e, axis=0, keepdims=True)
    cum = sm[0:1]
    first = cum
    for d in range(1, layer + 1):
        cum = cum + sm[d:d + 1]
    lb_all = cum - first
    log_lb_all = jnp.log(lb_all)
    log_1m_all = jnp.log1p(-lb_all)
    gn_all = gn_ref[...]

    tril = tril_ref[...]

    @pl.when(pl.program_id(2) == 0)
    def _():
        st_all[...] = jnp.zeros(st_all.shape, F32)

    def chunk(c, carry):
        for h0 in range(0, heads, HGRN_HEADS_INTERLEAVED):
            stages = [one_head(c, hh) for hh in range(h0, h0 + HGRN_HEADS_INTERLEAVED)]
            for _ in zip(*stages):
                pass
        return carry

    def one_head(c, hh):
        sl = pl.ds(pl.multiple_of(c * C, C), C)
        hs = slice(hh * HEAD_DIM, (hh + 1) * HEAD_DIM)
        lb, log_lb, log_1m, gn = lb_all[:, hs], log_lb_all[:, hs], log_1m_all[:, hs], gn_all[:, hs]
        st_ref, b_sc = st_all.at[hh], b_all.at[hh]
        qr = q_ref[0, sl, hs].astype(F32)
        x = f_ref[0, sl, hs].astype(F32)
        v = i_ref[0, sl, hs].astype(F32)
        gr = g_ref[0, sl, hs].astype(F32)
        q = qr / (1.0 + jnp.exp2(qr * (-LOG2_E)))
        ex = jnp.exp2(jnp.abs(x) * (-LOG2_E))
        u = 1.0 + ex
        r1 = 1.0 / u
        k = (1.0 - lb) * jnp.where(x >= 0.0, ex * r1, r1)
        c2 = log_1m + (jnp.minimum(x, 0.0) - jnp.log(u))
        e2 = jnp.exp2(jnp.abs(log_lb - c2) * (-LOG2_E))
        log_f = jnp.maximum(log_lb, c2) + jnp.log(1.0 + e2)
        yield

        lf_hi = log_f.astype(BF16)
        lf_lo = (log_f - lf_hi.astype(F32)).astype(BF16)
        b = (jnp.dot(tril, lf_hi, preferred_element_type=F32)
             + jnp.dot(tril, lf_lo, preferred_element_type=F32)) * LOG2_E
        b_sc[...] = b
        yield

        vb = v.astype(BF16)
        qb = q.astype(BF16)
        kb = k.astype(BF16)
        st = st_ref[...]
        o = lax.dot_general((q * jnp.exp2(b)).astype(BF16), st.astype(BF16), nt,
                            preferred_element_type=F32)

        a = jnp.where(msk_ref[0] > 0.5, jnp.sum(q * k, axis=-1, keepdims=True), 0.0)
        yield

        half = C // 2
        lvl = 1
        while half >= 1:
            grp = 2 * half
            anc = jnp.concatenate(
                [jnp.broadcast_to(b_sc[g0 + half - 1:g0 + half, :], (grp, b.shape[1]))
                 for g0 in range(0, C, grp)], axis=0)
            e = jnp.exp2(-jnp.abs(b - anc)).astype(BF16)
            al = lax.dot_general(qb * e, kb * e, nt, preferred_element_type=F32)
            a = jnp.where(msk_ref[lvl] > 0.5, al, a)
            half //= 2
            lvl += 1
            yield

        o = o + jnp.dot(a.astype(BF16), vb, preferred_element_type=F32)
        yield

        b_last = b_sc[C - 1:C, :]
        kh = (k * jnp.exp2(b_last - b)).astype(BF16)
        st_ref[...] = st * jnp.exp2(b_last) + jnp.dot(v.T.astype(BF16), kh,
                                                     preferred_element_type=F32)
        yield

        ms = jnp.mean(o * o, axis=-1, keepdims=True)
        o = o * lax.rsqrt(ms + RMS_EPS) * gn * (gr / (1.0 + jnp.exp2(gr * (-LOG2_E))))
        o_ref[0, sl, hs] = o.astype(o_ref.dtype)
        yield

    lax.fori_loop(0, n_chunks, chunk, 0)


def _hgrn(proj, hgrn_lb, onorm, layer, B, S):
    H = N_HEADS
    depth = hgrn_lb.shape[0]

    nh = HGRN_HEADS_PER_STEP
    width = nh * HEAD_DIM
    steps = H // nh

    nsb = HGRN_SEQ_BLOCKS
    sblk = S // nsb
    C = HGRN_CHUNK
    assert H % nh == 0 and S % nsb == 0 and sblk % C == 0

    def spec(part):
        return pl.BlockSpec((1, sblk, width),
                            lambda b, h, s, part=part: (b, s, part * steps + h))

    masks, tril = _hgrn_masks()
    est = (2 * 4 * sblk * width * proj.dtype.itemsize + 2 * sblk * width * 2
           + nh * 64 * C * C * 4)
    return pl.pallas_call(
        functools.partial(_hgrn_body, layer=layer, heads=nh),
        out_shape=jax.ShapeDtypeStruct((B, S, H * HEAD_DIM), BF16),
        grid=(B, steps, nsb),
        in_specs=[spec(0), spec(1), spec(2), spec(3),
                  pl.BlockSpec((depth, width), lambda b, h, s: (0, h)),
                  pl.BlockSpec((1, width), lambda b, h, s: (0, h)),
                  pl.BlockSpec(masks.shape, lambda b, h, s: (0, 0, 0)),
                  pl.BlockSpec((C, C), lambda b, h, s: (0, 0))],
        out_specs=pl.BlockSpec((1, sblk, width), lambda b, h, s: (b, s, h)),
        scratch_shapes=[pltpu.VMEM((nh, HEAD_DIM, HEAD_DIM), F32),
                        pltpu.VMEM((nh, C, HEAD_DIM), F32)],
        compiler_params=pltpu.CompilerParams(
            dimension_semantics=("arbitrary", "arbitrary", "arbitrary"),
            vmem_limit_bytes=_vmem_limit(est)),
        name="hgrn2_recurrence",
    )(proj, proj, proj, proj, hgrn_lb.astype(F32), onorm.reshape(1, -1).astype(F32),
      jnp.asarray(masks), jnp.asarray(tril, BF16))


def _mlp_body(x_ref, gi_ref, wu_ref, wd_ref, go_ref, o_ref, hn_ref, *, nf, tm):
    f = pl.program_id(1)

    def row_chunks(fn):
        def step(r, c):
            fn(pl.ds(pl.multiple_of(r * NORM_ROWS, NORM_ROWS), NORM_ROWS))
            return c

        lax.fori_loop(0, tm // NORM_ROWS, step, 0)

    @pl.when(f == 0)
    def _():
        g = gi_ref[...]

        def norm_in(rows):
            xs = x_ref[rows, :]
            ms = jnp.mean(xs * xs, axis=-1, keepdims=True)
            hn_ref[rows, :] = (xs * lax.rsqrt(ms + RMS_EPS) * g).astype(BF16)

        row_chunks(norm_in)
        o_ref[...] = jnp.zeros(o_ref.shape, F32)

    hid = jnp.dot(hn_ref[...], wu_ref[...], preferred_element_type=F32)
    hid = jnp.square(jnp.maximum(hid, 0.0)).astype(BF16)
    o_ref[...] += jnp.dot(hid, wd_ref[...], preferred_element_type=F32)

    @pl.when(f == nf - 1)
    def _():
        g = go_ref[...]

        def norm_out(rows):
            y = o_ref[rows, :]
            ms = jnp.mean(y * y, axis=-1, keepdims=True)
            o_ref[rows, :] = x_ref[rows, :] + y * lax.rsqrt(ms + RMS_EPS) * g

        row_chunks(norm_out)


def _mlp(xf, g_in, g_out, w_up, w_down, layer, *, tm=512, tf=1024):
    M, D = xf.shape
    F = w_up.shape[2]
    assert M % tm == 0 and F % tf == 0 and tm % NORM_ROWS == 0
    nf = F // tf
    est = (2 * tm * D * 4 + 2 * tm * D * 4 + tm * D * 2 + 2 * 2 * D * tf * 2
           + tm * tf * 6 + 2 * tm * D * 4)
    return pl.pallas_call(
        functools.partial(_mlp_body, nf=nf, tm=tm),
        out_shape=jax.ShapeDtypeStruct((M, D), F32),
        grid=(M // tm, nf),
        in_specs=[pl.BlockSpec((tm, D), lambda i, f: (i, 0)),
                  pl.BlockSpec((1, D), lambda i, f: (0, 0)),
                  pl.BlockSpec((None, D, tf), lambda i, f: (layer, 0, f)),
                  pl.BlockSpec((None, tf, D), lambda i, f: (layer, f, 0)),
                  pl.BlockSpec((1, D), lambda i, f: (0, 0))],
        out_specs=pl.BlockSpec((tm, D), lambda i, f: (i, 0)),
        scratch_shapes=[pltpu.VMEM((tm, D), BF16)],
        compiler_params=pltpu.CompilerParams(
            dimension_semantics=("arbitrary", "arbitrary"),
            vmem_limit_bytes=_vmem_limit(est)),
        name="mlp",
    )(xf, g_in.reshape(1, D).astype(F32), w_up.astype(BF16), w_down.astype(BF16),
      g_out.reshape(1, D).astype(F32))


def _nsa_layer(xf, B, S, g_in, g_out, rel_table, w_in, cmp_pe, cmp_w1, cmp_w2, w_out):
    D = xf.shape[1]
    G, R, Dh = N_GROUPS, HEADS_PER_GROUP, HEAD_DIM
    n_main = N_HEADS * Dh + 6 * G * Dh
    n_gate = 3 * N_HEADS
    w_all = w_in.astype(BF16)
    w_gate = jnp.pad(w_in[:, n_main:], ((0, 0), (0, LANES - n_gate))).astype(BF16)
    colscale = jnp.concatenate([jnp.full((N_HEADS * Dh,), Dh ** -0.5 * LOG2_E, F32),
                                jnp.ones((6 * G * Dh,), F32)])[None]

    proj, glog = _matmul(xf, w_all, tm=1024, tn=1024, norm_g=g_in, epi="colscale",
                         colscale=colscale, side_w=w_gate, n_out=n_main, out_dtype=BF16,
                         name="nsa_proj")
    gates_t = (glog[:, :n_gate].reshape(B, S, 3, G, R).transpose(0, 3, 1, 2, 4)
               .reshape(B, G, S, 3 * R))

    proj3 = proj.reshape(B, S, n_main)
    half = CMP_STRIDE * Dh
    kcmp, vcmp = _compress(proj3, cmp_w1.reshape(2, 2, half, Dh).astype(BF16),
                           cmp_pe.reshape(2, 2, 1, half).astype(F32),
                           cmp_w2.astype(BF16), B, S)

    bias_c, bias_t, bias_far = _bias_tables(rel_table, S)
    attn = _nsa_attention(proj3, kcmp, vcmp, bias_c, bias_t, bias_far, gates_t, B, S)
    return _matmul(attn.reshape(B * S, D), w_out.astype(BF16), tm=512, tn=D,
                   epi="resnorm", res=xf, res_g=g_out, out_dtype=F32, name="nsa_out")


def _hgrn_layer(xf, B, S, layer, g_in, g_out, w_in, hgrn_lb, onorm, w_out):
    D = xf.shape[1]
    proj = _matmul(xf, w_in, tm=1024, tn=1024, norm_g=g_in,
                   out_dtype=BF16, name="hgrn_proj")
    mixed = _hgrn(proj.reshape(B, S, 4 * D), hgrn_lb, onorm, layer, B, S)
    return _matmul(mixed.reshape(B * S, D), w_out.astype(BF16), tm=512, tn=D,
                   epi="resnorm", res=xf, res_g=g_out, out_dtype=F32, name="hgrn_out")


def kernel(x, norm_g, rel_table, nsa_w_in, nsa_cmp_pe, nsa_cmp_w1, nsa_cmp_w2, nsa_w_out,
           hgrn_w_in, hgrn_lb, hgrn_onorm, hgrn_w_out, mlp_w_up, mlp_w_down):
    B, S, D = x.shape
    depth = norm_g.shape[0]
    assert D == N_HEADS * HEAD_DIM and S % ATT_TILE == 0 and S % HGRN_CHUNK == 0
    xf = x.reshape(B * S, D).astype(F32)
    for layer in range(depth):
        j = layer // 2
        if layer % 2 == 0:
            xf = _nsa_layer(xf, B, S, norm_g[layer, 0], norm_g[layer, 1], rel_table,
                            nsa_w_in[j], nsa_cmp_pe[j], nsa_cmp_w1[j], nsa_cmp_w2[j],
                            nsa_w_out[j])
        else:
            xf = _hgrn_layer(xf, B, S, layer, norm_g[layer, 0], norm_g[layer, 1],
                             hgrn_w_in[j], hgrn_lb, hgrn_onorm[j], hgrn_w_out[j])
        xf = _mlp(xf, norm_g[layer, 2], norm_g[layer, 3], mlp_w_up, mlp_w_down, layer)
    return xf.reshape(B, S, D).astype(x.dtype)
```

```python
import functools
import math

import numpy as np
import jax
import jax.numpy as jnp
from jax import lax
from jax.experimental import pallas as pl
from jax.experimental.pallas import tpu as pltpu

F32 = jnp.float32
BF16 = jnp.bfloat16

N_HEADS = 16
N_GROUPS = 4
HEADS_PER_GROUP = N_HEADS // N_GROUPS
HEAD_DIM = 128
CMP_BLOCK = 32
CMP_STRIDE = 16
SEL_BLOCK = 64
SEL_TOP_N = 8
WINDOW = 512
FORCE_SCORE = 1.0e4
REL_BUCKETS = 32
REL_MAX_DIST = 128
RMS_EPS = 1e-6
NEG_INF = -1.0e30
LOG2_E = math.log2(math.e)

LANES = 128
SUBLANES = 8
VMEM_BYTES_V7X = 64 * 1024 * 1024
VMEM_LIMIT_CAP = VMEM_BYTES_V7X - 8 * 1024 * 1024

ATT_TILE = 256
ATT_ROWS = 128
SEL_TILES_PER_STEP = 4
ATT_TILES_PER_STEP = 2
PREV_TILE, DIAG_TILE, EDGE_TILE = 0, 1, 2
N_BIAS_TILES = 3
HGRN_CHUNK = 128
HGRN_SUB = 8
HGRN_HEADS_PER_STEP = 16
HGRN_HEADS_INTERLEAVED = 8
HGRN_SEQ_BLOCKS = 2
NORM_ROWS = 256


VMEM_LIMIT_FLOOR = 32 * 1024 * 1024


def _vmem_limit(nbytes):
    return int(min(VMEM_LIMIT_CAP, max(VMEM_LIMIT_FLOOR, nbytes)))


def _mm_body(*refs, norm, epi, side, tm):
    it = iter(refs)
    x_ref = next(it)
    g_ref = next(it) if norm else None
    w_ref = next(it)
    ws_ref = next(it) if side else None
    cs_ref = next(it) if epi == "colscale" else None
    res_ref = next(it) if epi == "resnorm" else None
    go_ref = next(it) if epi == "resnorm" else None
    o_ref = next(it)
    os_ref = next(it) if side else None
    hn_ref = next(it) if norm else None

    j = pl.program_id(1)

    if norm:
        @pl.when(j == 0)
        def _():
            g = g_ref[...]

            def step(r, c):
                rows = pl.ds(pl.multiple_of(r * NORM_ROWS, NORM_ROWS), NORM_ROWS)
                xs = x_ref[rows, :]
                ms = jnp.mean(xs * xs, axis=-1, keepdims=True)
                hn_ref[rows, :] = (xs * lax.rsqrt(ms + RMS_EPS) * g).astype(BF16)
                return c

            lax.fori_loop(0, tm // NORM_ROWS, step, 0)

        lhs = hn_ref[...]
    else:
        lhs = x_ref[...]

    if side:
        @pl.when(j == 0)
        def _():
            os_ref[...] = jnp.dot(lhs, ws_ref[...], preferred_element_type=F32)

    acc = jnp.dot(lhs, w_ref[...].astype(BF16), preferred_element_type=F32)
    if epi == "colscale":
        acc = acc * cs_ref[...]
    elif epi == "resnorm":
        ms = jnp.mean(acc * acc, axis=-1, keepdims=True)
        acc = res_ref[...] + acc * lax.rsqrt(ms + RMS_EPS) * go_ref[...]
    o_ref[...] = acc.astype(o_ref.dtype)


def _matmul(x, w, *, tm, tn, norm_g=None, epi="none", colscale=None, res=None, res_g=None,
            side_w=None, n_out=None, out_dtype=BF16, name="mm"):
    M, K = x.shape
    N = w.shape[1] if n_out is None else n_out
    assert N <= w.shape[1]
    norm = norm_g is not None
    side = side_w is not None
    assert M % tm == 0 and N % tn == 0
    assert not (epi == "resnorm" and tn != N)

    in_specs = [pl.BlockSpec((tm, K), lambda i, j: (i, 0))]
    args = [x]
    if norm:
        in_specs.append(pl.BlockSpec((1, K), lambda i, j: (0, 0)))
        args.append(norm_g.reshape(1, K).astype(F32))
    in_specs.append(pl.BlockSpec((K, tn), lambda i, j: (0, j)))
    args.append(w)
    ns = side_w.shape[1] if side else 0
    if side:
        in_specs.append(pl.BlockSpec((K, ns), lambda i, j: (0, 0)))
        args.append(side_w)
    if epi == "colscale":
        in_specs.append(pl.BlockSpec((1, tn), lambda i, j: (0, j)))
        args.append(colscale)
    if epi == "resnorm":
        in_specs.append(pl.BlockSpec((tm, tn), lambda i, j: (i, j)))
        args.append(res)
        in_specs.append(pl.BlockSpec((1, tn), lambda i, j: (0, j)))
        args.append(res_g.reshape(1, N).astype(F32))

    out_shape = jax.ShapeDtypeStruct((M, N), out_dtype)
    out_specs = pl.BlockSpec((tm, tn), lambda i, j: (i, j))
    if side:
        out_shape = (out_shape, jax.ShapeDtypeStruct((M, ns), F32))
        out_specs = (out_specs, pl.BlockSpec((tm, ns), lambda i, j: (i, 0)))

    xb = x.dtype.itemsize
    ob = jnp.dtype(out_dtype).itemsize
    wb = w.dtype.itemsize
    est = (2 * tm * K * xb + 2 * K * tn * wb + (K * tn * 2 if wb > 2 else 0) + 2 * tm * tn * ob
           + (tm * K * 2 if norm else 0)
           + (2 * tm * tn * 4 if epi == "resnorm" else 0) + 3 * tm * tn * 4
           + 2 * K * ns * 2 + 3 * tm * ns * 4)

    return pl.pallas_call(
        functools.partial(_mm_body, norm=norm, epi=epi, side=side, tm=tm),
        out_shape=out_shape,
        grid=(M // tm, N // tn),
        in_specs=in_specs,
        out_specs=out_specs,
        scratch_shapes=[pltpu.VMEM((tm, K), BF16)] if norm else [],
        compiler_params=pltpu.CompilerParams(
            dimension_semantics=("arbitrary", "arbitrary"),
            vmem_limit_bytes=_vmem_limit(est)),
        name=name,
    )(*args)


def _rel_bucket_np(dist):
    n = np.maximum(dist, 0)
    max_exact = REL_BUCKETS // 2
    nf = np.maximum(n, 1).astype(np.float32)
    ratio = np.log(nf / np.float32(max_exact)) / np.float32(math.log(REL_MAX_DIST / max_exact))
    large = max_exact + (ratio * np.float32(REL_BUCKETS - max_exact)).astype(np.int32)
    large = np.minimum(large, REL_BUCKETS - 1)
    return np.where(n < max_exact, n, large).astype(np.int32)


@functools.lru_cache(maxsize=None)
def _static_maps(seq):
    n_cmp = LANES
    pos = np.arange(seq, dtype=np.int32)[:, None]
    c_end = np.arange(n_cmp, dtype=np.int32)[None, :] * CMP_STRIDE + CMP_BLOCK - 1
    bucket_c = _rel_bucket_np(pos - c_end)
    t = np.arange(ATT_TILE, dtype=np.int32)[:, None]
    k = np.arange(ATT_TILE, dtype=np.int32)[None, :]
    bucket_t = np.stack([_rel_bucket_np(t - k), _rel_bucket_np(ATT_TILE + t - k)])
    assert _rel_bucket_np(np.array([ATT_TILE + 1]))[0] == REL_BUCKETS - 1
    nc = seq // CMP_STRIDE - CMP_BLOCK // CMP_STRIDE + 1
    nb = seq // SEL_BLOCK
    c_start = np.arange(nc)[:, None] * CMP_STRIDE
    b_start = np.arange(nb)[None, :] * SEL_BLOCK
    ov = ((c_start <= b_start + SEL_BLOCK - 1) & (c_start + CMP_BLOCK - 1 >= b_start))
    overlap = np.zeros((LANES, LANES), np.float32)
    overlap[:nc, :nb] = ov
    return bucket_c, bucket_t, overlap


def _bias_body(tab_ref, bc_ref, bt_ref, oc_ref, ot_ref, of_ref, *, seq):
    h = pl.program_id(0)

    lane = lax.broadcasted_iota(jnp.int32, (SUBLANES, LANES), 1)
    tab_row = jnp.zeros((SUBLANES, LANES), F32)
    for b in range(REL_BUCKETS):
        tab_row = jnp.where(lane == b, tab_ref[b, h], tab_row)

    def lookup(bmap):
        rows = bmap.shape[0]
        tab = jnp.broadcast_to(tab_row[0:1], (rows, LANES))
        return jnp.concatenate(
            [jnp.take_along_axis(tab, bmap[:, c0:c0 + LANES], axis=1)
             for c0 in range(0, bmap.shape[1], LANES)], axis=1)

    def step(r, c):
        rows = pl.ds(pl.multiple_of(r * ATT_TILE, ATT_TILE), ATT_TILE)
        oc_ref[0, rows, :] = lookup(bc_ref[rows, :]) * LOG2_E
        return c

    lax.fori_loop(0, seq // ATT_TILE, step, 0)

    tt = lax.broadcasted_iota(jnp.int32, (ATT_TILE, ATT_TILE), 0)
    kk = lax.broadcasted_iota(jnp.int32, (ATT_TILE, ATT_TILE), 1)
    far = tab_ref[REL_BUCKETS - 1, h]
    of_ref[0] = jnp.full(of_ref.shape[1:], far * LOG2_E, F32)
    ot_ref[PREV_TILE, 0] = (lookup(bt_ref[1]) - far) * LOG2_E
    ot_ref[DIAG_TILE, 0] = jnp.where(kk <= tt, (lookup(bt_ref[0]) - far) * LOG2_E, NEG_INF)
    ot_ref[EDGE_TILE, 0] = jnp.where(kk > tt, 0.0, NEG_INF)


def _bias_tables(rel_table, seq):
    bucket_c, bucket_t, _ = _static_maps(seq)
    return pl.pallas_call(
        functools.partial(_bias_body, seq=seq),
        out_shape=(jax.ShapeDtypeStruct((N_HEADS, seq, LANES), F32),
                   jax.ShapeDtypeStruct((N_BIAS_TILES, N_HEADS, ATT_TILE, ATT_TILE), F32),
                   jax.ShapeDtypeStruct((N_HEADS, SUBLANES, LANES), F32)),
        grid=(N_HEADS,),
        in_specs=[pl.BlockSpec(memory_space=pltpu.SMEM),
                  pl.BlockSpec((seq, LANES), lambda h: (0, 0)),
                  pl.BlockSpec((2, ATT_TILE, ATT_TILE), lambda h: (0, 0, 0))],
        out_specs=(pl.BlockSpec((1, seq, LANES), lambda h: (h, 0, 0)),
                   pl.BlockSpec((N_BIAS_TILES, 1, ATT_TILE, ATT_TILE), lambda h: (0, h, 0, 0)),
                   pl.BlockSpec((1, SUBLANES, LANES), lambda h: (h, 0, 0))),
        compiler_params=pltpu.CompilerParams(dimension_semantics=("arbitrary",)),
        name="rel_bias",
    )(rel_table.astype(F32), jnp.asarray(bucket_c), jnp.asarray(bucket_t))


def _compress_body(xk_ref, xv_ref, w1_ref, pe_ref, w2_ref, ok_ref, ov_ref, x_sc):
    n_grp = xk_ref.shape[1] // CMP_STRIDE

    def one(x_ref, idx, o_ref):
        x_sc[...] = x_ref[0].astype(F32)
        x = jnp.concatenate([x_sc[pl.ds(t, n_grp, stride=CMP_STRIDE), :]
                             for t in range(CMP_STRIDE)], axis=1)
        a0 = jnp.dot((x + pe_ref[idx, 0]).astype(BF16), w1_ref[idx, 0],
                     preferred_element_type=F32)
        a1 = jnp.dot((x + pe_ref[idx, 1]).astype(BF16), w1_ref[idx, 1],
                     preferred_element_type=F32)
        pre = a0 + pltpu.roll(a1, LANES - 1, 0)
        hid = jax.nn.gelu(pre).astype(BF16)
        o_ref[0, 0] = jnp.dot(hid, w2_ref[idx], preferred_element_type=F32).astype(BF16)

    one(xk_ref, 0, ok_ref)
    one(xv_ref, 1, ov_ref)


def _compress(proj, w1, pe, w2, B, S):
    G = N_GROUPS
    assert S // CMP_STRIDE == LANES
    half = CMP_STRIDE * HEAD_DIM
    q_cols = N_HEADS

    def spec_x(slot):
        return pl.BlockSpec((1, S, HEAD_DIM), lambda b, g, slot=slot: (b, 0, q_cols + slot * G + g))

    spec_o = pl.BlockSpec((1, 1, LANES, HEAD_DIM), lambda b, g: (b, g, 0, 0))
    out = jax.ShapeDtypeStruct((B, G, LANES, HEAD_DIM), BF16)
    return pl.pallas_call(
        _compress_body,
        out_shape=(out, out),
        grid=(B, G),
        in_specs=[spec_x(0), spec_x(1),
                  pl.BlockSpec((2, 2, half, HEAD_DIM), lambda b, g: (0, 0, 0, 0)),
                  pl.BlockSpec((2, 2, 1, half), lambda b, g: (0, 0, 0, 0)),
                  pl.BlockSpec((2, HEAD_DIM, HEAD_DIM), lambda b, g: (0, 0, 0))],
        out_specs=(spec_o, spec_o),
        scratch_shapes=[pltpu.VMEM((S, HEAD_DIM), F32)],
        compiler_params=pltpu.CompilerParams(dimension_semantics=("arbitrary", "arbitrary")),
        name="nsa_compress",
    )(proj, proj, w1, pe, w2)


def _nsa_body(q_ref, ks_ref, vs_ref, kw_ref, vw_ref, kc_ref, vc_ref, bc_ref, bt_ref, far_ref,
              ovl_ref, augs_ref, augw_ref, gl_ref, o_ref, kts_sc, ktw_sc, qa_sc, oc_sc, m_sc,
              acc_sc):
    R = HEADS_PER_GROUP
    tq = ATT_TILE
    qi = pl.program_id(2)
    n_tiles = ks_ref.shape[1] // tq
    nb = ovl_ref.shape[0]
    nt = (((1,), (1,)), ((), ()))

    cidx = lax.broadcasted_iota(jnp.int32, (1, 1, LANES), 2)
    jb = lax.broadcasted_iota(jnp.int32, (nb, 1), 0)
    lane = lax.broadcasted_iota(jnp.int32, (1, LANES), 1)

    def select_tile(t):
        rows = pl.ds(pl.multiple_of(t * tq, tq), tq)
        q = q_ref[0, rows, :]
        q4 = jnp.concatenate([q[:, r * HEAD_DIM:(r + 1) * HEAD_DIM] for r in range(R)], axis=0)
        qa_sc[t, :, :HEAD_DIM] = q4
        pos3 = t * tq + lax.broadcasted_iota(jnp.int32, (1, tq, 1), 1)

        sc = lax.dot_general(q4, kc_ref[0, 0], nt, preferred_element_type=F32)
        sc = sc.reshape(R, tq, LANES) + bc_ref[:, rows, :]
        valid = (cidx * CMP_STRIDE + (CMP_BLOCK - 1) <= pos3) & (cidx < LANES - 1)
        sc = jnp.where(valid, sc, NEG_INF)
        mc = jnp.max(sc, axis=-1, keepdims=True)
        ec = jnp.exp2(sc - mc)
        pc = ec / jnp.sum(ec, axis=-1, keepdims=True)
        pc = jnp.where(pos3 >= CMP_BLOCK - 1, pc, 0.0)
        oc_sc[t] = jnp.dot(pc.reshape(R * tq, LANES).astype(BF16), vc_ref[0, 0],
                           preferred_element_type=F32)

        psum = pc[0]
        for r in range(1, R):
            psum = psum + pc[r]
        p_hi = psum.astype(BF16)
        p_lo = (psum - p_hi.astype(F32)).astype(BF16)
        ovt = ovl_ref[...]
        imp = (lax.dot_general(ovt, p_hi, nt, preferred_element_type=F32)
               + lax.dot_general(ovt, p_lo, nt, preferred_element_type=F32))
        pos_t = t * tq + lax.broadcasted_iota(jnp.int32, (1, tq), 1)
        q_blk = lax.shift_right_logical(pos_t, int(math.log2(SEL_BLOCK)))
        forced = (jb == 0) | (jb == q_blk) | (jb == q_blk - 1)
        future = jb > q_blk
        imp = jnp.where(forced, FORCE_SCORE, jnp.where(future, -1.0, imp))
        cnt = jnp.zeros((nb, tq), F32)
        for i in range(nb):
            row = imp[i:i + 1, :]
            beats = (row > imp) | ((row == imp) & (jb > i))
            cnt = cnt + jnp.where(beats, 1.0, 0.0)
        sel_t = jnp.where(cnt < float(min(SEL_TOP_N, nb)), 1.0, 0.0)
        sel = jnp.concatenate([sel_t, jnp.zeros((LANES - nb, tq), F32)], axis=0).T

        sel_pad = jnp.where(lane < nb, (sel - 1.0) * (-NEG_INF), 0.0)
        for r in range(R):
            far = jnp.broadcast_to(far_ref[r, 0:1, :], (tq, LANES))
            far_hi = far.astype(BF16).astype(F32)
            pad = jnp.where(lane == nb, far_hi, jnp.where(lane == nb + 1, far - far_hi, sel_pad))
            pad = jnp.where(lane == nb + 2, 1.0, pad)
            qa_sc[t, r * tq:(r + 1) * tq, HEAD_DIM:] = pad.astype(BF16)

    @pl.when(qi == 0)
    def _():
        def tr(j, c):
            rows = pl.ds(pl.multiple_of(j * tq, tq), tq)
            kts_sc[j, :HEAD_DIM, :] = ks_ref[0, rows, :].T
            kts_sc[j, HEAD_DIM:, :] = augs_ref[j]
            ktw_sc[j, :HEAD_DIM, :] = kw_ref[0, rows, :].T
            ktw_sc[j, HEAD_DIM:, :] = augw_ref[0]
            return c

        lax.fori_loop(0, n_tiles, tr, 0)
        zeros = jnp.zeros((HEAD_DIM, tq), BF16)
        kts_sc[n_tiles, :HEAD_DIM, :] = zeros
        kts_sc[n_tiles, HEAD_DIM:, :] = augs_ref[n_tiles]
        ktw_sc[n_tiles, :HEAD_DIM, :] = zeros
        ktw_sc[n_tiles, HEAD_DIM:, :] = augw_ref[1]

        def sel_group(gi, c):
            for u in range(SEL_TILES_PER_STEP):
                select_tile(gi * SEL_TILES_PER_STEP + u)
            return c

        lax.fori_loop(0, n_tiles // SEL_TILES_PER_STEP, sel_group, 0)

    ones = jnp.ones((tq, HEAD_DIM), BF16)
    n_chunks = R * tq // ATT_ROWS
    chunks_per_head = tq // ATT_ROWS
    dead = n_tiles
    step_tiles = [ATT_TILES_PER_STEP * qi + u for u in range(ATT_TILES_PER_STEP)]

    def keys(kt_sc, tiles):
        return jnp.concatenate([kt_sc[t] for t in tiles], axis=1)

    def values(v_ref, tiles):
        parts = []
        for t in tiles:
            rows = pl.ds(pl.multiple_of(t * tq, tq), tq)
            parts.append(jnp.concatenate([v_ref[0, rows, :], ones], axis=1))
        return jnp.concatenate(parts, axis=0)

    def logits(q_sc, ci, kt, kinds):
        r, hh = divmod(ci, chunks_per_head)
        rs = slice(ci * ATT_ROWS, (ci + 1) * ATT_ROWS)
        qs = slice(hh * ATT_ROWS, (hh + 1) * ATT_ROWS)
        s = jnp.dot(q_sc[rs, :], kt, preferred_element_type=F32)
        if all(kd is None for kd in kinds):
            return s
        parts = [s[:, i * tq:(i + 1) * tq] for i in range(len(kinds))]
        return jnp.concatenate([p if kd is None else p + bt_ref[kd, r, qs, :]
                                for p, kd in zip(parts, kinds)], axis=1)

    def probs(s, m):
        return jnp.concatenate([jnp.exp2(s[:, k0:k0 + LANES] - m)
                                for k0 in range(0, s.shape[1], LANES)], axis=1).astype(BF16)

    m_sc[...] = jnp.full(m_sc.shape, NEG_INF, F32)
    acc_sc[...] = jnp.zeros(acc_sc.shape, F32)

    def sel_update(u, t, kt, vv, kinds):
        for ci in range(n_chunks):
            rs = slice((u * n_chunks + ci) * ATT_ROWS, (u * n_chunks + ci + 1) * ATT_ROWS)
            s = logits(qa_sc.at[t], ci, kt, kinds)
            m_old = m_sc[rs]
            m_new = jnp.maximum(m_old, jnp.max(s, axis=-1, keepdims=True))
            alpha = jnp.exp2(m_old - m_new)
            m_sc[rs] = m_new
            pv = jnp.dot(probs(s, m_new), vv, preferred_element_type=F32)
            acc_sc[rs] = acc_sc[rs] * jnp.concatenate([alpha, alpha], axis=1) + pv

    assert ATT_TILES_PER_STEP == 2
    ta, tb = step_tiles

    def far_pair(pi, c):
        pair_tiles = (2 * pi, 2 * pi + 1)
        kt, vv = keys(kts_sc, pair_tiles), values(vs_ref, pair_tiles)
        sel_update(0, ta, kt, vv, (None, None))
        sel_update(1, tb, kt, vv, (None, None))
        return c

    lax.fori_loop(0, jnp.maximum(qi - 1, 0), far_pair, 0)

    @pl.when(qi >= 1)
    def _():
        pair_tiles = (ta - 2, ta - 1)
        kt, vv = keys(kts_sc, pair_tiles), values(vs_ref, pair_tiles)
        sel_update(0, ta, kt, vv, (None, PREV_TILE))
        sel_update(1, tb, kt, vv, (None, None))

    sel_update(0, ta, kts_sc[ta], values(vs_ref, (ta,)), (DIAG_TILE,))
    sel_update(1, tb, keys(kts_sc, (ta, tb)), values(vs_ref, (ta, tb)), (PREV_TILE, DIAG_TILE))

    n_win = WINDOW // tq
    w_kinds = (EDGE_TILE,) + (None,) * (n_win - 2) + (PREV_TILE, DIAG_TILE)
    gates = jax.nn.sigmoid(gl_ref[0, 0])
    for u, t in enumerate(step_tiles):
        acc = acc_sc[u * R * tq:(u + 1) * R * tq, :]
        o_sel = acc[:, :HEAD_DIM] / acc[:, HEAD_DIM:]

        w_tiles = [t - n_win + w for w in range(n_win + 1)]
        kt = keys(ktw_sc, [jnp.where(w >= 0, w, dead) for w in w_tiles])
        vv = values(vw_ref, [jnp.maximum(w, 0) for w in w_tiles])
        o_parts = []
        for ci in range(n_chunks):
            s = logits(qa_sc.at[t], ci, kt, w_kinds)
            pv = jnp.dot(probs(s, jnp.max(s, axis=-1, keepdims=True)), vv,
                         preferred_element_type=F32)
            o_parts.append(pv[:, :HEAD_DIM] / pv[:, HEAD_DIM:])
        o_win = jnp.concatenate(o_parts, axis=0)

        o_cmp = oc_sc[t]
        g = gates[u * tq:(u + 1) * tq]
        outs = []
        for r in range(R):
            hs = slice(r * tq, (r + 1) * tq)
            outs.append(g[:, r:r + 1] * o_cmp[hs] + g[:, R + r:R + r + 1] * o_sel[hs]
                        + g[:, 2 * R + r:2 * R + r + 1] * o_win[hs])
        o_ref[0, u * tq:(u + 1) * tq, :] = jnp.concatenate(outs, axis=1).astype(o_ref.dtype)


def _nsa_attention(proj, kcmp, vcmp, bias_c, bias_t, bias_far, gates_t, B, S):
    assert WINDOW % ATT_TILE == 0 and S % ATT_TILE == 0
    assert S // CMP_STRIDE == LANES and S // SEL_BLOCK <= LANES
    R, G, tq = HEADS_PER_GROUP, N_GROUPS, ATT_TILE
    n_tiles = S // tq
    ts = ATT_TILES_PER_STEP
    assert n_tiles % ts == 0
    nb = S // SEL_BLOCK
    _, _, overlap = _static_maps(S)
    overlap_t = np.ascontiguousarray(overlap.T[:nb])
    assert nb + 3 <= LANES and n_tiles % 2 == 0
    blk_of_key = (np.arange(S) // SEL_BLOCK).reshape(n_tiles, 1, tq)
    aug_sel = np.zeros((n_tiles + 1, LANES, tq), np.float32)
    aug_sel[:n_tiles] = np.arange(LANES).reshape(1, LANES, 1) == blk_of_key
    aug_sel[:n_tiles, nb:nb + 2, :] = 1.0
    aug_sel[n_tiles, nb + 2, :] = NEG_INF
    aug_win = np.zeros((2, LANES, tq), np.float32)
    aug_win[0, nb:nb + 2, :] = 1.0
    aug_win[1, nb + 2, :] = NEG_INF
    q_cols = N_HEADS

    def kv_spec(slot):
        return pl.BlockSpec((1, S, HEAD_DIM),
                            lambda b, g, i, slot=slot: (b, 0, q_cols + slot * G + g))

    cmp_spec = pl.BlockSpec((1, 1, LANES, HEAD_DIM), lambda b, g, i: (b, g, 0, 0))
    in_specs = [
        pl.BlockSpec((1, S, R * HEAD_DIM), lambda b, g, i: (b, 0, g)),
        kv_spec(2), kv_spec(3), kv_spec(4), kv_spec(5),
        cmp_spec, cmp_spec,
        pl.BlockSpec((R, S, LANES), lambda b, g, i: (g, 0, 0)),
        pl.BlockSpec((N_BIAS_TILES, R, tq, tq), lambda b, g, i: (0, g, 0, 0)),
        pl.BlockSpec((R, SUBLANES, LANES), lambda b, g, i: (g, 0, 0)),
        pl.BlockSpec((nb, LANES), lambda b, g, i: (0, 0)),
        pl.BlockSpec((n_tiles + 1, LANES, tq), lambda b, g, i: (0, 0, 0)),
        pl.BlockSpec((2, LANES, tq), lambda b, g, i: (0, 0, 0)),
        pl.BlockSpec((1, 1, ts * tq, 3 * R), lambda b, g, i: (b, g, i, 0)),
    ]
    kdim = HEAD_DIM + LANES
    assert n_tiles % SEL_TILES_PER_STEP == 0
    est = (2 * 4 * S * HEAD_DIM * 2 + 2 * N_BIAS_TILES * R * tq * tq * 4 + 2 * R * S * LANES * 4
           + 2 * S * R * HEAD_DIM * 2 + 2 * 2 * S * LANES * 2 + 2 * S * kdim * 2
           + R * S * (kdim * 2 + HEAD_DIM * 4) + ts * R * tq * (LANES * 4 + 2 * HEAD_DIM * 4)
           + 3 * R * tq * LANES * 4 + 16 * ATT_ROWS * tq * 4 + 4 * tq * R * HEAD_DIM * 2)
    assert est <= VMEM_LIMIT_CAP
    return pl.pallas_call(
        _nsa_body,
        out_shape=jax.ShapeDtypeStruct((B, S, N_HEADS * HEAD_DIM), BF16),
        grid=(B, G, n_tiles // ts),
        in_specs=in_specs,
        out_specs=pl.BlockSpec((1, ts * tq, R * HEAD_DIM), lambda b, g, i: (b, i, g)),
        scratch_shapes=[pltpu.VMEM((n_tiles + 1, kdim, tq), BF16),
                        pltpu.VMEM((n_tiles + 1, kdim, tq), BF16),
                        pltpu.VMEM((n_tiles, R * tq, kdim), BF16),
                        pltpu.VMEM((n_tiles, R * tq, HEAD_DIM), F32),
                        pltpu.VMEM((ts * R * tq, LANES), F32),
                        pltpu.VMEM((ts * R * tq, 2 * HEAD_DIM), F32)],
        compiler_params=pltpu.CompilerParams(
            dimension_semantics=("arbitrary", "arbitrary", "arbitrary"),
            vmem_limit_bytes=VMEM_LIMIT_CAP),
        name="nsa_attention",
    )(proj, proj, proj, proj, proj, kcmp, vcmp, bias_c, bias_t, bias_far,
      jnp.asarray(overlap_t, BF16), jnp.asarray(aug_sel, BF16), jnp.asarray(aug_win, BF16),
      gates_t)


@functools.lru_cache(maxsize=None)
def _hgrn_masks():
    C = HGRN_CHUNK
    t = np.arange(C)[:, None]
    s = np.arange(C)[None, :]
    masks = [(t // HGRN_SUB == s // HGRN_SUB) & (s <= t)]
    half = C // 2
    while half >= HGRN_SUB:
        grp = 2 * half
        masks.append((t // grp == s // grp) & (t % grp >= half) & (s % grp < half))
        half //= 2
    assert np.array_equal(np.sum(masks, axis=0), (s <= t).astype(int))
    return np.stack(masks).astype(np.float32), (s <= t).astype(np.float32)


def _hgrn_body(q_ref, f_ref, i_ref, g_ref, lb_ref, gn_ref, msk_ref, tril_ref, o_ref,
               st_all, b_all, k_all, q_all, *, layer, heads):
    C = HGRN_CHUNK
    n_chunks = q_ref.shape[1] // C
    nt = (((1,), (1,)), ((), ()))

    lbp = lb_ref[...]
    e = jnp.exp(lbp - jnp.max(lbp, axis=0, keepdims=True))
    sm = e / jnp.sum(e, axis=0, keepdims=True)
    cum = sm[0:1]
    first = cum
    for d in range(1, layer + 1):
        cum = cum + sm[d:d + 1]
    lb_all = cum - first
    log_lb_all = jnp.log(lb_all)
    log_1m_all = jnp.log1p(-lb_all)
    gn_all = gn_ref[...]

    cols = lax.broadcasted_iota(jnp.int32, (1, C), 1)
    tril = tril_ref[...]

    @pl.when(pl.program_id(2) == 0)
    def _():
        st_all[...] = jnp.zeros(st_all.shape, F32)

    def chunk(c, carry):
        for h0 in range(0, heads, HGRN_HEADS_INTERLEAVED):
            stages = [one_head(c, hh) for hh in range(h0, h0 + HGRN_HEADS_INTERLEAVED)]
            for _ in zip(*stages):
                pass
        return carry

    def one_head(c, hh):
        sl = pl.ds(pl.multiple_of(c * C, C), C)
        hs = slice(hh * HEAD_DIM, (hh + 1) * HEAD_DIM)
        lb, log_lb, log_1m, gn = lb_all[:, hs], log_lb_all[:, hs], log_1m_all[:, hs], gn_all[:, hs]
        st_ref, b_sc, k_sc, q_sc = st_all.at[hh], b_all.at[hh], k_all.at[hh], q_all.at[hh]
        qr = q_ref[0, sl, hs].astype(F32)
        x = f_ref[0, sl, hs].astype(F32)
        v = i_ref[0, sl, hs].astype(F32)
        gr = g_ref[0, sl, hs].astype(F32)
        q = qr / (1.0 + jnp.exp2(qr * (-LOG2_E)))
        ex = jnp.exp2(jnp.abs(x) * (-LOG2_E))
        u = 1.0 + ex
        r1 = 1.0 / u
        k = (1.0 - lb) * jnp.where(x >= 0.0, ex * r1, r1)
        c2 = log_1m + (jnp.minimum(x, 0.0) - jnp.log(u))
        e2 = jnp.exp2(jnp.abs(log_lb - c2) * (-LOG2_E))
        log_f = jnp.maximum(log_lb, c2) + jnp.log(1.0 + e2)
        yield

        lf_hi = log_f.astype(BF16)
        lf_lo = (log_f - lf_hi.astype(F32)).astype(BF16)
        b = (jnp.dot(tril, lf_hi, preferred_element_type=F32)
             + jnp.dot(tril, lf_lo, preferred_element_type=F32)) * LOG2_E
        b_sc[...] = b
        k_sc[...] = k
        q_sc[...] = q
        yield

        vb = v.astype(BF16)
        qb = q.astype(BF16)
        kb = k.astype(BF16)
        st = st_ref[...]
        o = lax.dot_general((q * jnp.exp2(b)).astype(BF16), st.astype(BF16), nt,
                            preferred_element_type=F32)

        pieces = []
        for blk in range(C // HGRN_SUB):
            r0 = blk * HGRN_SUB
            bt = b_sc[r0:r0 + HGRN_SUB, :]
            qt = q_sc[r0:r0 + HGRN_SUB, :]
            arow = jnp.zeros((HGRN_SUB, C), F32)
            for s in range(HGRN_SUB):
                bs = b_sc[r0 + s:r0 + s + 1, :]
                ks = k_sc[r0 + s:r0 + s + 1, :]
                col = jnp.sum(jnp.exp2(bt - bs) * qt * ks, axis=-1, keepdims=True)
                arow = jnp.where(cols == r0 + s, col, arow)
            pieces.append(arow)
        a = jnp.where(msk_ref[0] > 0.5, jnp.concatenate(pieces, axis=0), 0.0)
        yield

        half = C // 2
        lvl = 1
        while half >= HGRN_SUB:
            grp = 2 * half
            anc = jnp.concatenate(
                [jnp.broadcast_to(b_sc[g0 + half - 1:g0 + half, :], (grp, b.shape[1]))
                 for g0 in range(0, C, grp)], axis=0)
            e = jnp.exp2(-jnp.abs(b - anc)).astype(BF16)
            al = lax.dot_general(qb * e, kb * e, nt, preferred_element_type=F32)
            a = jnp.where(msk_ref[lvl] > 0.5, al, a)
            half //= 2
            lvl += 1
            yield

        o = o + jnp.dot(a.astype(BF16), vb, preferred_element_type=F32)
        yield

        b_last = b_sc[C - 1:C, :]
        kh = (k * jnp.exp2(b_last - b)).astype(BF16)
        st_ref[...] = st * jnp.exp2(b_last) + jnp.dot(v.T.astype(BF16), kh,
                                                     preferred_element_type=F32)
        yield

        ms = jnp.mean(o * o, axis=-1, keepdims=True)
        o = o * lax.rsqrt(ms + RMS_EPS) * gn * (gr / (1.0 + jnp.exp2(gr * (-LOG2_E))))
        o_ref[0, sl, hs] = o.astype(o_ref.dtype)
        yield

    lax.fori_loop(0, n_chunks, chunk, 0)


def _hgrn(proj, hgrn_lb, onorm, layer, B, S):
    H = N_HEADS
    depth = hgrn_lb.shape[0]

    nh = HGRN_HEADS_PER_STEP
    width = nh * HEAD_DIM
    steps = H // nh

    nsb = HGRN_SEQ_BLOCKS
    sblk = S // nsb
    C = HGRN_CHUNK
    assert H % nh == 0 and nh % HGRN_HEADS_INTERLEAVED == 0 and S % nsb == 0 and sblk % C == 0

    def spec(part):
        return pl.BlockSpec((1, sblk, width),
                            lambda b, h, s, part=part: (b, s, part * steps + h))

    masks, tril = _hgrn_masks()
    est = (2 * 4 * sblk * width * proj.dtype.itemsize + 2 * sblk * width * 2
           + nh * 64 * C * C * 4)
    return pl.pallas_call(
        functools.partial(_hgrn_body, layer=layer, heads=nh),
        out_shape=jax.ShapeDtypeStruct((B, S, H * HEAD_DIM), BF16),
        grid=(B, steps, nsb),
        in_specs=[spec(0), spec(1), spec(2), spec(3),
                  pl.BlockSpec((depth, width), lambda b, h, s: (0, h)),
                  pl.BlockSpec((1, width), lambda b, h, s: (0, h)),
                  pl.BlockSpec(masks.shape, lambda b, h, s: (0, 0, 0)),
                  pl.BlockSpec((C, C), lambda b, h, s: (0, 0))],
        out_specs=pl.BlockSpec((1, sblk, width), lambda b, h, s: (b, s, h)),
        scratch_shapes=[pltpu.VMEM((nh, HEAD_DIM, HEAD_DIM), F32),
                        pltpu.VMEM((nh, C, HEAD_DIM), F32),
                        pltpu.VMEM((nh, C, HEAD_DIM), F32),
                        pltpu.VMEM((nh, C, HEAD_DIM), F32)],
        compiler_params=pltpu.CompilerParams(
            dimension_semantics=("arbitrary", "arbitrary", "arbitrary"),
            vmem_limit_bytes=_vmem_limit(est)),
        name="hgrn2_recurrence",
    )(proj, proj, proj, proj, hgrn_lb.astype(F32), onorm.reshape(1, -1).astype(F32),
      jnp.asarray(masks), jnp.asarray(tril, BF16))


def _mlp_body(x_ref, gi_ref, wu_ref, wd_ref, go_ref, o_ref, hn_ref, *, nf, tm):
    f = pl.program_id(1)

    def row_chunks(fn):
        def step(r, c):
            fn(pl.ds(pl.multiple_of(r * NORM_ROWS, NORM_ROWS), NORM_ROWS))
            return c

        lax.fori_loop(0, tm // NORM_ROWS, step, 0)

    @pl.when(f == 0)
    def _():
        g = gi_ref[...]

        def norm_in(rows):
            xs = x_ref[rows, :]
            ms = jnp.mean(xs * xs, axis=-1, keepdims=True)
            hn_ref[rows, :] = (xs * lax.rsqrt(ms + RMS_EPS) * g).astype(BF16)

        row_chunks(norm_in)
        o_ref[...] = jnp.zeros(o_ref.shape, F32)

    hid = jnp.dot(hn_ref[...], wu_ref[...], preferred_element_type=F32)
    hid = jnp.square(jnp.maximum(hid, 0.0)).astype(BF16)
    o_ref[...] += jnp.dot(hid, wd_ref[...], preferred_element_type=F32)

    @pl.when(f == nf - 1)
    def _():
        g = go_ref[...]

        def norm_out(rows):
            y = o_ref[rows, :]
            ms = jnp.mean(y * y, axis=-1, keepdims=True)
            o_ref[rows, :] = x_ref[rows, :] + y * lax.rsqrt(ms + RMS_EPS) * g

        row_chunks(norm_out)


def _mlp(xf, g_in, g_out, w_up, w_down, layer, *, tm=512, tf=1024):
    M, D = xf.shape
    F = w_up.shape[2]
    assert M % tm == 0 and F % tf == 0 and tm % NORM_ROWS == 0
    nf = F // tf
    est = (2 * tm * D * 4 + 2 * tm * D * 4 + tm * D * 2 + 2 * 2 * D * tf * 2
           + tm * tf * 6 + 2 * tm * D * 4)
    return pl.pallas_call(
        functools.partial(_mlp_body, nf=nf, tm=tm),
        out_shape=jax.ShapeDtypeStruct((M, D), F32),
        grid=(M // tm, nf),
        in_specs=[pl.BlockSpec((tm, D), lambda i, f: (i, 0)),
                  pl.BlockSpec((1, D), lambda i, f: (0, 0)),
                  pl.BlockSpec((None, D, tf), lambda i, f: (layer, 0, f)),
                  pl.BlockSpec((None, tf, D), lambda i, f: (layer, f, 0)),
                  pl.BlockSpec((1, D), lambda i, f: (0, 0))],
        out_specs=pl.BlockSpec((tm, D), lambda i, f: (i, 0)),
        scratch_shapes=[pltpu.VMEM((tm, D), BF16)],
        compiler_params=pltpu.CompilerParams(
            dimension_semantics=("arbitrary", "arbitrary"),
            vmem_limit_bytes=_vmem_limit(est)),
        name="mlp",
    )(xf, g_in.reshape(1, D).astype(F32), w_up.astype(BF16), w_down.astype(BF16),
      g_out.reshape(1, D).astype(F32))


def _nsa_layer(xf, B, S, g_in, g_out, rel_table, w_in, cmp_pe, cmp_w1, cmp_w2, w_out):
    D = xf.shape[1]
    G, R, Dh = N_GROUPS, HEADS_PER_GROUP, HEAD_DIM
    n_main = N_HEADS * Dh + 6 * G * Dh
    n_gate = 3 * N_HEADS
    w_all = w_in.astype(BF16)
    w_gate = jnp.pad(w_in[:, n_main:], ((0, 0), (0, LANES - n_gate))).astype(BF16)
    colscale = jnp.concatenate([jnp.full((N_HEADS * Dh,), Dh ** -0.5 * LOG2_E, F32),
                                jnp.ones((6 * G * Dh,), F32)])[None]

    proj, glog = _matmul(xf, w_all, tm=1024, tn=1024, norm_g=g_in, epi="colscale",
                         colscale=colscale, side_w=w_gate, n_out=n_main, out_dtype=BF16,
                         name="nsa_proj")
    gates_t = (glog[:, :n_gate].reshape(B, S, 3, G, R).transpose(0, 3, 1, 2, 4)
               .reshape(B, G, S, 3 * R))

    proj3 = proj.reshape(B, S, n_main)
    half = CMP_STRIDE * Dh
    kcmp, vcmp = _compress(proj3, cmp_w1.reshape(2, 2, half, Dh).astype(BF16),
                           cmp_pe.reshape(2, 2, 1, half).astype(F32),
                           cmp_w2.astype(BF16), B, S)

    bias_c, bias_t, bias_far = _bias_tables(rel_table, S)
    attn = _nsa_attention(proj3, kcmp, vcmp, bias_c, bias_t, bias_far, gates_t, B, S)
    return _matmul(attn.reshape(B * S, D), w_out.astype(BF16), tm=512, tn=D,
                   epi="resnorm", res=xf, res_g=g_out, out_dtype=F32, name="nsa_out")


def _hgrn_layer(xf, B, S, layer, g_in, g_out, w_in, hgrn_lb, onorm, w_out):
    D = xf.shape[1]
    proj = _matmul(xf, w_in, tm=1024, tn=1024, norm_g=g_in,
                   out_dtype=BF16, name="hgrn_proj")
    mixed = _hgrn(proj.reshape(B, S, 4 * D), hgrn_lb, onorm, layer, B, S)
    return _matmul(mixed.reshape(B * S, D), w_out.astype(BF16), tm=512, tn=D,
                   epi="resnorm", res=xf, res_g=g_out, out_dtype=F32, name="hgrn_out")


def kernel(x, norm_g, rel_table, nsa_w_in, nsa_cmp_pe, nsa_cmp_w1, nsa_cmp_w2, nsa_w_out,
           hgrn_w_in, hgrn_lb, hgrn_onorm, hgrn_w_out, mlp_w_up, mlp_w_down):
    B, S, D = x.shape
    depth = norm_g.shape[0]
    assert D == N_HEADS * HEAD_DIM and S % ATT_TILE == 0 and S % HGRN_CHUNK == 0
    xf = x.reshape(B * S, D).astype(F32)
    for layer in range(depth):
        j = layer // 2
        if layer % 2 == 0:
            xf = _nsa_layer(xf, B, S, norm_g[layer, 0], norm_g[layer, 1], rel_table,
                            nsa_w_in[j], nsa_cmp_pe[j], nsa_cmp_w1[j], nsa_cmp_w2[j],
                            nsa_w_out[j])
        else:
            xf = _hgrn_layer(xf, B, S, layer, norm_g[layer, 0], norm_g[layer, 1],
                             hgrn_w_in[j], hgrn_lb, hgrn_onorm[j], hgrn_w_out[j])
        xf = _mlp(xf, norm_g[layer, 2], norm_g[layer, 3], mlp_w_up, mlp_w_down, layer)
    return xf.reshape(B, S, D).astype(x.dtype)
```
